```python
import math
import jax
import jax.numpy as jnp
from jax import lax
import numpy as np

D_MODEL = 1024
BATCH = 8
SEQ = 4096
DEPTH = 2
DEC_BATCH = 128
DEC_SEQ = 4
PAST_LEN = 16384
PAGE_SIZE = 128

N_EVEN = (DEPTH + 1) // 2
N_ODD = DEPTH // 2

SWA_WINDOW = 128
SWA_HEADS = 8
SWA_KV_HEADS = 2
SWA_GROUP = SWA_HEADS // SWA_KV_HEADS
SWA_HEAD_DIM = 64
GDN_HEADS = 4
GDN_DK = 128
GDN_DV = 128
GDN_CONV = 4
GDN_CHUNK = 64
GDN_QKV = GDN_HEADS * (2 * GDN_DK + GDN_DV)
LRU_WIDTH = 512
LRU_BLOCKS = 8
LRU_BLOCK_W = LRU_WIDTH // LRU_BLOCKS
LRU_CONV = 4
LRU_C = 8.0
MLA_HEADS = 8
MLA_Q_RANK = 384
MLA_KV_RANK = 256
MLA_NOPE = 64
MLA_ROPE = 32
MLA_V = 64
MLA_QBLOCK = 128
ROPE_THETA = 10000.0
D_FF = 2816
N_EXPERTS = 8
TOP_K = 2
MOE_FF = 2048
MOE_BLOCK = 128
NORM_EPS = 1e-6

EVEN_SPLITS = (SWA_HEADS * SWA_HEAD_DIM, SWA_KV_HEADS * SWA_HEAD_DIM, SWA_KV_HEADS * SWA_HEAD_DIM,
               GDN_QKV, GDN_HEADS * GDN_DV, GDN_HEADS, GDN_HEADS)
ODD_SPLITS = (LRU_WIDTH, LRU_WIDTH, MLA_Q_RANK, MLA_KV_RANK, MLA_ROPE)
EVEN_IN = sum(EVEN_SPLITS)
ODD_IN = sum(ODD_SPLITS)
EVEN_MIX = SWA_HEADS * SWA_HEAD_DIM + GDN_HEADS * GDN_DV
ODD_MIX = LRU_WIDTH + MLA_HEADS * MLA_V
MLA_ROW = MLA_KV_RANK + MLA_ROPE

kernel_name = 'hybrid_swa_gdn_rglru_mla_decoder_step'


def _split(u, sizes):
    return jnp.split(u, np.cumsum(sizes)[:-1].tolist(), axis=-1)


def rms_norm(x, w):
    xf = x.astype(jnp.float32)
    y = xf * lax.rsqrt(jnp.mean(xf * xf, axis=-1, keepdims=True) + NORM_EPS)
    return (y * w.astype(jnp.float32)).astype(x.dtype)


def l2norm(x):
    xf = x.astype(jnp.float32)
    return xf * lax.rsqrt(jnp.sum(xf * xf, axis=-1, keepdims=True) + NORM_EPS)


def causal_conv(x, prev, w, b=None):
    l = x.shape[1]
    width = w.shape[0]
    xx = jnp.concatenate([prev.astype(x.dtype), x], axis=1)
    y = xx[:, 0:l] * w[0]
    for j in range(1, width):
        y = y + xx[:, j:j + l] * w[j]
    if b is not None:
        y = y + b
    return y, xx[:, l:]


def swiglu(x, wg, wu, wd):
    return (jax.nn.silu(x @ wg) * (x @ wu)) @ wd


def softmax_with_sink(s, sink):
    m = jnp.maximum(jnp.max(s, axis=-1, keepdims=True), sink)
    p = jnp.exp(s - m)
    return p / (jnp.sum(p, axis=-1, keepdims=True) + jnp.exp(sink - m))


def alibi_slopes(n):
    return 2.0 ** (-8.0 * jnp.arange(1, n + 1, dtype=jnp.float32) / n)


def rope(x, pos):
    r = x.shape[-1]
    half = r // 2
    inv = ROPE_THETA ** (-jnp.arange(half, dtype=jnp.float32) * 2.0 / r)
    ang = pos.astype(jnp.float32)[:, None] * inv[None, :]
    shape = (1, pos.shape[0]) + (1,) * (x.ndim - 3) + (half,)
    cos = jnp.cos(ang).reshape(shape)
    sin = jnp.sin(ang).reshape(shape)
    xf = x.astype(jnp.float32)
    x1, x2 = xf[..., :half], xf[..., half:]
    return jnp.concatenate([x1 * cos - x2 * sin, x2 * cos + x1 * sin], axis=-1).astype(x.dtype)


def swa_banded(q, k, v, sink, slopes):
    n, l = q.shape[:2]
    w = SWA_WINDOW
    nb = l // w
    qb = q.reshape(n, nb, w, SWA_KV_HEADS, SWA_GROUP, SWA_HEAD_DIM)

    def band(t):
        tp = jnp.concatenate([jnp.zeros_like(t[:, :w]), t], axis=1).reshape(n, nb + 1, w, SWA_KV_HEADS, SWA_HEAD_DIM)
        return jnp.concatenate([tp[:, :-1], tp[:, 1:]], axis=2)

    kb, vb = band(k), band(v)
    s = jnp.einsum('nbqhgd,nbkhd->nbhgqk', qb, kb, preferred_element_type=jnp.float32) * (SWA_HEAD_DIM ** -0.5)
    blk = jnp.arange(nb)[:, None] * w
    qpos = blk + jnp.arange(w)[None, :]
    kpos = blk - w + jnp.arange(2 * w)[None, :]
    dist = qpos[:, :, None] - kpos[:, None, :]
    valid = (dist >= 0) & (dist < w) & (kpos[:, None, :] >= 0)
    distf = dist.astype(jnp.float32)[None, :, None, None]
    s = jnp.where(valid[None, :, None, None], s - slopes[:, :, None, None] * distf, -jnp.inf)
    p = softmax_with_sink(s, sink[:, :, None, None])
    o = jnp.einsum('nbhgqk,nbkhd->nbqhgd', p.astype(v.dtype), vb)
    return o.reshape(n, l, SWA_HEADS * SWA_HEAD_DIM)


def swa_buffered(q, k, v, buf, past_len, sink, slopes):
    n, l = q.shape[:2]
    wb = buf.shape[1]
    kk = jnp.concatenate([buf[:, :, 0].astype(k.dtype), k], axis=1)
    vv = jnp.concatenate([buf[:, :, 1].astype(v.dtype), v], axis=1)
    qpos = past_len + jnp.arange(l)
    kpos = jnp.concatenate([past_len - wb + jnp.arange(wb), qpos])
    dist = qpos[:, None] - kpos[None, :]
    valid = (dist >= 0) & (dist < SWA_WINDOW)
    s = jnp.einsum('nqhgd,nkhd->nhgqk', q, kk, preferred_element_type=jnp.float32) * (SWA_HEAD_DIM ** -0.5)
    s = jnp.where(valid, s - slopes[:, :, None, None] * dist.astype(jnp.float32), -jnp.inf)
    p = softmax_with_sink(s, sink[:, :, None, None])
    o = jnp.einsum('nhgqk,nkhd->nqhgd', p.astype(vv.dtype), vv)
    new_buf = jnp.stack([kk, vv], axis=2)[:, l:]
    return o.reshape(n, l, SWA_HEADS * SWA_HEAD_DIM), new_buf


def gated_delta_rule(q, k, v, g, beta, s0):
    n, l, h, dk = q.shape
    dv = v.shape[-1]
    c = math.gcd(l, GDN_CHUNK)
    nc = l // c

    def chunks(t):
        t = t.reshape((n, nc, c, h) + t.shape[3:])
        return jnp.moveaxis(t, (1, 3), (0, 2))

    qc, kc, vc = chunks(q), chunks(k), chunks(v)
    gc = jnp.cumsum(chunks(g), axis=-1)
    bc = chunks(beta)
    incl = jnp.tril(jnp.ones((c, c), bool))
    strict = jnp.tril(jnp.ones((c, c), bool), -1)
    decay = jnp.exp(jnp.where(incl, gc[..., :, None] - gc[..., None, :], -jnp.inf))
    kk = jnp.einsum('...id,...jd->...ij', kc, kc)
    a_mat = jnp.eye(c, dtype=jnp.float32) + jnp.where(strict, bc[..., :, None] * kk * decay, 0.0)
    rhs = jnp.concatenate([vc * bc[..., None], kc * (bc * jnp.exp(gc))[..., None]], axis=-1)
    sol = lax.linalg.triangular_solve(a_mat, rhs, left_side=True, lower=True, unit_diagonal=True)
    u, w = sol[..., :dv], sol[..., dv:]
    qk = jnp.einsum('...id,...jd->...ij', qc, kc) * decay
    q_dec = qc * jnp.exp(gc)[..., None]
    k_dec = kc * jnp.exp(gc[..., -1:] - gc)[..., None]
    g_last = jnp.exp(gc[..., -1])

    def step(s, inp):
        u_i, w_i, qk_i, qd_i, kd_i, gl_i = inp
        v_new = u_i - jnp.einsum('nhck,nhkv->nhcv', w_i, s)
        o_i = jnp.einsum('nhck,nhkv->nhcv', qd_i, s) + jnp.einsum('nhij,nhjv->nhiv', qk_i, v_new)
        s = s * gl_i[..., None, None] + jnp.einsum('nhck,nhcv->nhkv', kd_i, v_new)
        return s, o_i

    s_fin, o = lax.scan(step, s0, (u, w, qk, q_dec, k_dec, g_last))
    o = jnp.moveaxis(o, (0, 2), (1, 3)).reshape(n, l, h, dv)
    return o, s_fin


def rg_lru(x, h0, w_a, b_a, w_x, b_x, lam):
    n, l, c = x.shape
    xf = x.astype(jnp.float32)
    xh = xf.reshape(n, l, LRU_BLOCKS, LRU_BLOCK_W)
    r = jax.nn.sigmoid(jnp.einsum('nlhi,hij->nlhj', xh, w_a.astype(jnp.float32)).reshape(n, l, c) + b_a.astype(jnp.float32))
    i = jax.nn.sigmoid(jnp.einsum('nlhi,hij->nlhj', xh, w_x.astype(jnp.float32)).reshape(n, l, c) + b_x.astype(jnp.float32))
    log_a = -LRU_C * r * jax.nn.softplus(-lam.astype(jnp.float32))
    a = jnp.exp(log_a)
    b = jnp.sqrt(-jnp.expm1(2.0 * log_a)) * (i * xf)

    def combine(lhs, rhs):
        return lhs[0] * rhs[0], rhs[0] * lhs[1] + rhs[1]

    a_cum, b_cum = lax.associative_scan(combine, (a, b), axis=1)
    hs = a_cum * h0.astype(jnp.float32)[:, None, :] + b_cum
    return hs.astype(x.dtype), hs[:, -1]


def mla_prompt(q_nope, q_pe, rows, w_uk, w_uv):
    n, l = q_nope.shape[:2]
    ckv, kpe = rows[..., :MLA_KV_RANK], rows[..., MLA_KV_RANK:]
    k_nope = jnp.einsum('nkc,chd->nkhd', ckv, w_uk)
    v = jnp.einsum('nkc,chd->nkhd', ckv, w_uv)
    nb = l // MLA_QBLOCK
    qn = jnp.moveaxis(q_nope.reshape(n, nb, MLA_QBLOCK, MLA_HEADS, MLA_NOPE), 1, 0)
    qp = jnp.moveaxis(q_pe.reshape(n, nb, MLA_QBLOCK, MLA_HEADS, MLA_ROPE), 1, 0)
    kpos = jnp.arange(l)
    scale = (MLA_NOPE + MLA_ROPE) ** -0.5

    def block(args):
        qn_b, qp_b, bi = args
        s = (jnp.einsum('nqhd,nkhd->nhqk', qn_b, k_nope, preferred_element_type=jnp.float32)
             + jnp.einsum('nqhr,nkr->nhqk', qp_b, kpe, preferred_element_type=jnp.float32)) * scale
        qpos = bi * MLA_QBLOCK + jnp.arange(MLA_QBLOCK)
        s = jnp.where(kpos[None, :] <= qpos[:, None], s, -jnp.inf)
        p = jax.nn.softmax(s, axis=-1)
        return jnp.einsum('nhqk,nkhd->nqhd', p.astype(v.dtype), v)

    o = lax.map(block, (qn, qp, jnp.arange(nb)))
    return jnp.moveaxis(o, 0, 1).reshape(n, l, MLA_HEADS * MLA_V)


def mla_cached(q_nope, q_pe, rows, past, w_uk, w_uv):
    n, l = q_nope.shape[:2]
    past = past.reshape(n, -1, past.shape[-1])
    n_past = past.shape[1]
    scale = (MLA_NOPE + MLA_ROPE) ** -0.5
    q_cat = jnp.concatenate([jnp.einsum('nlhd,chd->nlhc', q_nope, w_uk), q_pe], axis=-1)
    s_past = jnp.einsum('nlhc,npc->nhlp', q_cat, past.astype(q_cat.dtype), preferred_element_type=jnp.float32) * scale
    s_new = jnp.einsum('nlhc,nmc->nhlm', q_cat, rows, preferred_element_type=jnp.float32) * scale
    causal = jnp.arange(l)[:, None] >= jnp.arange(l)[None, :]
    s_new = jnp.where(causal, s_new, -jnp.inf)
    p = jax.nn.softmax(jnp.concatenate([s_past, s_new], axis=-1), axis=-1)
    o_lat = (jnp.einsum('nhlp,npc->nlhc', p[..., :n_past].astype(rows.dtype), past.astype(rows.dtype))
             + jnp.einsum('nhlm,nmc->nlhc', p[..., n_past:].astype(rows.dtype), rows))
    o = jnp.einsum('nlhc,chd->nlhd', o_lat[..., :MLA_KV_RANK], w_uv)
    return o.reshape(n, l, MLA_HEADS * MLA_V)


def moe_ffn(h, router_w, router_b, e_gate, e_up, e_down):
    n, l, d = h.shape
    x = h.reshape(n * l, d)
    t = n * l
    logits = jnp.einsum('td,de->te', x, router_w, preferred_element_type=jnp.float32) + router_b.astype(jnp.float32)
    top_v, top_e = lax.top_k(logits, TOP_K)
    gates = jax.nn.softmax(top_v, axis=-1)
    flat_e = top_e.reshape(-1)
    n_assign = t * TOP_K
    order = jnp.argsort(flat_e)
    sorted_e = flat_e[order]
    counts = jnp.bincount(flat_e, length=N_EXPERTS)
    padded = (counts + MOE_BLOCK - 1) // MOE_BLOCK * MOE_BLOCK
    pad_end = jnp.cumsum(padded)
    pad_start = pad_end - padded
    grp_start = jnp.cumsum(counts) - counts
    dest = pad_start[sorted_e] + jnp.arange(n_assign) - grp_start[sorted_e]
    n_blocks = -(-n_assign // MOE_BLOCK) + N_EXPERTS
    rows_tok = jnp.full((n_blocks * MOE_BLOCK,), t, jnp.int32).at[dest].set((order // TOP_K).astype(jnp.int32))
    x_pad = jnp.concatenate([x, jnp.zeros((1, d), x.dtype)], axis=0)
    xb = x_pad[rows_tok].reshape(n_blocks, MOE_BLOCK, d)
    blk_start = jnp.arange(n_blocks) * MOE_BLOCK
    blk_e = jnp.minimum(jnp.sum(blk_start[:, None] >= pad_end[None, :], axis=1), N_EXPERTS - 1)

    def expert_block(args):
        xblk, e = args
        return swiglu(xblk, e_gate[e], e_up[e], e_down[e])

    yb = lax.map(expert_block, (xb, blk_e)).reshape(-1, d)
    y_assign = jnp.zeros((n_assign, d), yb.dtype).at[order].set(yb[dest]).reshape(t, TOP_K, d)
    y = jnp.einsum('tkd,tk->td', y_assign, gates.astype(y_assign.dtype))
    return y.reshape(n, l, d)


def even_layer(x, swa_past, conv_prev, s0, past_len, ln_mix, w_in, conv_w, a_log, dt_bias, gdn_norm,
               sinks, w_out, ln_ffn, w_gate, w_up, w_down):
    n, l, _ = x.shape
    h = rms_norm(x, ln_mix)
    q_a, k_a, v_a, qkv_b, z_b, beta_raw, a_raw = _split(h @ w_in, EVEN_SPLITS)
    q_a = q_a.reshape(n, l, SWA_KV_HEADS, SWA_GROUP, SWA_HEAD_DIM)
    k_a = k_a.reshape(n, l, SWA_KV_HEADS, SWA_HEAD_DIM)
    v_a = v_a.reshape(n, l, SWA_KV_HEADS, SWA_HEAD_DIM)
    slopes = alibi_slopes(SWA_HEADS).reshape(SWA_KV_HEADS, SWA_GROUP)
    sink = sinks.astype(jnp.float32).reshape(SWA_KV_HEADS, SWA_GROUP)
    if swa_past is None:
        o_a = swa_banded(q_a, k_a, v_a, sink, slopes)
        new_swa = jnp.stack([k_a[:, l - SWA_WINDOW:], v_a[:, l - SWA_WINDOW:]], axis=2)
    else:
        o_a, new_swa = swa_buffered(q_a, k_a, v_a, swa_past, past_len, sink, slopes)
    qkv_c, new_conv = causal_conv(qkv_b, conv_prev, conv_w)
    q_b, k_b, v_b = _split(jax.nn.silu(qkv_c), (GDN_HEADS * GDN_DK, GDN_HEADS * GDN_DK, GDN_HEADS * GDN_DV))
    q_b = l2norm(q_b.reshape(n, l, GDN_HEADS, GDN_DK)) * (GDN_DK ** -0.5)
    k_b = l2norm(k_b.reshape(n, l, GDN_HEADS, GDN_DK))
    v_b = v_b.reshape(n, l, GDN_HEADS, GDN_DV).astype(jnp.float32)
    beta = jax.nn.sigmoid(beta_raw.astype(jnp.float32))
    g = -jnp.exp(a_log.astype(jnp.float32)) * jax.nn.softplus(a_raw.astype(jnp.float32) + dt_bias.astype(jnp.float32))
    o_b, s_new = gated_delta_rule(q_b, k_b, v_b, g, beta, s0.astype(jnp.float32))
    o_b = rms_norm(o_b, gdn_norm) * jax.nn.silu(z_b.reshape(n, l, GDN_HEADS, GDN_DV).astype(jnp.float32))
    mix = jnp.concatenate([o_a, o_b.reshape(n, l, GDN_HEADS * GDN_DV).astype(x.dtype)], axis=-1)
    x = x + mix @ w_out
    x = x + swiglu(rms_norm(x, ln_ffn), w_gate, w_up, w_down)
    return x, new_swa, new_conv, s_new


def odd_layer(x, conv_prev, h0, mla_past, past_len, ln_mix, w_in, conv_w, conv_b, w_a, b_a, w_x, b_x, lam,
              q_norm, w_uq, kv_norm, w_uk, w_uv, w_out, ln_ffn, router_w, router_b, e_gate, e_up, e_down):
    n, l, _ = x.shape
    h = rms_norm(x, ln_mix)
    xb, gb, cq, ckv, kpe = _split(h @ w_in, ODD_SPLITS)
    xc, new_conv = causal_conv(xb, conv_prev, conv_w, conv_b)
    y_c, h_last = rg_lru(xc, h0, w_a, b_a, w_x, b_x, lam)
    y_c = y_c * jax.nn.gelu(gb)
    pos = past_len + jnp.arange(l)
    q = (rms_norm(cq, q_norm) @ w_uq).reshape(n, l, MLA_HEADS, MLA_NOPE + MLA_ROPE)
    q_nope = q[..., :MLA_NOPE]
    q_pe = rope(q[..., MLA_NOPE:], pos)
    rows = jnp.concatenate([rms_norm(ckv, kv_norm), rope(kpe, pos)], axis=-1)
    if mla_past is None:
        o_d = mla_prompt(q_nope, q_pe, rows, w_uk, w_uv)
    else:
        o_d = mla_cached(q_nope, q_pe, rows, mla_past, w_uk, w_uv)
    mix = jnp.concatenate([y_c, o_d.astype(x.dtype)], axis=-1)
    x = x + mix @ w_out
    x = x + moe_ffn(rms_norm(x, ln_ffn), router_w, router_b, e_gate, e_up, e_down)
    return x, new_conv, h_last, rows


def setup_inputs(seed: int = 0) -> dict:
    key = jax.random.key(seed)
    ks = iter(jax.random.split(key, 64))

    def nrm(shape, scale):
        return jax.random.normal(next(ks), shape, jnp.float32) * scale

    def gain(shape):
        return 1.0 + nrm(shape, 0.02)

    n_pages = PAST_LEN // PAGE_SIZE
    n_used = DEC_BATCH * n_pages
    n_phys = n_used + n_used // 4
    wbuf = min(SWA_WINDOW, PAST_LEN)
    page_table = jax.random.permutation(next(ks), n_phys)[:n_used].reshape(DEC_BATCH, n_pages).astype(jnp.int32)
    a_log = jnp.log(jax.random.uniform(next(ks), (N_EVEN, GDN_HEADS), jnp.float32, 1.0, 16.0))
    dt = jnp.exp(jax.random.uniform(next(ks), (N_EVEN, GDN_HEADS), jnp.float32, math.log(1e-3), math.log(0.1)))
    dt_bias = dt + jnp.log(-jnp.expm1(-dt))
    a_c = jax.random.uniform(next(ks), (N_ODD, LRU_WIDTH), jnp.float32, 0.9, 0.999)
    a_base = a_c ** (1.0 / LRU_C)
    lam = jnp.log(a_base) - jnp.log1p(-a_base)
    dm = D_MODEL ** -0.5
    return {
        'x_prompt': nrm((BATCH, SEQ, D_MODEL), 1.0),
        'x_sample': nrm((DEC_BATCH, DEC_SEQ, D_MODEL), 1.0),
        'state_swa_kv': nrm((N_EVEN, DEC_BATCH, wbuf, 2, SWA_KV_HEADS, SWA_HEAD_DIM), 1.0),
        'state_gdn_conv': nrm((N_EVEN, DEC_BATCH, GDN_CONV - 1, GDN_QKV), 1.0),
        'state_gdn_s': nrm((N_EVEN, DEC_BATCH, GDN_HEADS, GDN_DK, GDN_DV), 0.1),
        'state_lru_conv': nrm((N_ODD, DEC_BATCH, LRU_CONV - 1, LRU_WIDTH), 1.0),
        'state_lru_h': nrm((N_ODD, DEC_BATCH, LRU_WIDTH), 0.5),
        'cache_mla': nrm((N_ODD, n_phys, PAGE_SIZE, MLA_ROW), 1.0),
        'page_table': page_table,
        'e_ln_mix': gain((N_EVEN, D_MODEL)),
        'e_w_in': nrm((N_EVEN, D_MODEL, EVEN_IN), dm),
        'e_gdn_conv_w': nrm((N_EVEN, GDN_CONV, GDN_QKV), 0.5),
        'e_gdn_a_log': a_log,
        'e_gdn_dt_bias': dt_bias,
        'e_gdn_norm': gain((N_EVEN, GDN_DV)),
        'e_swa_sinks': nrm((N_EVEN, SWA_HEADS), 0.5),
        'e_w_out': nrm((N_EVEN, EVEN_MIX, D_MODEL), EVEN_MIX ** -0.5),
        'e_ln_ffn': gain((N_EVEN, D_MODEL)),
        'e_ffn_gate': nrm((N_EVEN, D_MODEL, D_FF), dm),
        'e_ffn_up': nrm((N_EVEN, D_MODEL, D_FF), dm),
        'e_ffn_down': nrm((N_EVEN, D_FF, D_MODEL), D_FF ** -0.5),
        'o_ln_mix': gain((N_ODD, D_MODEL)),
        'o_w_in': nrm((N_ODD, D_MODEL, ODD_IN), dm),
        'o_lru_conv_w': nrm((N_ODD, LRU_CONV, LRU_WIDTH), 0.5),
        'o_lru_conv_b': nrm((N_ODD, LRU_WIDTH), 0.1),
        'o_lru_w_a': nrm((N_ODD, LRU_BLOCKS, LRU_BLOCK_W, LRU_BLOCK_W), LRU_BLOCK_W ** -0.5),
        'o_lru_b_a': nrm((N_ODD, LRU_WIDTH), 0.1),
        'o_lru_w_x': nrm((N_ODD, LRU_BLOCKS, LRU_BLOCK_W, LRU_BLOCK_W), LRU_BLOCK_W ** -0.5),
        'o_lru_b_x': nrm((N_ODD, LRU_WIDTH), 0.1),
        'o_lru_lambda': lam,
        'o_mla_q_norm': gain((N_ODD, MLA_Q_RANK)),
        'o_mla_w_uq': nrm((N_ODD, MLA_Q_RANK, MLA_HEADS * (MLA_NOPE + MLA_ROPE)), MLA_Q_RANK ** -0.5),
        'o_mla_kv_norm': gain((N_ODD, MLA_KV_RANK)),
        'o_mla_w_uk': nrm((N_ODD, MLA_KV_RANK, MLA_HEADS, MLA_NOPE), MLA_KV_RANK ** -0.5),
        'o_mla_w_uv': nrm((N_ODD, MLA_KV_RANK, MLA_HEADS, MLA_V), MLA_KV_RANK ** -0.5),
        'o_w_out': nrm((N_ODD, ODD_MIX, D_MODEL), ODD_MIX ** -0.5),
        'o_ln_ffn': gain((N_ODD, D_MODEL)),
        'o_router_w': nrm((N_ODD, D_MODEL, N_EXPERTS), dm),
        'o_router_b': nrm((N_ODD, N_EXPERTS), 0.01),
        'o_exp_gate': nrm((N_ODD, N_EXPERTS, D_MODEL, MOE_FF), dm),
        'o_exp_up': nrm((N_ODD, N_EXPERTS, D_MODEL, MOE_FF), dm),
        'o_exp_down': nrm((N_ODD, N_EXPERTS, MOE_FF, D_MODEL), MOE_FF ** -0.5),
        'final_norm': gain((D_MODEL,)),
    }


def reference(x_prompt, x_sample, state_swa_kv, state_gdn_conv, state_gdn_s, state_lru_conv, state_lru_h,
              cache_mla, page_table, e_ln_mix, e_w_in, e_gdn_conv_w, e_gdn_a_log, e_gdn_dt_bias, e_gdn_norm,
              e_swa_sinks, e_w_out, e_ln_ffn, e_ffn_gate, e_ffn_up, e_ffn_down, o_ln_mix, o_w_in, o_lru_conv_w,
              o_lru_conv_b, o_lru_w_a, o_lru_b_a, o_lru_w_x, o_lru_b_x, o_lru_lambda, o_mla_q_norm, o_mla_w_uq,
              o_mla_kv_norm, o_mla_w_uk, o_mla_w_uv, o_w_out, o_ln_ffn, o_router_w, o_router_b, o_exp_gate,
              o_exp_up, o_exp_down, final_norm):
    past_len = page_table.shape[1] * PAGE_SIZE
    b = x_prompt.shape[0]
    l_p = x_prompt.shape[1]
    xp, xs = x_prompt, x_sample
    swa_p, swa_s, gconv_p, gconv_s, gs_p, gs_s = [], [], [], [], [], []
    lconv_p, lconv_s, lh_p, lh_s, mla_p, mla_s = [], [], [], [], [], []
    for layer in range(DEPTH):
        j = layer // 2
        if layer % 2 == 0:
            ew = (e_ln_mix[j], e_w_in[j], e_gdn_conv_w[j], e_gdn_a_log[j], e_gdn_dt_bias[j], e_gdn_norm[j],
                  e_swa_sinks[j], e_w_out[j], e_ln_ffn[j], e_ffn_gate[j], e_ffn_up[j], e_ffn_down[j])
            xp, kv_p, cv_p, s_p = even_layer(
                xp, None, jnp.zeros((b, GDN_CONV - 1, GDN_QKV), xp.dtype),
                jnp.zeros((b, GDN_HEADS, GDN_DK, GDN_DV), jnp.float32), 0, *ew)
            xs, kv_s, cv_s, s_s = even_layer(
                xs, state_swa_kv[j], state_gdn_conv[j], state_gdn_s[j], past_len, *ew)
            swa_p.append(kv_p)
            swa_s.append(kv_s)
            gconv_p.append(cv_p)
            gconv_s.append(cv_s)
            gs_p.append(s_p)
            gs_s.append(s_s)
        else:
            ow = (o_ln_mix[j], o_w_in[j], o_lru_conv_w[j], o_lru_conv_b[j], o_lru_w_a[j], o_lru_b_a[j],
                  o_lru_w_x[j], o_lru_b_x[j], o_lru_lambda[j], o_mla_q_norm[j], o_mla_w_uq[j], o_mla_kv_norm[j],
                  o_mla_w_uk[j], o_mla_w_uv[j], o_w_out[j], o_ln_ffn[j], o_router_w[j], o_router_b[j],
                  o_exp_gate[j], o_exp_up[j], o_exp_down[j])
            xp, c_p, h_p, r_p = odd_layer(
                xp, jnp.zeros((b, LRU_CONV - 1, LRU_WIDTH), xp.dtype),
                jnp.zeros((b, LRU_WIDTH), jnp.float32), None, 0, *ow)
            mla_past = cache_mla[j, page_table]
            xs, c_s, h_s, r_s = odd_layer(
                xs, state_lru_conv[j], state_lru_h[j], mla_past, past_len, *ow)
            lconv_p.append(c_p)
            lconv_s.append(c_s)
            lh_p.append(h_p)
            lh_s.append(h_s)
            mla_p.append(r_p.reshape(b, l_p // PAGE_SIZE, PAGE_SIZE, MLA_ROW))
            mla_s.append(r_s)
    y_prompt = rms_norm(xp, final_norm)
    y_sample = rms_norm(xs, final_norm)
    return (y_prompt, y_sample,
            jnp.stack(swa_p), jnp.stack(swa_s),
            jnp.stack(gconv_p), jnp.stack(gconv_s),
            jnp.stack(gs_p), jnp.stack(gs_s),
            jnp.stack(lconv_p), jnp.stack(lconv_s),
            jnp.stack(lh_p), jnp.stack(lh_s),
            jnp.stack(mla_p), jnp.stack(mla_s))
```

```python
import functools
import math

import jax
import jax.numpy as jnp
from jax import lax
from jax.experimental import pallas as pl
from jax.experimental.pallas import tpu as pltpu

F32 = jnp.float32
BF16 = jnp.bfloat16
I32 = jnp.int32
HI = lax.Precision.HIGHEST

D_MODEL = 1024
PAGE_SIZE = 128
SWA_WINDOW = 128
SWA_HEADS = 8
SWA_KV_HEADS = 2
SWA_GROUP = SWA_HEADS // SWA_KV_HEADS
SWA_HEAD_DIM = 64
GDN_HEADS = 4
GDN_DK = 128
GDN_DV = 128
GDN_CONV = 4
GDN_CHUNK = 64
GDN_QKV = GDN_HEADS * (2 * GDN_DK + GDN_DV)
LRU_WIDTH = 512
LRU_BLOCKS = 8
LRU_BLOCK_W = LRU_WIDTH // LRU_BLOCKS
LRU_C = 8.0
MLA_HEADS = 8
MLA_Q_RANK = 384
MLA_KV_RANK = 256
MLA_NOPE = 64
MLA_ROPE = 32
MLA_V = 64
MLA_ROW = MLA_KV_RANK + MLA_ROPE
ROPE_THETA = 10000.0
D_FF = 2816
N_EXPERTS = 8
TOP_K = 2
MOE_FF = 2048
NORM_EPS = 1e-6

LANES = 128
SUBLANES = 8
VMEM_LIMIT = 56 << 20

NN = (((1,), (0,)), ((), ()))
NT = (((1,), (1,)), ((), ()))
TN = (((0,), (0,)), ((), ()))


def _mm(a, b, dims=NN, precision=None):
    return lax.dot_general(a, b, dims, precision=precision, preferred_element_type=F32)


def _cp(sem):
    return pltpu.CompilerParams(dimension_semantics=sem, vmem_limit_bytes=VMEM_LIMIT)


def _rms(x, w):
    return x * lax.rsqrt(jnp.mean(x * x, axis=-1, keepdims=True) + NORM_EPS) * w


def _silu(x):
    return x * jax.nn.sigmoid(x)


def _full(shape):
    nd = len(shape)
    return pl.BlockSpec(shape, lambda *a: (0,) * nd)


def _row_tile(t, pref):
    for c in pref:
        if t % c == 0:
            return c
    return t


def _norm_proj_kernel(x_ref, ln_ref, *refs):
    n = len(refs) // 2
    h = _rms(x_ref[...], ln_ref[...]).astype(BF16)
    for w_ref, o_ref in zip(refs[:n], refs[n:]):
        o_ref[...] = _mm(h, w_ref[...])


def norm_proj(x, ln, ws):
    t, d = x.shape
    tm = _row_tile(t, (512, 256, 128, 64, 32, 16, 8))
    return pl.pallas_call(
        _norm_proj_kernel,
        grid=(t // tm,),
        in_specs=[pl.BlockSpec((tm, d), lambda i: (i, 0)), _full((1, d))] + [_full(w.shape) for w in ws],
        out_specs=[pl.BlockSpec((tm, w.shape[1]), lambda i: (i, 0)) for w in ws],
        out_shape=[jax.ShapeDtypeStruct((t, w.shape[1]), F32) for w in ws],
        compiler_params=_cp(("parallel",)),
        name="norm_proj",
    )(x, ln, *ws)


def _swa_softmax_pv(parts, sink):
    m = sink
    for s, _ in parts:
        m = jnp.maximum(m, jnp.max(s, axis=-1, keepdims=True))
    den = jnp.exp(sink - m)
    o = None
    for s, v in parts:
        p = jnp.exp(s - m)
        den = den + jnp.sum(p, axis=-1, keepdims=True)
        pv = _mm(p.astype(BF16), v)
        o = pv if o is None else o + pv
    return o / den


def _swa_prompt_kernel(sink_ref, q_ref, kvc_ref, kvp_ref, o_ref):
    b = pl.program_id(1)
    w = SWA_WINDOW
    hd = SWA_HEAD_DIM
    q = q_ref[...]
    kvc = kvc_ref[...].astype(BF16)
    kvp = kvp_ref[...].astype(BF16)
    qi = lax.broadcasted_iota(I32, (w, w), 0)
    kj = lax.broadcasted_iota(I32, (w, w), 1)
    dist_c = (qi - kj).astype(F32)
    dist_p = dist_c + float(w)
    valid_c = kj <= qi
    valid_p = jnp.logical_and(kj > qi, b > 0)
    outs = []
    for h in range(SWA_HEADS):
        j = h // SWA_GROUP
        slope = 2.0 ** (-8.0 * (h + 1) / SWA_HEADS)
        qh = q[:, h * hd:(h + 1) * hd].astype(BF16)
        kc = kvc[:, j * hd:(j + 1) * hd]
        kp = kvp[:, j * hd:(j + 1) * hd]
        vc = kvc[:, (SWA_KV_HEADS + j) * hd:(SWA_KV_HEADS + j + 1) * hd]
        vp = kvp[:, (SWA_KV_HEADS + j) * hd:(SWA_KV_HEADS + j + 1) * hd]
        s_c = _mm(qh, kc, NT) * (hd ** -0.5)
        s_p = _mm(qh, kp, NT) * (hd ** -0.5)
        s_c = jnp.where(valid_c, s_c - slope * dist_c, -jnp.inf)
        s_p = jnp.where(valid_p, s_p - slope * dist_p, -jnp.inf)
        outs.append(_swa_softmax_pv([(s_p, vp), (s_c, vc)], sink_ref[h]))
    o_ref[...] = jnp.concatenate(outs, axis=1)


def swa_prompt(qkv, sinks, n, l):
    w = SWA_WINDOW
    nb = l // w
    nq = SWA_HEADS * SWA_HEAD_DIM
    nkv = 2 * SWA_KV_HEADS * SWA_HEAD_DIM
    return pl.pallas_call(
        _swa_prompt_kernel,
        grid=(n, nb),
        in_specs=[
            pl.BlockSpec(memory_space=pltpu.SMEM),
            pl.BlockSpec((w, nq), lambda i, b: (i * nb + b, 0)),
            pl.BlockSpec((w, nkv), lambda i, b: (i * nb + b, nq // nkv)),
            pl.BlockSpec((w, nkv), lambda i, b: (i * nb + jnp.maximum(b - 1, 0), nq // nkv)),
        ],
        out_specs=pl.BlockSpec((w, nq), lambda i, b: (i * nb + b, 0)),
        out_shape=jax.ShapeDtypeStruct((n * l, nq), F32),
        compiler_params=_cp(("parallel", "arbitrary")),
        name="swa_prompt",
    )(sinks, qkv, qkv, qkv)


def _swa_sample_kernel(sink_ref, q_ref, buf_ref, o_ref, *, l):
    hd = SWA_HEAD_DIM
    lp = q_ref.shape[1]
    wb = buf_ref.shape[1]
    x = q_ref[0]
    buf = buf_ref[0].astype(BF16)
    kvn = x[:, SWA_HEADS * hd:].astype(BF16)
    qi_b = lax.broadcasted_iota(I32, (lp, wb), 0)
    kj_b = lax.broadcasted_iota(I32, (lp, wb), 1)
    dist_b = (qi_b + wb - kj_b).astype(F32)
    valid_b = (qi_b + wb - kj_b) < SWA_WINDOW
    qi_n = lax.broadcasted_iota(I32, (lp, lp), 0)
    kj_n = lax.broadcasted_iota(I32, (lp, lp), 1)
    dist_n = (qi_n - kj_n).astype(F32)
    valid_n = jnp.logical_and(kj_n <= qi_n, kj_n < l)
    outs = []
    for h in range(SWA_HEADS):
        j = h // SWA_GROUP
        slope = 2.0 ** (-8.0 * (h + 1) / SWA_HEADS)
        qh = x[:, h * hd:(h + 1) * hd].astype(BF16)
        kb = buf[:, j * hd:(j + 1) * hd]
        vb = buf[:, (SWA_KV_HEADS + j) * hd:(SWA_KV_HEADS + j + 1) * hd]
        kn = kvn[:, j * hd:(j + 1) * hd]
        vn = kvn[:, (SWA_KV_HEADS + j) * hd:(SWA_KV_HEADS + j + 1) * hd]
        s_b = _mm(qh, kb, NT) * (hd ** -0.5)
        s_n = _mm(qh, kn, NT) * (hd ** -0.5)
        s_b = jnp.where(valid_b, s_b - slope * dist_b, -jnp.inf)
        s_n = jnp.where(valid_n, s_n - slope * dist_n, -jnp.inf)
        outs.append(_swa_softmax_pv([(s_b, vb), (s_n, vn)], sink_ref[h]))
    o_ref[0] = jnp.concatenate(outs, axis=1)


def swa_sample(qkv_pad, buf, sinks, l):
    n, lp, c = qkv_pad.shape
    wb = buf.shape[1]
    nq = SWA_HEADS * SWA_HEAD_DIM
    return pl.pallas_call(
        functools.partial(_swa_sample_kernel, l=l),
        grid=(n,),
        in_specs=[
            pl.BlockSpec(memory_space=pltpu.SMEM),
            pl.BlockSpec((1, lp, c), lambda i: (i, 0, 0)),
            pl.BlockSpec((1, wb, buf.shape[2]), lambda i: (i, 0, 0)),
        ],
        out_specs=pl.BlockSpec((1, lp, nq), lambda i: (i, 0, 0)),
        out_shape=jax.ShapeDtypeStruct((n, lp, nq), F32),
        compiler_params=_cp(("parallel",)),
        name="swa_sample",
    )(sinks, qkv_pad, buf)


def _gdn_kernel(x_ref, z_ref, bg_ref, cprev_ref, s0_ref, cw_ref, alog_ref, dtb_ref, gn_ref,
                o_ref, sfin_ref, xbuf, s_sc, q_sc, k_sc, v_sc, be_sc, gc_sc, *, chunk, valid):
    t = pl.program_id(1)
    tl = x_ref.shape[1]
    c = chunk
    nck = tl // c
    nh = GDN_HEADS
    dk = GDN_DK

    @pl.when(t == 0)
    def _():
        xbuf[0:SUBLANES, :] = cprev_ref[0]
        s_sc[...] = s0_ref[0]

    xbuf[SUBLANES:SUBLANES + tl, :] = x_ref[0]
    cw = cw_ref[...]
    conv = xbuf[5:5 + tl, :] * cw[0:1, :]
    for j in range(1, GDN_CONV):
        conv = conv + xbuf[5 + j:5 + j + tl, :] * cw[j:j + 1, :]
    xbuf[0:SUBLANES, :] = xbuf[tl:tl + SUBLANES, :]
    act = _silu(conv)

    row = lax.broadcasted_iota(I32, (tl, 1), 0)
    rmask = (row % c) < valid
    for h in range(nh):
        qh = act[:, h * dk:(h + 1) * dk]
        kh = act[:, (nh + h) * dk:(nh + h + 1) * dk]
        vh = act[:, (2 * nh + h) * dk:(2 * nh + h + 1) * dk]
        qh = qh * lax.rsqrt(jnp.sum(qh * qh, axis=-1, keepdims=True) + NORM_EPS) * (dk ** -0.5)
        kh = kh * lax.rsqrt(jnp.sum(kh * kh, axis=-1, keepdims=True) + NORM_EPS)
        q_sc[:, h * dk:(h + 1) * dk] = jnp.where(rmask, qh, 0.0)
        k_sc[:, h * dk:(h + 1) * dk] = jnp.where(rmask, kh, 0.0)
        v_sc[:, h * dk:(h + 1) * dk] = jnp.where(rmask, vh, 0.0)
    bg = bg_ref[0]
    be_sc[...] = jax.nn.sigmoid(bg)
    g = -jnp.exp(alog_ref[...]) * jax.nn.softplus(bg + dtb_ref[...])
    g = jnp.where(rmask, g, 0.0)
    rc = row % c
    s = 1
    while s < c:
        g = g + jnp.where(rc >= s, pltpu.roll(g, s, axis=0), 0.0)
        s *= 2
    gc_sc[...] = g

    ii = lax.broadcasted_iota(I32, (c, c), 0)
    jj = lax.broadcasted_iota(I32, (c, c), 1)
    eye = (ii == jj).astype(F32)
    gn = gn_ref[...]
    n_sq = int(round(math.log2(c)))

    def chunk_body(ci, carry):
        r0 = pl.multiple_of(ci * c, c)
        gcs = gc_sc[pl.ds(r0, c), :]
        bes = be_sc[pl.ds(r0, c), :]
        for h in range(nh):
            q = q_sc[pl.ds(r0, c), h * dk:(h + 1) * dk]
            k = k_sc[pl.ds(r0, c), h * dk:(h + 1) * dk]
            v = v_sc[pl.ds(r0, c), h * dk:(h + 1) * dk]
            beta = bes[:, h:h + 1]
            gcol = gcs[:, nh + h:nh + h + 1]
            grow = jnp.sum(jnp.where(ii == jj, gcol, 0.0), axis=0, keepdims=True)
            glast = gcol[c - 1:c, :]
            decay = jnp.where(ii >= jj, jnp.exp(gcol - grow), 0.0)
            kk = _mm(k, k, NT, HI)
            a = jnp.where(ii > jj, beta * kk * decay, 0.0)
            tinv = eye - a
            p = a
            for _ in range(n_sq - 1):
                p = _mm(p, p, NN, HI)
                tinv = _mm(tinv, eye + p, NN, HI)
            egc = jnp.exp(gcol)
            rhs = jnp.concatenate([v * beta, k * (beta * egc)], axis=1)
            sol = _mm(tinv, rhs, NN, HI)
            u = sol[:, :GDN_DV]
            wmat = sol[:, GDN_DV:]
            qk = _mm(q, k, NT, HI) * decay
            q_dec = q * egc
            k_dec = k * jnp.exp(glast - gcol)
            st = s_sc[h]
            v_new = u - _mm(wmat, st, NN, HI)
            o = _mm(q_dec, st, NN, HI) + _mm(qk, v_new, NN, HI)
            s_sc[h] = st * jnp.exp(glast) + _mm(k_dec, v_new, TN, HI)
            zh = z_ref[0, pl.ds(r0, c), h * GDN_DV:(h + 1) * GDN_DV]
            o_ref[0, pl.ds(r0, c), h * GDN_DV:(h + 1) * GDN_DV] = _rms(o, gn) * _silu(zh)
        return carry

    lax.fori_loop(0, nck, chunk_body, 0)

    @pl.when(t == pl.num_programs(1) - 1)
    def _():
        sfin_ref[0] = s_sc[...]


def gdn(x, z, bg, cprev, s0, conv_w, a_log, dt_bias, gnorm, chunk, valid, tl):
    n, lp, cq = x.shape
    nt = lp // tl
    nv = GDN_HEADS * GDN_DV
    alog = jnp.zeros((1, LANES), F32).at[0, GDN_HEADS:2 * GDN_HEADS].set(a_log)
    dtb = jnp.zeros((1, LANES), F32).at[0, GDN_HEADS:2 * GDN_HEADS].set(dt_bias)
    seq3 = lambda w: pl.BlockSpec((1, tl, w), lambda i, t: (i, t, 0))
    return pl.pallas_call(
        functools.partial(_gdn_kernel, chunk=chunk, valid=valid),
        grid=(n, nt),
        in_specs=[
            seq3(cq), seq3(nv), seq3(LANES),
            pl.BlockSpec((1, SUBLANES, cq), lambda i, t: (i, 0, 0)),
            pl.BlockSpec((1, GDN_HEADS, GDN_DK, GDN_DV), lambda i, t: (i, 0, 0, 0)),
            _full((GDN_CONV, cq)), _full((1, LANES)), _full((1, LANES)), _full((1, GDN_DV)),
        ],
        out_specs=[seq3(nv), pl.BlockSpec((1, GDN_HEADS, GDN_DK, GDN_DV), lambda i, t: (i, 0, 0, 0))],
        out_shape=[jax.ShapeDtypeStruct((n, lp, nv), F32),
                   jax.ShapeDtypeStruct((n, GDN_HEADS, GDN_DK, GDN_DV), F32)],
        scratch_shapes=[
            pltpu.VMEM((tl + SUBLANES, cq), F32),
            pltpu.VMEM((GDN_HEADS, GDN_DK, GDN_DV), F32),
            pltpu.VMEM((tl, nv), F32), pltpu.VMEM((tl, nv), F32), pltpu.VMEM((tl, nv), F32),
            pltpu.VMEM((tl, LANES), F32), pltpu.VMEM((tl, LANES), F32),
        ],
        compiler_params=_cp(("parallel", "arbitrary")),
        name="gdn",
    )(x, z, bg, cprev, s0, conv_w, alog, dtb, gnorm.reshape(1, GDN_DV))


def _even_out_kernel(x_ref, oa_ref, ob_ref, woa_ref, wob_ref, ln_ref, wg_ref, wu_ref, wd_ref,
                     o_ref, x1_sc, h_sc, acc_sc):
    f = pl.program_id(1)

    @pl.when(f == 0)
    def _():
        x1 = x_ref[...] + (_mm(oa_ref[...].astype(BF16), woa_ref[...]) + _mm(ob_ref[...].astype(BF16), wob_ref[...]))
        x1_sc[...] = x1
        h_sc[...] = _rms(x1, ln_ref[...]).astype(BF16)
        acc_sc[...] = jnp.zeros_like(acc_sc)

    h = h_sc[...]
    act = (_silu(_mm(h, wg_ref[...])) * _mm(h, wu_ref[...])).astype(BF16)
    acc_sc[...] += _mm(act, wd_ref[...])

    @pl.when(f == pl.num_programs(1) - 1)
    def _():
        o_ref[...] = x1_sc[...] + acc_sc[...]


def even_out(x, oa, ob, woa, wob, ln, wg, wu, wd):
    t, d = x.shape
    ff = wg.shape[1]
    tm = _row_tile(t, (1024, 512, 256, 128, 64, 32, 16, 8))
    tf = 256 if ff % 256 == 0 else LANES
    row = lambda w: pl.BlockSpec((tm, w), lambda i, f: (i, 0))
    return pl.pallas_call(
        _even_out_kernel,
        grid=(t // tm, ff // tf),
        in_specs=[row(d), row(oa.shape[1]), row(ob.shape[1]), _full(woa.shape), _full(wob.shape), _full((1, d)),
                  pl.BlockSpec((d, tf), lambda i, f: (0, f)), pl.BlockSpec((d, tf), lambda i, f: (0, f)),
                  pl.BlockSpec((tf, d), lambda i, f: (f, 0))],
        out_specs=row(d),
        out_shape=jax.ShapeDtypeStruct((t, d), F32),
        scratch_shapes=[pltpu.VMEM((tm, d), F32), pltpu.VMEM((tm, d), BF16), pltpu.VMEM((tm, d), F32)],
        compiler_params=_cp(("parallel", "arbitrary")),
        name="even_out",
    )(x, oa, ob, woa, wob, ln, wg, wu, wd)


def _rope_lanes(v, c, s):
    n = v.shape[1]
    lane = lax.broadcasted_iota(I32, (1, n), 1) % LANES
    sw = jnp.where(lane < MLA_ROPE // 2, pltpu.roll(v, n - MLA_ROPE // 2, axis=1), pltpu.roll(v, MLA_ROPE // 2, axis=1))
    return v * c + sw * s


def _odd_in_kernel(x_ref, ln_ref, wxb_ref, wgb_ref, wcq_ref, wckv_ref, wkpe_ref, qn_ref, kvn_ref, wuq_ref,
                   cos_ref, sin_ref, *rest, absorbed):
    if absorbed:
        wabs_ref, xb_ref, gb_ref, q_ref, qlat_ref, rows_ref = rest
    else:
        wuk_ref, wuv_ref, xb_ref, gb_ref, q_ref, k_ref, v_ref, rows_ref = rest
    h = _rms(x_ref[...], ln_ref[...]).astype(BF16)
    xb_ref[...] = _mm(h, wxb_ref[...])
    gb_ref[...] = _mm(h, wgb_ref[...])
    cq = _mm(h, wcq_ref[...])
    ckv = _mm(h, wckv_ref[...])
    kpe = _mm(h, wkpe_ref[...])
    cqn = _rms(cq, qn_ref[...]).astype(BF16)
    ckvn = _rms(ckv, kvn_ref[...])
    c128 = cos_ref[...]
    s128 = sin_ref[...]
    nh = MLA_HEADS
    q = _rope_lanes(_mm(cqn, wuq_ref[...]), jnp.concatenate([c128] * nh, axis=1), jnp.concatenate([s128] * nh, axis=1))
    kpe_r = _rope_lanes(kpe, c128, s128)
    qb = q.astype(BF16)
    q_ref[...] = qb
    rows_ref[:, 0:MLA_KV_RANK] = ckvn
    rows_ref[:, MLA_KV_RANK:MLA_ROW] = kpe_r[:, 0:MLA_ROPE]
    if absorbed:
        qlat_ref[...] = _mm(qb, wabs_ref[...])
    else:
        ckvb = ckvn.astype(BF16)
        k_ref[...] = (_mm(ckvb, wuk_ref[...]) + jnp.concatenate([kpe_r] * nh, axis=1)).astype(BF16)
        v_ref[...] = _mm(ckvb, wuv_ref[...]).astype(BF16)


def odd_in(x, ln, ws, qn, kvn, wuq, cos_t, sin_t, extra, absorbed):
    t, d = x.shape
    tm = _row_tile(t, (512, 256, 128, 64, 32, 16, 8))
    nblk = cos_t.shape[0] // tm
    hb = MLA_HEADS * LANES
    row = lambda w: pl.BlockSpec((tm, w), lambda i: (i, 0))
    tbl = pl.BlockSpec((tm, LANES), lambda i: (i % nblk, 0))
    if absorbed:
        outs = [(LRU_WIDTH, F32), (LRU_WIDTH, F32), (hb, BF16), (extra[0].shape[1], F32), (MLA_ROW, F32)]
    else:
        outs = [(LRU_WIDTH, F32), (LRU_WIDTH, F32), (hb, BF16), (hb, BF16), (hb, BF16), (MLA_ROW, F32)]
    return pl.pallas_call(
        functools.partial(_odd_in_kernel, absorbed=absorbed),
        grid=(t // tm,),
        in_specs=[row(d), _full((1, d))] + [_full(w.shape) for w in ws]
        + [_full(qn.shape), _full(kvn.shape), _full(wuq.shape), tbl, tbl] + [_full(w.shape) for w in extra],
        out_specs=[row(w) for w, _ in outs],
        out_shape=[jax.ShapeDtypeStruct((t, w), dt) for w, dt in outs],
        compiler_params=_cp(("parallel",)),
        name="odd_in",
    )(x, ln, *ws, qn, kvn, wuq, cos_t, sin_t, *extra)


def _expm1(x):
    u = jnp.exp(x)
    um1 = u - 1.0
    small = um1 * x / jnp.log(u)
    return jnp.where(um1 == 0.0, x, jnp.where(jnp.abs(x) < 0.5, small, um1))


def _lru_gates(xc, wa_ref, wx_ref, ba_ref, bx_ref, lam_ref):
    xcb = xc.astype(BF16)
    r = jax.nn.sigmoid(_mm(xcb, wa_ref[...]) + ba_ref[...])
    i = jax.nn.sigmoid(_mm(xcb, wx_ref[...]) + bx_ref[...])
    log_a = -LRU_C * r * jax.nn.softplus(-lam_ref[...])
    a = jnp.exp(log_a)
    b = jnp.sqrt(-_expm1(2.0 * log_a)) * (i * xc)
    return a, b


def _lru_prompt_kernel(x_ref, g_ref, cprev_ref, h0_ref, cw_ref, cb_ref, wa_ref, wx_ref, ba_ref, bx_ref, lam_ref,
                       y_ref, hl_ref, xbuf, h_sc):
    t = pl.program_id(1)
    tl = x_ref.shape[1]

    @pl.when(t == 0)
    def _():
        xbuf[0:SUBLANES, :] = cprev_ref[0]
        h_sc[0:1, :] = h0_ref[0]

    xbuf[SUBLANES:SUBLANES + tl, :] = x_ref[0]
    cw = cw_ref[...]
    xc = xbuf[5:5 + tl, :] * cw[0:1, :]
    for j in range(1, cw.shape[0]):
        xc = xc + xbuf[5 + j:5 + j + tl, :] * cw[j:j + 1, :]
    xc = xc + cb_ref[...]
    xbuf[0:SUBLANES, :] = xbuf[tl:tl + SUBLANES, :]
    a, b = _lru_gates(xc, wa_ref, wx_ref, ba_ref, bx_ref, lam_ref)
    row = lax.broadcasted_iota(I32, (tl, 1), 0)
    s = 1
    while s < tl:
        m = row >= s
        b = jnp.where(m, a * pltpu.roll(b, s, axis=0) + b, b)
        a = jnp.where(m, a * pltpu.roll(a, s, axis=0), a)
        s *= 2
    hs = a * h_sc[0:1, :] + b
    h_sc[0:1, :] = hs[tl - 1:tl, :]
    y_ref[0] = hs * jax.nn.gelu(g_ref[0])
    hl_ref[0] = hs[tl - 1:tl, :]


def lru_prompt(xb, gb, cprev, h0, cw, cb, wa, wx, ba, bx, lam, tl):
    n, l, c = xb.shape
    seq = pl.BlockSpec((1, tl, c), lambda i, t: (i, t, 0))
    return pl.pallas_call(
        _lru_prompt_kernel,
        grid=(n, l // tl),
        in_specs=[seq, seq, pl.BlockSpec((1, SUBLANES, c), lambda i, t: (i, 0, 0)),
                  pl.BlockSpec((1, 1, c), lambda i, t: (i, 0, 0)),
                  _full(cw.shape), _full((1, c)), _full(wa.shape), _full(wx.shape), _full((1, c)), _full((1, c)),
                  _full((1, c))],
        out_specs=[seq, pl.BlockSpec((1, 1, c), lambda i, t: (i, 0, 0))],
        out_shape=[jax.ShapeDtypeStruct((n, l, c), F32), jax.ShapeDtypeStruct((n, 1, c), F32)],
        scratch_shapes=[pltpu.VMEM((tl + SUBLANES, c), F32), pltpu.VMEM((SUBLANES, c), F32)],
        compiler_params=_cp(("parallel", "arbitrary")),
        name="lru_prompt",
    )(xb, gb, cprev, h0, cw, cb, wa, wx, ba, bx, lam)


def _lru_sample_kernel(x_ref, g_ref, cprev_ref, h0_ref, cw_ref, cb_ref, wa_ref, wx_ref, ba_ref, bx_ref, lam_ref,
                       y_ref, hl_ref):
    l = x_ref.shape[0]
    cw = cw_ref[...]
    nw = cw.shape[0]
    xx = [cprev_ref[j] for j in range(nw - 1)] + [x_ref[j] for j in range(l)]
    h = h0_ref[...]
    for i in range(l):
        xc = xx[i] * cw[0:1, :]
        for j in range(1, nw):
            xc = xc + xx[i + j] * cw[j:j + 1, :]
        xc = xc + cb_ref[...]
        a, b = _lru_gates(xc, wa_ref, wx_ref, ba_ref, bx_ref, lam_ref)
        h = a * h + b
        y_ref[i] = h * jax.nn.gelu(g_ref[i])
    hl_ref[...] = h


def lru_sample(xb, gb, cprev, h0, cw, cb, wa, wx, ba, bx, lam):
    l, n, c = xb.shape
    return pl.pallas_call(
        _lru_sample_kernel,
        out_shape=[jax.ShapeDtypeStruct((l, n, c), F32), jax.ShapeDtypeStruct((n, c), F32)],
        compiler_params=pltpu.CompilerParams(vmem_limit_bytes=VMEM_LIMIT),
        name="lru_sample",
    )(xb, gb, cprev, h0, cw, cb, wa, wx, ba, bx, lam)


def _mla_prompt_kernel(qi_ref, ki_ref, q_ref, k_ref, v_ref, o_ref, m_sc, l_sc, acc_sc, *, scale):
    p_id = pl.program_id(2)
    qi = qi_ref[p_id]
    ki = ki_ref[p_id]
    tq = q_ref.shape[0]
    tk = k_ref.shape[0]

    @pl.when(ki == 0)
    def _():
        m_sc[...] = jnp.full_like(m_sc, -jnp.inf)
        l_sc[...] = jnp.zeros_like(l_sc)
        acc_sc[...] = jnp.zeros_like(acc_sc)

    s = _mm(q_ref[...], k_ref[...], NT) * scale
    qpos = qi * tq + lax.broadcasted_iota(I32, (tq, tk), 0)
    kpos = ki * tk + lax.broadcasted_iota(I32, (tq, tk), 1)
    s = jnp.where(kpos <= qpos, s, -jnp.inf)
    m_prev = m_sc[...]
    m_new = jnp.maximum(m_prev, jnp.max(s, axis=-1, keepdims=True))
    alpha = jnp.exp(m_prev - m_new)
    p = jnp.exp(s - m_new)
    l_sc[...] = alpha * l_sc[...] + jnp.sum(p, axis=-1, keepdims=True)
    acc_sc[...] = alpha * acc_sc[...] + _mm(p.astype(BF16), v_ref[...])
    m_sc[...] = m_new

    @pl.when(ki == qi)
    def _():
        o_ref[...] = (acc_sc[...] / l_sc[...]).astype(o_ref.dtype)


def mla_prompt(q, k, v, n, l, tq):
    nq = l // tq
    pairs = [(a, b) for a in range(nq) for b in range(a + 1)]
    qi_t = jnp.asarray([a for a, _ in pairs], I32)
    ki_t = jnp.asarray([b for _, b in pairs], I32)
    scale = (MLA_NOPE + MLA_ROPE) ** -0.5
    qmap = lambda i, h, p, qi, ki: (i * nq + qi[p], h)
    kmap = lambda i, h, p, qi, ki: (i * nq + ki[p], h)
    return pl.pallas_call(
        functools.partial(_mla_prompt_kernel, scale=scale),
        grid_spec=pltpu.PrefetchScalarGridSpec(
            num_scalar_prefetch=2,
            grid=(n, MLA_HEADS, len(pairs)),
            in_specs=[pl.BlockSpec((tq, LANES), qmap), pl.BlockSpec((tq, LANES), kmap), pl.BlockSpec((tq, LANES), kmap)],
            out_specs=pl.BlockSpec((tq, LANES), qmap),
            scratch_shapes=[pltpu.VMEM((tq, 1), F32), pltpu.VMEM((tq, 1), F32), pltpu.VMEM((tq, LANES), F32)],
        ),
        out_shape=jax.ShapeDtypeStruct(q.shape, BF16),
        compiler_params=_cp(("parallel", "parallel", "arbitrary")),
        name="mla_prompt",
    )(qi_t, ki_t, q, k, v)


def _mla_sample_kernel(pt_ref, q_ref, rows_ref, *rest, n_pages_step, l, scale):
    page_refs = rest[:n_pages_step]
    o_ref, m_sc, l_sc, acc_sc = rest[n_pages_step:]
    j = pl.program_id(1)

    @pl.when(j == 0)
    def _():
        m_sc[...] = jnp.full_like(m_sc, -jnp.inf)
        l_sc[...] = jnp.zeros_like(l_sc)
        acc_sc[...] = jnp.zeros_like(acc_sc)

    q = q_ref[0]
    pages = [r[0].astype(BF16) for r in page_refs]
    s = jnp.concatenate([_mm(q, pg, NT) for pg in pages], axis=1) * scale
    m_prev = m_sc[...]
    m_new = jnp.maximum(m_prev, jnp.max(s, axis=-1, keepdims=True))
    alpha = jnp.exp(m_prev - m_new)
    p32 = jnp.exp(s - m_new)
    p = p32.astype(BF16)
    ps = PAGE_SIZE
    pv = _mm(p[:, 0:ps], pages[0])
    for i in range(1, n_pages_step):
        pv = pv + _mm(p[:, i * ps:(i + 1) * ps], pages[i])
    l_sc[...] = alpha * l_sc[...] + jnp.sum(p32, axis=-1, keepdims=True)
    acc_sc[...] = alpha * acc_sc[...] + pv
    m_sc[...] = m_new

    @pl.when(j == pl.num_programs(1) - 1)
    def _():
        qf = q.astype(F32)
        rows = rows_ref[0]
        tok = lax.broadcasted_iota(I32, (qf.shape[0], 1), 0) // MLA_HEADS
        sn = []
        for mm in range(l):
            sm = jnp.sum(qf * rows[mm:mm + 1, :], axis=-1, keepdims=True) * scale
            sn.append(jnp.where(tok >= mm, sm, -jnp.inf))
        m_old = m_sc[...]
        m_fin = m_old
        for sm in sn:
            m_fin = jnp.maximum(m_fin, sm)
        al = jnp.exp(m_old - m_fin)
        lsum = al * l_sc[...]
        acc = al * acc_sc[...]
        for mm in range(l):
            pm = jnp.exp(sn[mm] - m_fin)
            lsum = lsum + pm
            acc = acc + pm * rows[mm:mm + 1, :]
        o_ref[0] = acc / lsum


def mla_sample(page_table, q_cat, rows, cache, n_pages_step):
    n, r, c = q_cat.shape
    l = rows.shape[1]
    n_pages = page_table.shape[1]
    steps = n_pages // n_pages_step
    scale = (MLA_NOPE + MLA_ROPE) ** -0.5
    pt = page_table.reshape(-1)

    def page_map(k):
        return lambda i, j, pt_ref: (pt_ref[i * n_pages + j * n_pages_step + k], 0, 0)

    return pl.pallas_call(
        functools.partial(_mla_sample_kernel, n_pages_step=n_pages_step, l=l, scale=scale),
        grid_spec=pltpu.PrefetchScalarGridSpec(
            num_scalar_prefetch=1,
            grid=(n, steps),
            in_specs=[pl.BlockSpec((1, r, c), lambda i, j, pt_ref: (i, 0, 0)),
                      pl.BlockSpec((1, l, c), lambda i, j, pt_ref: (i, 0, 0))]
            + [pl.BlockSpec((1, PAGE_SIZE, c), page_map(k)) for k in range(n_pages_step)],
            out_specs=pl.BlockSpec((1, r, c), lambda i, j, pt_ref: (i, 0, 0)),
            scratch_shapes=[pltpu.VMEM((r, 1), F32), pltpu.VMEM((r, 1), F32), pltpu.VMEM((r, c), F32)],
        ),
        out_shape=jax.ShapeDtypeStruct((n, r, c), F32),
        compiler_params=_cp(("parallel", "arbitrary")),
        name="mla_sample",
    )(pt, q_cat, rows, *([cache] * n_pages_step))


def _matmul_kernel(x_ref, w_ref, o_ref):
    o_ref[...] = _mm(x_ref[...].astype(BF16), w_ref[...]).astype(o_ref.dtype)


def matmul(x, w, out_dtype):
    t, kd = x.shape
    tm = _row_tile(t, (512, 256, 128, 64, 32, 16, 8))
    return pl.pallas_call(
        _matmul_kernel,
        grid=(t // tm,),
        in_specs=[pl.BlockSpec((tm, kd), lambda i: (i, 0)), _full(w.shape)],
        out_specs=pl.BlockSpec((tm, w.shape[1]), lambda i: (i, 0)),
        out_shape=jax.ShapeDtypeStruct((t, w.shape[1]), out_dtype),
        compiler_params=_cp(("parallel",)),
        name="matmul",
    )(x, w)


def _odd_out_kernel(x_ref, yc_ref, od_ref, woc_ref, wod_ref, ln_ref, rw_ref, rb_ref,
                    x1_ref, hn_ref, idx_ref, gate_ref):
    x1 = x_ref[...] + (_mm(yc_ref[...].astype(BF16), woc_ref[...]) + _mm(od_ref[...], wod_ref[...]))
    x1_ref[...] = x1
    hn = _rms(x1, ln_ref[...])
    hn_ref[...] = hn
    logits = _mm(hn, rw_ref[...], NN, HI) + rb_ref[...]
    lane = lax.broadcasted_iota(I32, logits.shape, 1)
    logits = jnp.where(lane < N_EXPERTS, logits, -jnp.inf)
    m1 = jnp.max(logits, axis=-1, keepdims=True)
    i1 = jnp.min(jnp.where(logits == m1, lane, LANES), axis=-1, keepdims=True)
    l2 = jnp.where(lane == i1, -jnp.inf, logits)
    m2 = jnp.max(l2, axis=-1, keepdims=True)
    i2 = jnp.min(jnp.where(l2 == m2, lane, LANES), axis=-1, keepdims=True)
    e2 = jnp.exp(m2 - m1)
    den = 1.0 + e2
    idx_ref[...] = jnp.where(lane == 0, i1, jnp.where(lane == 1, i2, 0))
    gate_ref[...] = jnp.where(lane == 0, 1.0 / den, jnp.where(lane == 1, e2 / den, 0.0))


def odd_out(x, yc, od, woc, wod, ln, rw, rb):
    t, d = x.shape
    tm = _row_tile(t, (512, 256, 128, 64, 32, 16, 8))
    row = lambda w: pl.BlockSpec((tm, w), lambda i: (i, 0))
    return pl.pallas_call(
        _odd_out_kernel,
        grid=(t // tm,),
        in_specs=[row(d), row(yc.shape[1]), row(od.shape[1]), _full(woc.shape), _full(wod.shape), _full((1, d)),
                  _full(rw.shape), _full(rb.shape)],
        out_specs=[row(d), row(d), row(LANES), row(LANES)],
        out_shape=[jax.ShapeDtypeStruct((t, d), F32), jax.ShapeDtypeStruct((t, d), F32),
                   jax.ShapeDtypeStruct((t, LANES), I32), jax.ShapeDtypeStruct((t, LANES), F32)],
        compiler_params=_cp(("parallel",)),
        name="odd_out",
    )(x, yc, od, woc, wod, ln, rw, rb)


ROWS_PER_STEP = 512


def _row_copy(src_hbm, dst_hbm, s, d, sem):
    return pltpu.make_async_copy(src_hbm.at[pl.ds(s, 1)], dst_hbm.at[pl.ds(d, 1)], sem)


def _row_move_kernel(si_ref, di_ref, src_hbm, *rest, rows):
    dst_hbm, sem = rest[-2:]

    def issue(j, c):
        _row_copy(src_hbm, dst_hbm, si_ref[j], di_ref[j], sem).start()
        return c

    lax.fori_loop(0, rows, issue, 0)

    def drain(j, c):
        _row_copy(src_hbm, dst_hbm, 0, 0, sem).wait()
        return c

    lax.fori_loop(0, rows, drain, 0)


def row_move(src_idx, dst_idx, src, dst=None, n_dst=None):
    n = src_idx.shape[0]
    rows = math.gcd(n, ROWS_PER_STEP)
    idx_spec = pl.BlockSpec((rows,), lambda i: (i,), memory_space=pltpu.SMEM)
    hbm = pl.BlockSpec(memory_space=pl.ANY)
    args = (src_idx, dst_idx, src) + (() if dst is None else (dst,))
    return pl.pallas_call(
        functools.partial(_row_move_kernel, rows=rows),
        grid=(n // rows,),
        in_specs=[idx_spec, idx_spec, hbm] + ([] if dst is None else [hbm]),
        out_specs=hbm,
        out_shape=jax.ShapeDtypeStruct((n_dst, src.shape[1]) if dst is None else dst.shape, src.dtype),
        scratch_shapes=[pltpu.SemaphoreType.DMA(())],
        input_output_aliases={} if dst is None else {3: 0},
        compiler_params=pltpu.CompilerParams(dimension_semantics=("arbitrary",)),
        name="row_move",
    )(*args)


EXPERT_ROWS = 512
EXPERT_FF_CHUNK = 512


def _expert_kernel(be_ref, nu_ref, x_ref, wg_ref, wu_ref, wd_ref, o_ref):
    i = pl.program_id(0)

    @pl.when(i < nu_ref[0])
    def _():
        x = x_ref[...].astype(BF16)
        acc = None
        for f in range(0, wg_ref.shape[2], EXPERT_FF_CHUNK):
            g = _mm(x, wg_ref[0, :, f:f + EXPERT_FF_CHUNK])
            u = _mm(x, wu_ref[0, :, f:f + EXPERT_FF_CHUNK])
            y = _mm((_silu(g) * u).astype(BF16), wd_ref[0, f:f + EXPERT_FF_CHUNK, :])
            acc = y if acc is None else acc + y
        o_ref[...] = acc

    @pl.when(i >= nu_ref[0])
    def _():
        o_ref[...] = jnp.zeros_like(o_ref)


def experts(blk_e, n_used, xs, wg, wu, wd):
    ns, d = xs.shape
    ff = wg.shape[2]
    wmap = lambda i, be, nu: (be[i], 0, 0)
    return pl.pallas_call(
        _expert_kernel,
        grid_spec=pltpu.PrefetchScalarGridSpec(
            num_scalar_prefetch=2,
            grid=(ns // EXPERT_ROWS,),
            in_specs=[pl.BlockSpec((EXPERT_ROWS, d), lambda i, be, nu: (i, 0)),
                      pl.BlockSpec((1, d, ff), wmap), pl.BlockSpec((1, d, ff), wmap), pl.BlockSpec((1, ff, d), wmap)],
            out_specs=pl.BlockSpec((EXPERT_ROWS, d), lambda i, be, nu: (i, 0)),
        ),
        out_shape=jax.ShapeDtypeStruct((ns, d), F32),
        compiler_params=_cp(("arbitrary",)),
        name="experts",
    )(blk_e, n_used, xs, wg, wu, wd)


def _final_kernel(x1_ref, ya0_ref, ya1_ref, gate_ref, fn_ref, o_ref):
    g = gate_ref[...]
    x2 = x1_ref[...] + (ya0_ref[...] * g[:, 0:1] + ya1_ref[...] * g[:, 1:2])
    o_ref[...] = _rms(x2, fn_ref[...])


def final_combine(x1, ya0, ya1, gates, fnorm):
    t, d = x1.shape
    tm = _row_tile(t, (512, 256, 128, 64, 32, 16, 8))
    row = lambda w: pl.BlockSpec((tm, w), lambda i: (i, 0))
    return pl.pallas_call(
        _final_kernel,
        grid=(t // tm,),
        in_specs=[row(d), row(d), row(d), row(LANES), _full((1, d))],
        out_specs=row(d),
        out_shape=jax.ShapeDtypeStruct((t, d), F32),
        compiler_params=_cp(("parallel",)),
        name="final_combine",
    )(x1, ya0, ya1, gates, fnorm)


def _rope_tables(pos):
    half = MLA_ROPE // 2
    inv = ROPE_THETA ** (-jnp.arange(half, dtype=F32) * 2.0 / MLA_ROPE)
    ang = pos.astype(F32)[:, None] * inv[None, :]
    cos = jnp.cos(ang)
    sin = jnp.sin(ang)
    n = pos.shape[0]
    cos_t = jnp.concatenate([cos, cos, jnp.ones((n, LANES - MLA_ROPE), F32)], axis=1)
    sin_t = jnp.concatenate([-sin, sin, jnp.zeros((n, LANES - MLA_ROPE), F32)], axis=1)
    return cos_t, sin_t


def _block_diag(w):
    b, i, j = w.shape
    eye = jnp.eye(b, dtype=w.dtype)
    return (eye[:, None, :, None] * w[:, :, None, :]).reshape(b * i, b * j)


def _pad_rows8(a, l):
    return jnp.pad(a, ((0, 0), (0, SUBLANES - l), (0, 0)))


def _moe_slots(e_all):
    flat = e_all.reshape(-1)
    onehot = (flat[:, None] == jnp.arange(N_EXPERTS, dtype=I32)[None, :]).astype(I32)
    csum = jnp.cumsum(onehot, axis=0)
    rank = jnp.sum((csum - onehot) * onehot, axis=1)
    counts = csum[-1]
    padded = (counts + EXPERT_ROWS - 1) // EXPERT_ROWS * EXPERT_ROWS
    pad_end = jnp.cumsum(padded)
    pad_start = pad_end - padded
    slot = jnp.sum(onehot * pad_start[None, :], axis=1) + rank
    n_blocks = -(-flat.shape[0] // EXPERT_ROWS) + N_EXPERTS
    blk_start = jnp.arange(n_blocks, dtype=I32) * EXPERT_ROWS
    blk_e = jnp.minimum(jnp.sum((blk_start[:, None] >= pad_end[None, :]).astype(I32), axis=1), N_EXPERTS - 1)
    n_used = (pad_end[-1] // EXPERT_ROWS).astype(I32).reshape(1)
    return slot.astype(I32), blk_e.astype(I32), n_used, n_blocks


def kernel(x_prompt, x_sample, state_swa_kv, state_gdn_conv, state_gdn_s, state_lru_conv, state_lru_h, cache_mla, page_table, e_ln_mix, e_w_in, e_gdn_conv_w, e_gdn_a_log, e_gdn_dt_bias, e_gdn_norm, e_swa_sinks, e_w_out, e_ln_ffn, e_ffn_gate, e_ffn_up, e_ffn_down, o_ln_mix, o_w_in, o_lru_conv_w, o_lru_conv_b, o_lru_w_a, o_lru_b_a, o_lru_w_x, o_lru_b_x, o_lru_lambda, o_mla_q_norm, o_mla_w_uq, o_mla_kv_norm, o_mla_w_uk, o_mla_w_uv, o_w_out, o_ln_ffn, o_router_w, o_router_b, o_exp_gate, o_exp_up, o_exp_down, final_norm):
    nb, lp, d = x_prompt.shape
    ns, ls, _ = x_sample.shape
    past_len = page_table.shape[1] * PAGE_SIZE
    tp, ts = nb * lp, ns * ls
    xp = x_prompt.reshape(tp, d)
    xs = x_sample.reshape(ts, d)
    row1 = lambda v: v.reshape(1, -1)

    na = (SWA_HEADS + 2 * SWA_KV_HEADS) * SWA_HEAD_DIM
    nz = GDN_HEADS * GDN_DV
    w_in = e_w_in[0].astype(BF16)
    o1, o2, o3 = na, na + GDN_QKV, na + GDN_QKV + nz
    w_groups = [w_in[:, :o1], w_in[:, o1:o2], w_in[:, o2:o3],
                jnp.pad(w_in[:, o3:], ((0, 0), (0, LANES - 2 * GDN_HEADS)))]
    ln = row1(e_ln_mix[0])
    qkv_p, gx_p, z_p, bg_p = norm_proj(xp, ln, w_groups)
    qkv_s, gx_s, z_s, bg_s = norm_proj(xs, ln, w_groups)

    sinks = e_swa_sinks[0]
    oa_p = swa_prompt(qkv_p, sinks, nb, lp)
    wbuf = state_swa_kv.shape[2]
    nkv = 2 * SWA_KV_HEADS * SWA_HEAD_DIM
    buf = state_swa_kv[0].reshape(ns, wbuf, nkv)
    qkv_s3 = qkv_s.reshape(ns, ls, na)
    oa_s = swa_sample(_pad_rows8(qkv_s3, ls), buf, sinks, ls)[:, :ls].reshape(ts, -1)
    kv_p = qkv_p.reshape(nb, lp, na)[:, lp - SWA_WINDOW:, na - nkv:]
    swa_kv_p = kv_p.reshape(1, nb, SWA_WINDOW, 2, SWA_KV_HEADS, SWA_HEAD_DIM)
    kv_s = jnp.concatenate([buf, qkv_s3[:, :, na - nkv:]], axis=1)[:, ls:]
    swa_kv_s = kv_s.reshape(1, ns, wbuf, 2, SWA_KV_HEADS, SWA_HEAD_DIM)

    gx_p3 = gx_p.reshape(nb, lp, GDN_QKV)
    gx_s3 = gx_s.reshape(ns, ls, GDN_QKV)
    gargs = (e_gdn_conv_w[0], e_gdn_a_log[0], e_gdn_dt_bias[0], e_gdn_norm[0])
    ob_p, gs_p = gdn(gx_p3, z_p.reshape(nb, lp, nz), bg_p.reshape(nb, lp, LANES),
                     jnp.zeros((nb, SUBLANES, GDN_QKV), F32), jnp.zeros((nb, GDN_HEADS, GDN_DK, GDN_DV), F32),
                     *gargs, chunk=math.gcd(lp, GDN_CHUNK), valid=math.gcd(lp, GDN_CHUNK), tl=min(lp, 512))
    cprev_s = jnp.pad(state_gdn_conv[0], ((0, 0), (SUBLANES - (GDN_CONV - 1), 0), (0, 0)))
    ob_s, gs_s = gdn(_pad_rows8(gx_s3, ls), _pad_rows8(z_s.reshape(ns, ls, nz), ls),
                     _pad_rows8(bg_s.reshape(ns, ls, LANES), ls), cprev_s, state_gdn_s[0],
                     *gargs, chunk=SUBLANES, valid=ls, tl=SUBLANES)
    ob_s = ob_s[:, :ls].reshape(ts, nz)
    gconv_p = gx_p3[:, lp - (GDN_CONV - 1):][None]
    gconv_s = gx_s3[:, ls - (GDN_CONV - 1):][None]

    w_out = e_w_out[0].astype(BF16)
    nqa = SWA_HEADS * SWA_HEAD_DIM
    ffn = (row1(e_ln_ffn[0]), e_ffn_gate[0].astype(BF16), e_ffn_up[0].astype(BF16), e_ffn_down[0].astype(BF16))
    xp = even_out(xp, oa_p, ob_p.reshape(tp, nz), w_out[:nqa], w_out[nqa:], *ffn)
    xs = even_out(xs, oa_s, ob_s, w_out[:nqa], w_out[nqa:], *ffn)

    w_in = o_w_in[0].astype(BF16)
    c0 = LRU_WIDTH
    c1 = 2 * LRU_WIDTH
    c2 = c1 + MLA_Q_RANK
    c3 = c2 + MLA_KV_RANK
    w_groups = [w_in[:, :c0], w_in[:, c0:c1], w_in[:, c1:c2], w_in[:, c2:c3],
                jnp.pad(w_in[:, c3:], ((0, 0), (0, LANES - MLA_ROPE)))]
    hd_q = MLA_NOPE + MLA_ROPE
    half = MLA_ROPE // 2
    wq = o_mla_w_uq[0].reshape(MLA_Q_RANK, MLA_HEADS, hd_q)
    wuq = jnp.concatenate([wq[:, :, MLA_NOPE:], wq[:, :, :MLA_NOPE],
                           jnp.zeros((MLA_Q_RANK, MLA_HEADS, LANES - hd_q), F32)], axis=2)
    wuq = wuq.reshape(MLA_Q_RANK, MLA_HEADS * LANES).astype(BF16)
    w_uk = o_mla_w_uk[0]
    w_uv = o_mla_w_uv[0]
    wuk = jnp.pad(w_uk, ((0, 0), (0, 0), (MLA_ROPE, LANES - hd_q))).reshape(MLA_KV_RANK, -1).astype(BF16)
    wuv = jnp.pad(w_uv, ((0, 0), (0, 0), (0, LANES - MLA_V))).reshape(MLA_KV_RANK, -1).astype(BF16)
    wabs = _block_diag(jnp.pad(jnp.transpose(w_uk, (1, 2, 0)), ((0, 0), (MLA_ROPE, LANES - hd_q), (0, 0)))).astype(BF16)
    wuv_bd = _block_diag(jnp.pad(jnp.transpose(w_uv, (1, 0, 2)), ((0, 0), (0, 0), (0, LANES - MLA_V)))).astype(BF16)
    ln = row1(o_ln_mix[0])
    qn, kvn = row1(o_mla_q_norm[0]), row1(o_mla_kv_norm[0])
    tm_p = _row_tile(tp, (512, 256, 128, 64, 32, 16, 8))
    assert lp % tm_p == 0 or tm_p % lp == 0
    pos_p = jnp.arange(max(lp, tm_p), dtype=I32) % lp
    pos_s = past_len + (jnp.arange(ts, dtype=I32) % ls)
    cos_p, sin_p = _rope_tables(pos_p)
    cos_s, sin_s = _rope_tables(pos_s)
    xb_p, gb_p, q_p, k_p, v_p, rows_p = odd_in(xp, ln, w_groups, qn, kvn, wuq, cos_p, sin_p, [wuk, wuv], False)
    xb_s, gb_s, q_s, qlat_s, rows_s = odd_in(xs, ln, w_groups, qn, kvn, wuq, cos_s, sin_s, [wabs], True)

    lru_w = (o_lru_conv_w[0], row1(o_lru_conv_b[0]), _block_diag(o_lru_w_a[0]).astype(BF16),
             _block_diag(o_lru_w_x[0]).astype(BF16), row1(o_lru_b_a[0]), row1(o_lru_b_x[0]), row1(o_lru_lambda[0]))
    xb_p3 = xb_p.reshape(nb, lp, LRU_WIDTH)
    yc_p, lh_p = lru_prompt(xb_p3, gb_p.reshape(nb, lp, LRU_WIDTH), jnp.zeros((nb, SUBLANES, LRU_WIDTH), F32),
                            jnp.zeros((nb, 1, LRU_WIDTH), F32), *lru_w, tl=min(lp, 256))
    xb_s3 = xb_s.reshape(ns, ls, LRU_WIDTH)
    tmaj = lambda a: jnp.transpose(a, (1, 0, 2))
    yc_s, lh_s = lru_sample(tmaj(xb_s3), tmaj(gb_s.reshape(ns, ls, LRU_WIDTH)), tmaj(state_lru_conv[0]),
                            state_lru_h[0], *lru_w)
    yc_s = tmaj(yc_s).reshape(ts, LRU_WIDTH)
    lconv_p = xb_p3[:, lp - 3:][None]
    lconv_s = xb_s3[:, ls - 3:][None]

    od_p = mla_prompt(q_p, k_p, v_p, nb, lp, tq=min(lp, 512))
    q_pe = q_s.reshape(ts, MLA_HEADS, LANES)[:, :, :MLA_ROPE]
    q_cat = jnp.concatenate([qlat_s.reshape(ts, MLA_HEADS, MLA_KV_RANK).astype(BF16), q_pe], axis=-1)
    q_cat = q_cat.reshape(ns, ls * MLA_HEADS, MLA_ROW)
    n_pages = page_table.shape[1]
    o_lat = mla_sample(page_table, q_cat, rows_s.reshape(ns, ls, MLA_ROW), cache_mla[0],
                       n_pages_step=math.gcd(n_pages, 16))
    o_lat = o_lat.reshape(ts, MLA_HEADS, MLA_ROW)[:, :, :MLA_KV_RANK].reshape(ts, MLA_HEADS * MLA_KV_RANK)
    od_s = matmul(o_lat, wuv_bd, BF16)

    w_out = o_w_out[0].astype(BF16)
    wod = jnp.pad(w_out[LRU_WIDTH:].reshape(MLA_HEADS, MLA_V, d), ((0, 0), (0, LANES - MLA_V), (0, 0)))
    wod = wod.reshape(MLA_HEADS * LANES, d)
    rw = jnp.pad(o_router_w[0], ((0, 0), (0, LANES - N_EXPERTS)))
    rb = jnp.pad(o_router_b[0], (0, LANES - N_EXPERTS)).reshape(1, LANES)
    lnf = row1(o_ln_ffn[0])
    x1_p, hn_p, idx_p, gate_p = odd_out(xp, yc_p.reshape(tp, LRU_WIDTH), od_p, w_out[:LRU_WIDTH], wod, lnf, rw, rb)
    x1_s, hn_s, idx_s, gate_s = odd_out(xs, yc_s, od_s, w_out[:LRU_WIDTH], wod, lnf, rw, rb)

    e_all = jnp.concatenate([idx_p[:, :TOP_K], idx_s[:, :TOP_K]], axis=0)
    slot, blk_e, n_used, n_blocks = _moe_slots(e_all)
    slot = slot.reshape(tp + ts, TOP_K)
    slot_p, slot_s = slot[:tp], slot[tp:]
    tok_p = jnp.arange(tp, dtype=I32)
    tok_s = jnp.arange(ts, dtype=I32)
    xsort = jnp.zeros((n_blocks * EXPERT_ROWS, d), F32)
    for k in range(TOP_K):
        xsort = row_move(tok_p, slot_p[:, k], hn_p, xsort)
        xsort = row_move(tok_s, slot_s[:, k], hn_s, xsort)
    yb = experts(blk_e, n_used, xsort, o_exp_gate[0].astype(BF16), o_exp_up[0].astype(BF16),
                 o_exp_down[0].astype(BF16))
    ya_p = [row_move(slot_p[:, k], tok_p, yb, n_dst=tp) for k in range(TOP_K)]
    ya_s = [row_move(slot_s[:, k], tok_s, yb, n_dst=ts) for k in range(TOP_K)]
    fnw = row1(final_norm)
    y_p = final_combine(x1_p, *ya_p, gate_p, fnw).reshape(nb, lp, d)
    y_s = final_combine(x1_s, *ya_s, gate_s, fnw).reshape(ns, ls, d)

    return (y_p, y_s, swa_kv_p, swa_kv_s, gconv_p, gconv_s, gs_p[None], gs_s[None],
            lconv_p, lconv_s, lh_p.reshape(1, nb, LRU_WIDTH), lh_s[None],
            rows_p.reshape(1, nb, lp // PAGE_SIZE, PAGE_SIZE, MLA_ROW), rows_s.reshape(1, ns, ls, MLA_ROW))
```

```python
import functools
import math

import jax
import jax.numpy as jnp
from jax import lax
from jax.experimental import pallas as pl
from jax.experimental.pallas import tpu as pltpu

F32 = jnp.float32
BF16 = jnp.bfloat16
I32 = jnp.int32
HI = lax.Precision.HIGHEST

D_MODEL = 1024
PAGE_SIZE = 128
SWA_WINDOW = 128
SWA_HEADS = 8
SWA_KV_HEADS = 2
SWA_GROUP = SWA_HEADS // SWA_KV_HEADS
SWA_HEAD_DIM = 64
GDN_HEADS = 4
GDN_DK = 128
GDN_DV = 128
GDN_CONV = 4
GDN_CHUNK = 64
GDN_QKV = GDN_HEADS * (2 * GDN_DK + GDN_DV)
LRU_WIDTH = 512
LRU_BLOCKS = 8
LRU_BLOCK_W = LRU_WIDTH // LRU_BLOCKS
LRU_C = 8.0
MLA_HEADS = 8
MLA_Q_RANK = 384
MLA_KV_RANK = 256
MLA_NOPE = 64
MLA_ROPE = 32
MLA_V = 64
MLA_ROW = MLA_KV_RANK + MLA_ROPE
ROPE_THETA = 10000.0
D_FF = 2816
N_EXPERTS = 8
TOP_K = 2
MOE_FF = 2048
NORM_EPS = 1e-6

LANES = 128
SUBLANES = 8
VMEM_LIMIT = 56 << 20

NN = (((1,), (0,)), ((), ()))
NT = (((1,), (1,)), ((), ()))
TN = (((0,), (0,)), ((), ()))


def _mm(a, b, dims=NN, precision=None):
    return lax.dot_general(a, b, dims, precision=precision, preferred_element_type=F32)


def _cp(sem):
    return pltpu.CompilerParams(dimension_semantics=sem, vmem_limit_bytes=VMEM_LIMIT)


def _rms(x, w):
    return x * lax.rsqrt(jnp.mean(x * x, axis=-1, keepdims=True) + NORM_EPS) * w


def _silu(x):
    return x * jax.nn.sigmoid(x)


def _full(shape):
    nd = len(shape)
    return pl.BlockSpec(shape, lambda *a: (0,) * nd)


def _row_tile(t, pref):
    for c in pref:
        if t % c == 0:
            return c
    return t


def _norm_proj_kernel(x_ref, ln_ref, *refs):
    n = len(refs) // 2
    h = _rms(x_ref[...], ln_ref[...]).astype(BF16)
    for w_ref, o_ref in zip(refs[:n], refs[n:]):
        o_ref[...] = _mm(h, w_ref[...])


def norm_proj(x, ln, ws):
    t, d = x.shape
    tm = _row_tile(t, (512, 256, 128, 64, 32, 16, 8))
    return pl.pallas_call(
        _norm_proj_kernel,
        grid=(t // tm,),
        in_specs=[pl.BlockSpec((tm, d), lambda i: (i, 0)), _full((1, d))] + [_full(w.shape) for w in ws],
        out_specs=[pl.BlockSpec((tm, w.shape[1]), lambda i: (i, 0)) for w in ws],
        out_shape=[jax.ShapeDtypeStruct((t, w.shape[1]), F32) for w in ws],
        compiler_params=_cp(("parallel",)),
        name="norm_proj",
    )(x, ln, *ws)


def _swa_softmax_pv(parts, sink):
    m = sink
    for s, _ in parts:
        m = jnp.maximum(m, jnp.max(s, axis=-1, keepdims=True))
    den = jnp.exp(sink - m)
    o = None
    for s, v in parts:
        p = jnp.exp(s - m)
        den = den + jnp.sum(p, axis=-1, keepdims=True)
        pv = _mm(p.astype(BF16), v)
        o = pv if o is None else o + pv
    return o / den


def _swa_prompt_kernel(sink_ref, q_ref, kvc_ref, kvp_ref, o_ref):
    b = pl.program_id(1)
    w = SWA_WINDOW
    hd = SWA_HEAD_DIM
    q = q_ref[...]
    kvc = kvc_ref[...].astype(BF16)
    kvp = kvp_ref[...].astype(BF16)
    qi = lax.broadcasted_iota(I32, (w, w), 0)
    kj = lax.broadcasted_iota(I32, (w, w), 1)
    dist_c = (qi - kj).astype(F32)
    dist_p = dist_c + float(w)
    valid_c = kj <= qi
    valid_p = jnp.logical_and(kj > qi, b > 0)
    outs = []
    for h in range(SWA_HEADS):
        j = h // SWA_GROUP
        slope = 2.0 ** (-8.0 * (h + 1) / SWA_HEADS)
        qh = q[:, h * hd:(h + 1) * hd].astype(BF16)
        kc = kvc[:, j * hd:(j + 1) * hd]
        kp = kvp[:, j * hd:(j + 1) * hd]
        vc = kvc[:, (SWA_KV_HEADS + j) * hd:(SWA_KV_HEADS + j + 1) * hd]
        vp = kvp[:, (SWA_KV_HEADS + j) * hd:(SWA_KV_HEADS + j + 1) * hd]
        s_c = _mm(qh, kc, NT) * (hd ** -0.5)
        s_p = _mm(qh, kp, NT) * (hd ** -0.5)
        s_c = jnp.where(valid_c, s_c - slope * dist_c, -jnp.inf)
        s_p = jnp.where(valid_p, s_p - slope * dist_p, -jnp.inf)
        outs.append(_swa_softmax_pv([(s_p, vp), (s_c, vc)], sink_ref[h]))
    o_ref[...] = jnp.concatenate(outs, axis=1)


def swa_prompt(qkv, sinks, n, l):
    w = SWA_WINDOW
    nb = l // w
    nq = SWA_HEADS * SWA_HEAD_DIM
    nkv = 2 * SWA_KV_HEADS * SWA_HEAD_DIM
    return pl.pallas_call(
        _swa_prompt_kernel,
        grid=(n, nb),
        in_specs=[
            pl.BlockSpec(memory_space=pltpu.SMEM),
            pl.BlockSpec((w, nq), lambda i, b: (i * nb + b, 0)),
            pl.BlockSpec((w, nkv), lambda i, b: (i * nb + b, nq // nkv)),
            pl.BlockSpec((w, nkv), lambda i, b: (i * nb + jnp.maximum(b - 1, 0), nq // nkv)),
        ],
        out_specs=pl.BlockSpec((w, nq), lambda i, b: (i * nb + b, 0)),
        out_shape=jax.ShapeDtypeStruct((n * l, nq), F32),
        compiler_params=_cp(("parallel", "arbitrary")),
        name="swa_prompt",
    )(sinks, qkv, qkv, qkv)


def _swa_sample_kernel(sink_ref, q_ref, buf_ref, o_ref, *, l):
    hd = SWA_HEAD_DIM
    lp = q_ref.shape[1]
    wb = buf_ref.shape[1]
    x = q_ref[0]
    buf = buf_ref[0].astype(BF16)
    kvn = x[:, SWA_HEADS * hd:].astype(BF16)
    qi_b = lax.broadcasted_iota(I32, (lp, wb), 0)
    kj_b = lax.broadcasted_iota(I32, (lp, wb), 1)
    dist_b = (qi_b + wb - kj_b).astype(F32)
    valid_b = (qi_b + wb - kj_b) < SWA_WINDOW
    qi_n = lax.broadcasted_iota(I32, (lp, lp), 0)
    kj_n = lax.broadcasted_iota(I32, (lp, lp), 1)
    dist_n = (qi_n - kj_n).astype(F32)
    valid_n = jnp.logical_and(kj_n <= qi_n, kj_n < l)
    outs = []
    for h in range(SWA_HEADS):
        j = h // SWA_GROUP
        slope = 2.0 ** (-8.0 * (h + 1) / SWA_HEADS)
        qh = x[:, h * hd:(h + 1) * hd].astype(BF16)
        kb = buf[:, j * hd:(j + 1) * hd]
        vb = buf[:, (SWA_KV_HEADS + j) * hd:(SWA_KV_HEADS + j + 1) * hd]
        kn = kvn[:, j * hd:(j + 1) * hd]
        vn = kvn[:, (SWA_KV_HEADS + j) * hd:(SWA_KV_HEADS + j + 1) * hd]
        s_b = _mm(qh, kb, NT) * (hd ** -0.5)
        s_n = _mm(qh, kn, NT) * (hd ** -0.5)
        s_b = jnp.where(valid_b, s_b - slope * dist_b, -jnp.inf)
        s_n = jnp.where(valid_n, s_n - slope * dist_n, -jnp.inf)
        outs.append(_swa_softmax_pv([(s_b, vb), (s_n, vn)], sink_ref[h]))
    o_ref[0] = jnp.concatenate(outs, axis=1)


def swa_sample(qkv_pad, buf, sinks, l):
    n, lp, c = qkv_pad.shape
    wb = buf.shape[1]
    nq = SWA_HEADS * SWA_HEAD_DIM
    return pl.pallas_call(
        functools.partial(_swa_sample_kernel, l=l),
        grid=(n,),
        in_specs=[
            pl.BlockSpec(memory_space=pltpu.SMEM),
            pl.BlockSpec((1, lp, c), lambda i: (i, 0, 0)),
            pl.BlockSpec((1, wb, buf.shape[2]), lambda i: (i, 0, 0)),
        ],
        out_specs=pl.BlockSpec((1, lp, nq), lambda i: (i, 0, 0)),
        out_shape=jax.ShapeDtypeStruct((n, lp, nq), F32),
        compiler_params=_cp(("parallel",)),
        name="swa_sample",
    )(sinks, qkv_pad, buf)


def _mm3(a, b):
    ah = a.astype(BF16)
    al = (a - ah.astype(F32)).astype(BF16)
    bh = b.astype(BF16)
    bl = (b - bh.astype(F32)).astype(BF16)
    return _mm(ah, bh) + (_mm(ah, bl) + _mm(al, bh))


def _block_rows(x, c, nh):
    blk = lax.broadcasted_iota(I32, x.shape, 1) // c
    return jnp.concatenate([jnp.where(blk == i, x, 0.0) for i in range(nh)], axis=0)


def _gdn_kernel(x_ref, z_ref, bg_ref, cprev_ref, s0_ref, cw_ref, alog_ref, dtb_ref, gn_ref,
                o_ref, sfin_ref, xbuf, s_sc, q_sc, k_sc, vb_sc, kb_sc, qd_sc, kd_sc, be_sc, gc_sc, egl_sc,
                *, chunk, valid):
    t = pl.program_id(1)
    tl = x_ref.shape[1]
    c = chunk
    nck = tl // c
    nh = GDN_HEADS
    dk = GDN_DK

    @pl.when(t == 0)
    def _():
        xbuf[0:SUBLANES, :] = cprev_ref[0]
        s_sc[...] = s0_ref[0]

    xbuf[SUBLANES:SUBLANES + tl, :] = x_ref[0]
    cw = cw_ref[...]
    conv = xbuf[5:5 + tl, :] * cw[0:1, :]
    for j in range(1, GDN_CONV):
        conv = conv + xbuf[5 + j:5 + j + tl, :] * cw[j:j + 1, :]
    xbuf[0:SUBLANES, :] = xbuf[tl:tl + SUBLANES, :]
    act = _silu(conv)

    row = lax.broadcasted_iota(I32, (tl, 1), 0)
    rmask = (row % c) < valid
    bg = bg_ref[0]
    beta = jax.nn.sigmoid(bg)
    g = -jnp.exp(alog_ref[...]) * jax.nn.softplus(bg + dtb_ref[...])
    g = jnp.where(rmask, g, 0.0)
    rc = row % c
    s = 1
    while s < c:
        g = g + jnp.where(rc >= s, pltpu.roll(g, s, axis=0), 0.0)
        s *= 2
    glast = jnp.broadcast_to(g.reshape(nck, c, LANES)[:, c - 1:c, :], (nck, c, LANES)).reshape(tl, LANES)
    egc = jnp.exp(g)
    kfac = jnp.exp(glast - g)
    be_sc[...] = beta
    gc_sc[...] = g
    egl_sc[...] = jnp.exp(glast)
    for h in range(nh):
        hs = slice(h * dk, (h + 1) * dk)
        qh = act[:, h * dk:(h + 1) * dk]
        kh = act[:, (nh + h) * dk:(nh + h + 1) * dk]
        vh = act[:, (2 * nh + h) * dk:(2 * nh + h + 1) * dk]
        qh = qh * lax.rsqrt(jnp.sum(qh * qh, axis=-1, keepdims=True) + NORM_EPS) * (dk ** -0.5)
        kh = kh * lax.rsqrt(jnp.sum(kh * kh, axis=-1, keepdims=True) + NORM_EPS)
        qh = jnp.where(rmask, qh, 0.0)
        kh = jnp.where(rmask, kh, 0.0)
        vh = jnp.where(rmask, vh, 0.0)
        b_h = beta[:, h:h + 1]
        e_h = egc[:, nh + h:nh + h + 1]
        q_sc[:, hs] = qh
        k_sc[:, hs] = kh
        vb_sc[:, hs] = vh * b_h
        kb_sc[:, hs] = kh * (b_h * e_h)
        qd_sc[:, hs] = qh * e_h
        kd_sc[:, hs] = kh * kfac[:, nh + h:nh + h + 1]

    ii = lax.broadcasted_iota(I32, (c, nh * c), 0)
    jl = lax.broadcasted_iota(I32, (c, nh * c), 1) % c
    eye_cat = (ii == jl).astype(F32)
    gn = gn_ref[...]
    n_factors = max(1, int(math.ceil(math.log2(valid))))

    def chunk_body(ci, carry):
        rows = pl.ds(pl.multiple_of(ci * c, c), c)
        gcs = gc_sc[rows, :]
        bes = be_sc[rows, :]
        egl = egl_sc[rows, :]
        kk, qk, gexp, bexp = [], [], [], []
        for h in range(nh):
            hs = slice(h * dk, (h + 1) * dk)
            kb16 = k_sc[rows, hs].astype(BF16)
            kk.append(_mm(kb16, kb16, NT))
            qk.append(_mm(q_sc[rows, hs].astype(BF16), kb16, NT))
            gexp.append(jnp.broadcast_to(gcs[:, nh + h:nh + h + 1], (c, c)))
            bexp.append(jnp.broadcast_to(bes[:, h:h + 1], (c, c)))
        kk, qk, gexp, bexp = [jnp.concatenate(v, axis=1) for v in (kk, qk, gexp, bexp)]
        grow = jnp.sum(jnp.where(ii == jl, gexp, 0.0), axis=0, keepdims=True)
        decay = jnp.where(ii >= jl, jnp.exp(gexp - grow), 0.0)
        a = jnp.where(ii > jl, bexp * kk * decay, 0.0)
        tinv = eye_cat - a
        p = a
        for _ in range(n_factors - 1):
            p = _mm3(p, _block_rows(p, c, nh))
            tinv = _mm3(tinv, _block_rows(eye_cat + p, c, nh))
        rhs = jnp.concatenate(
            [jnp.concatenate([vb_sc[rows, h * dk:(h + 1) * dk], kb_sc[rows, h * dk:(h + 1) * dk]], axis=1)
             for h in range(nh)], axis=0)
        sol = _mm3(_block_rows(tinv, c, nh), rhs)
        sts, v_news, o1s = [], [], []
        for h in range(nh):
            hs = slice(h * dk, (h + 1) * dk)
            st = s_sc[h]
            wq = jnp.concatenate([sol[h * c:(h + 1) * c, GDN_DV:], qd_sc[rows, hs]], axis=0)
            r = _mm(wq.astype(BF16), st.astype(BF16))
            sts.append(st)
            v_news.append(sol[h * c:(h + 1) * c, :GDN_DV] - r[:c])
            o1s.append(r[c:])
        o2 = _mm(_block_rows(qk * decay, c, nh).astype(BF16), jnp.concatenate(v_news, axis=0).astype(BF16))
        for h in range(nh):
            hs = slice(h * dk, (h + 1) * dk)
            upd = _mm(kd_sc[rows, hs].astype(BF16), v_news[h].astype(BF16), TN)
            s_sc[h] = sts[h] * egl[0:1, nh + h:nh + h + 1] + upd
            o = o1s[h] + o2[h * c:(h + 1) * c]
            o_ref[0, rows, hs] = _rms(o, gn) * _silu(z_ref[0, rows, hs])
        return carry

    lax.fori_loop(0, nck, chunk_body, 0)

    @pl.when(t == pl.num_programs(1) - 1)
    def _():
        sfin_ref[0] = s_sc[...]


def gdn(x, z, bg, cprev, s0, conv_w, a_log, dt_bias, gnorm, chunk, valid, tl):
    n, lp, cq = x.shape
    nt = lp // tl
    nv = GDN_HEADS * GDN_DV
    alog = jnp.zeros((1, LANES), F32).at[0, GDN_HEADS:2 * GDN_HEADS].set(a_log)
    dtb = jnp.zeros((1, LANES), F32).at[0, GDN_HEADS:2 * GDN_HEADS].set(dt_bias)
    seq3 = lambda w: pl.BlockSpec((1, tl, w), lambda i, t: (i, t, 0))
    return pl.pallas_call(
        functools.partial(_gdn_kernel, chunk=chunk, valid=valid),
        grid=(n, nt),
        in_specs=[
            seq3(cq), seq3(nv), seq3(LANES),
            pl.BlockSpec((1, SUBLANES, cq), lambda i, t: (i, 0, 0)),
            pl.BlockSpec((1, GDN_HEADS, GDN_DK, GDN_DV), lambda i, t: (i, 0, 0, 0)),
            _full((GDN_CONV, cq)), _full((1, LANES)), _full((1, LANES)), _full((1, GDN_DV)),
        ],
        out_specs=[seq3(nv), pl.BlockSpec((1, GDN_HEADS, GDN_DK, GDN_DV), lambda i, t: (i, 0, 0, 0))],
        out_shape=[jax.ShapeDtypeStruct((n, lp, nv), F32),
                   jax.ShapeDtypeStruct((n, GDN_HEADS, GDN_DK, GDN_DV), F32)],
        scratch_shapes=[
            pltpu.VMEM((tl + SUBLANES, cq), F32),
            pltpu.VMEM((GDN_HEADS, GDN_DK, GDN_DV), F32),
        ] + [pltpu.VMEM((tl, nv), F32)] * 6 + [pltpu.VMEM((tl, LANES), F32)] * 3,
        compiler_params=_cp(("parallel", "arbitrary")),
        name="gdn",
    )(x, z, bg, cprev, s0, conv_w, alog, dtb, gnorm.reshape(1, GDN_DV))


def _even_out_kernel(x_ref, oa_ref, ob_ref, woa_ref, wob_ref, ln_ref, wg_ref, wu_ref, wd_ref,
                     o_ref, x1_sc, h_sc, acc_sc):
    f = pl.program_id(1)

    @pl.when(f == 0)
    def _():
        x1 = x_ref[...] + (_mm(oa_ref[...].astype(BF16), woa_ref[...]) + _mm(ob_ref[...].astype(BF16), wob_ref[...]))
        x1_sc[...] = x1
        h_sc[...] = _rms(x1, ln_ref[...]).astype(BF16)
        acc_sc[...] = jnp.zeros_like(acc_sc)

    h = h_sc[...]
    act = (_silu(_mm(h, wg_ref[...])) * _mm(h, wu_ref[...])).astype(BF16)
    acc_sc[...] += _mm(act, wd_ref[...])

    @pl.when(f == pl.num_programs(1) - 1)
    def _():
        o_ref[...] = x1_sc[...] + acc_sc[...]


def even_out(x, oa, ob, woa, wob, ln, wg, wu, wd):
    t, d = x.shape
    ff = wg.shape[1]
    tm = _row_tile(t, (1024, 512, 256, 128, 64, 32, 16, 8))
    tf = 256 if ff % 256 == 0 else LANES
    row = lambda w: pl.BlockSpec((tm, w), lambda i, f: (i, 0))
    return pl.pallas_call(
        _even_out_kernel,
        grid=(t // tm, ff // tf),
        in_specs=[row(d), row(oa.shape[1]), row(ob.shape[1]), _full(woa.shape), _full(wob.shape), _full((1, d)),
                  pl.BlockSpec((d, tf), lambda i, f: (0, f)), pl.BlockSpec((d, tf), lambda i, f: (0, f)),
                  pl.BlockSpec((tf, d), lambda i, f: (f, 0))],
        out_specs=row(d),
        out_shape=jax.ShapeDtypeStruct((t, d), F32),
        scratch_shapes=[pltpu.VMEM((tm, d), F32), pltpu.VMEM((tm, d), BF16), pltpu.VMEM((tm, d), F32)],
        compiler_params=_cp(("parallel", "arbitrary")),
        name="even_out",
    )(x, oa, ob, woa, wob, ln, wg, wu, wd)


def _rope_lanes(v, c, s):
    n = v.shape[1]
    lane = lax.broadcasted_iota(I32, (1, n), 1) % LANES
    sw = jnp.where(lane < MLA_ROPE // 2, pltpu.roll(v, n - MLA_ROPE // 2, axis=1), pltpu.roll(v, MLA_ROPE // 2, axis=1))
    return v * c + sw * s


def _odd_in_kernel(x_ref, ln_ref, wxb_ref, wgb_ref, wcq_ref, wckv_ref, wkpe_ref, qn_ref, kvn_ref, wuq_ref,
                   cos_ref, sin_ref, *rest, absorbed):
    if absorbed:
        wabs_ref, xb_ref, gb_ref, q_ref, qlat_ref, rows_ref = rest
    else:
        wuk_ref, wuv_ref, xb_ref, gb_ref, q_ref, k_ref, v_ref, rows_ref = rest
    h = _rms(x_ref[...], ln_ref[...]).astype(BF16)
    xb_ref[...] = _mm(h, wxb_ref[...])
    gb_ref[...] = _mm(h, wgb_ref[...])
    cq = _mm(h, wcq_ref[...])
    ckv = _mm(h, wckv_ref[...])
    kpe = _mm(h, wkpe_ref[...])
    cqn = _rms(cq, qn_ref[...]).astype(BF16)
    ckvn = _rms(ckv, kvn_ref[...])
    c128 = cos_ref[...]
    s128 = sin_ref[...]
    nh = MLA_HEADS
    q = _rope_lanes(_mm(cqn, wuq_ref[...]), jnp.concatenate([c128] * nh, axis=1), jnp.concatenate([s128] * nh, axis=1))
    kpe_r = _rope_lanes(kpe, c128, s128)
    qb = q.astype(BF16)
    q_ref[...] = qb
    rows_ref[:, 0:MLA_KV_RANK] = ckvn
    rows_ref[:, MLA_KV_RANK:MLA_ROW] = kpe_r[:, 0:MLA_ROPE]
    if absorbed:
        qlat_ref[...] = _mm(qb, wabs_ref[...])
    else:
        ckvb = ckvn.astype(BF16)
        k_ref[...] = (_mm(ckvb, wuk_ref[...]) + jnp.concatenate([kpe_r] * nh, axis=1)).astype(BF16)
        v_ref[...] = _mm(ckvb, wuv_ref[...]).astype(BF16)


def odd_in(x, ln, ws, qn, kvn, wuq, cos_t, sin_t, extra, absorbed):
    t, d = x.shape
    tm = _row_tile(t, (512, 256, 128, 64, 32, 16, 8))
    nblk = cos_t.shape[0] // tm
    hb = MLA_HEADS * LANES
    row = lambda w: pl.BlockSpec((tm, w), lambda i: (i, 0))
    tbl = pl.BlockSpec((tm, LANES), lambda i: (i % nblk, 0))
    if absorbed:
        outs = [(LRU_WIDTH, F32), (LRU_WIDTH, F32), (hb, BF16), (extra[0].shape[1], F32), (MLA_ROW, F32)]
    else:
        outs = [(LRU_WIDTH, F32), (LRU_WIDTH, F32), (hb, BF16), (hb, BF16), (hb, BF16), (MLA_ROW, F32)]
    return pl.pallas_call(
        functools.partial(_odd_in_kernel, absorbed=absorbed),
        grid=(t // tm,),
        in_specs=[row(d), _full((1, d))] + [_full(w.shape) for w in ws]
        + [_full(qn.shape), _full(kvn.shape), _full(wuq.shape), tbl, tbl] + [_full(w.shape) for w in extra],
        out_specs=[row(w) for w, _ in outs],
        out_shape=[jax.ShapeDtypeStruct((t, w), dt) for w, dt in outs],
        compiler_params=_cp(("parallel",)),
        name="odd_in",
    )(x, ln, *ws, qn, kvn, wuq, cos_t, sin_t, *extra)


def _expm1(x):
    u = jnp.exp(x)
    um1 = u - 1.0
    small = um1 * x / jnp.log(u)
    return jnp.where(um1 == 0.0, x, jnp.where(jnp.abs(x) < 0.5, small, um1))


def _lru_gates(xc, wa_ref, wx_ref, ba_ref, bx_ref, lam_ref):
    xcb = xc.astype(BF16)
    r = jax.nn.sigmoid(_mm(xcb, wa_ref[...]) + ba_ref[...])
    i = jax.nn.sigmoid(_mm(xcb, wx_ref[...]) + bx_ref[...])
    log_a = -LRU_C * r * jax.nn.softplus(-lam_ref[...])
    a = jnp.exp(log_a)
    b = jnp.sqrt(-_expm1(2.0 * log_a)) * (i * xc)
    return a, b


def _lru_prompt_kernel(x_ref, g_ref, cprev_ref, h0_ref, cw_ref, cb_ref, wa_ref, wx_ref, ba_ref, bx_ref, lam_ref,
                       y_ref, hl_ref, xbuf, h_sc):
    t = pl.program_id(1)
    tl = x_ref.shape[1]

    @pl.when(t == 0)
    def _():
        xbuf[0:SUBLANES, :] = cprev_ref[0]
        h_sc[0:1, :] = h0_ref[0]

    xbuf[SUBLANES:SUBLANES + tl, :] = x_ref[0]
    cw = cw_ref[...]
    xc = xbuf[5:5 + tl, :] * cw[0:1, :]
    for j in range(1, cw.shape[0]):
        xc = xc + xbuf[5 + j:5 + j + tl, :] * cw[j:j + 1, :]
    xc = xc + cb_ref[...]
    xbuf[0:SUBLANES, :] = xbuf[tl:tl + SUBLANES, :]
    a, b = _lru_gates(xc, wa_ref, wx_ref, ba_ref, bx_ref, lam_ref)
    row = lax.broadcasted_iota(I32, (tl, 1), 0)
    s = 1
    while s < tl:
        m = row >= s
        b = jnp.where(m, a * pltpu.roll(b, s, axis=0) + b, b)
        a = jnp.where(m, a * pltpu.roll(a, s, axis=0), a)
        s *= 2
    hs = a * h_sc[0:1, :] + b
    h_sc[0:1, :] = hs[tl - 1:tl, :]
    y_ref[0] = hs * jax.nn.gelu(g_ref[0])
    hl_ref[0] = hs[tl - 1:tl, :]


def lru_prompt(xb, gb, cprev, h0, cw, cb, wa, wx, ba, bx, lam, tl):
    n, l, c = xb.shape
    seq = pl.BlockSpec((1, tl, c), lambda i, t: (i, t, 0))
    return pl.pallas_call(
        _lru_prompt_kernel,
        grid=(n, l // tl),
        in_specs=[seq, seq, pl.BlockSpec((1, SUBLANES, c), lambda i, t: (i, 0, 0)),
                  pl.BlockSpec((1, 1, c), lambda i, t: (i, 0, 0)),
                  _full(cw.shape), _full((1, c)), _full(wa.shape), _full(wx.shape), _full((1, c)), _full((1, c)),
                  _full((1, c))],
        out_specs=[seq, pl.BlockSpec((1, 1, c), lambda i, t: (i, 0, 0))],
        out_shape=[jax.ShapeDtypeStruct((n, l, c), F32), jax.ShapeDtypeStruct((n, 1, c), F32)],
        scratch_shapes=[pltpu.VMEM((tl + SUBLANES, c), F32), pltpu.VMEM((SUBLANES, c), F32)],
        compiler_params=_cp(("parallel", "arbitrary")),
        name="lru_prompt",
    )(xb, gb, cprev, h0, cw, cb, wa, wx, ba, bx, lam)


def _lru_sample_kernel(x_ref, g_ref, cprev_ref, h0_ref, cw_ref, cb_ref, wa_ref, wx_ref, ba_ref, bx_ref, lam_ref,
                       y_ref, hl_ref):
    l = x_ref.shape[0]
    cw = cw_ref[...]
    nw = cw.shape[0]
    xx = [cprev_ref[j] for j in range(nw - 1)] + [x_ref[j] for j in range(l)]
    h = h0_ref[...]
    for i in range(l):
        xc = xx[i] * cw[0:1, :]
        for j in range(1, nw):
            xc = xc + xx[i + j] * cw[j:j + 1, :]
        xc = xc + cb_ref[...]
        a, b = _lru_gates(xc, wa_ref, wx_ref, ba_ref, bx_ref, lam_ref)
        h = a * h + b
        y_ref[i] = h * jax.nn.gelu(g_ref[i])
    hl_ref[...] = h


def lru_sample(xb, gb, cprev, h0, cw, cb, wa, wx, ba, bx, lam):
    l, n, c = xb.shape
    return pl.pallas_call(
        _lru_sample_kernel,
        out_shape=[jax.ShapeDtypeStruct((l, n, c), F32), jax.ShapeDtypeStruct((n, c), F32)],
        compiler_params=pltpu.CompilerParams(vmem_limit_bytes=VMEM_LIMIT),
        name="lru_sample",
    )(xb, gb, cprev, h0, cw, cb, wa, wx, ba, bx, lam)


MLA_HEADS_PER_STEP = 2


def _mla_prompt_kernel(q_ref, k_ref, v_ref, o_ref, m_sc, l_sc, acc_sc, *, scale):
    qi = pl.program_id(2)
    tq = q_ref.shape[0]
    tk = tq
    g = q_ref.shape[1] // LANES
    c1 = scale * math.log2(math.e)
    m_sc[...] = jnp.full_like(m_sc, -jnp.inf)
    l_sc[...] = jnp.zeros_like(l_sc)
    acc_sc[...] = jnp.zeros_like(acc_sc)
    on_or_below = lax.broadcasted_iota(I32, (tq, tk), 1) <= lax.broadcasted_iota(I32, (tq, tk), 0)

    def step(ki, masked):
        rows = pl.ds(pl.multiple_of(ki * tk, tk), tk)
        for h in range(g):
            hs = slice(h * LANES, (h + 1) * LANES)
            s = _mm(q_ref[:, hs], k_ref[rows, hs], NT) * c1
            if masked:
                s = jnp.where(on_or_below, s, -jnp.inf)
            m_prev = m_sc[h]
            m_new = jnp.maximum(m_prev, jnp.max(s, axis=-1, keepdims=True))
            alpha = jnp.exp2(m_prev - m_new)
            p = jnp.exp2(s - jnp.concatenate([m_new] * (tk // LANES), axis=1))
            l_sc[h] = alpha * l_sc[h] + jnp.sum(p, axis=-1, keepdims=True)
            acc_sc[:, hs] = alpha * acc_sc[:, hs] + _mm(p.astype(BF16), v_ref[rows, hs])
            m_sc[h] = m_new

    def body(ki, carry):
        step(ki, False)
        return carry

    lax.fori_loop(0, qi, body, 0)
    step(qi, True)
    for h in range(g):
        hs = slice(h * LANES, (h + 1) * LANES)
        o_ref[:, hs] = (acc_sc[:, hs] / l_sc[h]).astype(o_ref.dtype)


def mla_prompt(q, k, v, n, l, tq):
    nq = l // tq
    g = MLA_HEADS_PER_STEP
    w = g * LANES
    scale = (MLA_NOPE + MLA_ROPE) ** -0.5
    return pl.pallas_call(
        functools.partial(_mla_prompt_kernel, scale=scale),
        grid=(n, MLA_HEADS // g, nq),
        in_specs=[pl.BlockSpec((tq, w), lambda i, h, j: (i * nq + j, h)),
                  pl.BlockSpec((l, w), lambda i, h, j: (i, h)), pl.BlockSpec((l, w), lambda i, h, j: (i, h))],
        out_specs=pl.BlockSpec((tq, w), lambda i, h, j: (i * nq + j, h)),
        out_shape=jax.ShapeDtypeStruct(q.shape, BF16),
        scratch_shapes=[pltpu.VMEM((g, tq, LANES), F32), pltpu.VMEM((g, tq, LANES), F32), pltpu.VMEM((tq, w), F32)],
        compiler_params=_cp(("parallel", "parallel", "arbitrary")),
        name="mla_prompt",
    )(q, k, v)


def _mla_sample_kernel(pt_ref, q_ref, rows_ref, *rest, n_pages_step, l, scale):
    page_refs = rest[:n_pages_step]
    o_ref, m_sc, l_sc, acc_sc = rest[n_pages_step:]
    j = pl.program_id(1)

    @pl.when(j == 0)
    def _():
        m_sc[...] = jnp.full_like(m_sc, -jnp.inf)
        l_sc[...] = jnp.zeros_like(l_sc)
        acc_sc[...] = jnp.zeros_like(acc_sc)

    q = q_ref[0]
    pages = [r[0].astype(BF16) for r in page_refs]
    s = jnp.concatenate([_mm(q, pg, NT) for pg in pages], axis=1) * scale
    m_prev = m_sc[...]
    m_new = jnp.maximum(m_prev, jnp.max(s, axis=-1, keepdims=True))
    alpha = jnp.exp(m_prev - m_new)
    p32 = jnp.exp(s - m_new)
    p = p32.astype(BF16)
    ps = PAGE_SIZE
    pv = _mm(p[:, 0:ps], pages[0])
    for i in range(1, n_pages_step):
        pv = pv + _mm(p[:, i * ps:(i + 1) * ps], pages[i])
    l_sc[...] = alpha * l_sc[...] + jnp.sum(p32, axis=-1, keepdims=True)
    acc_sc[...] = alpha * acc_sc[...] + pv
    m_sc[...] = m_new

    @pl.when(j == pl.num_programs(1) - 1)
    def _():
        qf = q.astype(F32)
        rows = rows_ref[0]
        tok = lax.broadcasted_iota(I32, (qf.shape[0], 1), 0) // MLA_HEADS
        sn = []
        for mm in range(l):
            sm = jnp.sum(qf * rows[mm:mm + 1, :], axis=-1, keepdims=True) * scale
            sn.append(jnp.where(tok >= mm, sm, -jnp.inf))
        m_old = m_sc[...]
        m_fin = m_old
        for sm in sn:
            m_fin = jnp.maximum(m_fin, sm)
        al = jnp.exp(m_old - m_fin)
        lsum = al * l_sc[...]
        acc = al * acc_sc[...]
        for mm in range(l):
            pm = jnp.exp(sn[mm] - m_fin)
            lsum = lsum + pm
            acc = acc + pm * rows[mm:mm + 1, :]
        o_ref[0] = acc / lsum


def mla_sample(page_table, q_cat, rows, cache, n_pages_step):
    n, r, c = q_cat.shape
    l = rows.shape[1]
    n_pages = page_table.shape[1]
    steps = n_pages // n_pages_step
    scale = (MLA_NOPE + MLA_ROPE) ** -0.5
    pt = page_table.reshape(-1)

    def page_map(k):
        return lambda i, j, pt_ref: (pt_ref[i * n_pages + j * n_pages_step + k], 0, 0)

    return pl.pallas_call(
        functools.partial(_mla_sample_kernel, n_pages_step=n_pages_step, l=l, scale=scale),
        grid_spec=pltpu.PrefetchScalarGridSpec(
            num_scalar_prefetch=1,
            grid=(n, steps),
            in_specs=[pl.BlockSpec((1, r, c), lambda i, j, pt_ref: (i, 0, 0)),
                      pl.BlockSpec((1, l, c), lambda i, j, pt_ref: (i, 0, 0))]
            + [pl.BlockSpec((1, PAGE_SIZE, c), page_map(k)) for k in range(n_pages_step)],
            out_specs=pl.BlockSpec((1, r, c), lambda i, j, pt_ref: (i, 0, 0)),
            scratch_shapes=[pltpu.VMEM((r, 1), F32), pltpu.VMEM((r, 1), F32), pltpu.VMEM((r, c), F32)],
        ),
        out_shape=jax.ShapeDtypeStruct((n, r, c), F32),
        compiler_params=_cp(("parallel", "arbitrary")),
        name="mla_sample",
    )(pt, q_cat, rows, *([cache] * n_pages_step))


def _matmul_kernel(x_ref, w_ref, o_ref):
    o_ref[...] = _mm(x_ref[...].astype(BF16), w_ref[...]).astype(o_ref.dtype)


def matmul(x, w, out_dtype):
    t, kd = x.shape
    tm = _row_tile(t, (512, 256, 128, 64, 32, 16, 8))
    return pl.pallas_call(
        _matmul_kernel,
        grid=(t // tm,),
        in_specs=[pl.BlockSpec((tm, kd), lambda i: (i, 0)), _full(w.shape)],
        out_specs=pl.BlockSpec((tm, w.shape[1]), lambda i: (i, 0)),
        out_shape=jax.ShapeDtypeStruct((t, w.shape[1]), out_dtype),
        compiler_params=_cp(("parallel",)),
        name="matmul",
    )(x, w)


def _odd_out_kernel(x_ref, yc_ref, od_ref, woc_ref, wod_ref, ln_ref, rw_ref, rb_ref, *rest):
    x1_ref, hn_ref, idx_ref, gate_ref = rest[-4:]
    x1 = x_ref[...] + (_mm(yc_ref[...].astype(BF16), woc_ref[...]) + _mm(od_ref[...], wod_ref[...]))
    x1_ref[...] = x1
    hn = _rms(x1, ln_ref[...])
    hn_ref[...] = hn.astype(BF16)
    logits = _mm(hn, rw_ref[...], NN, HI) + rb_ref[...]
    lane = lax.broadcasted_iota(I32, logits.shape, 1)
    logits = jnp.where(lane < N_EXPERTS, logits, -jnp.inf)
    m1 = jnp.max(logits, axis=-1, keepdims=True)
    i1 = jnp.min(jnp.where(logits == m1, lane, LANES), axis=-1, keepdims=True)
    l2 = jnp.where(lane == i1, -jnp.inf, logits)
    m2 = jnp.max(l2, axis=-1, keepdims=True)
    i2 = jnp.min(jnp.where(l2 == m2, lane, LANES), axis=-1, keepdims=True)
    e2 = jnp.exp(m2 - m1)
    den = 1.0 + e2
    idx_ref[...] = jnp.where(lane == 0, i1, jnp.where(lane == 1, i2, 0))
    gate_ref[...] = jnp.where(lane == 0, 1.0 / den, jnp.where(lane == 1, e2 / den, 0.0))


def odd_out(x, yc, od, woc, wod, ln, rw, rb, hn_rows, hn_row0, hn_buf=None):
    t, d = x.shape
    tm = _row_tile(t, (512, 256, 128, 64, 32, 16, 8))
    assert hn_row0 % tm == 0
    off = hn_row0 // tm
    row = lambda w: pl.BlockSpec((tm, w), lambda i: (i, 0))
    args = (x, yc, od, woc, wod, ln, rw, rb) + (() if hn_buf is None else (hn_buf,))
    return pl.pallas_call(
        _odd_out_kernel,
        grid=(t // tm,),
        in_specs=[row(d), row(yc.shape[1]), row(od.shape[1]), _full(woc.shape), _full(wod.shape), _full((1, d)),
                  _full(rw.shape), _full(rb.shape)] + ([] if hn_buf is None else [pl.BlockSpec(memory_space=pl.ANY)]),
        out_specs=[row(d), pl.BlockSpec((tm, d), lambda i: (i + off, 0)), row(LANES), row(LANES)],
        out_shape=[jax.ShapeDtypeStruct((t, d), F32), jax.ShapeDtypeStruct((hn_rows, d), BF16),
                   jax.ShapeDtypeStruct((t, LANES), I32), jax.ShapeDtypeStruct((t, LANES), F32)],
        input_output_aliases={} if hn_buf is None else {8: 1},
        compiler_params=_cp(("parallel",)),
        name="odd_out",
    )(*args)


EXPERT_ROWS = 512
EXPERT_FF_CHUNK = 512
F_VALID, F_FIRST, F_LAST = 1, 2, 4


def _moe_expert_kernel(pb_ref, pj_ref, fl_ref, be_ref, slot_ref, gate_ref, hn_ref, wg_ref, wu_ref, wd_ref,
                       o_ref, xacc, gacc):
    p = pl.program_id(0)
    fl = fl_ref[p]
    er = xacc.shape[0]
    tile = hn_ref.shape[0]

    @pl.when((fl & F_VALID) != 0)
    def _():
        @pl.when((fl & F_FIRST) != 0)
        def _():
            xacc[...] = jnp.zeros_like(xacc)
            gacc[...] = jnp.zeros_like(gacc)

        srow = pb_ref[p] * er + lax.broadcasted_iota(I32, (er, tile), 0)
        m0 = slot_ref[0:1, :] == srow
        m1 = slot_ref[1:2, :] == srow
        sel = jnp.where(m0, 1.0, jnp.where(m1, 1.0, 0.0)).astype(BF16)
        xacc[...] += _mm(sel, hn_ref[...])
        g = jnp.where(m0, gate_ref[0:1, :], jnp.where(m1, gate_ref[1:2, :], 0.0))
        gacc[...] += jnp.sum(g, axis=1, keepdims=True)

        @pl.when((fl & F_LAST) != 0)
        def _():
            x = xacc[...].astype(BF16)
            acc = None
            for f in range(0, wg_ref.shape[2], EXPERT_FF_CHUNK):
                gg = _mm(x, wg_ref[0, :, f:f + EXPERT_FF_CHUNK])
                uu = _mm(x, wu_ref[0, :, f:f + EXPERT_FF_CHUNK])
                y = _mm((_silu(gg) * uu).astype(BF16), wd_ref[0, f:f + EXPERT_FF_CHUNK, :])
                acc = y if acc is None else acc + y
            o_ref[...] = (acc * gacc[...]).astype(o_ref.dtype)


def moe_experts(plan, slot_t, gate_t, hn, wg, wu, wd, tile):
    pb, pj, fl, blk_e, n_blocks = plan
    d = hn.shape[1]
    ff = wg.shape[2]
    wmap = lambda p, pb, pj, fl, be: (be[pb[p]], 0, 0)
    tmap = lambda p, pb, pj, fl, be: (0, pj[p])
    return pl.pallas_call(
        _moe_expert_kernel,
        grid_spec=pltpu.PrefetchScalarGridSpec(
            num_scalar_prefetch=4,
            grid=(pb.shape[0],),
            in_specs=[pl.BlockSpec((TOP_K, tile), tmap), pl.BlockSpec((TOP_K, tile), tmap),
                      pl.BlockSpec((tile, d), lambda p, pb, pj, fl, be: (pj[p], 0)),
                      pl.BlockSpec((1, d, ff), wmap), pl.BlockSpec((1, d, ff), wmap), pl.BlockSpec((1, ff, d), wmap)],
            out_specs=pl.BlockSpec((EXPERT_ROWS, d), lambda p, pb, pj, fl, be: (pb[p], 0)),
            scratch_shapes=[pltpu.VMEM((EXPERT_ROWS, d), F32), pltpu.VMEM((EXPERT_ROWS, 1), F32)],
        ),
        out_shape=jax.ShapeDtypeStruct((n_blocks * EXPERT_ROWS, d), BF16),
        compiler_params=_cp(("arbitrary",)),
        name="moe_experts",
    )(pb, pj, fl, blk_e, slot_t, gate_t, hn, wg, wu, wd)


def _moe_combine_kernel(qj_ref, qb_ref, fl_ref, slot_ref, x1_ref, yb_ref, fn_ref, o_ref, yacc):
    p = pl.program_id(0)
    fl = fl_ref[p]
    tile = x1_ref.shape[0]
    er = yb_ref.shape[0]

    @pl.when((fl & F_VALID) != 0)
    def _():
        @pl.when((fl & F_FIRST) != 0)
        def _():
            yacc[...] = jnp.zeros_like(yacc)

        scol = qb_ref[p] * er + lax.broadcasted_iota(I32, (tile, er), 1)
        sl = slot_ref[...]
        sel = jnp.where(sl[:, 0:1] == scol, 1.0, jnp.where(sl[:, 1:2] == scol, 1.0, 0.0)).astype(BF16)
        yacc[...] += _mm(sel, yb_ref[...])

        @pl.when((fl & F_LAST) != 0)
        def _():
            o_ref[...] = _rms(x1_ref[...] + yacc[...], fn_ref[...])


def moe_combine(sched, slot_cols, x1, yb, fnorm, tile):
    qj, qb, fl = sched
    t, d = x1.shape
    tmap = lambda p, qj, qb, fl: (qj[p], 0)
    return pl.pallas_call(
        _moe_combine_kernel,
        grid_spec=pltpu.PrefetchScalarGridSpec(
            num_scalar_prefetch=3,
            grid=(qj.shape[0],),
            in_specs=[pl.BlockSpec((tile, LANES), tmap), pl.BlockSpec((tile, d), tmap),
                      pl.BlockSpec((EXPERT_ROWS, d), lambda p, qj, qb, fl: (qb[p], 0)),
                      pl.BlockSpec((1, d), lambda p, qj, qb, fl: (0, 0))],
            out_specs=pl.BlockSpec((tile, d), tmap),
            scratch_shapes=[pltpu.VMEM((tile, d), F32)],
        ),
        out_shape=jax.ShapeDtypeStruct((t, d), F32),
        compiler_params=_cp(("arbitrary",)),
        name="moe_combine",
    )(qj, qb, fl, slot_cols, x1, yb, fnorm)


def _ragged_steps(counts, pmax):
    cum = jnp.cumsum(counts)
    total = cum[-1]
    ar = jnp.arange(pmax, dtype=I32)
    p = jnp.minimum(ar, jnp.maximum(total - 1, 0))
    row = jnp.minimum(jnp.searchsorted(cum, p, side="right").astype(I32), counts.shape[0] - 1)
    off = p - (cum[row] - counts[row])
    return row, off, ar < total, total


def _group_flags(gid, valid, total):
    pmax = gid.shape[0]
    ar = jnp.arange(pmax, dtype=I32)
    prev = jnp.concatenate([gid[:1] - 1, gid[:-1]])
    nxt = jnp.concatenate([gid[1:], gid[-1:] + 1])
    first = jnp.logical_or(ar == 0, gid != prev)
    last = jnp.logical_or(ar == total - 1, gid != nxt)
    fl = F_VALID + F_FIRST * first.astype(I32) + F_LAST * last.astype(I32)
    return jnp.where(valid, fl, 0).astype(I32)


def _moe_plan(e_all, tile):
    t = e_all.shape[0]
    nt = t // tile
    er = EXPERT_ROWS
    ex = jnp.arange(N_EXPERTS, dtype=I32)[None, :]
    oh = jnp.logical_or(e_all[:, 0:1] == ex, e_all[:, 1:2] == ex).astype(I32)
    cs = jnp.cumsum(oh, axis=0)
    rank = cs - oh
    counts = cs[-1]
    padded = (counts + er - 1) // er * er
    pad_end = jnp.cumsum(padded)
    pad_start = pad_end - padded
    slot = jnp.take_along_axis(pad_start[None, :] + rank, e_all, axis=1).astype(I32)
    n_blocks = -(-(t * TOP_K) // er) + N_EXPERTS
    blk = jnp.arange(n_blocks, dtype=I32)
    blk_e = jnp.minimum(jnp.sum((blk[:, None] * er >= pad_end[None, :]).astype(I32), axis=1), N_EXPERTS - 1)
    n_used = pad_end[-1] // er
    tstart = jnp.concatenate([rank[::tile], counts[None, :]], axis=0)

    r0 = blk * er - pad_start[blk_e]
    r1 = jnp.minimum(r0 + er - 1, counts[blk_e] - 1)
    ts_b = tstart.T[blk_e][:, 1:]
    jlo = jnp.minimum(jnp.sum((ts_b <= r0[:, None]).astype(I32), axis=1), nt - 1)
    jhi = jnp.minimum(jnp.sum((ts_b <= r1[:, None]).astype(I32), axis=1), nt - 1)
    cnt_b = jnp.where(blk < n_used, jhi - jlo + 1, 0)
    row, off, valid, total = _ragged_steps(cnt_b, n_blocks + nt * N_EXPERTS)
    expert_plan = (row, (jlo[row] + off).astype(I32), _group_flags(row, valid, total), blk_e.astype(I32), n_blocks)

    def combine_sched(j0, j1):
        ntr = j1 - j0
        first_slot = pad_start[None, :] + tstart[j0:j1]
        cnt = tstart[j0 + 1:j1 + 1] - tstart[j0:j1]
        blo = first_slot // er
        bhi = (first_slot + cnt - 1) // er
        nb = jnp.where(cnt > 0, bhi - blo + 1, 0).reshape(-1)
        pmax = min(n_blocks + ntr * N_EXPERTS, 2 * ntr * N_EXPERTS)
        r, o, v, tot = _ragged_steps(nb, pmax)
        qj = (r // N_EXPERTS).astype(I32)
        return qj, (blo.reshape(-1)[r] + o).astype(I32), _group_flags(qj, v, tot)

    return slot, expert_plan, combine_sched


def _rope_tables(pos):
    half = MLA_ROPE // 2
    inv = ROPE_THETA ** (-jnp.arange(half, dtype=F32) * 2.0 / MLA_ROPE)
    ang = pos.astype(F32)[:, None] * inv[None, :]
    cos = jnp.cos(ang)
    sin = jnp.sin(ang)
    n = pos.shape[0]
    cos_t = jnp.concatenate([cos, cos, jnp.ones((n, LANES - MLA_ROPE), F32)], axis=1)
    sin_t = jnp.concatenate([-sin, sin, jnp.zeros((n, LANES - MLA_ROPE), F32)], axis=1)
    return cos_t, sin_t


def _block_diag(w):
    b, i, j = w.shape
    eye = jnp.eye(b, dtype=w.dtype)
    return (eye[:, None, :, None] * w[:, :, None, :]).reshape(b * i, b * j)


def _pad_rows(a, rows):
    return jnp.pad(a, ((0, 0), (0, rows - a.shape[1]), (0, 0)))


def kernel(x_prompt, x_sample, state_swa_kv, state_gdn_conv, state_gdn_s, state_lru_conv, state_lru_h, cache_mla, page_table, e_ln_mix, e_w_in, e_gdn_conv_w, e_gdn_a_log, e_gdn_dt_bias, e_gdn_norm, e_swa_sinks, e_w_out, e_ln_ffn, e_ffn_gate, e_ffn_up, e_ffn_down, o_ln_mix, o_w_in, o_lru_conv_w, o_lru_conv_b, o_lru_w_a, o_lru_b_a, o_lru_w_x, o_lru_b_x, o_lru_lambda, o_mla_q_norm, o_mla_w_uq, o_mla_kv_norm, o_mla_w_uk, o_mla_w_uv, o_w_out, o_ln_ffn, o_router_w, o_router_b, o_exp_gate, o_exp_up, o_exp_down, final_norm):
    nb, lp, d = x_prompt.shape
    ns, ls, _ = x_sample.shape
    past_len = page_table.shape[1] * PAGE_SIZE
    tp, ts = nb * lp, ns * ls
    xp = x_prompt.reshape(tp, d)
    xs = x_sample.reshape(ts, d)
    row1 = lambda v: v.reshape(1, -1)

    na = (SWA_HEADS + 2 * SWA_KV_HEADS) * SWA_HEAD_DIM
    nz = GDN_HEADS * GDN_DV
    w_in = e_w_in[0].astype(BF16)
    o1, o2, o3 = na, na + GDN_QKV, na + GDN_QKV + nz
    w_groups = [w_in[:, :o1], w_in[:, o1:o2], w_in[:, o2:o3],
                jnp.pad(w_in[:, o3:], ((0, 0), (0, LANES - 2 * GDN_HEADS)))]
    ln = row1(e_ln_mix[0])
    qkv_p, gx_p, z_p, bg_p = norm_proj(xp, ln, w_groups)
    qkv_s, gx_s, z_s, bg_s = norm_proj(xs, ln, w_groups)

    sinks = e_swa_sinks[0]
    oa_p = swa_prompt(qkv_p, sinks, nb, lp)
    wbuf = state_swa_kv.shape[2]
    nkv = 2 * SWA_KV_HEADS * SWA_HEAD_DIM
    buf = state_swa_kv[0].reshape(ns, wbuf, nkv)
    qkv_s3 = qkv_s.reshape(ns, ls, na)
    oa_s = swa_sample(_pad_rows(qkv_s3, SUBLANES), buf, sinks, ls)[:, :ls].reshape(ts, -1)
    kv_p = qkv_p.reshape(nb, lp, na)[:, lp - SWA_WINDOW:, na - nkv:]
    swa_kv_p = kv_p.reshape(1, nb, SWA_WINDOW, 2, SWA_KV_HEADS, SWA_HEAD_DIM)
    kv_s = jnp.concatenate([buf, qkv_s3[:, :, na - nkv:]], axis=1)[:, ls:]
    swa_kv_s = kv_s.reshape(1, ns, wbuf, 2, SWA_KV_HEADS, SWA_HEAD_DIM)

    gx_p3 = gx_p.reshape(nb, lp, GDN_QKV)
    gx_s3 = gx_s.reshape(ns, ls, GDN_QKV)
    gargs = (e_gdn_conv_w[0], e_gdn_a_log[0], e_gdn_dt_bias[0], e_gdn_norm[0])
    ob_p, gs_p = gdn(gx_p3, z_p.reshape(nb, lp, nz), bg_p.reshape(nb, lp, LANES),
                     jnp.zeros((nb, SUBLANES, GDN_QKV), F32), jnp.zeros((nb, GDN_HEADS, GDN_DK, GDN_DV), F32),
                     *gargs, chunk=math.gcd(lp, GDN_CHUNK), valid=math.gcd(lp, GDN_CHUNK), tl=min(lp, 512))
    cprev_s = jnp.pad(state_gdn_conv[0], ((0, 0), (SUBLANES - (GDN_CONV - 1), 0), (0, 0)))
    assert ls <= GDN_CHUNK and math.gcd(ls, GDN_CHUNK) == ls
    ob_s, gs_s = gdn(_pad_rows(gx_s3, GDN_CHUNK), _pad_rows(z_s.reshape(ns, ls, nz), GDN_CHUNK),
                     _pad_rows(bg_s.reshape(ns, ls, LANES), GDN_CHUNK), cprev_s, state_gdn_s[0],
                     *gargs, chunk=GDN_CHUNK, valid=ls, tl=GDN_CHUNK)
    ob_s = ob_s[:, :ls].reshape(ts, nz)
    gconv_p = gx_p3[:, lp - (GDN_CONV - 1):][None]
    gconv_s = gx_s3[:, ls - (GDN_CONV - 1):][None]

    w_out = e_w_out[0].astype(BF16)
    nqa = SWA_HEADS * SWA_HEAD_DIM
    ffn = (row1(e_ln_ffn[0]), e_ffn_gate[0].astype(BF16), e_ffn_up[0].astype(BF16), e_ffn_down[0].astype(BF16))
    xp = even_out(xp, oa_p, ob_p.reshape(tp, nz), w_out[:nqa], w_out[nqa:], *ffn)
    xs = even_out(xs, oa_s, ob_s, w_out[:nqa], w_out[nqa:], *ffn)

    w_in = o_w_in[0].astype(BF16)
    c0 = LRU_WIDTH
    c1 = 2 * LRU_WIDTH
    c2 = c1 + MLA_Q_RANK
    c3 = c2 + MLA_KV_RANK
    w_groups = [w_in[:, :c0], w_in[:, c0:c1], w_in[:, c1:c2], w_in[:, c2:c3],
                jnp.pad(w_in[:, c3:], ((0, 0), (0, LANES - MLA_ROPE)))]
    hd_q = MLA_NOPE + MLA_ROPE
    half = MLA_ROPE // 2
    wq = o_mla_w_uq[0].reshape(MLA_Q_RANK, MLA_HEADS, hd_q)
    wuq = jnp.concatenate([wq[:, :, MLA_NOPE:], wq[:, :, :MLA_NOPE],
                           jnp.zeros((MLA_Q_RANK, MLA_HEADS, LANES - hd_q), F32)], axis=2)
    wuq = wuq.reshape(MLA_Q_RANK, MLA_HEADS * LANES).astype(BF16)
    w_uk = o_mla_w_uk[0]
    w_uv = o_mla_w_uv[0]
    wuk = jnp.pad(w_uk, ((0, 0), (0, 0), (MLA_ROPE, LANES - hd_q))).reshape(MLA_KV_RANK, -1).astype(BF16)
    wuv = jnp.pad(w_uv, ((0, 0), (0, 0), (0, LANES - MLA_V))).reshape(MLA_KV_RANK, -1).astype(BF16)
    wabs = _block_diag(jnp.pad(jnp.transpose(w_uk, (1, 2, 0)), ((0, 0), (MLA_ROPE, LANES - hd_q), (0, 0)))).astype(BF16)
    wuv_bd = _block_diag(jnp.pad(jnp.transpose(w_uv, (1, 0, 2)), ((0, 0), (0, 0), (0, LANES - MLA_V)))).astype(BF16)
    ln = row1(o_ln_mix[0])
    qn, kvn = row1(o_mla_q_norm[0]), row1(o_mla_kv_norm[0])
    tm_p = _row_tile(tp, (512, 256, 128, 64, 32, 16, 8))
    assert lp % tm_p == 0 or tm_p % lp == 0
    pos_p = jnp.arange(max(lp, tm_p), dtype=I32) % lp
    pos_s = past_len + (jnp.arange(ts, dtype=I32) % ls)
    cos_p, sin_p = _rope_tables(pos_p)
    cos_s, sin_s = _rope_tables(pos_s)
    xb_p, gb_p, q_p, k_p, v_p, rows_p = odd_in(xp, ln, w_groups, qn, kvn, wuq, cos_p, sin_p, [wuk, wuv], False)
    xb_s, gb_s, q_s, qlat_s, rows_s = odd_in(xs, ln, w_groups, qn, kvn, wuq, cos_s, sin_s, [wabs], True)

    lru_w = (o_lru_conv_w[0], row1(o_lru_conv_b[0]), _block_diag(o_lru_w_a[0]).astype(BF16),
             _block_diag(o_lru_w_x[0]).astype(BF16), row1(o_lru_b_a[0]), row1(o_lru_b_x[0]), row1(o_lru_lambda[0]))
    xb_p3 = xb_p.reshape(nb, lp, LRU_WIDTH)
    yc_p, lh_p = lru_prompt(xb_p3, gb_p.reshape(nb, lp, LRU_WIDTH), jnp.zeros((nb, SUBLANES, LRU_WIDTH), F32),
                            jnp.zeros((nb, 1, LRU_WIDTH), F32), *lru_w, tl=min(lp, 256))
    xb_s3 = xb_s.reshape(ns, ls, LRU_WIDTH)
    tmaj = lambda a: jnp.transpose(a, (1, 0, 2))
    yc_s, lh_s = lru_sample(tmaj(xb_s3), tmaj(gb_s.reshape(ns, ls, LRU_WIDTH)), tmaj(state_lru_conv[0]),
                            state_lru_h[0], *lru_w)
    yc_s = tmaj(yc_s).reshape(ts, LRU_WIDTH)
    lconv_p = xb_p3[:, lp - 3:][None]
    lconv_s = xb_s3[:, ls - 3:][None]

    od_p = mla_prompt(q_p, k_p, v_p, nb, lp, tq=min(lp, 512))
    q_pe = q_s.reshape(ts, MLA_HEADS, LANES)[:, :, :MLA_ROPE]
    q_cat = jnp.concatenate([qlat_s.reshape(ts, MLA_HEADS, MLA_KV_RANK).astype(BF16), q_pe], axis=-1)
    q_cat = q_cat.reshape(ns, ls * MLA_HEADS, MLA_ROW)
    n_pages = page_table.shape[1]
    o_lat = mla_sample(page_table, q_cat, rows_s.reshape(ns, ls, MLA_ROW), cache_mla.reshape(cache_mla.shape[1:]),
                       n_pages_step=math.gcd(n_pages, 16))
    o_lat = o_lat.reshape(ts, MLA_HEADS, MLA_ROW)[:, :, :MLA_KV_RANK].reshape(ts, MLA_HEADS * MLA_KV_RANK)
    od_s = matmul(o_lat, wuv_bd, BF16)

    w_out = o_w_out[0].astype(BF16)
    wod = jnp.pad(w_out[LRU_WIDTH:].reshape(MLA_HEADS, MLA_V, d), ((0, 0), (0, LANES - MLA_V), (0, 0)))
    wod = wod.reshape(MLA_HEADS * LANES, d)
    rw = jnp.pad(o_router_w[0], ((0, 0), (0, LANES - N_EXPERTS)))
    rb = jnp.pad(o_router_b[0], (0, LANES - N_EXPERTS)).reshape(1, LANES)
    lnf = row1(o_ln_ffn[0])
    tall = tp + ts
    x1_p, hn, idx_p, gate_p = odd_out(xp, yc_p.reshape(tp, LRU_WIDTH), od_p, w_out[:LRU_WIDTH], wod, lnf, rw, rb,
                                      tall, 0)
    x1_s, hn, idx_s, gate_s = odd_out(xs, yc_s, od_s, w_out[:LRU_WIDTH], wod, lnf, rw, rb, tall, tp, hn)

    tile = math.gcd(math.gcd(tp, ts), EXPERT_ROWS)
    e_all = jnp.concatenate([idx_p[:, :TOP_K], idx_s[:, :TOP_K]], axis=0)
    g_all = jnp.concatenate([gate_p[:, :TOP_K], gate_s[:, :TOP_K]], axis=0)
    slot, expert_plan, combine_sched = _moe_plan(e_all, tile)
    yb = moe_experts(expert_plan, slot.T, g_all.T, hn, o_exp_gate[0].astype(BF16), o_exp_up[0].astype(BF16),
                     o_exp_down[0].astype(BF16), tile)
    slot_cols = jnp.pad(slot, ((0, 0), (0, LANES - TOP_K)))
    fnw = row1(final_norm)
    y_p = moe_combine(combine_sched(0, tp // tile), slot_cols[:tp], x1_p, yb, fnw, tile).reshape(nb, lp, d)
    y_s = moe_combine(combine_sched(tp // tile, tall // tile), slot_cols[tp:], x1_s, yb, fnw, tile).reshape(ns, ls, d)

    return (y_p, y_s, swa_kv_p, swa_kv_s, gconv_p, gconv_s, gs_p[None], gs_s[None],
            lconv_p, lconv_s, lh_p.reshape(1, nb, LRU_WIDTH), lh_s[None],
            rows_p.reshape(1, nb, lp // PAGE_SIZE, PAGE_SIZE, MLA_ROW), rows_s.reshape(1, ns, ls, MLA_ROW))
```

```python
import functools
import math

import jax
import jax.numpy as jnp
from jax import lax
from jax.experimental import pallas as pl
from jax.experimental.pallas import tpu as pltpu

F32 = jnp.float32
BF16 = jnp.bfloat16
I32 = jnp.int32
HI = lax.Precision.HIGHEST

D_MODEL = 1024
PAGE_SIZE = 128
SWA_WINDOW = 128
SWA_HEADS = 8
SWA_KV_HEADS = 2
SWA_GROUP = SWA_HEADS // SWA_KV_HEADS
SWA_HEAD_DIM = 64
GDN_HEADS = 4
GDN_DK = 128
GDN_DV = 128
GDN_CONV = 4
GDN_CHUNK = 64
GDN_QKV = GDN_HEADS * (2 * GDN_DK + GDN_DV)
LRU_WIDTH = 512
LRU_BLOCKS = 8
LRU_BLOCK_W = LRU_WIDTH // LRU_BLOCKS
LRU_C = 8.0
MLA_HEADS = 8
MLA_Q_RANK = 384
MLA_KV_RANK = 256
MLA_NOPE = 64
MLA_ROPE = 32
MLA_V = 64
MLA_ROW = MLA_KV_RANK + MLA_ROPE
ROPE_THETA = 10000.0
D_FF = 2816
N_EXPERTS = 8
TOP_K = 2
MOE_FF = 2048
NORM_EPS = 1e-6

LANES = 128
SUBLANES = 8
VMEM_LIMIT = 56 << 20

NN = (((1,), (0,)), ((), ()))
NT = (((1,), (1,)), ((), ()))
TN = (((0,), (0,)), ((), ()))


def _mm(a, b, dims=NN, precision=None):
    return lax.dot_general(a, b, dims, precision=precision, preferred_element_type=F32)


def _cp(sem):
    return pltpu.CompilerParams(dimension_semantics=sem, vmem_limit_bytes=VMEM_LIMIT)


def _rms(x, w):
    return x * lax.rsqrt(jnp.mean(x * x, axis=-1, keepdims=True) + NORM_EPS) * w


def _silu(x):
    return x * jax.nn.sigmoid(x)


def _full(shape):
    nd = len(shape)
    return pl.BlockSpec(shape, lambda *a: (0,) * nd)


def _row_tile(t, pref):
    for c in pref:
        if t % c == 0:
            return c
    return t


def _norm_proj_kernel(x_ref, ln_ref, *refs):
    n = len(refs) // 2
    h = _rms(x_ref[...], ln_ref[...]).astype(BF16)
    for w_ref, o_ref in zip(refs[:n], refs[n:]):
        o_ref[...] = _mm(h, w_ref[...])


def norm_proj(x, ln, ws):
    t, d = x.shape
    tm = _row_tile(t, (512, 256, 128, 64, 32, 16, 8))
    return pl.pallas_call(
        _norm_proj_kernel,
        grid=(t // tm,),
        in_specs=[pl.BlockSpec((tm, d), lambda i: (i, 0)), _full((1, d))] + [_full(w.shape) for w in ws],
        out_specs=[pl.BlockSpec((tm, w.shape[1]), lambda i: (i, 0)) for w in ws],
        out_shape=[jax.ShapeDtypeStruct((t, w.shape[1]), F32) for w in ws],
        compiler_params=_cp(("parallel",)),
        name="norm_proj",
    )(x, ln, *ws)


def _swa_softmax_pv(parts, sink):
    m = sink
    for s, _ in parts:
        m = jnp.maximum(m, jnp.max(s, axis=-1, keepdims=True))
    den = jnp.exp(sink - m)
    o = None
    for s, v in parts:
        p = jnp.exp(s - m)
        den = den + jnp.sum(p, axis=-1, keepdims=True)
        pv = _mm(p.astype(BF16), v)
        o = pv if o is None else o + pv
    return o / den


def _swa_prompt_kernel(sink_ref, q_ref, kvc_ref, kvp_ref, o_ref):
    b = pl.program_id(1)
    w = SWA_WINDOW
    hd = SWA_HEAD_DIM
    q = q_ref[...]
    kvc = kvc_ref[...].astype(BF16)
    kvp = kvp_ref[...].astype(BF16)
    qi = lax.broadcasted_iota(I32, (w, w), 0)
    kj = lax.broadcasted_iota(I32, (w, w), 1)
    dist_c = (qi - kj).astype(F32)
    dist_p = dist_c + float(w)
    valid_c = kj <= qi
    valid_p = jnp.logical_and(kj > qi, b > 0)
    outs = []
    for h in range(SWA_HEADS):
        j = h // SWA_GROUP
        slope = 2.0 ** (-8.0 * (h + 1) / SWA_HEADS)
        qh = q[:, h * hd:(h + 1) * hd].astype(BF16)
        kc = kvc[:, j * hd:(j + 1) * hd]
        kp = kvp[:, j * hd:(j + 1) * hd]
        vc = kvc[:, (SWA_KV_HEADS + j) * hd:(SWA_KV_HEADS + j + 1) * hd]
        vp = kvp[:, (SWA_KV_HEADS + j) * hd:(SWA_KV_HEADS + j + 1) * hd]
        s_c = _mm(qh, kc, NT) * (hd ** -0.5)
        s_p = _mm(qh, kp, NT) * (hd ** -0.5)
        s_c = jnp.where(valid_c, s_c - slope * dist_c, -jnp.inf)
        s_p = jnp.where(valid_p, s_p - slope * dist_p, -jnp.inf)
        outs.append(_swa_softmax_pv([(s_p, vp), (s_c, vc)], sink_ref[h]))
    o_ref[...] = jnp.concatenate(outs, axis=1)


def swa_prompt(qkv, sinks, n, l):
    w = SWA_WINDOW
    nb = l // w
    nq = SWA_HEADS * SWA_HEAD_DIM
    nkv = 2 * SWA_KV_HEADS * SWA_HEAD_DIM
    return pl.pallas_call(
        _swa_prompt_kernel,
        grid=(n, nb),
        in_specs=[
            pl.BlockSpec(memory_space=pltpu.SMEM),
            pl.BlockSpec((w, nq), lambda i, b: (i * nb + b, 0)),
            pl.BlockSpec((w, nkv), lambda i, b: (i * nb + b, nq // nkv)),
            pl.BlockSpec((w, nkv), lambda i, b: (i * nb + jnp.maximum(b - 1, 0), nq // nkv)),
        ],
        out_specs=pl.BlockSpec((w, nq), lambda i, b: (i * nb + b, 0)),
        out_shape=jax.ShapeDtypeStruct((n * l, nq), F32),
        compiler_params=_cp(("parallel", "arbitrary")),
        name="swa_prompt",
    )(sinks, qkv, qkv, qkv)


def _swa_sample_kernel(sink_ref, q_ref, buf_ref, o_ref, *, l):
    hd = SWA_HEAD_DIM
    lp = q_ref.shape[1]
    wb = buf_ref.shape[1]
    x = q_ref[0]
    buf = buf_ref[0].astype(BF16)
    kvn = x[:, SWA_HEADS * hd:].astype(BF16)
    qi_b = lax.broadcasted_iota(I32, (lp, wb), 0)
    kj_b = lax.broadcasted_iota(I32, (lp, wb), 1)
    dist_b = (qi_b + wb - kj_b).astype(F32)
    valid_b = (qi_b + wb - kj_b) < SWA_WINDOW
    qi_n = lax.broadcasted_iota(I32, (lp, lp), 0)
    kj_n = lax.broadcasted_iota(I32, (lp, lp), 1)
    dist_n = (qi_n - kj_n).astype(F32)
    valid_n = jnp.logical_and(kj_n <= qi_n, kj_n < l)
    outs = []
    for h in range(SWA_HEADS):
        j = h // SWA_GROUP
        slope = 2.0 ** (-8.0 * (h + 1) / SWA_HEADS)
        qh = x[:, h * hd:(h + 1) * hd].astype(BF16)
        kb = buf[:, j * hd:(j + 1) * hd]
        vb = buf[:, (SWA_KV_HEADS + j) * hd:(SWA_KV_HEADS + j + 1) * hd]
        kn = kvn[:, j * hd:(j + 1) * hd]
        vn = kvn[:, (SWA_KV_HEADS + j) * hd:(SWA_KV_HEADS + j + 1) * hd]
        s_b = _mm(qh, kb, NT) * (hd ** -0.5)
        s_n = _mm(qh, kn, NT) * (hd ** -0.5)
        s_b = jnp.where(valid_b, s_b - slope * dist_b, -jnp.inf)
        s_n = jnp.where(valid_n, s_n - slope * dist_n, -jnp.inf)
        outs.append(_swa_softmax_pv([(s_b, vb), (s_n, vn)], sink_ref[h]))
    o_ref[0] = jnp.concatenate(outs, axis=1)


def swa_sample(qkv_pad, buf, sinks, l):
    n, lp, c = qkv_pad.shape
    wb = buf.shape[1]
    nq = SWA_HEADS * SWA_HEAD_DIM
    return pl.pallas_call(
        functools.partial(_swa_sample_kernel, l=l),
        grid=(n,),
        in_specs=[
            pl.BlockSpec(memory_space=pltpu.SMEM),
            pl.BlockSpec((1, lp, c), lambda i: (i, 0, 0)),
            pl.BlockSpec((1, wb, buf.shape[2]), lambda i: (i, 0, 0)),
        ],
        out_specs=pl.BlockSpec((1, lp, nq), lambda i: (i, 0, 0)),
        out_shape=jax.ShapeDtypeStruct((n, lp, nq), F32),
        compiler_params=_cp(("parallel",)),
        name="swa_sample",
    )(sinks, qkv_pad, buf)


GDN_SOLVE_GROUP = 4


def _mm3(a, b):
    ah = a.astype(BF16)
    al = (a - ah.astype(F32)).astype(BF16)
    bh = b.astype(BF16)
    bl = (b - bh.astype(F32)).astype(BF16)
    return _mm(jnp.concatenate([ah, ah, al], axis=1), jnp.concatenate([bh, bl, bh], axis=0))


def _block_rows(x, c, nh):
    blk = lax.broadcasted_iota(I32, x.shape, 1) // c
    return jnp.concatenate([jnp.where(blk == i, x, 0.0) for i in range(nh)], axis=0)


def _gdn_kernel(x_ref, z_ref, bg_ref, cprev_ref, s0_ref, cw_ref, alog_ref, dtb_ref, gn_ref,
                o_ref, sfin_ref, xbuf, s_sc, q_sc, k_sc, vb_sc, kb_sc, qd_sc, kd_sc, u_sc, w_sc, be_sc, gc_sc, egl_sc,
                qkd_sc, *, chunk, valid):
    t = pl.program_id(1)
    tl = x_ref.shape[1]
    c = chunk
    nck = tl // c
    nh = GDN_HEADS
    dk = GDN_DK

    @pl.when(t == 0)
    def _():
        xbuf[0:SUBLANES, :] = cprev_ref[0]
        s_sc[...] = s0_ref[0]

    xbuf[SUBLANES:SUBLANES + tl, :] = x_ref[0]
    cw = cw_ref[...]
    conv = xbuf[5:5 + tl, :] * cw[0:1, :]
    for j in range(1, GDN_CONV):
        conv = conv + xbuf[5 + j:5 + j + tl, :] * cw[j:j + 1, :]
    xbuf[0:SUBLANES, :] = xbuf[tl:tl + SUBLANES, :]
    act = _silu(conv)

    row = lax.broadcasted_iota(I32, (tl, 1), 0)
    rmask = (row % c) < valid
    bg = bg_ref[0]
    beta = jax.nn.sigmoid(bg)
    g = -jnp.exp(alog_ref[...]) * jax.nn.softplus(bg + dtb_ref[...])
    g = jnp.where(rmask, g, 0.0)
    rc = row % c
    s = 1
    while s < c:
        g = g + jnp.where(rc >= s, pltpu.roll(g, s, axis=0), 0.0)
        s *= 2
    glast = jnp.broadcast_to(g.reshape(nck, c, LANES)[:, c - 1:c, :], (nck, c, LANES)).reshape(tl, LANES)
    egc = jnp.exp(g)
    kfac = jnp.exp(glast - g)
    be_sc[...] = beta
    gc_sc[...] = g
    egl_sc[...] = jnp.exp(glast)
    for h in range(nh):
        hs = slice(h * dk, (h + 1) * dk)
        qh = act[:, h * dk:(h + 1) * dk]
        kh = act[:, (nh + h) * dk:(nh + h + 1) * dk]
        vh = act[:, (2 * nh + h) * dk:(2 * nh + h + 1) * dk]
        qh = qh * lax.rsqrt(jnp.sum(qh * qh, axis=-1, keepdims=True) + NORM_EPS) * (dk ** -0.5)
        kh = kh * lax.rsqrt(jnp.sum(kh * kh, axis=-1, keepdims=True) + NORM_EPS)
        qh = jnp.where(rmask, qh, 0.0)
        kh = jnp.where(rmask, kh, 0.0)
        vh = jnp.where(rmask, vh, 0.0)
        b_h = beta[:, h:h + 1]
        e_h = egc[:, nh + h:nh + h + 1]
        q_sc[:, hs] = qh
        k_sc[:, hs] = kh
        vb_sc[:, hs] = vh * b_h
        kb_sc[:, hs] = kh * (b_h * e_h)
        qd_sc[:, hs] = qh * e_h
        kd_sc[:, hs] = kh * kfac[:, nh + h:nh + h + 1]

    ii = lax.broadcasted_iota(I32, (c, nh * c), 0)
    jl = lax.broadcasted_iota(I32, (c, nh * c), 1) % c
    eye_cat = (ii == jl).astype(F32)
    gn = gn_ref[...]
    n_factors = max(1, int(math.ceil(math.log2(valid))))

    ng = math.gcd(nck, GDN_SOLVE_GROUP)

    def solve_body(gi, carry):
        rows_g = [pl.ds(pl.multiple_of((gi * ng + g) * c, c), c) for g in range(ng)]
        a_g, qkd_g = [], []
        for rows in rows_g:
            gcs = gc_sc[rows, :]
            bes = be_sc[rows, :]
            kk, qk, gexp, bexp = [], [], [], []
            for h in range(nh):
                hs = slice(h * dk, (h + 1) * dk)
                kb16 = k_sc[rows, hs].astype(BF16)
                kk.append(_mm(kb16, kb16, NT))
                qk.append(_mm(q_sc[rows, hs].astype(BF16), kb16, NT))
                gexp.append(jnp.broadcast_to(gcs[:, nh + h:nh + h + 1], (c, c)))
                bexp.append(jnp.broadcast_to(bes[:, h:h + 1], (c, c)))
            kk, qk, gexp, bexp = [jnp.concatenate(v, axis=1) for v in (kk, qk, gexp, bexp)]
            grow = jnp.sum(jnp.where(ii == jl, gexp, 0.0), axis=0, keepdims=True)
            decay = jnp.where(ii >= jl, jnp.exp(gexp - grow), 0.0)
            a_g.append(jnp.where(ii > jl, bexp * kk * decay, 0.0))
            qkd_g.append(qk * decay)
        t_g = [eye_cat - a for a in a_g]
        p_g = a_g
        for _ in range(n_factors - 1):
            p_g = [_mm3(p, _block_rows(p, c, nh)) for p in p_g]
            t_g = [_mm3(tv, _block_rows(eye_cat + p, c, nh)) for tv, p in zip(t_g, p_g)]
        for rows, tv, qkd in zip(rows_g, t_g, qkd_g):
            rhs = jnp.concatenate(
                [jnp.concatenate([vb_sc[rows, h * dk:(h + 1) * dk], kb_sc[rows, h * dk:(h + 1) * dk]], axis=1)
                 for h in range(nh)], axis=0)
            sol = _mm3(_block_rows(tv, c, nh), rhs)
            for h in range(nh):
                hs = slice(h * dk, (h + 1) * dk)
                u_sc[rows, hs] = sol[h * c:(h + 1) * c, :GDN_DV]
                w_sc[rows, hs] = sol[h * c:(h + 1) * c, GDN_DV:]
            qkd_sc[rows, :] = qkd
        return carry

    lax.fori_loop(0, nck // ng, solve_body, 0)

    def scan_body(ci, carry):
        rows = pl.ds(pl.multiple_of(ci * c, c), c)
        egl = egl_sc[rows, :]
        sts, v_news, o1s = [], [], []
        for h in range(nh):
            hs = slice(h * dk, (h + 1) * dk)
            st = s_sc[h]
            wq = jnp.concatenate([w_sc[rows, hs], qd_sc[rows, hs]], axis=0)
            r = _mm(wq.astype(BF16), st.astype(BF16))
            sts.append(st)
            v_news.append(u_sc[rows, hs] - r[:c])
            o1s.append(r[c:])
        o2 = _mm(_block_rows(qkd_sc[rows, :], c, nh).astype(BF16), jnp.concatenate(v_news, axis=0).astype(BF16))
        for h in range(nh):
            hs = slice(h * dk, (h + 1) * dk)
            upd = _mm(kd_sc[rows, hs].astype(BF16), v_news[h].astype(BF16), TN)
            s_sc[h] = sts[h] * egl[0:1, nh + h:nh + h + 1] + upd
            o = o1s[h] + o2[h * c:(h + 1) * c]
            o_ref[0, rows, hs] = _rms(o, gn) * _silu(z_ref[0, rows, hs])
        return carry

    lax.fori_loop(0, nck, scan_body, 0)

    @pl.when(t == pl.num_programs(1) - 1)
    def _():
        sfin_ref[0] = s_sc[...]


def gdn(x, z, bg, cprev, s0, conv_w, a_log, dt_bias, gnorm, chunk, valid, tl):
    n, lp, cq = x.shape
    nt = lp // tl
    nv = GDN_HEADS * GDN_DV
    alog = jnp.zeros((1, LANES), F32).at[0, GDN_HEADS:2 * GDN_HEADS].set(a_log)
    dtb = jnp.zeros((1, LANES), F32).at[0, GDN_HEADS:2 * GDN_HEADS].set(dt_bias)
    seq3 = lambda w: pl.BlockSpec((1, tl, w), lambda i, t: (i, t, 0))
    return pl.pallas_call(
        functools.partial(_gdn_kernel, chunk=chunk, valid=valid),
        grid=(n, nt),
        in_specs=[
            seq3(cq), seq3(nv), seq3(LANES),
            pl.BlockSpec((1, SUBLANES, cq), lambda i, t: (i, 0, 0)),
            pl.BlockSpec((1, GDN_HEADS, GDN_DK, GDN_DV), lambda i, t: (i, 0, 0, 0)),
            _full((GDN_CONV, cq)), _full((1, LANES)), _full((1, LANES)), _full((1, GDN_DV)),
        ],
        out_specs=[seq3(nv), pl.BlockSpec((1, GDN_HEADS, GDN_DK, GDN_DV), lambda i, t: (i, 0, 0, 0))],
        out_shape=[jax.ShapeDtypeStruct((n, lp, nv), F32),
                   jax.ShapeDtypeStruct((n, GDN_HEADS, GDN_DK, GDN_DV), F32)],
        scratch_shapes=[
            pltpu.VMEM((tl + SUBLANES, cq), F32),
            pltpu.VMEM((GDN_HEADS, GDN_DK, GDN_DV), F32),
        ] + [pltpu.VMEM((tl, nv), F32)] * 8 + [pltpu.VMEM((tl, LANES), F32)] * 3
        + [pltpu.VMEM((tl, GDN_HEADS * chunk), F32)],
        compiler_params=_cp(("parallel", "arbitrary")),
        name="gdn",
    )(x, z, bg, cprev, s0, conv_w, alog, dtb, gnorm.reshape(1, GDN_DV))


def _even_out_kernel(x_ref, oa_ref, ob_ref, woa_ref, wob_ref, ln_ref, wg_ref, wu_ref, wd_ref,
                     o_ref, x1_sc, h_sc, acc_sc):
    f = pl.program_id(1)

    @pl.when(f == 0)
    def _():
        x1 = x_ref[...] + (_mm(oa_ref[...].astype(BF16), woa_ref[...]) + _mm(ob_ref[...].astype(BF16), wob_ref[...]))
        x1_sc[...] = x1
        h_sc[...] = _rms(x1, ln_ref[...]).astype(BF16)
        acc_sc[...] = jnp.zeros_like(acc_sc)

    h = h_sc[...]
    act = (_silu(_mm(h, wg_ref[...])) * _mm(h, wu_ref[...])).astype(BF16)
    acc_sc[...] += _mm(act, wd_ref[...])

    @pl.when(f == pl.num_programs(1) - 1)
    def _():
        o_ref[...] = x1_sc[...] + acc_sc[...]


def even_out(x, oa, ob, woa, wob, ln, wg, wu, wd):
    t, d = x.shape
    ff = wg.shape[1]
    tm = _row_tile(t, (1024, 512, 256, 128, 64, 32, 16, 8))
    tf = 256 if ff % 256 == 0 else LANES
    row = lambda w: pl.BlockSpec((tm, w), lambda i, f: (i, 0))
    return pl.pallas_call(
        _even_out_kernel,
        grid=(t // tm, ff // tf),
        in_specs=[row(d), row(oa.shape[1]), row(ob.shape[1]), _full(woa.shape), _full(wob.shape), _full((1, d)),
                  pl.BlockSpec((d, tf), lambda i, f: (0, f)), pl.BlockSpec((d, tf), lambda i, f: (0, f)),
                  pl.BlockSpec((tf, d), lambda i, f: (f, 0))],
        out_specs=row(d),
        out_shape=jax.ShapeDtypeStruct((t, d), F32),
        scratch_shapes=[pltpu.VMEM((tm, d), F32), pltpu.VMEM((tm, d), BF16), pltpu.VMEM((tm, d), F32)],
        compiler_params=_cp(("parallel", "arbitrary")),
        name="even_out",
    )(x, oa, ob, woa, wob, ln, wg, wu, wd)


def _rope_lanes(v, c, s):
    n = v.shape[1]
    lane = lax.broadcasted_iota(I32, (1, n), 1) % LANES
    sw = jnp.where(lane < MLA_ROPE // 2, pltpu.roll(v, n - MLA_ROPE // 2, axis=1), pltpu.roll(v, MLA_ROPE // 2, axis=1))
    return v * c + sw * s


def _odd_in_kernel(x_ref, ln_ref, wxb_ref, wgb_ref, wcq_ref, wckv_ref, wkpe_ref, qn_ref, kvn_ref, wuq_ref,
                   cos_ref, sin_ref, *rest, absorbed):
    if absorbed:
        wabs_ref, xb_ref, gb_ref, q_ref, qlat_ref, rows_ref = rest
    else:
        wuk_ref, wuv_ref, xb_ref, gb_ref, q_ref, k_ref, v_ref, rows_ref = rest
    h = _rms(x_ref[...], ln_ref[...]).astype(BF16)
    xb_ref[...] = _mm(h, wxb_ref[...])
    gb_ref[...] = _mm(h, wgb_ref[...])
    cq = _mm(h, wcq_ref[...])
    ckv = _mm(h, wckv_ref[...])
    kpe = _mm(h, wkpe_ref[...])
    cqn = _rms(cq, qn_ref[...]).astype(BF16)
    ckvn = _rms(ckv, kvn_ref[...])
    c128 = cos_ref[...]
    s128 = sin_ref[...]
    nh = MLA_HEADS
    q = _rope_lanes(_mm(cqn, wuq_ref[...]), jnp.concatenate([c128] * nh, axis=1), jnp.concatenate([s128] * nh, axis=1))
    kpe_r = _rope_lanes(kpe, c128, s128)
    qb = q.astype(BF16)
    q_ref[...] = qb
    rows_ref[:, 0:MLA_KV_RANK] = ckvn
    rows_ref[:, MLA_KV_RANK:MLA_ROW] = kpe_r[:, 0:MLA_ROPE]
    if absorbed:
        qlat_ref[...] = _mm(qb, wabs_ref[...])
    else:
        ckvb = ckvn.astype(BF16)
        k_ref[...] = (_mm(ckvb, wuk_ref[...]) + jnp.concatenate([kpe_r] * nh, axis=1)).astype(BF16)
        v_ref[...] = _mm(ckvb, wuv_ref[...]).astype(BF16)


def odd_in(x, ln, ws, qn, kvn, wuq, cos_t, sin_t, extra, absorbed):
    t, d = x.shape
    tm = _row_tile(t, (512, 256, 128, 64, 32, 16, 8))
    nblk = cos_t.shape[0] // tm
    hb = MLA_HEADS * LANES
    row = lambda w: pl.BlockSpec((tm, w), lambda i: (i, 0))
    tbl = pl.BlockSpec((tm, LANES), lambda i: (i % nblk, 0))
    if absorbed:
        outs = [(LRU_WIDTH, F32), (LRU_WIDTH, F32), (hb, BF16), (extra[0].shape[1], F32), (MLA_ROW, F32)]
    else:
        outs = [(LRU_WIDTH, F32), (LRU_WIDTH, F32), (hb, BF16), (hb, BF16), (hb, BF16), (MLA_ROW, F32)]
    return pl.pallas_call(
        functools.partial(_odd_in_kernel, absorbed=absorbed),
        grid=(t // tm,),
        in_specs=[row(d), _full((1, d))] + [_full(w.shape) for w in ws]
        + [_full(qn.shape), _full(kvn.shape), _full(wuq.shape), tbl, tbl] + [_full(w.shape) for w in extra],
        out_specs=[row(w) for w, _ in outs],
        out_shape=[jax.ShapeDtypeStruct((t, w), dt) for w, dt in outs],
        compiler_params=_cp(("parallel",)),
        name="odd_in",
    )(x, ln, *ws, qn, kvn, wuq, cos_t, sin_t, *extra)


def _expm1(x):
    u = jnp.exp(x)
    um1 = u - 1.0
    small = um1 * x / jnp.log(u)
    return jnp.where(um1 == 0.0, x, jnp.where(jnp.abs(x) < 0.5, small, um1))


def _lru_gates(xc, wa_ref, wx_ref, ba_ref, bx_ref, lam_ref):
    xcb = xc.astype(BF16)
    r = jax.nn.sigmoid(_mm(xcb, wa_ref[...]) + ba_ref[...])
    i = jax.nn.sigmoid(_mm(xcb, wx_ref[...]) + bx_ref[...])
    log_a = -LRU_C * r * jax.nn.softplus(-lam_ref[...])
    a = jnp.exp(log_a)
    b = jnp.sqrt(-_expm1(2.0 * log_a)) * (i * xc)
    return a, b


def _lru_prompt_kernel(x_ref, g_ref, cprev_ref, h0_ref, cw_ref, cb_ref, wa_ref, wx_ref, ba_ref, bx_ref, lam_ref,
                       y_ref, hl_ref, xbuf, h_sc):
    t = pl.program_id(1)
    tl = x_ref.shape[1]

    @pl.when(t == 0)
    def _():
        xbuf[0:SUBLANES, :] = cprev_ref[0]
        h_sc[0:1, :] = h0_ref[0]

    xbuf[SUBLANES:SUBLANES + tl, :] = x_ref[0]
    cw = cw_ref[...]
    xc = xbuf[5:5 + tl, :] * cw[0:1, :]
    for j in range(1, cw.shape[0]):
        xc = xc + xbuf[5 + j:5 + j + tl, :] * cw[j:j + 1, :]
    xc = xc + cb_ref[...]
    xbuf[0:SUBLANES, :] = xbuf[tl:tl + SUBLANES, :]
    a, b = _lru_gates(xc, wa_ref, wx_ref, ba_ref, bx_ref, lam_ref)
    row = lax.broadcasted_iota(I32, (tl, 1), 0)
    s = 1
    while s < tl:
        m = row >= s
        b = jnp.where(m, a * pltpu.roll(b, s, axis=0) + b, b)
        a = jnp.where(m, a * pltpu.roll(a, s, axis=0), a)
        s *= 2
    hs = a * h_sc[0:1, :] + b
    h_sc[0:1, :] = hs[tl - 1:tl, :]
    y_ref[0] = hs * jax.nn.gelu(g_ref[0])
    hl_ref[0] = hs[tl - 1:tl, :]


def lru_prompt(xb, gb, cprev, h0, cw, cb, wa, wx, ba, bx, lam, tl):
    n, l, c = xb.shape
    seq = pl.BlockSpec((1, tl, c), lambda i, t: (i, t, 0))
    return pl.pallas_call(
        _lru_prompt_kernel,
        grid=(n, l // tl),
        in_specs=[seq, seq, pl.BlockSpec((1, SUBLANES, c), lambda i, t: (i, 0, 0)),
                  pl.BlockSpec((1, 1, c), lambda i, t: (i, 0, 0)),
                  _full(cw.shape), _full((1, c)), _full(wa.shape), _full(wx.shape), _full((1, c)), _full((1, c)),
                  _full((1, c))],
        out_specs=[seq, pl.BlockSpec((1, 1, c), lambda i, t: (i, 0, 0))],
        out_shape=[jax.ShapeDtypeStruct((n, l, c), F32), jax.ShapeDtypeStruct((n, 1, c), F32)],
        scratch_shapes=[pltpu.VMEM((tl + SUBLANES, c), F32), pltpu.VMEM((SUBLANES, c), F32)],
        compiler_params=_cp(("parallel", "arbitrary")),
        name="lru_prompt",
    )(xb, gb, cprev, h0, cw, cb, wa, wx, ba, bx, lam)


def _lru_sample_kernel(x_ref, g_ref, cprev_ref, h0_ref, cw_ref, cb_ref, wa_ref, wx_ref, ba_ref, bx_ref, lam_ref,
                       y_ref, hl_ref):
    l = x_ref.shape[0]
    cw = cw_ref[...]
    nw = cw.shape[0]
    xx = [cprev_ref[j] for j in range(nw - 1)] + [x_ref[j] for j in range(l)]
    h = h0_ref[...]
    for i in range(l):
        xc = xx[i] * cw[0:1, :]
        for j in range(1, nw):
            xc = xc + xx[i + j] * cw[j:j + 1, :]
        xc = xc + cb_ref[...]
        a, b = _lru_gates(xc, wa_ref, wx_ref, ba_ref, bx_ref, lam_ref)
        h = a * h + b
        y_ref[i] = h * jax.nn.gelu(g_ref[i])
    hl_ref[...] = h


def lru_sample(xb, gb, cprev, h0, cw, cb, wa, wx, ba, bx, lam):
    l, n, c = xb.shape
    return pl.pallas_call(
        _lru_sample_kernel,
        out_shape=[jax.ShapeDtypeStruct((l, n, c), F32), jax.ShapeDtypeStruct((n, c), F32)],
        compiler_params=pltpu.CompilerParams(vmem_limit_bytes=VMEM_LIMIT),
        name="lru_sample",
    )(xb, gb, cprev, h0, cw, cb, wa, wx, ba, bx, lam)


MLA_HEADS_PER_STEP = 2


def _mla_prompt_kernel(q_ref, k_ref, v_ref, o_ref, m_sc, l_sc, acc_sc, *, scale):
    qi = pl.program_id(2)
    tq = q_ref.shape[0]
    tk = tq
    g = q_ref.shape[1] // LANES
    c1 = scale * math.log2(math.e)
    m_sc[...] = jnp.full_like(m_sc, -jnp.inf)
    l_sc[...] = jnp.zeros_like(l_sc)
    acc_sc[...] = jnp.zeros_like(acc_sc)
    on_or_below = lax.broadcasted_iota(I32, (tq, tk), 1) <= lax.broadcasted_iota(I32, (tq, tk), 0)

    def step(ki, masked):
        rows = pl.ds(pl.multiple_of(ki * tk, tk), tk)
        for h in range(g):
            hs = slice(h * LANES, (h + 1) * LANES)
            s = _mm(q_ref[:, hs], k_ref[rows, hs], NT) * c1
            if masked:
                s = jnp.where(on_or_below, s, -jnp.inf)
            m_prev = m_sc[h]
            m_new = jnp.maximum(m_prev, jnp.max(s, axis=-1, keepdims=True))
            alpha = jnp.exp2(m_prev - m_new)
            p = jnp.exp2(s - jnp.concatenate([m_new] * (tk // LANES), axis=1))
            l_sc[h] = alpha * l_sc[h] + jnp.sum(p, axis=-1, keepdims=True)
            acc_sc[:, hs] = alpha * acc_sc[:, hs] + _mm(p.astype(BF16), v_ref[rows, hs])
            m_sc[h] = m_new

    def body(ki, carry):
        step(ki, False)
        return carry

    lax.fori_loop(0, qi, body, 0)
    step(qi, True)
    for h in range(g):
        hs = slice(h * LANES, (h + 1) * LANES)
        o_ref[:, hs] = (acc_sc[:, hs] / l_sc[h]).astype(o_ref.dtype)


def mla_prompt(q, k, v, n, l, tq):
    nq = l // tq
    g = MLA_HEADS_PER_STEP
    w = g * LANES
    scale = (MLA_NOPE + MLA_ROPE) ** -0.5
    return pl.pallas_call(
        functools.partial(_mla_prompt_kernel, scale=scale),
        grid=(n, MLA_HEADS // g, nq),
        in_specs=[pl.BlockSpec((tq, w), lambda i, h, j: (i * nq + j, h)),
                  pl.BlockSpec((l, w), lambda i, h, j: (i, h)), pl.BlockSpec((l, w), lambda i, h, j: (i, h))],
        out_specs=pl.BlockSpec((tq, w), lambda i, h, j: (i * nq + j, h)),
        out_shape=jax.ShapeDtypeStruct(q.shape, BF16),
        scratch_shapes=[pltpu.VMEM((g, tq, LANES), F32), pltpu.VMEM((g, tq, LANES), F32), pltpu.VMEM((tq, w), F32)],
        compiler_params=_cp(("parallel", "parallel", "arbitrary")),
        name="mla_prompt",
    )(q, k, v)


def _mla_sample_kernel(pt_ref, q_ref, rows_ref, *rest, n_pages_step, l, scale):
    page_refs = rest[:n_pages_step]
    o_ref, m_sc, l_sc, acc_sc = rest[n_pages_step:]
    j = pl.program_id(1)

    @pl.when(j == 0)
    def _():
        m_sc[...] = jnp.full_like(m_sc, -jnp.inf)
        l_sc[...] = jnp.zeros_like(l_sc)
        acc_sc[...] = jnp.zeros_like(acc_sc)

    q = q_ref[0]
    kt = jnp.concatenate([r[0].astype(BF16) for r in page_refs], axis=1)
    s = _mm(q, kt) * scale
    m_prev = m_sc[...]
    m_new = jnp.maximum(m_prev, jnp.max(s, axis=-1, keepdims=True))
    alpha = jnp.exp(m_prev - m_new)
    p32 = jnp.exp(s - m_new)
    l_sc[...] = alpha * l_sc[...] + jnp.sum(p32, axis=-1, keepdims=True)
    acc_sc[...] = alpha * acc_sc[...] + _mm(p32.astype(BF16), kt, NT)
    m_sc[...] = m_new

    @pl.when(j == pl.num_programs(1) - 1)
    def _():
        qf = q.astype(F32)
        rows = rows_ref[0]
        tok = lax.broadcasted_iota(I32, (qf.shape[0], 1), 0) // MLA_HEADS
        sn = []
        for mm in range(l):
            sm = jnp.sum(qf * rows[mm:mm + 1, :], axis=-1, keepdims=True) * scale
            sn.append(jnp.where(tok >= mm, sm, -jnp.inf))
        m_old = m_sc[...]
        m_fin = m_old
        for sm in sn:
            m_fin = jnp.maximum(m_fin, sm)
        al = jnp.exp(m_old - m_fin)
        lsum = al * l_sc[...]
        acc = al * acc_sc[...]
        for mm in range(l):
            pm = jnp.exp(sn[mm] - m_fin)
            lsum = lsum + pm
            acc = acc + pm * rows[mm:mm + 1, :]
        o_ref[0] = acc / lsum


def mla_sample(page_table, q_cat, rows, cache_t, n_pages_step):
    n, r, c = q_cat.shape
    l = rows.shape[1]
    n_pages = page_table.shape[1]
    steps = n_pages // n_pages_step
    scale = (MLA_NOPE + MLA_ROPE) ** -0.5
    pt = page_table.reshape(-1)

    def page_map(k):
        return lambda i, j, pt_ref: (pt_ref[i * n_pages + j * n_pages_step + k], 0, 0)

    return pl.pallas_call(
        functools.partial(_mla_sample_kernel, n_pages_step=n_pages_step, l=l, scale=scale),
        grid_spec=pltpu.PrefetchScalarGridSpec(
            num_scalar_prefetch=1,
            grid=(n, steps),
            in_specs=[pl.BlockSpec((1, r, c), lambda i, j, pt_ref: (i, 0, 0)),
                      pl.BlockSpec((1, l, c), lambda i, j, pt_ref: (i, 0, 0))]
            + [pl.BlockSpec((1, c, PAGE_SIZE), page_map(k)) for k in range(n_pages_step)],
            out_specs=pl.BlockSpec((1, r, c), lambda i, j, pt_ref: (i, 0, 0)),
            scratch_shapes=[pltpu.VMEM((r, 1), F32), pltpu.VMEM((r, 1), F32), pltpu.VMEM((r, c), F32)],
        ),
        out_shape=jax.ShapeDtypeStruct((n, r, c), F32),
        compiler_params=_cp(("parallel", "arbitrary")),
        name="mla_sample",
    )(pt, q_cat, rows, *([cache_t] * n_pages_step))


def _matmul_kernel(x_ref, w_ref, o_ref):
    o_ref[...] = _mm(x_ref[...].astype(BF16), w_ref[...]).astype(o_ref.dtype)


def matmul(x, w, out_dtype):
    t, kd = x.shape
    tm = _row_tile(t, (512, 256, 128, 64, 32, 16, 8))
    return pl.pallas_call(
        _matmul_kernel,
        grid=(t // tm,),
        in_specs=[pl.BlockSpec((tm, kd), lambda i: (i, 0)), _full(w.shape)],
        out_specs=pl.BlockSpec((tm, w.shape[1]), lambda i: (i, 0)),
        out_shape=jax.ShapeDtypeStruct((t, w.shape[1]), out_dtype),
        compiler_params=_cp(("parallel",)),
        name="matmul",
    )(x, w)


def _odd_out_kernel(x_ref, yc_ref, od_ref, woc_ref, wod_ref, ln_ref, rw_ref, rb_ref, *rest):
    x1_ref, hn_ref, idx_ref, gate_ref = rest[-4:]
    x1 = x_ref[...] + (_mm(yc_ref[...].astype(BF16), woc_ref[...]) + _mm(od_ref[...], wod_ref[...]))
    x1_ref[...] = x1
    hn = _rms(x1, ln_ref[...])
    hn_ref[...] = hn.astype(BF16)
    logits = _mm(hn, rw_ref[...], NN, HI) + rb_ref[...]
    lane = lax.broadcasted_iota(I32, logits.shape, 1)
    logits = jnp.where(lane < N_EXPERTS, logits, -jnp.inf)
    m1 = jnp.max(logits, axis=-1, keepdims=True)
    i1 = jnp.min(jnp.where(logits == m1, lane, LANES), axis=-1, keepdims=True)
    l2 = jnp.where(lane == i1, -jnp.inf, logits)
    m2 = jnp.max(l2, axis=-1, keepdims=True)
    i2 = jnp.min(jnp.where(l2 == m2, lane, LANES), axis=-1, keepdims=True)
    e2 = jnp.exp(m2 - m1)
    den = 1.0 + e2
    idx_ref[...] = jnp.where(lane == 0, i1, jnp.where(lane == 1, i2, 0))
    gate_ref[...] = jnp.where(lane == 0, 1.0 / den, jnp.where(lane == 1, e2 / den, 0.0))


def odd_out(x, yc, od, woc, wod, ln, rw, rb, hn_rows, hn_row0, hn_buf=None):
    t, d = x.shape
    tm = _row_tile(t, (512, 256, 128, 64, 32, 16, 8))
    assert hn_row0 % tm == 0
    off = hn_row0 // tm
    row = lambda w: pl.BlockSpec((tm, w), lambda i: (i, 0))
    args = (x, yc, od, woc, wod, ln, rw, rb) + (() if hn_buf is None else (hn_buf,))
    return pl.pallas_call(
        _odd_out_kernel,
        grid=(t // tm,),
        in_specs=[row(d), row(yc.shape[1]), row(od.shape[1]), _full(woc.shape), _full(wod.shape), _full((1, d)),
                  _full(rw.shape), _full(rb.shape)] + ([] if hn_buf is None else [pl.BlockSpec(memory_space=pl.ANY)]),
        out_specs=[row(d), pl.BlockSpec((tm, d), lambda i: (i + off, 0)), row(LANES), row(LANES)],
        out_shape=[jax.ShapeDtypeStruct((t, d), F32), jax.ShapeDtypeStruct((hn_rows, d), BF16),
                   jax.ShapeDtypeStruct((t, LANES), I32), jax.ShapeDtypeStruct((t, LANES), F32)],
        input_output_aliases={} if hn_buf is None else {8: 1},
        compiler_params=_cp(("parallel",)),
        name="odd_out",
    )(*args)


EXPERT_ROWS = 512
EXPERT_FF_CHUNK = 512
MOE_SUB = 128
F_VALID, F_FIRST, F_LAST = 1, 2, 4


def _moe_expert_kernel(pb_ref, pj_ref, fl_ref, lo_ref, hi_ref, be_ref, slot_ref, gate_ref, hn_ref,
                       wg_ref, wu_ref, wd_ref, o_ref, xacc, gacc):
    p = pl.program_id(0)
    fl = fl_ref[p]
    er = xacc.shape[0]
    tile = hn_ref.shape[0]

    @pl.when((fl & F_VALID) != 0)
    def _():
        @pl.when((fl & F_FIRST) != 0)
        def _():
            xacc[...] = jnp.zeros_like(xacc)
            gacc[...] = jnp.zeros_like(gacc)

        for sb in range(er // MOE_SUB):
            @pl.when(jnp.logical_and(lo_ref[p] <= sb, sb <= hi_ref[p]))
            def _(sb=sb):
                rs = slice(sb * MOE_SUB, (sb + 1) * MOE_SUB)
                srow = pb_ref[p] * er + sb * MOE_SUB + lax.broadcasted_iota(I32, (MOE_SUB, tile), 0)
                m0 = slot_ref[0:1, :] == srow
                m1 = slot_ref[1:2, :] == srow
                sel = jnp.where(m0, 1.0, jnp.where(m1, 1.0, 0.0)).astype(BF16)
                xacc[rs, :] += _mm(sel, hn_ref[...])
                g = jnp.where(m0, gate_ref[0:1, :], jnp.where(m1, gate_ref[1:2, :], 0.0))
                gacc[rs, :] += jnp.sum(g, axis=1, keepdims=True)

        @pl.when((fl & F_LAST) != 0)
        def _():
            x = xacc[...].astype(BF16)
            acc = None
            for f in range(0, wg_ref.shape[2], EXPERT_FF_CHUNK):
                gg = _mm(x, wg_ref[0, :, f:f + EXPERT_FF_CHUNK])
                uu = _mm(x, wu_ref[0, :, f:f + EXPERT_FF_CHUNK])
                y = _mm((_silu(gg) * uu).astype(BF16), wd_ref[0, f:f + EXPERT_FF_CHUNK, :])
                acc = y if acc is None else acc + y
            o_ref[...] = (acc * gacc[...]).astype(o_ref.dtype)


def moe_experts(plan, slot_t, gate_t, hn, wg, wu, wd, tile):
    pb, pj, fl, lo, hi, blk_e, n_blocks = plan
    d = hn.shape[1]
    ff = wg.shape[2]
    wmap = lambda p, pb, pj, fl, lo, hi, be: (be[pb[p]], 0, 0)
    tmap = lambda p, pb, pj, fl, lo, hi, be: (0, pj[p])
    return pl.pallas_call(
        _moe_expert_kernel,
        grid_spec=pltpu.PrefetchScalarGridSpec(
            num_scalar_prefetch=6,
            grid=(pb.shape[0],),
            in_specs=[pl.BlockSpec((TOP_K, tile), tmap), pl.BlockSpec((TOP_K, tile), tmap),
                      pl.BlockSpec((tile, d), lambda p, pb, pj, fl, lo, hi, be: (pj[p], 0)),
                      pl.BlockSpec((1, d, ff), wmap), pl.BlockSpec((1, d, ff), wmap), pl.BlockSpec((1, ff, d), wmap)],
            out_specs=pl.BlockSpec((EXPERT_ROWS, d), lambda p, pb, pj, fl, lo, hi, be: (pb[p], 0)),
            scratch_shapes=[pltpu.VMEM((EXPERT_ROWS, d), F32), pltpu.VMEM((EXPERT_ROWS, 1), F32)],
        ),
        out_shape=jax.ShapeDtypeStruct((n_blocks * EXPERT_ROWS, d), BF16),
        compiler_params=_cp(("arbitrary",)),
        name="moe_experts",
    )(pb, pj, fl, lo, hi, blk_e, slot_t, gate_t, hn, wg, wu, wd)


def _moe_combine_kernel(qj_ref, qb_ref, fl_ref, lo_ref, hi_ref, slot_ref, x1_ref, yb_ref, fn_ref, o_ref, yacc):
    p = pl.program_id(0)
    fl = fl_ref[p]
    tile = x1_ref.shape[0]
    er = yb_ref.shape[0]

    @pl.when((fl & F_VALID) != 0)
    def _():
        @pl.when((fl & F_FIRST) != 0)
        def _():
            yacc[...] = jnp.zeros_like(yacc)

        for sb in range(er // MOE_SUB):
            @pl.when(jnp.logical_and(lo_ref[p] <= sb, sb <= hi_ref[p]))
            def _(sb=sb):
                scol = qb_ref[p] * er + sb * MOE_SUB + lax.broadcasted_iota(I32, (tile, MOE_SUB), 1)
                sl = slot_ref[...]
                sel = jnp.where(sl[:, 0:1] == scol, 1.0, jnp.where(sl[:, 1:2] == scol, 1.0, 0.0)).astype(BF16)
                yacc[...] += _mm(sel, yb_ref[sb * MOE_SUB:(sb + 1) * MOE_SUB, :])

        @pl.when((fl & F_LAST) != 0)
        def _():
            o_ref[...] = _rms(x1_ref[...] + yacc[...], fn_ref[...])


def moe_combine(sched, slot_cols, x1, yb, fnorm, tile):
    qj, qb, fl, lo, hi = sched
    t, d = x1.shape
    tmap = lambda p, qj, qb, fl, lo, hi: (qj[p], 0)
    return pl.pallas_call(
        _moe_combine_kernel,
        grid_spec=pltpu.PrefetchScalarGridSpec(
            num_scalar_prefetch=5,
            grid=(qj.shape[0],),
            in_specs=[pl.BlockSpec((tile, LANES), tmap), pl.BlockSpec((tile, d), tmap),
                      pl.BlockSpec((EXPERT_ROWS, d), lambda p, qj, qb, fl, lo, hi: (qb[p], 0)),
                      pl.BlockSpec((1, d), lambda p, qj, qb, fl, lo, hi: (0, 0))],
            out_specs=pl.BlockSpec((tile, d), tmap),
            scratch_shapes=[pltpu.VMEM((tile, d), F32)],
        ),
        out_shape=jax.ShapeDtypeStruct((t, d), F32),
        compiler_params=_cp(("arbitrary",)),
        name="moe_combine",
    )(qj, qb, fl, lo, hi, slot_cols, x1, yb, fnorm)


def _ragged_steps(counts, pmax):
    cum = jnp.cumsum(counts)
    total = cum[-1]
    ar = jnp.arange(pmax, dtype=I32)
    p = jnp.minimum(ar, jnp.maximum(total - 1, 0))
    row = jnp.minimum(jnp.sum((cum[None, :] <= p[:, None]).astype(I32), axis=1), counts.shape[0] - 1)
    off = p - (cum[row] - counts[row])
    return row, off, ar < total, total


def _group_flags(gid, valid, total):
    pmax = gid.shape[0]
    ar = jnp.arange(pmax, dtype=I32)
    prev = jnp.concatenate([gid[:1] - 1, gid[:-1]])
    nxt = jnp.concatenate([gid[1:], gid[-1:] + 1])
    first = jnp.logical_or(ar == 0, gid != prev)
    last = jnp.logical_or(ar == total - 1, gid != nxt)
    fl = F_VALID + F_FIRST * first.astype(I32) + F_LAST * last.astype(I32)
    return jnp.where(valid, fl, 0).astype(I32)


def _moe_plan(e_all, tile):
    t = e_all.shape[0]
    nt = t // tile
    er = EXPERT_ROWS
    ex = jnp.arange(N_EXPERTS, dtype=I32)[None, :]
    oh = jnp.logical_or(e_all[:, 0:1] == ex, e_all[:, 1:2] == ex).astype(I32)
    cs = jnp.cumsum(oh, axis=0)
    rank = cs - oh
    counts = cs[-1]
    padded = (counts + er - 1) // er * er
    pad_end = jnp.cumsum(padded)
    pad_start = pad_end - padded
    slot_e = pad_start[None, :] + rank
    slot = jnp.stack([jnp.sum(jnp.where(e_all[:, k:k + 1] == ex, slot_e, 0), axis=1) for k in range(TOP_K)],
                     axis=1).astype(I32)
    n_blocks = -(-(t * TOP_K) // er) + N_EXPERTS
    blk = jnp.arange(n_blocks, dtype=I32)
    blk_e = jnp.minimum(jnp.sum((blk[:, None] * er >= pad_end[None, :]).astype(I32), axis=1), N_EXPERTS - 1)
    n_used = pad_end[-1] // er
    tstart = jnp.concatenate([rank[::tile], counts[None, :]], axis=0)
    first_slot = pad_start[None, :] + tstart[:-1]
    last_slot = pad_start[None, :] + tstart[1:] - 1

    def sub_range(b, fs, ls):
        lo = jnp.maximum(fs - b * er, 0) // MOE_SUB
        hi = jnp.where(ls >= fs, jnp.minimum(ls - b * er, er - 1), -1) // MOE_SUB
        return lo.astype(I32), jnp.maximum(hi, -1).astype(I32)

    be_oh = (blk_e[:, None] == ex).astype(I32)
    r0 = blk * er - jnp.sum(be_oh * pad_start[None, :], axis=1)
    r1 = jnp.minimum(r0 + er - 1, jnp.sum(be_oh * counts[None, :], axis=1) - 1)
    ts_b = jnp.sum(be_oh[:, None, :] * tstart[None, 1:, :], axis=2)
    jlo = jnp.minimum(jnp.sum((ts_b <= r0[:, None]).astype(I32), axis=1), nt - 1)
    jhi = jnp.minimum(jnp.sum((ts_b <= r1[:, None]).astype(I32), axis=1), nt - 1)
    cnt_b = jnp.where(blk < n_used, jhi - jlo + 1, 0)
    row, off, valid, total = _ragged_steps(cnt_b, n_blocks + nt * N_EXPERTS)
    pj = (jlo[row] + off).astype(I32)
    je = pj * N_EXPERTS + blk_e[row]
    lo, hi = sub_range(row, first_slot.reshape(-1)[je], last_slot.reshape(-1)[je])
    expert_plan = (row, pj, _group_flags(row, valid, total), lo, hi, blk_e.astype(I32), n_blocks)

    def combine_sched(j0, j1):
        ntr = j1 - j0
        fs = first_slot[j0:j1].reshape(-1)
        ls = last_slot[j0:j1].reshape(-1)
        blo = fs // er
        nb = jnp.where(ls >= fs, ls // er - blo + 1, 0)
        pmax = min(n_blocks + ntr * N_EXPERTS, 2 * ntr * N_EXPERTS)
        r, o, v, tot = _ragged_steps(nb, pmax)
        qj = (r // N_EXPERTS).astype(I32)
        qb = (blo[r] + o).astype(I32)
        lo, hi = sub_range(qb, fs[r], ls[r])
        return qj, qb, _group_flags(qj, v, tot), lo, hi

    return slot, expert_plan, combine_sched


def _rope_tables(pos):
    half = MLA_ROPE // 2
    inv = ROPE_THETA ** (-jnp.arange(half, dtype=F32) * 2.0 / MLA_ROPE)
    ang = pos.astype(F32)[:, None] * inv[None, :]
    cos = jnp.cos(ang)
    sin = jnp.sin(ang)
    n = pos.shape[0]
    cos_t = jnp.concatenate([cos, cos, jnp.ones((n, LANES - MLA_ROPE), F32)], axis=1)
    sin_t = jnp.concatenate([-sin, sin, jnp.zeros((n, LANES - MLA_ROPE), F32)], axis=1)
    return cos_t, sin_t


def _block_diag(w):
    b, i, j = w.shape
    eye = jnp.eye(b, dtype=w.dtype)
    return (eye[:, None, :, None] * w[:, :, None, :]).reshape(b * i, b * j)


def _pad_rows(a, rows):
    return jnp.pad(a, ((0, 0), (0, rows - a.shape[1]), (0, 0)))


def kernel(x_prompt, x_sample, state_swa_kv, state_gdn_conv, state_gdn_s, state_lru_conv, state_lru_h, cache_mla, page_table, e_ln_mix, e_w_in, e_gdn_conv_w, e_gdn_a_log, e_gdn_dt_bias, e_gdn_norm, e_swa_sinks, e_w_out, e_ln_ffn, e_ffn_gate, e_ffn_up, e_ffn_down, o_ln_mix, o_w_in, o_lru_conv_w, o_lru_conv_b, o_lru_w_a, o_lru_b_a, o_lru_w_x, o_lru_b_x, o_lru_lambda, o_mla_q_norm, o_mla_w_uq, o_mla_kv_norm, o_mla_w_uk, o_mla_w_uv, o_w_out, o_ln_ffn, o_router_w, o_router_b, o_exp_gate, o_exp_up, o_exp_down, final_norm):
    nb, lp, d = x_prompt.shape
    ns, ls, _ = x_sample.shape
    past_len = page_table.shape[1] * PAGE_SIZE
    tp, ts = nb * lp, ns * ls
    xp = x_prompt.reshape(tp, d)
    xs = x_sample.reshape(ts, d)
    row1 = lambda v: v.reshape(1, -1)

    na = (SWA_HEADS + 2 * SWA_KV_HEADS) * SWA_HEAD_DIM
    nz = GDN_HEADS * GDN_DV
    w_in = e_w_in[0].astype(BF16)
    o1, o2, o3 = na, na + GDN_QKV, na + GDN_QKV + nz
    w_groups = [w_in[:, :o1], w_in[:, o1:o2], w_in[:, o2:o3],
                jnp.pad(w_in[:, o3:], ((0, 0), (0, LANES - 2 * GDN_HEADS)))]
    ln = row1(e_ln_mix[0])
    qkv_p, gx_p, z_p, bg_p = norm_proj(xp, ln, w_groups)
    qkv_s, gx_s, z_s, bg_s = norm_proj(xs, ln, w_groups)

    sinks = e_swa_sinks[0]
    oa_p = swa_prompt(qkv_p, sinks, nb, lp)
    wbuf = state_swa_kv.shape[2]
    nkv = 2 * SWA_KV_HEADS * SWA_HEAD_DIM
    buf = state_swa_kv[0].reshape(ns, wbuf, nkv)
    qkv_s3 = qkv_s.reshape(ns, ls, na)
    oa_s = swa_sample(_pad_rows(qkv_s3, SUBLANES), buf, sinks, ls)[:, :ls].reshape(ts, -1)
    kv_p = qkv_p.reshape(nb, lp, na)[:, lp - SWA_WINDOW:, na - nkv:]
    swa_kv_p = kv_p.reshape(1, nb, SWA_WINDOW, 2, SWA_KV_HEADS, SWA_HEAD_DIM)
    kv_s = jnp.concatenate([buf, qkv_s3[:, :, na - nkv:]], axis=1)[:, ls:]
    swa_kv_s = kv_s.reshape(1, ns, wbuf, 2, SWA_KV_HEADS, SWA_HEAD_DIM)

    gx_p3 = gx_p.reshape(nb, lp, GDN_QKV)
    gx_s3 = gx_s.reshape(ns, ls, GDN_QKV)
    gargs = (e_gdn_conv_w[0], e_gdn_a_log[0], e_gdn_dt_bias[0], e_gdn_norm[0])
    ob_p, gs_p = gdn(gx_p3, z_p.reshape(nb, lp, nz), bg_p.reshape(nb, lp, LANES),
                     jnp.zeros((nb, SUBLANES, GDN_QKV), F32), jnp.zeros((nb, GDN_HEADS, GDN_DK, GDN_DV), F32),
                     *gargs, chunk=math.gcd(lp, GDN_CHUNK), valid=math.gcd(lp, GDN_CHUNK), tl=min(lp, 512))
    cprev_s = jnp.pad(state_gdn_conv[0], ((0, 0), (SUBLANES - (GDN_CONV - 1), 0), (0, 0)))
    assert ls <= GDN_CHUNK and math.gcd(ls, GDN_CHUNK) == ls
    ob_s, gs_s = gdn(_pad_rows(gx_s3, GDN_CHUNK), _pad_rows(z_s.reshape(ns, ls, nz), GDN_CHUNK),
                     _pad_rows(bg_s.reshape(ns, ls, LANES), GDN_CHUNK), cprev_s, state_gdn_s[0],
                     *gargs, chunk=GDN_CHUNK, valid=ls, tl=GDN_CHUNK)
    ob_s = ob_s[:, :ls].reshape(ts, nz)
    gconv_p = gx_p3[:, lp - (GDN_CONV - 1):][None]
    gconv_s = gx_s3[:, ls - (GDN_CONV - 1):][None]

    w_out = e_w_out[0].astype(BF16)
    nqa = SWA_HEADS * SWA_HEAD_DIM
    ffn = (row1(e_ln_ffn[0]), e_ffn_gate[0].astype(BF16), e_ffn_up[0].astype(BF16), e_ffn_down[0].astype(BF16))
    xp = even_out(xp, oa_p, ob_p.reshape(tp, nz), w_out[:nqa], w_out[nqa:], *ffn)
    xs = even_out(xs, oa_s, ob_s, w_out[:nqa], w_out[nqa:], *ffn)

    w_in = o_w_in[0].astype(BF16)
    c0 = LRU_WIDTH
    c1 = 2 * LRU_WIDTH
    c2 = c1 + MLA_Q_RANK
    c3 = c2 + MLA_KV_RANK
    w_groups = [w_in[:, :c0], w_in[:, c0:c1], w_in[:, c1:c2], w_in[:, c2:c3],
                jnp.pad(w_in[:, c3:], ((0, 0), (0, LANES - MLA_ROPE)))]
    hd_q = MLA_NOPE + MLA_ROPE
    half = MLA_ROPE // 2
    wq = o_mla_w_uq[0].reshape(MLA_Q_RANK, MLA_HEADS, hd_q)
    wuq = jnp.concatenate([wq[:, :, MLA_NOPE:], wq[:, :, :MLA_NOPE],
                           jnp.zeros((MLA_Q_RANK, MLA_HEADS, LANES - hd_q), F32)], axis=2)
    wuq = wuq.reshape(MLA_Q_RANK, MLA_HEADS * LANES).astype(BF16)
    w_uk = o_mla_w_uk[0]
    w_uv = o_mla_w_uv[0]
    wuk = jnp.pad(w_uk, ((0, 0), (0, 0), (MLA_ROPE, LANES - hd_q))).reshape(MLA_KV_RANK, -1).astype(BF16)
    wuv = jnp.pad(w_uv, ((0, 0), (0, 0), (0, LANES - MLA_V))).reshape(MLA_KV_RANK, -1).astype(BF16)
    wabs = _block_diag(jnp.pad(jnp.transpose(w_uk, (1, 2, 0)), ((0, 0), (MLA_ROPE, LANES - hd_q), (0, 0)))).astype(BF16)
    wuv_bd = _block_diag(jnp.pad(jnp.transpose(w_uv, (1, 0, 2)), ((0, 0), (0, 0), (0, LANES - MLA_V)))).astype(BF16)
    ln = row1(o_ln_mix[0])
    qn, kvn = row1(o_mla_q_norm[0]), row1(o_mla_kv_norm[0])
    tm_p = _row_tile(tp, (512, 256, 128, 64, 32, 16, 8))
    assert lp % tm_p == 0 or tm_p % lp == 0
    pos_p = jnp.arange(max(lp, tm_p), dtype=I32) % lp
    pos_s = past_len + (jnp.arange(ts, dtype=I32) % ls)
    cos_p, sin_p = _rope_tables(pos_p)
    cos_s, sin_s = _rope_tables(pos_s)
    xb_p, gb_p, q_p, k_p, v_p, rows_p = odd_in(xp, ln, w_groups, qn, kvn, wuq, cos_p, sin_p, [wuk, wuv], False)
    xb_s, gb_s, q_s, qlat_s, rows_s = odd_in(xs, ln, w_groups, qn, kvn, wuq, cos_s, sin_s, [wabs], True)

    lru_w = (o_lru_conv_w[0], row1(o_lru_conv_b[0]), _block_diag(o_lru_w_a[0]).astype(BF16),
             _block_diag(o_lru_w_x[0]).astype(BF16), row1(o_lru_b_a[0]), row1(o_lru_b_x[0]), row1(o_lru_lambda[0]))
    xb_p3 = xb_p.reshape(nb, lp, LRU_WIDTH)
    yc_p, lh_p = lru_prompt(xb_p3, gb_p.reshape(nb, lp, LRU_WIDTH), jnp.zeros((nb, SUBLANES, LRU_WIDTH), F32),
                            jnp.zeros((nb, 1, LRU_WIDTH), F32), *lru_w, tl=min(lp, 256))
    xb_s3 = xb_s.reshape(ns, ls, LRU_WIDTH)
    tmaj = lambda a: jnp.transpose(a, (1, 0, 2))
    yc_s, lh_s = lru_sample(tmaj(xb_s3), tmaj(gb_s.reshape(ns, ls, LRU_WIDTH)), tmaj(state_lru_conv[0]),
                            state_lru_h[0], *lru_w)
    yc_s = tmaj(yc_s).reshape(ts, LRU_WIDTH)
    lconv_p = xb_p3[:, lp - 3:][None]
    lconv_s = xb_s3[:, ls - 3:][None]

    od_p = mla_prompt(q_p, k_p, v_p, nb, lp, tq=min(lp, 512))
    q_pe = q_s.reshape(ts, MLA_HEADS, LANES)[:, :, :MLA_ROPE]
    q_cat = jnp.concatenate([qlat_s.reshape(ts, MLA_HEADS, MLA_KV_RANK).astype(BF16), q_pe], axis=-1)
    q_cat = q_cat.reshape(ns, ls * MLA_HEADS, MLA_ROW)
    n_pages = page_table.shape[1]
    cache_t = jnp.swapaxes(cache_mla.reshape(cache_mla.shape[1:]), 1, 2)
    o_lat = mla_sample(page_table, q_cat, rows_s.reshape(ns, ls, MLA_ROW), cache_t,
                       n_pages_step=math.gcd(n_pages, 16))
    o_lat = o_lat.reshape(ts, MLA_HEADS, MLA_ROW)[:, :, :MLA_KV_RANK].reshape(ts, MLA_HEADS * MLA_KV_RANK)
    od_s = matmul(o_lat, wuv_bd, BF16)

    w_out = o_w_out[0].astype(BF16)
    wod = jnp.pad(w_out[LRU_WIDTH:].reshape(MLA_HEADS, MLA_V, d), ((0, 0), (0, LANES - MLA_V), (0, 0)))
    wod = wod.reshape(MLA_HEADS * LANES, d)
    rw = jnp.pad(o_router_w[0], ((0, 0), (0, LANES - N_EXPERTS)))
    rb = jnp.pad(o_router_b[0], (0, LANES - N_EXPERTS)).reshape(1, LANES)
    lnf = row1(o_ln_ffn[0])
    tall = tp + ts
    x1_p, hn, idx_p, gate_p = odd_out(xp, yc_p.reshape(tp, LRU_WIDTH), od_p, w_out[:LRU_WIDTH], wod, lnf, rw, rb,
                                      tall, 0)
    x1_s, hn, idx_s, gate_s = odd_out(xs, yc_s, od_s, w_out[:LRU_WIDTH], wod, lnf, rw, rb, tall, tp, hn)

    tile = math.gcd(math.gcd(tp, ts), EXPERT_ROWS)
    e_all = jnp.concatenate([idx_p[:, :TOP_K], idx_s[:, :TOP_K]], axis=0)
    g_all = jnp.concatenate([gate_p[:, :TOP_K], gate_s[:, :TOP_K]], axis=0)
    slot, expert_plan, combine_sched = _moe_plan(e_all, tile)
    yb = moe_experts(expert_plan, slot.T, g_all.T, hn, o_exp_gate[0].astype(BF16), o_exp_up[0].astype(BF16),
                     o_exp_down[0].astype(BF16), tile)
    slot_cols = jnp.pad(slot, ((0, 0), (0, LANES - TOP_K)))
    fnw = row1(final_norm)
    y_p = moe_combine(combine_sched(0, tp // tile), slot_cols[:tp], x1_p, yb, fnw, tile).reshape(nb, lp, d)
    y_s = moe_combine(combine_sched(tp // tile, tall // tile), slot_cols[tp:], x1_s, yb, fnw, tile).reshape(ns, ls, d)

    return (y_p, y_s, swa_kv_p, swa_kv_s, gconv_p, gconv_s, gs_p[None], gs_s[None],
            lconv_p, lconv_s, lh_p.reshape(1, nb, LRU_WIDTH), lh_s[None],
            rows_p.reshape(1, nb, lp // PAGE_SIZE, PAGE_SIZE, MLA_ROW), rows_s.reshape(1, ns, ls, MLA_ROW))
```

```python
import functools
import math

import jax
import jax.numpy as jnp
from jax import lax
from jax.experimental import pallas as pl
from jax.experimental.pallas import tpu as pltpu

F32 = jnp.float32
BF16 = jnp.bfloat16
I32 = jnp.int32
HI = lax.Precision.HIGHEST

D_MODEL = 1024
PAGE_SIZE = 128
SWA_WINDOW = 128
SWA_HEADS = 8
SWA_KV_HEADS = 2
SWA_GROUP = SWA_HEADS // SWA_KV_HEADS
SWA_HEAD_DIM = 64
GDN_HEADS = 4
GDN_DK = 128
GDN_DV = 128
GDN_CONV = 4
GDN_CHUNK = 64
GDN_QKV = GDN_HEADS * (2 * GDN_DK + GDN_DV)
LRU_WIDTH = 512
LRU_BLOCKS = 8
LRU_BLOCK_W = LRU_WIDTH // LRU_BLOCKS
LRU_C = 8.0
MLA_HEADS = 8
MLA_Q_RANK = 384
MLA_KV_RANK = 256
MLA_NOPE = 64
MLA_ROPE = 32
MLA_V = 64
MLA_ROW = MLA_KV_RANK + MLA_ROPE
ROPE_THETA = 10000.0
D_FF = 2816
N_EXPERTS = 8
TOP_K = 2
MOE_FF = 2048
NORM_EPS = 1e-6

LANES = 128
SUBLANES = 8
VMEM_LIMIT = 56 << 20

NN = (((1,), (0,)), ((), ()))
NT = (((1,), (1,)), ((), ()))
TN = (((0,), (0,)), ((), ()))


def _mm(a, b, dims=NN, precision=None):
    return lax.dot_general(a, b, dims, precision=precision, preferred_element_type=F32)


def _cp(sem):
    return pltpu.CompilerParams(dimension_semantics=sem, vmem_limit_bytes=VMEM_LIMIT)


def _rms(x, w):
    return x * lax.rsqrt(jnp.mean(x * x, axis=-1, keepdims=True) + NORM_EPS) * w


def _silu(x):
    return x * jax.nn.sigmoid(x)


def _full(shape):
    nd = len(shape)
    return pl.BlockSpec(shape, lambda *a: (0,) * nd)


def _row_tile(t, pref):
    for c in pref:
        if t % c == 0:
            return c
    return t


def _norm_proj_kernel(x_ref, ln_ref, *refs):
    n = len(refs) // 2
    h = _rms(x_ref[...], ln_ref[...]).astype(BF16)
    for w_ref, o_ref in zip(refs[:n], refs[n:]):
        o_ref[...] = _mm(h, w_ref[...])


def norm_proj(x, ln, ws):
    t, d = x.shape
    tm = _row_tile(t, (512, 256, 128, 64, 32, 16, 8))
    return pl.pallas_call(
        _norm_proj_kernel,
        grid=(t // tm,),
        in_specs=[pl.BlockSpec((tm, d), lambda i: (i, 0)), _full((1, d))] + [_full(w.shape) for w in ws],
        out_specs=[pl.BlockSpec((tm, w.shape[1]), lambda i: (i, 0)) for w in ws],
        out_shape=[jax.ShapeDtypeStruct((t, w.shape[1]), F32) for w in ws],
        compiler_params=_cp(("parallel",)),
        name="norm_proj",
    )(x, ln, *ws)


def _swa_softmax_pv(parts, sink):
    m = sink
    for s, _ in parts:
        m = jnp.maximum(m, jnp.max(s, axis=-1, keepdims=True))
    den = jnp.exp(sink - m)
    o = None
    for s, v in parts:
        p = jnp.exp(s - m)
        den = den + jnp.sum(p, axis=-1, keepdims=True)
        pv = _mm(p.astype(BF16), v)
        o = pv if o is None else o + pv
    return o / den


def _swa_prompt_kernel(sink_ref, q_ref, kvc_ref, kvp_ref, o_ref):
    b = pl.program_id(1)
    w = SWA_WINDOW
    hd = SWA_HEAD_DIM
    ng = SWA_GROUP
    q = q_ref[...]
    band = jnp.concatenate([kvp_ref[...], kvc_ref[...]], axis=0).astype(BF16)
    r = lax.broadcasted_iota(I32, (ng * w, 1), 0)
    grp = r // w
    qi = r % w
    t = lax.broadcasted_iota(I32, (ng * w, 2 * w), 1)
    dist = qi + w - t
    valid = jnp.logical_and(jnp.logical_and(dist >= 0, dist < w), jnp.logical_or(t >= w, b > 0))
    distf = dist.astype(F32)
    outs = [None] * SWA_HEADS
    for j in range(SWA_KV_HEADS):
        heads = [j * ng + g for g in range(ng)]
        slope = jnp.zeros((ng * w, 1), F32)
        sink = jnp.zeros((ng * w, 1), F32)
        for g, h in enumerate(heads):
            slope = jnp.where(grp == g, 2.0 ** (-8.0 * (h + 1) / SWA_HEADS), slope)
            sink = jnp.where(grp == g, sink_ref[h], sink)
        qs = jnp.concatenate([q[:, h * hd:(h + 1) * hd] for h in heads], axis=0).astype(BF16)
        kb = band[:, j * hd:(j + 1) * hd]
        vb = band[:, (SWA_KV_HEADS + j) * hd:(SWA_KV_HEADS + j + 1) * hd]
        s = _mm(qs, kb, NT) * (hd ** -0.5)
        s = jnp.where(valid, s - slope * distf, -jnp.inf)
        o = _swa_softmax_pv([(s, vb)], sink)
        for g, h in enumerate(heads):
            outs[h] = o[g * w:(g + 1) * w, :]
    o_ref[...] = jnp.concatenate(outs, axis=1)


def swa_prompt(qkv, sinks, n, l):
    w = SWA_WINDOW
    nb = l // w
    nq = SWA_HEADS * SWA_HEAD_DIM
    nkv = 2 * SWA_KV_HEADS * SWA_HEAD_DIM
    return pl.pallas_call(
        _swa_prompt_kernel,
        grid=(n, nb),
        in_specs=[
            pl.BlockSpec(memory_space=pltpu.SMEM),
            pl.BlockSpec((w, nq), lambda i, b: (i * nb + b, 0)),
            pl.BlockSpec((w, nkv), lambda i, b: (i * nb + b, nq // nkv)),
            pl.BlockSpec((w, nkv), lambda i, b: (i * nb + jnp.maximum(b - 1, 0), nq // nkv)),
        ],
        out_specs=pl.BlockSpec((w, nq), lambda i, b: (i * nb + b, 0)),
        out_shape=jax.ShapeDtypeStruct((n * l, nq), F32),
        compiler_params=_cp(("parallel", "arbitrary")),
        name="swa_prompt",
    )(sinks, qkv, qkv, qkv)


def _swa_sample_kernel(sink_ref, q_ref, buf_ref, o_ref, *, l):
    hd = SWA_HEAD_DIM
    lp = q_ref.shape[1]
    wb = buf_ref.shape[1]
    x = q_ref[0]
    buf = buf_ref[0].astype(BF16)
    kvn = x[:, SWA_HEADS * hd:].astype(BF16)
    qi_b = lax.broadcasted_iota(I32, (lp, wb), 0)
    kj_b = lax.broadcasted_iota(I32, (lp, wb), 1)
    dist_b = (qi_b + wb - kj_b).astype(F32)
    valid_b = (qi_b + wb - kj_b) < SWA_WINDOW
    qi_n = lax.broadcasted_iota(I32, (lp, lp), 0)
    kj_n = lax.broadcasted_iota(I32, (lp, lp), 1)
    dist_n = (qi_n - kj_n).astype(F32)
    valid_n = jnp.logical_and(kj_n <= qi_n, kj_n < l)
    outs = []
    for h in range(SWA_HEADS):
        j = h // SWA_GROUP
        slope = 2.0 ** (-8.0 * (h + 1) / SWA_HEADS)
        qh = x[:, h * hd:(h + 1) * hd].astype(BF16)
        kb = buf[:, j * hd:(j + 1) * hd]
        vb = buf[:, (SWA_KV_HEADS + j) * hd:(SWA_KV_HEADS + j + 1) * hd]
        kn = kvn[:, j * hd:(j + 1) * hd]
        vn = kvn[:, (SWA_KV_HEADS + j) * hd:(SWA_KV_HEADS + j + 1) * hd]
        s_b = _mm(qh, kb, NT) * (hd ** -0.5)
        s_n = _mm(qh, kn, NT) * (hd ** -0.5)
        s_b = jnp.where(valid_b, s_b - slope * dist_b, -jnp.inf)
        s_n = jnp.where(valid_n, s_n - slope * dist_n, -jnp.inf)
        outs.append(_swa_softmax_pv([(s_b, vb), (s_n, vn)], sink_ref[h]))
    o_ref[0] = jnp.concatenate(outs, axis=1)


def swa_sample(qkv_pad, buf, sinks, l):
    n, lp, c = qkv_pad.shape
    wb = buf.shape[1]
    nq = SWA_HEADS * SWA_HEAD_DIM
    return pl.pallas_call(
        functools.partial(_swa_sample_kernel, l=l),
        grid=(n,),
        in_specs=[
            pl.BlockSpec(memory_space=pltpu.SMEM),
            pl.BlockSpec((1, lp, c), lambda i: (i, 0, 0)),
            pl.BlockSpec((1, wb, buf.shape[2]), lambda i: (i, 0, 0)),
        ],
        out_specs=pl.BlockSpec((1, lp, nq), lambda i: (i, 0, 0)),
        out_shape=jax.ShapeDtypeStruct((n, lp, nq), F32),
        compiler_params=_cp(("parallel",)),
        name="swa_sample",
    )(sinks, qkv_pad, buf)


GDN_SOLVE_GROUP = 4


def _mm3(a, b):
    ah = a.astype(BF16)
    al = (a - ah.astype(F32)).astype(BF16)
    bh = b.astype(BF16)
    bl = (b - bh.astype(F32)).astype(BF16)
    return _mm(jnp.concatenate([ah, ah, al], axis=1), jnp.concatenate([bh, bl, bh], axis=0))


def _block_rows(x, c, nh):
    blk = lax.broadcasted_iota(I32, x.shape, 1) // c
    return jnp.concatenate([jnp.where(blk == i, x, 0.0) for i in range(nh)], axis=0)


def _gdn_kernel(x_ref, z_ref, bg_ref, cprev_ref, s0_ref, cw_ref, alog_ref, dtb_ref, gn_ref,
                o_ref, sfin_ref, xbuf, s_sc, q_sc, k_sc, vb_sc, kb_sc, qd_sc, kd_sc, u_sc, w_sc, be_sc, gc_sc, egl_sc,
                qkd_sc, *, chunk, valid):
    t = pl.program_id(1)
    tl = x_ref.shape[1]
    c = chunk
    nck = tl // c
    nh = GDN_HEADS
    dk = GDN_DK

    @pl.when(t == 0)
    def _():
        xbuf[0:SUBLANES, :] = cprev_ref[0]
        s_sc[...] = s0_ref[0]

    xbuf[SUBLANES:SUBLANES + tl, :] = x_ref[0]
    cw = cw_ref[...]
    conv = xbuf[5:5 + tl, :] * cw[0:1, :]
    for j in range(1, GDN_CONV):
        conv = conv + xbuf[5 + j:5 + j + tl, :] * cw[j:j + 1, :]
    xbuf[0:SUBLANES, :] = xbuf[tl:tl + SUBLANES, :]
    act = _silu(conv)

    row = lax.broadcasted_iota(I32, (tl, 1), 0)
    rmask = (row % c) < valid
    bg = bg_ref[0]
    beta = jax.nn.sigmoid(bg)
    g = -jnp.exp(alog_ref[...]) * jax.nn.softplus(bg + dtb_ref[...])
    g = jnp.where(rmask, g, 0.0)
    rc = row % c
    s = 1
    while s < c:
        g = g + jnp.where(rc >= s, pltpu.roll(g, s, axis=0), 0.0)
        s *= 2
    glast = jnp.broadcast_to(g.reshape(nck, c, LANES)[:, c - 1:c, :], (nck, c, LANES)).reshape(tl, LANES)
    egc = jnp.exp(g)
    kfac = jnp.exp(glast - g)
    be_sc[...] = beta
    gc_sc[...] = g
    egl_sc[...] = jnp.exp(glast)
    for h in range(nh):
        hs = slice(h * dk, (h + 1) * dk)
        qh = act[:, h * dk:(h + 1) * dk]
        kh = act[:, (nh + h) * dk:(nh + h + 1) * dk]
        vh = act[:, (2 * nh + h) * dk:(2 * nh + h + 1) * dk]
        qh = qh * lax.rsqrt(jnp.sum(qh * qh, axis=-1, keepdims=True) + NORM_EPS) * (dk ** -0.5)
        kh = kh * lax.rsqrt(jnp.sum(kh * kh, axis=-1, keepdims=True) + NORM_EPS)
        qh = jnp.where(rmask, qh, 0.0)
        kh = jnp.where(rmask, kh, 0.0)
        vh = jnp.where(rmask, vh, 0.0)
        b_h = beta[:, h:h + 1]
        e_h = egc[:, nh + h:nh + h + 1]
        q_sc[:, hs] = qh
        k_sc[:, hs] = kh
        vb_sc[:, hs] = vh * b_h
        kb_sc[:, hs] = kh * (b_h * e_h)
        qd_sc[:, hs] = qh * e_h
        kd_sc[:, hs] = kh * kfac[:, nh + h:nh + h + 1]

    ii = lax.broadcasted_iota(I32, (c, nh * c), 0)
    jl = lax.broadcasted_iota(I32, (c, nh * c), 1) % c
    eye_cat = (ii == jl).astype(F32)
    gn = gn_ref[...]
    n_factors = max(1, int(math.ceil(math.log2(valid))))

    ng = math.gcd(nck, GDN_SOLVE_GROUP)

    def solve_body(gi, carry):
        rows_g = [pl.ds(pl.multiple_of((gi * ng + g) * c, c), c) for g in range(ng)]
        a_g, qkd_g = [], []
        for rows in rows_g:
            gcs = gc_sc[rows, :]
            bes = be_sc[rows, :]
            kk, qk, gexp, bexp = [], [], [], []
            for h in range(nh):
                hs = slice(h * dk, (h + 1) * dk)
                kb16 = k_sc[rows, hs].astype(BF16)
                kk.append(_mm(kb16, kb16, NT))
                qk.append(_mm(q_sc[rows, hs].astype(BF16), kb16, NT))
                gexp.append(jnp.broadcast_to(gcs[:, nh + h:nh + h + 1], (c, c)))
                bexp.append(jnp.broadcast_to(bes[:, h:h + 1], (c, c)))
            kk, qk, gexp, bexp = [jnp.concatenate(v, axis=1) for v in (kk, qk, gexp, bexp)]
            grow = jnp.sum(jnp.where(ii == jl, gexp, 0.0), axis=0, keepdims=True)
            decay = jnp.where(ii >= jl, jnp.exp(gexp - grow), 0.0)
            a_g.append(jnp.where(ii > jl, bexp * kk * decay, 0.0))
            qkd_g.append(qk * decay)
        t_g = [eye_cat - a for a in a_g]
        p_g = a_g
        for _ in range(n_factors - 1):
            p_g = [_mm3(p, _block_rows(p, c, nh)) for p in p_g]
            t_g = [_mm3(tv, _block_rows(eye_cat + p, c, nh)) for tv, p in zip(t_g, p_g)]
        for rows, tv, qkd in zip(rows_g, t_g, qkd_g):
            rhs = jnp.concatenate(
                [jnp.concatenate([vb_sc[rows, h * dk:(h + 1) * dk], kb_sc[rows, h * dk:(h + 1) * dk]], axis=1)
                 for h in range(nh)], axis=0)
            sol = _mm3(_block_rows(tv, c, nh), rhs)
            for h in range(nh):
                hs = slice(h * dk, (h + 1) * dk)
                u_sc[rows, hs] = sol[h * c:(h + 1) * c, :GDN_DV]
                w_sc[rows, hs] = sol[h * c:(h + 1) * c, GDN_DV:]
            qkd_sc[rows, :] = qkd
        return carry

    lax.fori_loop(0, nck // ng, solve_body, 0)

    def scan_body(ci, carry):
        rows = pl.ds(pl.multiple_of(ci * c, c), c)
        egl = egl_sc[rows, :]
        sts, v_news, o1s = [], [], []
        for h in range(nh):
            hs = slice(h * dk, (h + 1) * dk)
            st = s_sc[h]
            wq = jnp.concatenate([w_sc[rows, hs], qd_sc[rows, hs]], axis=0)
            r = _mm(wq.astype(BF16), st.astype(BF16))
            sts.append(st)
            v_news.append(u_sc[rows, hs] - r[:c])
            o1s.append(r[c:])
        o2 = _mm(_block_rows(qkd_sc[rows, :], c, nh).astype(BF16), jnp.concatenate(v_news, axis=0).astype(BF16))
        for h in range(nh):
            hs = slice(h * dk, (h + 1) * dk)
            upd = _mm(kd_sc[rows, hs].astype(BF16), v_news[h].astype(BF16), TN)
            s_sc[h] = sts[h] * egl[0:1, nh + h:nh + h + 1] + upd
            o = o1s[h] + o2[h * c:(h + 1) * c]
            o_ref[0, rows, hs] = _rms(o, gn) * _silu(z_ref[0, rows, hs])
        return carry

    lax.fori_loop(0, nck, scan_body, 0)

    @pl.when(t == pl.num_programs(1) - 1)
    def _():
        sfin_ref[0] = s_sc[...]


def gdn(x, z, bg, cprev, s0, conv_w, a_log, dt_bias, gnorm, chunk, valid, tl):
    n, lp, cq = x.shape
    nt = lp // tl
    nv = GDN_HEADS * GDN_DV
    alog = jnp.zeros((1, LANES), F32).at[0, GDN_HEADS:2 * GDN_HEADS].set(a_log)
    dtb = jnp.zeros((1, LANES), F32).at[0, GDN_HEADS:2 * GDN_HEADS].set(dt_bias)
    seq3 = lambda w: pl.BlockSpec((1, tl, w), lambda i, t: (i, t, 0))
    return pl.pallas_call(
        functools.partial(_gdn_kernel, chunk=chunk, valid=valid),
        grid=(n, nt),
        in_specs=[
            seq3(cq), seq3(nv), seq3(LANES),
            pl.BlockSpec((1, SUBLANES, cq), lambda i, t: (i, 0, 0)),
            pl.BlockSpec((1, GDN_HEADS, GDN_DK, GDN_DV), lambda i, t: (i, 0, 0, 0)),
            _full((GDN_CONV, cq)), _full((1, LANES)), _full((1, LANES)), _full((1, GDN_DV)),
        ],
        out_specs=[seq3(nv), pl.BlockSpec((1, GDN_HEADS, GDN_DK, GDN_DV), lambda i, t: (i, 0, 0, 0))],
        out_shape=[jax.ShapeDtypeStruct((n, lp, nv), F32),
                   jax.ShapeDtypeStruct((n, GDN_HEADS, GDN_DK, GDN_DV), F32)],
        scratch_shapes=[
            pltpu.VMEM((tl + SUBLANES, cq), F32),
            pltpu.VMEM((GDN_HEADS, GDN_DK, GDN_DV), F32),
        ] + [pltpu.VMEM((tl, nv), F32)] * 8 + [pltpu.VMEM((tl, LANES), F32)] * 3
        + [pltpu.VMEM((tl, GDN_HEADS * chunk), F32)],
        compiler_params=_cp(("parallel", "arbitrary")),
        name="gdn",
    )(x, z, bg, cprev, s0, conv_w, alog, dtb, gnorm.reshape(1, GDN_DV))


def _even_out_kernel(x_ref, oa_ref, ob_ref, woa_ref, wob_ref, ln_ref, wg_ref, wu_ref, wd_ref,
                     o_ref, x1_sc, h_sc, acc_sc):
    f = pl.program_id(1)

    @pl.when(f == 0)
    def _():
        x1 = x_ref[...] + (_mm(oa_ref[...].astype(BF16), woa_ref[...]) + _mm(ob_ref[...].astype(BF16), wob_ref[...]))
        x1_sc[...] = x1
        h_sc[...] = _rms(x1, ln_ref[...]).astype(BF16)
        acc_sc[...] = jnp.zeros_like(acc_sc)

    h = h_sc[...]
    act = (_silu(_mm(h, wg_ref[...])) * _mm(h, wu_ref[...])).astype(BF16)
    acc_sc[...] += _mm(act, wd_ref[...])

    @pl.when(f == pl.num_programs(1) - 1)
    def _():
        o_ref[...] = x1_sc[...] + acc_sc[...]


def even_out(x, oa, ob, woa, wob, ln, wg, wu, wd):
    t, d = x.shape
    ff = wg.shape[1]
    tm = _row_tile(t, (1024, 512, 256, 128, 64, 32, 16, 8))
    tf = 256 if ff % 256 == 0 else LANES
    row = lambda w: pl.BlockSpec((tm, w), lambda i, f: (i, 0))
    return pl.pallas_call(
        _even_out_kernel,
        grid=(t // tm, ff // tf),
        in_specs=[row(d), row(oa.shape[1]), row(ob.shape[1]), _full(woa.shape), _full(wob.shape), _full((1, d)),
                  pl.BlockSpec((d, tf), lambda i, f: (0, f)), pl.BlockSpec((d, tf), lambda i, f: (0, f)),
                  pl.BlockSpec((tf, d), lambda i, f: (f, 0))],
        out_specs=row(d),
        out_shape=jax.ShapeDtypeStruct((t, d), F32),
        scratch_shapes=[pltpu.VMEM((tm, d), F32), pltpu.VMEM((tm, d), BF16), pltpu.VMEM((tm, d), F32)],
        compiler_params=_cp(("parallel", "arbitrary")),
        name="even_out",
    )(x, oa, ob, woa, wob, ln, wg, wu, wd)


def _rope_lanes(v, c, s):
    n = v.shape[1]
    lane = lax.broadcasted_iota(I32, (1, n), 1) % LANES
    sw = jnp.where(lane < MLA_ROPE // 2, pltpu.roll(v, n - MLA_ROPE // 2, axis=1), pltpu.roll(v, MLA_ROPE // 2, axis=1))
    return v * c + sw * s


def _odd_in_kernel(x_ref, ln_ref, wxb_ref, wgb_ref, wcq_ref, wckv_ref, wkpe_ref, qn_ref, kvn_ref, wuq_ref,
                   cos_ref, sin_ref, *rest, absorbed):
    if absorbed:
        wabs_ref, xb_ref, gb_ref, q_ref, qlat_ref, rows_ref = rest
    else:
        wuk_ref, wuv_ref, xb_ref, gb_ref, q_ref, k_ref, v_ref, rows_ref = rest
    h = _rms(x_ref[...], ln_ref[...]).astype(BF16)
    xb_ref[...] = _mm(h, wxb_ref[...])
    gb_ref[...] = _mm(h, wgb_ref[...])
    cq = _mm(h, wcq_ref[...])
    ckv = _mm(h, wckv_ref[...])
    kpe = _mm(h, wkpe_ref[...])
    cqn = _rms(cq, qn_ref[...]).astype(BF16)
    ckvn = _rms(ckv, kvn_ref[...])
    c128 = cos_ref[...]
    s128 = sin_ref[...]
    nh = MLA_HEADS
    q = _rope_lanes(_mm(cqn, wuq_ref[...]), jnp.concatenate([c128] * nh, axis=1), jnp.concatenate([s128] * nh, axis=1))
    kpe_r = _rope_lanes(kpe, c128, s128)
    qb = q.astype(BF16)
    q_ref[...] = qb
    rows_ref[:, 0:MLA_KV_RANK] = ckvn
    rows_ref[:, MLA_KV_RANK:MLA_ROW] = kpe_r[:, 0:MLA_ROPE]
    if absorbed:
        qlat_ref[...] = _mm(qb, wabs_ref[...])
    else:
        ckvb = ckvn.astype(BF16)
        k_ref[...] = (_mm(ckvb, wuk_ref[...]) + jnp.concatenate([kpe_r] * nh, axis=1)).astype(BF16)
        v_ref[...] = _mm(ckvb, wuv_ref[...]).astype(BF16)


def odd_in(x, ln, ws, qn, kvn, wuq, cos_t, sin_t, extra, absorbed):
    t, d = x.shape
    tm = _row_tile(t, (512, 256, 128, 64, 32, 16, 8))
    nblk = cos_t.shape[0] // tm
    hb = MLA_HEADS * LANES
    row = lambda w: pl.BlockSpec((tm, w), lambda i: (i, 0))
    tbl = pl.BlockSpec((tm, LANES), lambda i: (i % nblk, 0))
    if absorbed:
        outs = [(LRU_WIDTH, F32), (LRU_WIDTH, F32), (hb, BF16), (extra[0].shape[1], F32), (MLA_ROW, F32)]
    else:
        outs = [(LRU_WIDTH, F32), (LRU_WIDTH, F32), (hb, BF16), (hb, BF16), (hb, BF16), (MLA_ROW, F32)]
    return pl.pallas_call(
        functools.partial(_odd_in_kernel, absorbed=absorbed),
        grid=(t // tm,),
        in_specs=[row(d), _full((1, d))] + [_full(w.shape) for w in ws]
        + [_full(qn.shape), _full(kvn.shape), _full(wuq.shape), tbl, tbl] + [_full(w.shape) for w in extra],
        out_specs=[row(w) for w, _ in outs],
        out_shape=[jax.ShapeDtypeStruct((t, w), dt) for w, dt in outs],
        compiler_params=_cp(("parallel",)),
        name="odd_in",
    )(x, ln, *ws, qn, kvn, wuq, cos_t, sin_t, *extra)


def _expm1(x):
    u = jnp.exp(x)
    um1 = u - 1.0
    small = um1 * x / jnp.log(u)
    return jnp.where(um1 == 0.0, x, jnp.where(jnp.abs(x) < 0.5, small, um1))


def _lru_gates(xc, wa_ref, wx_ref, ba_ref, bx_ref, lam_ref):
    xcb = xc.astype(BF16)
    r = jax.nn.sigmoid(_mm(xcb, wa_ref[...]) + ba_ref[...])
    i = jax.nn.sigmoid(_mm(xcb, wx_ref[...]) + bx_ref[...])
    log_a = -LRU_C * r * jax.nn.softplus(-lam_ref[...])
    a = jnp.exp(log_a)
    b = jnp.sqrt(-_expm1(2.0 * log_a)) * (i * xc)
    return a, b


def _lru_prompt_kernel(x_ref, g_ref, cprev_ref, h0_ref, cw_ref, cb_ref, wa_ref, wx_ref, ba_ref, bx_ref, lam_ref,
                       y_ref, hl_ref, xbuf, h_sc):
    t = pl.program_id(1)
    tl = x_ref.shape[1]

    @pl.when(t == 0)
    def _():
        xbuf[0:SUBLANES, :] = cprev_ref[0]
        h_sc[0:1, :] = h0_ref[0]

    xbuf[SUBLANES:SUBLANES + tl, :] = x_ref[0]
    cw = cw_ref[...]
    xc = xbuf[5:5 + tl, :] * cw[0:1, :]
    for j in range(1, cw.shape[0]):
        xc = xc + xbuf[5 + j:5 + j + tl, :] * cw[j:j + 1, :]
    xc = xc + cb_ref[...]
    xbuf[0:SUBLANES, :] = xbuf[tl:tl + SUBLANES, :]
    a, b = _lru_gates(xc, wa_ref, wx_ref, ba_ref, bx_ref, lam_ref)
    row = lax.broadcasted_iota(I32, (tl, 1), 0)
    s = 1
    while s < tl:
        m = row >= s
        b = jnp.where(m, a * pltpu.roll(b, s, axis=0) + b, b)
        a = jnp.where(m, a * pltpu.roll(a, s, axis=0), a)
        s *= 2
    hs = a * h_sc[0:1, :] + b
    h_sc[0:1, :] = hs[tl - 1:tl, :]
    y_ref[0] = hs * jax.nn.gelu(g_ref[0])
    hl_ref[0] = hs[tl - 1:tl, :]


def lru_prompt(xb, gb, cprev, h0, cw, cb, wa, wx, ba, bx, lam, tl):
    n, l, c = xb.shape
    seq = pl.BlockSpec((1, tl, c), lambda i, t: (i, t, 0))
    return pl.pallas_call(
        _lru_prompt_kernel,
        grid=(n, l // tl),
        in_specs=[seq, seq, pl.BlockSpec((1, SUBLANES, c), lambda i, t: (i, 0, 0)),
                  pl.BlockSpec((1, 1, c), lambda i, t: (i, 0, 0)),
                  _full(cw.shape), _full((1, c)), _full(wa.shape), _full(wx.shape), _full((1, c)), _full((1, c)),
                  _full((1, c))],
        out_specs=[seq, pl.BlockSpec((1, 1, c), lambda i, t: (i, 0, 0))],
        out_shape=[jax.ShapeDtypeStruct((n, l, c), F32), jax.ShapeDtypeStruct((n, 1, c), F32)],
        scratch_shapes=[pltpu.VMEM((tl + SUBLANES, c), F32), pltpu.VMEM((SUBLANES, c), F32)],
        compiler_params=_cp(("parallel", "arbitrary")),
        name="lru_prompt",
    )(xb, gb, cprev, h0, cw, cb, wa, wx, ba, bx, lam)


def _lru_sample_kernel(x_ref, g_ref, cprev_ref, h0_ref, cw_ref, cb_ref, wa_ref, wx_ref, ba_ref, bx_ref, lam_ref,
                       y_ref, hl_ref):
    l = x_ref.shape[0]
    cw = cw_ref[...]
    nw = cw.shape[0]
    xx = [cprev_ref[j] for j in range(nw - 1)] + [x_ref[j] for j in range(l)]
    h = h0_ref[...]
    for i in range(l):
        xc = xx[i] * cw[0:1, :]
        for j in range(1, nw):
            xc = xc + xx[i + j] * cw[j:j + 1, :]
        xc = xc + cb_ref[...]
        a, b = _lru_gates(xc, wa_ref, wx_ref, ba_ref, bx_ref, lam_ref)
        h = a * h + b
        y_ref[i] = h * jax.nn.gelu(g_ref[i])
    hl_ref[...] = h


def lru_sample(xb, gb, cprev, h0, cw, cb, wa, wx, ba, bx, lam):
    l, n, c = xb.shape
    return pl.pallas_call(
        _lru_sample_kernel,
        out_shape=[jax.ShapeDtypeStruct((l, n, c), F32), jax.ShapeDtypeStruct((n, c), F32)],
        compiler_params=pltpu.CompilerParams(vmem_limit_bytes=VMEM_LIMIT),
        name="lru_sample",
    )(xb, gb, cprev, h0, cw, cb, wa, wx, ba, bx, lam)


MLA_HEADS_PER_STEP = 2


def _mla_prompt_kernel(q_ref, k_ref, v_ref, o_ref, m_sc, l_sc, acc_sc, *, scale):
    qi = pl.program_id(2)
    tq = q_ref.shape[0]
    tk = tq
    g = q_ref.shape[1] // LANES
    c1 = scale * math.log2(math.e)
    m_sc[...] = jnp.full_like(m_sc, -jnp.inf)
    l_sc[...] = jnp.zeros_like(l_sc)
    acc_sc[...] = jnp.zeros_like(acc_sc)
    on_or_below = lax.broadcasted_iota(I32, (tq, tk), 1) <= lax.broadcasted_iota(I32, (tq, tk), 0)

    def step(ki, masked):
        rows = pl.ds(pl.multiple_of(ki * tk, tk), tk)
        for h in range(g):
            hs = slice(h * LANES, (h + 1) * LANES)
            s = _mm(q_ref[:, hs], k_ref[rows, hs], NT) * c1
            if masked:
                s = jnp.where(on_or_below, s, -jnp.inf)
            m_prev = m_sc[h]
            m_new = jnp.maximum(m_prev, jnp.max(s, axis=-1, keepdims=True))
            alpha = jnp.exp2(m_prev - m_new)
            p = jnp.exp2(s - jnp.concatenate([m_new] * (tk // LANES), axis=1))
            l_sc[h] = alpha * l_sc[h] + jnp.sum(p, axis=-1, keepdims=True)
            acc_sc[:, hs] = alpha * acc_sc[:, hs] + _mm(p.astype(BF16), v_ref[rows, hs])
            m_sc[h] = m_new

    def body(ki, carry):
        step(ki, False)
        return carry

    lax.fori_loop(0, qi, body, 0)
    step(qi, True)
    for h in range(g):
        hs = slice(h * LANES, (h + 1) * LANES)
        o_ref[:, hs] = (acc_sc[:, hs] / l_sc[h]).astype(o_ref.dtype)


def mla_prompt(q, k, v, n, l, tq):
    nq = l // tq
    g = MLA_HEADS_PER_STEP
    w = g * LANES
    scale = (MLA_NOPE + MLA_ROPE) ** -0.5
    return pl.pallas_call(
        functools.partial(_mla_prompt_kernel, scale=scale),
        grid=(n, MLA_HEADS // g, nq),
        in_specs=[pl.BlockSpec((tq, w), lambda i, h, j: (i * nq + j, h)),
                  pl.BlockSpec((l, w), lambda i, h, j: (i, h)), pl.BlockSpec((l, w), lambda i, h, j: (i, h))],
        out_specs=pl.BlockSpec((tq, w), lambda i, h, j: (i * nq + j, h)),
        out_shape=jax.ShapeDtypeStruct(q.shape, BF16),
        scratch_shapes=[pltpu.VMEM((g, tq, LANES), F32), pltpu.VMEM((g, tq, LANES), F32), pltpu.VMEM((tq, w), F32)],
        compiler_params=_cp(("parallel", "parallel", "arbitrary")),
        name="mla_prompt",
    )(q, k, v)


MLA_SAMPLE_PAGES = 32
MLA_SAMPLE_GROUPS = 2


def _mla_sample_kernel(pt_ref, q_ref, rows_ref, *rest, n_pages_step, l, scale):
    page_refs = rest[:n_pages_step]
    o_ref, m_sc, l_sc, acc_sc = rest[n_pages_step:]
    j = pl.program_id(1)

    @pl.when(j == 0)
    def _():
        m_sc[...] = jnp.full_like(m_sc, -jnp.inf)
        l_sc[...] = jnp.zeros_like(l_sc)
        acc_sc[...] = jnp.zeros_like(acc_sc)

    q = q_ref[0]
    per = n_pages_step // MLA_SAMPLE_GROUPS
    kts = [jnp.concatenate([r[0].astype(BF16) for r in page_refs[g * per:(g + 1) * per]], axis=1)
           for g in range(MLA_SAMPLE_GROUPS)]
    ss = [_mm(q, kt) * scale for kt in kts]
    m_run = m_sc[...]
    l_run = l_sc[...]
    acc = acc_sc[...]
    for kt, s in zip(kts, ss):
        m_new = jnp.maximum(m_run, jnp.max(s, axis=-1, keepdims=True))
        alpha = jnp.exp(m_run - m_new)
        p32 = jnp.exp(s - m_new)
        l_run = alpha * l_run + jnp.sum(p32, axis=-1, keepdims=True)
        acc = alpha * acc + _mm(p32.astype(BF16), kt, NT)
        m_run = m_new
    m_sc[...] = m_run
    l_sc[...] = l_run
    acc_sc[...] = acc

    @pl.when(j == pl.num_programs(1) - 1)
    def _():
        qf = q.astype(F32)
        rows = rows_ref[0]
        tok = lax.broadcasted_iota(I32, (qf.shape[0], 1), 0) // MLA_HEADS
        sn = []
        for mm in range(l):
            sm = jnp.sum(qf * rows[mm:mm + 1, :], axis=-1, keepdims=True) * scale
            sn.append(jnp.where(tok >= mm, sm, -jnp.inf))
        m_old = m_sc[...]
        m_fin = m_old
        for sm in sn:
            m_fin = jnp.maximum(m_fin, sm)
        al = jnp.exp(m_old - m_fin)
        lsum = al * l_sc[...]
        acc = al * acc_sc[...]
        for mm in range(l):
            pm = jnp.exp(sn[mm] - m_fin)
            lsum = lsum + pm
            acc = acc + pm * rows[mm:mm + 1, :]
        o_ref[0] = acc / lsum


def mla_sample(page_table, q_cat, rows, cache_t, n_pages_step):
    n, r, c = q_cat.shape
    l = rows.shape[1]
    n_pages = page_table.shape[1]
    steps = n_pages // n_pages_step
    scale = (MLA_NOPE + MLA_ROPE) ** -0.5
    pt = page_table.reshape(-1)

    def page_map(k):
        return lambda i, j, pt_ref: (pt_ref[i * n_pages + j * n_pages_step + k], 0, 0)

    return pl.pallas_call(
        functools.partial(_mla_sample_kernel, n_pages_step=n_pages_step, l=l, scale=scale),
        grid_spec=pltpu.PrefetchScalarGridSpec(
            num_scalar_prefetch=1,
            grid=(n, steps),
            in_specs=[pl.BlockSpec((1, r, c), lambda i, j, pt_ref: (i, 0, 0)),
                      pl.BlockSpec((1, l, c), lambda i, j, pt_ref: (i, 0, 0))]
            + [pl.BlockSpec((1, c, PAGE_SIZE), page_map(k)) for k in range(n_pages_step)],
            out_specs=pl.BlockSpec((1, r, c), lambda i, j, pt_ref: (i, 0, 0)),
            scratch_shapes=[pltpu.VMEM((r, 1), F32), pltpu.VMEM((r, 1), F32), pltpu.VMEM((r, c), F32)],
        ),
        out_shape=jax.ShapeDtypeStruct((n, r, c), F32),
        compiler_params=_cp(("parallel", "arbitrary")),
        name="mla_sample",
    )(pt, q_cat, rows, *([cache_t] * n_pages_step))


def _matmul_kernel(x_ref, w_ref, o_ref):
    o_ref[...] = _mm(x_ref[...].astype(BF16), w_ref[...]).astype(o_ref.dtype)


def matmul(x, w, out_dtype):
    t, kd = x.shape
    tm = _row_tile(t, (512, 256, 128, 64, 32, 16, 8))
    return pl.pallas_call(
        _matmul_kernel,
        grid=(t // tm,),
        in_specs=[pl.BlockSpec((tm, kd), lambda i: (i, 0)), _full(w.shape)],
        out_specs=pl.BlockSpec((tm, w.shape[1]), lambda i: (i, 0)),
        out_shape=jax.ShapeDtypeStruct((t, w.shape[1]), out_dtype),
        compiler_params=_cp(("parallel",)),
        name="matmul",
    )(x, w)


def _odd_out_kernel(x_ref, yc_ref, od_ref, woc_ref, wod_ref, ln_ref, rw_ref, rb_ref, *rest):
    x1_ref, hn_ref, idx_ref, gate_ref = rest[-4:]
    x1 = x_ref[...] + (_mm(yc_ref[...].astype(BF16), woc_ref[...]) + _mm(od_ref[...], wod_ref[...]))
    x1_ref[...] = x1
    hn = _rms(x1, ln_ref[...])
    hn_ref[...] = hn.astype(BF16)
    logits = _mm(hn, rw_ref[...], NN, HI) + rb_ref[...]
    lane = lax.broadcasted_iota(I32, logits.shape, 1)
    logits = jnp.where(lane < N_EXPERTS, logits, -jnp.inf)
    m1 = jnp.max(logits, axis=-1, keepdims=True)
    i1 = jnp.min(jnp.where(logits == m1, lane, LANES), axis=-1, keepdims=True)
    l2 = jnp.where(lane == i1, -jnp.inf, logits)
    m2 = jnp.max(l2, axis=-1, keepdims=True)
    i2 = jnp.min(jnp.where(l2 == m2, lane, LANES), axis=-1, keepdims=True)
    e2 = jnp.exp(m2 - m1)
    den = 1.0 + e2
    idx_ref[...] = jnp.where(lane == 0, i1, jnp.where(lane == 1, i2, 0))
    gate_ref[...] = jnp.where(lane == 0, 1.0 / den, jnp.where(lane == 1, e2 / den, 0.0))


def odd_out(x, yc, od, woc, wod, ln, rw, rb, hn_rows, hn_row0, hn_buf=None):
    t, d = x.shape
    tm = _row_tile(t, (512, 256, 128, 64, 32, 16, 8))
    assert hn_row0 % tm == 0
    off = hn_row0 // tm
    row = lambda w: pl.BlockSpec((tm, w), lambda i: (i, 0))
    args = (x, yc, od, woc, wod, ln, rw, rb) + (() if hn_buf is None else (hn_buf,))
    return pl.pallas_call(
        _odd_out_kernel,
        grid=(t // tm,),
        in_specs=[row(d), row(yc.shape[1]), row(od.shape[1]), _full(woc.shape), _full(wod.shape), _full((1, d)),
                  _full(rw.shape), _full(rb.shape)] + ([] if hn_buf is None else [pl.BlockSpec(memory_space=pl.ANY)]),
        out_specs=[row(d), pl.BlockSpec((tm, d), lambda i: (i + off, 0)), row(LANES), row(LANES)],
        out_shape=[jax.ShapeDtypeStruct((t, d), F32), jax.ShapeDtypeStruct((hn_rows, d), BF16),
                   jax.ShapeDtypeStruct((t, LANES), I32), jax.ShapeDtypeStruct((t, LANES), F32)],
        input_output_aliases={} if hn_buf is None else {8: 1},
        compiler_params=_cp(("parallel",)),
        name="odd_out",
    )(*args)


EXPERT_ROWS = 512
EXPERT_FF_CHUNK = 512
MOE_SUB = 128
MOE_PAIRS_PER_STEP = 4
F_VALID, F_FIRST, F_LAST = 1, 2, 4


def _moe_expert_kernel(sb_ref, fl_ref, pj_ref, lo_ref, hi_ref, be_ref, *refs):
    del pj_ref, be_ref
    nu = MOE_PAIRS_PER_STEP
    tiles = [refs[3 * u:3 * u + 3] for u in range(nu)]
    wg_ref, wu_ref, wd_ref, o_ref, xacc, gacc = refs[3 * nu:]
    s = pl.program_id(0)
    fl = fl_ref[s]
    er = xacc.shape[0]

    @pl.when((fl & F_VALID) != 0)
    def _():
        @pl.when((fl & F_FIRST) != 0)
        def _():
            xacc[...] = jnp.zeros_like(xacc)
            gacc[...] = jnp.zeros_like(gacc)

        for u, (slot_ref, gate_ref, hn_ref) in enumerate(tiles):
            tile = hn_ref.shape[0]
            for sb in range(er // MOE_SUB):
                @pl.when(jnp.logical_and(lo_ref[s * nu + u] <= sb, sb <= hi_ref[s * nu + u]))
                def _(sb=sb, slot_ref=slot_ref, gate_ref=gate_ref, hn_ref=hn_ref, tile=tile):
                    rs = slice(sb * MOE_SUB, (sb + 1) * MOE_SUB)
                    srow = sb_ref[s] * er + sb * MOE_SUB + lax.broadcasted_iota(I32, (MOE_SUB, tile), 0)
                    m0 = slot_ref[0:1, :] == srow
                    m1 = slot_ref[1:2, :] == srow
                    sel = jnp.where(m0, 1.0, jnp.where(m1, 1.0, 0.0)).astype(BF16)
                    xacc[rs, :] += _mm(sel, hn_ref[...])
                    g = jnp.where(m0, gate_ref[0:1, :], jnp.where(m1, gate_ref[1:2, :], 0.0))
                    gacc[rs, :] += jnp.sum(g, axis=1, keepdims=True)

        @pl.when((fl & F_LAST) != 0)
        def _():
            x = xacc[...].astype(BF16)
            acc = None
            for f in range(0, wg_ref.shape[2], EXPERT_FF_CHUNK):
                gg = _mm(x, wg_ref[0, :, f:f + EXPERT_FF_CHUNK])
                uu = _mm(x, wu_ref[0, :, f:f + EXPERT_FF_CHUNK])
                y = _mm((_silu(gg) * uu).astype(BF16), wd_ref[0, f:f + EXPERT_FF_CHUNK, :])
                acc = y if acc is None else acc + y
            o_ref[...] = (acc * gacc[...]).astype(o_ref.dtype)


def moe_experts(plan, slot_t, gate_t, hn, wg, wu, wd, tile):
    sb, fl, pj, lo, hi, blk_e, n_blocks = plan
    d = hn.shape[1]
    ff = wg.shape[2]
    nu = MOE_PAIRS_PER_STEP
    wmap = lambda s, sb, fl, pj, lo, hi, be: (be[sb[s]], 0, 0)
    tile_specs, tile_args = [], []
    for u in range(nu):
        tmap = lambda s, sb, fl, pj, lo, hi, be, u=u: (0, pj[s * nu + u])
        hmap = lambda s, sb, fl, pj, lo, hi, be, u=u: (pj[s * nu + u], 0)
        tile_specs += [pl.BlockSpec((TOP_K, tile), tmap), pl.BlockSpec((TOP_K, tile), tmap),
                       pl.BlockSpec((tile, d), hmap)]
        tile_args += [slot_t, gate_t, hn]
    return pl.pallas_call(
        _moe_expert_kernel,
        grid_spec=pltpu.PrefetchScalarGridSpec(
            num_scalar_prefetch=6,
            grid=(sb.shape[0],),
            in_specs=tile_specs + [pl.BlockSpec((1, d, ff), wmap), pl.BlockSpec((1, d, ff), wmap),
                                   pl.BlockSpec((1, ff, d), wmap)],
            out_specs=pl.BlockSpec((EXPERT_ROWS, d), lambda s, sb, fl, pj, lo, hi, be: (sb[s], 0)),
            scratch_shapes=[pltpu.VMEM((EXPERT_ROWS, d), F32), pltpu.VMEM((EXPERT_ROWS, 1), F32)],
        ),
        out_shape=jax.ShapeDtypeStruct((n_blocks * EXPERT_ROWS, d), BF16),
        compiler_params=_cp(("arbitrary",)),
        name="moe_experts",
    )(sb, fl, pj, lo, hi, blk_e, *tile_args, wg, wu, wd)


def _moe_combine_kernel(sj_ref, fl_ref, qb_ref, lo_ref, hi_ref, slot_ref, x1_ref, *refs):
    del sj_ref
    nu = MOE_PAIRS_PER_STEP
    yb_refs = refs[:nu]
    fn_ref, o_ref, yacc = refs[nu:]
    s = pl.program_id(0)
    fl = fl_ref[s]
    tile = x1_ref.shape[0]
    er = yb_refs[0].shape[0]

    @pl.when((fl & F_VALID) != 0)
    def _():
        @pl.when((fl & F_FIRST) != 0)
        def _():
            yacc[...] = jnp.zeros_like(yacc)

        for u, yb_ref in enumerate(yb_refs):
            for sb in range(er // MOE_SUB):
                @pl.when(jnp.logical_and(lo_ref[s * nu + u] <= sb, sb <= hi_ref[s * nu + u]))
                def _(sb=sb, u=u, yb_ref=yb_ref):
                    scol = qb_ref[s * nu + u] * er + sb * MOE_SUB + lax.broadcasted_iota(I32, (tile, MOE_SUB), 1)
                    sl = slot_ref[...]
                    sel = jnp.where(sl[:, 0:1] == scol, 1.0, jnp.where(sl[:, 1:2] == scol, 1.0, 0.0)).astype(BF16)
                    yacc[...] += _mm(sel, yb_ref[sb * MOE_SUB:(sb + 1) * MOE_SUB, :])

        @pl.when((fl & F_LAST) != 0)
        def _():
            o_ref[...] = _rms(x1_ref[...] + yacc[...], fn_ref[...])


def moe_combine(sched, slot_cols, x1, yb, fnorm, tile):
    sj, fl, qb, lo, hi = sched
    t, d = x1.shape
    nu = MOE_PAIRS_PER_STEP
    tmap = lambda s, sj, fl, qb, lo, hi: (sj[s], 0)
    yb_specs = [pl.BlockSpec((EXPERT_ROWS, d), lambda s, sj, fl, qb, lo, hi, u=u: (qb[s * nu + u], 0))
                for u in range(nu)]
    return pl.pallas_call(
        _moe_combine_kernel,
        grid_spec=pltpu.PrefetchScalarGridSpec(
            num_scalar_prefetch=5,
            grid=(sj.shape[0],),
            in_specs=[pl.BlockSpec((tile, LANES), tmap), pl.BlockSpec((tile, d), tmap)] + yb_specs
            + [pl.BlockSpec((1, d), lambda s, sj, fl, qb, lo, hi: (0, 0))],
            out_specs=pl.BlockSpec((tile, d), tmap),
            scratch_shapes=[pltpu.VMEM((tile, d), F32)],
        ),
        out_shape=jax.ShapeDtypeStruct((t, d), F32),
        compiler_params=_cp(("arbitrary",)),
        name="moe_combine",
    )(sj, fl, qb, lo, hi, slot_cols, x1, *([yb] * nu), fnorm)


def _ragged_steps(counts, pmax):
    cum = jnp.cumsum(counts)
    total = cum[-1]
    ar = jnp.arange(pmax, dtype=I32)
    p = jnp.minimum(ar, jnp.maximum(total - 1, 0))
    row = jnp.minimum(jnp.sum((cum[None, :] <= p[:, None]).astype(I32), axis=1), counts.shape[0] - 1)
    off = p - (cum[row] - counts[row])
    return row, off, ar < total, total


def _group_flags(gid, valid, total):
    pmax = gid.shape[0]
    ar = jnp.arange(pmax, dtype=I32)
    prev = jnp.concatenate([gid[:1] - 1, gid[:-1]])
    nxt = jnp.concatenate([gid[1:], gid[-1:] + 1])
    first = jnp.logical_or(ar == 0, gid != prev)
    last = jnp.logical_or(ar == total - 1, gid != nxt)
    fl = F_VALID + F_FIRST * first.astype(I32) + F_LAST * last.astype(I32)
    return jnp.where(valid, fl, 0).astype(I32)


def _grouped_steps(cnt, smax):
    nu = MOE_PAIRS_PER_STEP
    row, off, valid, total = _ragged_steps((cnt + nu - 1) // nu, smax)
    k = off[:, None] * nu + jnp.arange(nu, dtype=I32)[None, :]
    c = cnt[row][:, None]
    pvalid = jnp.logical_and(valid[:, None], k < c)
    pidx = (jnp.cumsum(cnt) - cnt)[row][:, None] + jnp.minimum(k, jnp.maximum(c - 1, 0))
    return row, _group_flags(row, valid, total), pidx.reshape(-1), pvalid.reshape(-1)


def _moe_plan(e_all, tile):
    t = e_all.shape[0]
    nt = t // tile
    er = EXPERT_ROWS
    ex = jnp.arange(N_EXPERTS, dtype=I32)[None, :]
    oh = jnp.logical_or(e_all[:, 0:1] == ex, e_all[:, 1:2] == ex).astype(I32)
    cs = jnp.cumsum(oh, axis=0)
    rank = cs - oh
    counts = cs[-1]
    padded = (counts + er - 1) // er * er
    pad_end = jnp.cumsum(padded)
    pad_start = pad_end - padded
    slot_e = pad_start[None, :] + rank
    slot = jnp.stack([jnp.sum(jnp.where(e_all[:, k:k + 1] == ex, slot_e, 0), axis=1) for k in range(TOP_K)],
                     axis=1).astype(I32)
    n_blocks = -(-(t * TOP_K) // er) + N_EXPERTS
    blk = jnp.arange(n_blocks, dtype=I32)
    blk_e = jnp.minimum(jnp.sum((blk[:, None] * er >= pad_end[None, :]).astype(I32), axis=1), N_EXPERTS - 1)
    n_used = pad_end[-1] // er
    tstart = jnp.concatenate([rank[::tile], counts[None, :]], axis=0)
    first_slot = pad_start[None, :] + tstart[:-1]
    last_slot = pad_start[None, :] + tstart[1:] - 1

    def sub_range(b, fs, ls):
        lo = jnp.maximum(fs - b * er, 0) // MOE_SUB
        hi = jnp.where(ls >= fs, jnp.minimum(ls - b * er, er - 1), -1) // MOE_SUB
        return lo.astype(I32), jnp.maximum(hi, -1).astype(I32)

    be_oh = (blk_e[:, None] == ex).astype(I32)
    r0 = blk * er - jnp.sum(be_oh * pad_start[None, :], axis=1)
    r1 = jnp.minimum(r0 + er - 1, jnp.sum(be_oh * counts[None, :], axis=1) - 1)
    ts_b = jnp.sum(be_oh[:, None, :] * tstart[None, 1:, :], axis=2)
    jlo = jnp.minimum(jnp.sum((ts_b <= r0[:, None]).astype(I32), axis=1), nt - 1)
    jhi = jnp.minimum(jnp.sum((ts_b <= r1[:, None]).astype(I32), axis=1), nt - 1)
    cnt_b = jnp.where(blk < n_used, jhi - jlo + 1, 0)
    nu = MOE_PAIRS_PER_STEP
    pmax = n_blocks + nt * N_EXPERTS
    row, off, _, _ = _ragged_steps(cnt_b, pmax)
    pj = (jlo[row] + off).astype(I32)
    je = pj * N_EXPERTS + blk_e[row]
    lo, hi = sub_range(row, first_slot.reshape(-1)[je], last_slot.reshape(-1)[je])
    sb, sfl, pidx, pvalid = _grouped_steps(cnt_b, n_blocks + -(-pmax // nu))
    expert_plan = (sb, sfl, pj[pidx], jnp.where(pvalid, lo[pidx], 1), jnp.where(pvalid, hi[pidx], 0),
                   blk_e.astype(I32), n_blocks)

    def combine_sched(j0, j1):
        ntr = j1 - j0
        fs = first_slot[j0:j1].reshape(-1)
        ls = last_slot[j0:j1].reshape(-1)
        blo = fs // er
        nb = jnp.where(ls >= fs, ls // er - blo + 1, 0)
        pmax_c = min(n_blocks + ntr * N_EXPERTS, 2 * ntr * N_EXPERTS)
        r, o, _, _ = _ragged_steps(nb, pmax_c)
        qb = (blo[r] + o).astype(I32)
        lo_c, hi_c = sub_range(qb, fs[r], ls[r])
        cnt_j = jnp.sum(nb.reshape(ntr, N_EXPERTS), axis=1)
        sj, sfl_c, pidx_c, pvalid_c = _grouped_steps(cnt_j, ntr + -(-pmax_c // nu))
        return (sj, sfl_c, qb[pidx_c], jnp.where(pvalid_c, lo_c[pidx_c], 1), jnp.where(pvalid_c, hi_c[pidx_c], 0))

    return slot, expert_plan, combine_sched


def _rope_tables(pos):
    half = MLA_ROPE // 2
    inv = ROPE_THETA ** (-jnp.arange(half, dtype=F32) * 2.0 / MLA_ROPE)
    ang = pos.astype(F32)[:, None] * inv[None, :]
    cos = jnp.cos(ang)
    sin = jnp.sin(ang)
    n = pos.shape[0]
    cos_t = jnp.concatenate([cos, cos, jnp.ones((n, LANES - MLA_ROPE), F32)], axis=1)
    sin_t = jnp.concatenate([-sin, sin, jnp.zeros((n, LANES - MLA_ROPE), F32)], axis=1)
    return cos_t, sin_t


def _block_diag(w):
    b, i, j = w.shape
    eye = jnp.eye(b, dtype=w.dtype)
    return (eye[:, None, :, None] * w[:, :, None, :]).reshape(b * i, b * j)


def _pad_rows(a, rows):
    return jnp.pad(a, ((0, 0), (0, rows - a.shape[1]), (0, 0)))


def kernel(x_prompt, x_sample, state_swa_kv, state_gdn_conv, state_gdn_s, state_lru_conv, state_lru_h, cache_mla, page_table, e_ln_mix, e_w_in, e_gdn_conv_w, e_gdn_a_log, e_gdn_dt_bias, e_gdn_norm, e_swa_sinks, e_w_out, e_ln_ffn, e_ffn_gate, e_ffn_up, e_ffn_down, o_ln_mix, o_w_in, o_lru_conv_w, o_lru_conv_b, o_lru_w_a, o_lru_b_a, o_lru_w_x, o_lru_b_x, o_lru_lambda, o_mla_q_norm, o_mla_w_uq, o_mla_kv_norm, o_mla_w_uk, o_mla_w_uv, o_w_out, o_ln_ffn, o_router_w, o_router_b, o_exp_gate, o_exp_up, o_exp_down, final_norm):
    nb, lp, d = x_prompt.shape
    ns, ls, _ = x_sample.shape
    past_len = page_table.shape[1] * PAGE_SIZE
    tp, ts = nb * lp, ns * ls
    xp = x_prompt.reshape(tp, d)
    xs = x_sample.reshape(ts, d)
    row1 = lambda v: v.reshape(1, -1)

    na = (SWA_HEADS + 2 * SWA_KV_HEADS) * SWA_HEAD_DIM
    nz = GDN_HEADS * GDN_DV
    w_in = e_w_in[0].astype(BF16)
    o1, o2, o3 = na, na + GDN_QKV, na + GDN_QKV + nz
    w_groups = [w_in[:, :o1], w_in[:, o1:o2], w_in[:, o2:o3],
                jnp.pad(w_in[:, o3:], ((0, 0), (0, LANES - 2 * GDN_HEADS)))]
    ln = row1(e_ln_mix[0])
    qkv_p, gx_p, z_p, bg_p = norm_proj(xp, ln, w_groups)
    qkv_s, gx_s, z_s, bg_s = norm_proj(xs, ln, w_groups)

    sinks = e_swa_sinks[0]
    oa_p = swa_prompt(qkv_p, sinks, nb, lp)
    wbuf = state_swa_kv.shape[2]
    nkv = 2 * SWA_KV_HEADS * SWA_HEAD_DIM
    buf = state_swa_kv[0].reshape(ns, wbuf, nkv)
    qkv_s3 = qkv_s.reshape(ns, ls, na)
    oa_s = swa_sample(_pad_rows(qkv_s3, SUBLANES), buf, sinks, ls)[:, :ls].reshape(ts, -1)
    kv_p = qkv_p.reshape(nb, lp, na)[:, lp - SWA_WINDOW:, na - nkv:]
    swa_kv_p = kv_p.reshape(1, nb, SWA_WINDOW, 2, SWA_KV_HEADS, SWA_HEAD_DIM)
    kv_s = jnp.concatenate([buf, qkv_s3[:, :, na - nkv:]], axis=1)[:, ls:]
    swa_kv_s = kv_s.reshape(1, ns, wbuf, 2, SWA_KV_HEADS, SWA_HEAD_DIM)

    gx_p3 = gx_p.reshape(nb, lp, GDN_QKV)
    gx_s3 = gx_s.reshape(ns, ls, GDN_QKV)
    gargs = (e_gdn_conv_w[0], e_gdn_a_log[0], e_gdn_dt_bias[0], e_gdn_norm[0])
    ob_p, gs_p = gdn(gx_p3, z_p.reshape(nb, lp, nz), bg_p.reshape(nb, lp, LANES),
                     jnp.zeros((nb, SUBLANES, GDN_QKV), F32), jnp.zeros((nb, GDN_HEADS, GDN_DK, GDN_DV), F32),
                     *gargs, chunk=math.gcd(lp, GDN_CHUNK), valid=math.gcd(lp, GDN_CHUNK), tl=min(lp, 512))
    cprev_s = jnp.pad(state_gdn_conv[0], ((0, 0), (SUBLANES - (GDN_CONV - 1), 0), (0, 0)))
    assert ls <= GDN_CHUNK and math.gcd(ls, GDN_CHUNK) == ls
    ob_s, gs_s = gdn(_pad_rows(gx_s3, GDN_CHUNK), _pad_rows(z_s.reshape(ns, ls, nz), GDN_CHUNK),
                     _pad_rows(bg_s.reshape(ns, ls, LANES), GDN_CHUNK), cprev_s, state_gdn_s[0],
                     *gargs, chunk=GDN_CHUNK, valid=ls, tl=GDN_CHUNK)
    ob_s = ob_s[:, :ls].reshape(ts, nz)
    gconv_p = gx_p3[:, lp - (GDN_CONV - 1):][None]
    gconv_s = gx_s3[:, ls - (GDN_CONV - 1):][None]

    w_out = e_w_out[0].astype(BF16)
    nqa = SWA_HEADS * SWA_HEAD_DIM
    ffn = (row1(e_ln_ffn[0]), e_ffn_gate[0].astype(BF16), e_ffn_up[0].astype(BF16), e_ffn_down[0].astype(BF16))
    xp = even_out(xp, oa_p, ob_p.reshape(tp, nz), w_out[:nqa], w_out[nqa:], *ffn)
    xs = even_out(xs, oa_s, ob_s, w_out[:nqa], w_out[nqa:], *ffn)

    w_in = o_w_in[0].astype(BF16)
    c0 = LRU_WIDTH
    c1 = 2 * LRU_WIDTH
    c2 = c1 + MLA_Q_RANK
    c3 = c2 + MLA_KV_RANK
    w_groups = [w_in[:, :c0], w_in[:, c0:c1], w_in[:, c1:c2], w_in[:, c2:c3],
                jnp.pad(w_in[:, c3:], ((0, 0), (0, LANES - MLA_ROPE)))]
    hd_q = MLA_NOPE + MLA_ROPE
    half = MLA_ROPE // 2
    wq = o_mla_w_uq[0].reshape(MLA_Q_RANK, MLA_HEADS, hd_q)
    wuq = jnp.concatenate([wq[:, :, MLA_NOPE:], wq[:, :, :MLA_NOPE],
                           jnp.zeros((MLA_Q_RANK, MLA_HEADS, LANES - hd_q), F32)], axis=2)
    wuq = wuq.reshape(MLA_Q_RANK, MLA_HEADS * LANES).astype(BF16)
    w_uk = o_mla_w_uk[0]
    w_uv = o_mla_w_uv[0]
    wuk = jnp.pad(w_uk, ((0, 0), (0, 0), (MLA_ROPE, LANES - hd_q))).reshape(MLA_KV_RANK, -1).astype(BF16)
    wuv = jnp.pad(w_uv, ((0, 0), (0, 0), (0, LANES - MLA_V))).reshape(MLA_KV_RANK, -1).astype(BF16)
    wabs = _block_diag(jnp.pad(jnp.transpose(w_uk, (1, 2, 0)), ((0, 0), (MLA_ROPE, LANES - hd_q), (0, 0)))).astype(BF16)
    wuv_bd = _block_diag(jnp.pad(jnp.transpose(w_uv, (1, 0, 2)), ((0, 0), (0, 0), (0, LANES - MLA_V)))).astype(BF16)
    ln = row1(o_ln_mix[0])
    qn, kvn = row1(o_mla_q_norm[0]), row1(o_mla_kv_norm[0])
    tm_p = _row_tile(tp, (512, 256, 128, 64, 32, 16, 8))
    assert lp % tm_p == 0 or tm_p % lp == 0
    pos_p = jnp.arange(max(lp, tm_p), dtype=I32) % lp
    pos_s = past_len + (jnp.arange(ts, dtype=I32) % ls)
    cos_p, sin_p = _rope_tables(pos_p)
    cos_s, sin_s = _rope_tables(pos_s)
    xb_p, gb_p, q_p, k_p, v_p, rows_p = odd_in(xp, ln, w_groups, qn, kvn, wuq, cos_p, sin_p, [wuk, wuv], False)
    xb_s, gb_s, q_s, qlat_s, rows_s = odd_in(xs, ln, w_groups, qn, kvn, wuq, cos_s, sin_s, [wabs], True)

    lru_w = (o_lru_conv_w[0], row1(o_lru_conv_b[0]), _block_diag(o_lru_w_a[0]).astype(BF16),
             _block_diag(o_lru_w_x[0]).astype(BF16), row1(o_lru_b_a[0]), row1(o_lru_b_x[0]), row1(o_lru_lambda[0]))
    xb_p3 = xb_p.reshape(nb, lp, LRU_WIDTH)
    yc_p, lh_p = lru_prompt(xb_p3, gb_p.reshape(nb, lp, LRU_WIDTH), jnp.zeros((nb, SUBLANES, LRU_WIDTH), F32),
                            jnp.zeros((nb, 1, LRU_WIDTH), F32), *lru_w, tl=min(lp, 256))
    xb_s3 = xb_s.reshape(ns, ls, LRU_WIDTH)
    tmaj = lambda a: jnp.transpose(a, (1, 0, 2))
    yc_s, lh_s = lru_sample(tmaj(xb_s3), tmaj(gb_s.reshape(ns, ls, LRU_WIDTH)), tmaj(state_lru_conv[0]),
                            state_lru_h[0], *lru_w)
    yc_s = tmaj(yc_s).reshape(ts, LRU_WIDTH)
    lconv_p = xb_p3[:, lp - 3:][None]
    lconv_s = xb_s3[:, ls - 3:][None]

    od_p = mla_prompt(q_p, k_p, v_p, nb, lp, tq=min(lp, 512))
    q_pe = q_s.reshape(ts, MLA_HEADS, LANES)[:, :, :MLA_ROPE]
    q_cat = jnp.concatenate([qlat_s.reshape(ts, MLA_HEADS, MLA_KV_RANK).astype(BF16), q_pe], axis=-1)
    q_cat = q_cat.reshape(ns, ls * MLA_HEADS, MLA_ROW)
    n_pages = page_table.shape[1]
    cache_t = jnp.swapaxes(cache_mla.reshape(cache_mla.shape[1:]), 1, 2)
    o_lat = mla_sample(page_table, q_cat, rows_s.reshape(ns, ls, MLA_ROW), cache_t,
                       n_pages_step=math.gcd(n_pages, MLA_SAMPLE_PAGES))
    o_lat = o_lat.reshape(ts, MLA_HEADS, MLA_ROW)[:, :, :MLA_KV_RANK].reshape(ts, MLA_HEADS * MLA_KV_RANK)
    od_s = matmul(o_lat, wuv_bd, BF16)

    w_out = o_w_out[0].astype(BF16)
    wod = jnp.pad(w_out[LRU_WIDTH:].reshape(MLA_HEADS, MLA_V, d), ((0, 0), (0, LANES - MLA_V), (0, 0)))
    wod = wod.reshape(MLA_HEADS * LANES, d)
    rw = jnp.pad(o_router_w[0], ((0, 0), (0, LANES - N_EXPERTS)))
    rb = jnp.pad(o_router_b[0], (0, LANES - N_EXPERTS)).reshape(1, LANES)
    lnf = row1(o_ln_ffn[0])
    tall = tp + ts
    x1_p, hn, idx_p, gate_p = odd_out(xp, yc_p.reshape(tp, LRU_WIDTH), od_p, w_out[:LRU_WIDTH], wod, lnf, rw, rb,
                                      tall, 0)
    x1_s, hn, idx_s, gate_s = odd_out(xs, yc_s, od_s, w_out[:LRU_WIDTH], wod, lnf, rw, rb, tall, tp, hn)

    tile = math.gcd(math.gcd(tp, ts), EXPERT_ROWS)
    e_all = jnp.concatenate([idx_p[:, :TOP_K], idx_s[:, :TOP_K]], axis=0)
    g_all = jnp.concatenate([gate_p[:, :TOP_K], gate_s[:, :TOP_K]], axis=0)
    slot, expert_plan, combine_sched = _moe_plan(e_all, tile)
    yb = moe_experts(expert_plan, slot.T, g_all.T, hn, o_exp_gate[0].astype(BF16), o_exp_up[0].astype(BF16),
                     o_exp_down[0].astype(BF16), tile)
    slot_cols = jnp.pad(slot, ((0, 0), (0, LANES - TOP_K)))
    fnw = row1(final_norm)
    y_p = moe_combine(combine_sched(0, tp // tile), slot_cols[:tp], x1_p, yb, fnw, tile).reshape(nb, lp, d)
    y_s = moe_combine(combine_sched(tp // tile, tall // tile), slot_cols[tp:], x1_s, yb, fnw, tile).reshape(ns, ls, d)

    return (y_p, y_s, swa_kv_p, swa_kv_s, gconv_p, gconv_s, gs_p[None], gs_s[None],
            lconv_p, lconv_s, lh_p.reshape(1, nb, LRU_WIDTH), lh_s[None],
            rows_p.reshape(1, nb, lp // PAGE_SIZE, PAGE_SIZE, MLA_ROW), rows_s.reshape(1, ns, ls, MLA_ROW))
```

```python
import functools
import math

import jax
import jax.numpy as jnp
from jax import lax
from jax.experimental import pallas as pl
from jax.experimental.pallas import tpu as pltpu

F32 = jnp.float32
BF16 = jnp.bfloat16
I32 = jnp.int32
HI = lax.Precision.HIGHEST

D_MODEL = 1024
PAGE_SIZE = 128
SWA_WINDOW = 128
SWA_HEADS = 8
SWA_KV_HEADS = 2
SWA_GROUP = SWA_HEADS // SWA_KV_HEADS
SWA_HEAD_DIM = 64
GDN_HEADS = 4
GDN_DK = 128
GDN_DV = 128
GDN_CONV = 4
GDN_CHUNK = 64
GDN_QKV = GDN_HEADS * (2 * GDN_DK + GDN_DV)
LRU_WIDTH = 512
LRU_BLOCKS = 8
LRU_BLOCK_W = LRU_WIDTH // LRU_BLOCKS
LRU_C = 8.0
MLA_HEADS = 8
MLA_Q_RANK = 384
MLA_KV_RANK = 256
MLA_NOPE = 64
MLA_ROPE = 32
MLA_V = 64
MLA_ROW = MLA_KV_RANK + MLA_ROPE
ROPE_THETA = 10000.0
D_FF = 2816
N_EXPERTS = 8
TOP_K = 2
MOE_FF = 2048
NORM_EPS = 1e-6

LANES = 128
SUBLANES = 8
VMEM_LIMIT = 56 << 20

NN = (((1,), (0,)), ((), ()))
NT = (((1,), (1,)), ((), ()))
TN = (((0,), (0,)), ((), ()))


def _mm(a, b, dims=NN, precision=None):
    return lax.dot_general(a, b, dims, precision=precision, preferred_element_type=F32)


def _cp(sem):
    return pltpu.CompilerParams(dimension_semantics=sem, vmem_limit_bytes=VMEM_LIMIT)


def _rms(x, w):
    return x * lax.rsqrt(jnp.mean(x * x, axis=-1, keepdims=True) + NORM_EPS) * w


def _silu(x):
    return x * jax.nn.sigmoid(x)


def _full(shape):
    nd = len(shape)
    return pl.BlockSpec(shape, lambda *a: (0,) * nd)


def _row_tile(t, pref):
    for c in pref:
        if t % c == 0:
            return c
    return t


def _norm_proj_kernel(x_ref, ln_ref, *refs):
    n = len(refs) // 2
    h = _rms(x_ref[...], ln_ref[...]).astype(BF16)
    for w_ref, o_ref in zip(refs[:n], refs[n:]):
        o_ref[...] = _mm(h, w_ref[...])


def norm_proj(x, ln, ws):
    t, d = x.shape
    tm = _row_tile(t, (512, 256, 128, 64, 32, 16, 8))
    return pl.pallas_call(
        _norm_proj_kernel,
        grid=(t // tm,),
        in_specs=[pl.BlockSpec((tm, d), lambda i: (i, 0)), _full((1, d))] + [_full(w.shape) for w in ws],
        out_specs=[pl.BlockSpec((tm, w.shape[1]), lambda i: (i, 0)) for w in ws],
        out_shape=[jax.ShapeDtypeStruct((t, w.shape[1]), F32) for w in ws],
        compiler_params=_cp(("parallel",)),
        name="norm_proj",
    )(x, ln, *ws)


def _swa_softmax_pv(parts, sink):
    m = sink
    for s, _ in parts:
        m = jnp.maximum(m, jnp.max(s, axis=-1, keepdims=True))
    den = jnp.exp(sink - m)
    o = None
    for s, v in parts:
        p = jnp.exp(s - m)
        den = den + jnp.sum(p, axis=-1, keepdims=True)
        pv = _mm(p.astype(BF16), v)
        o = pv if o is None else o + pv
    return o / den


def _swa_prompt_kernel(sink_ref, q_ref, kvc_ref, kvp_ref, o_ref):
    b = pl.program_id(1)
    w = SWA_WINDOW
    hd = SWA_HEAD_DIM
    ng = SWA_GROUP
    q = q_ref[...]
    band = jnp.concatenate([kvp_ref[...], kvc_ref[...]], axis=0).astype(BF16)
    r = lax.broadcasted_iota(I32, (ng * w, 1), 0)
    grp = r // w
    qi = r % w
    t = lax.broadcasted_iota(I32, (ng * w, 2 * w), 1)
    dist = qi + w - t
    valid = jnp.logical_and(jnp.logical_and(dist >= 0, dist < w), jnp.logical_or(t >= w, b > 0))
    distf = dist.astype(F32)
    outs = [None] * SWA_HEADS
    scores = []
    for j in range(SWA_KV_HEADS):
        qs = jnp.concatenate([q[:, (j * ng + g) * hd:(j * ng + g + 1) * hd] for g in range(ng)], axis=0)
        scores.append(_mm(qs.astype(BF16), band[:, j * hd:(j + 1) * hd], NT) * (hd ** -0.5))
    for j in range(SWA_KV_HEADS):
        heads = [j * ng + g for g in range(ng)]
        slope = jnp.zeros((ng * w, 1), F32)
        sink = jnp.zeros((ng * w, 1), F32)
        for g, h in enumerate(heads):
            slope = jnp.where(grp == g, 2.0 ** (-8.0 * (h + 1) / SWA_HEADS), slope)
            sink = jnp.where(grp == g, sink_ref[h], sink)
        vb = band[:, (SWA_KV_HEADS + j) * hd:(SWA_KV_HEADS + j + 1) * hd]
        s = jnp.where(valid, scores[j] - slope * distf, -jnp.inf)
        o = _swa_softmax_pv([(s, vb)], sink)
        for g, h in enumerate(heads):
            outs[h] = o[g * w:(g + 1) * w, :]
    o_ref[...] = jnp.concatenate(outs, axis=1)


def swa_prompt(qkv, sinks, n, l):
    w = SWA_WINDOW
    nb = l // w
    nq = SWA_HEADS * SWA_HEAD_DIM
    nkv = 2 * SWA_KV_HEADS * SWA_HEAD_DIM
    return pl.pallas_call(
        _swa_prompt_kernel,
        grid=(n, nb),
        in_specs=[
            pl.BlockSpec(memory_space=pltpu.SMEM),
            pl.BlockSpec((w, nq), lambda i, b: (i * nb + b, 0)),
            pl.BlockSpec((w, nkv), lambda i, b: (i * nb + b, nq // nkv)),
            pl.BlockSpec((w, nkv), lambda i, b: (i * nb + jnp.maximum(b - 1, 0), nq // nkv)),
        ],
        out_specs=pl.BlockSpec((w, nq), lambda i, b: (i * nb + b, 0)),
        out_shape=jax.ShapeDtypeStruct((n * l, nq), F32),
        compiler_params=_cp(("parallel", "arbitrary")),
        name="swa_prompt",
    )(sinks, qkv, qkv, qkv)


SWA_SAMPLE_REQS = 4


def _swa_sample_kernel(sink_ref, q_ref, buf_ref, o_ref, *, l):
    hd = SWA_HEAD_DIM
    ng = SWA_GROUP
    nr, lp, _ = q_ref.shape
    wb = buf_ref.shape[1]
    r = lax.broadcasted_iota(I32, (ng * lp, 1), 0)
    grp = r // lp
    qi = r % lp
    dist_b = qi + wb - lax.broadcasted_iota(I32, (ng * lp, wb), 1)
    valid_b = dist_b < SWA_WINDOW
    tn = lax.broadcasted_iota(I32, (ng * lp, lp), 1)
    dist_n = qi - tn
    valid_n = jnp.logical_and(tn <= qi, tn < l)
    slopes, sinks = [], []
    for j in range(SWA_KV_HEADS):
        slope = jnp.zeros((ng * lp, 1), F32)
        sink = jnp.zeros((ng * lp, 1), F32)
        for g in range(ng):
            h = j * ng + g
            slope = jnp.where(grp == g, 2.0 ** (-8.0 * (h + 1) / SWA_HEADS), slope)
            sink = jnp.where(grp == g, sink_ref[h], sink)
        slopes.append(slope)
        sinks.append(sink)
    chains = []
    for rq in range(nr):
        x = q_ref[rq]
        buf = buf_ref[rq].astype(BF16)
        kvn = x[:, SWA_HEADS * hd:].astype(BF16)
        for j in range(SWA_KV_HEADS):
            qs = jnp.concatenate([x[:, (j * ng + g) * hd:(j * ng + g + 1) * hd] for g in range(ng)], axis=0)
            qs = qs.astype(BF16)
            ks = slice(j * hd, (j + 1) * hd)
            vs = slice((SWA_KV_HEADS + j) * hd, (SWA_KV_HEADS + j + 1) * hd)
            s_b = _mm(qs, buf[:, ks], NT) * (hd ** -0.5)
            s_n = _mm(qs, kvn[:, ks], NT) * (hd ** -0.5)
            chains.append((rq, j, s_b, s_n, buf[:, vs], kvn[:, vs]))
    outs = [[None] * SWA_HEADS for _ in range(nr)]
    for rq, j, s_b, s_n, vb, vn in chains:
        s_b = jnp.where(valid_b, s_b - slopes[j] * dist_b.astype(F32), -jnp.inf)
        s_n = jnp.where(valid_n, s_n - slopes[j] * dist_n.astype(F32), -jnp.inf)
        o = _swa_softmax_pv([(s_b, vb), (s_n, vn)], sinks[j])
        for g in range(ng):
            outs[rq][j * ng + g] = o[g * lp:(g + 1) * lp, :]
    for rq in range(nr):
        o_ref[rq] = jnp.concatenate(outs[rq], axis=1)


def swa_sample(qkv_pad, buf, sinks, l):
    n, lp, c = qkv_pad.shape
    wb = buf.shape[1]
    nq = SWA_HEADS * SWA_HEAD_DIM
    nr = math.gcd(n, SWA_SAMPLE_REQS)
    return pl.pallas_call(
        functools.partial(_swa_sample_kernel, l=l),
        grid=(n // nr,),
        in_specs=[
            pl.BlockSpec(memory_space=pltpu.SMEM),
            pl.BlockSpec((nr, lp, c), lambda i: (i, 0, 0)),
            pl.BlockSpec((nr, wb, buf.shape[2]), lambda i: (i, 0, 0)),
        ],
        out_specs=pl.BlockSpec((nr, lp, nq), lambda i: (i, 0, 0)),
        out_shape=jax.ShapeDtypeStruct((n, lp, nq), F32),
        compiler_params=_cp(("parallel",)),
        name="swa_sample",
    )(sinks, qkv_pad, buf)


GDN_SOLVE_GROUP = 8


def _mm3(a, b):
    ah = a.astype(BF16)
    al = (a - ah.astype(F32)).astype(BF16)
    bh = b.astype(BF16)
    bl = (b - bh.astype(F32)).astype(BF16)
    return _mm(jnp.concatenate([ah, ah, al], axis=1), jnp.concatenate([bh, bl, bh], axis=0))


def _block_rows(x, c, nh):
    blk = lax.broadcasted_iota(I32, x.shape, 1) // c
    return jnp.concatenate([jnp.where(blk == i, x, 0.0) for i in range(nh)], axis=0)


def _gdn_kernel(x_ref, z_ref, bg_ref, cprev_ref, s0_ref, cw_ref, alog_ref, dtb_ref, gn_ref,
                o_ref, sfin_ref, xbuf, s_sc, q_sc, k_sc, vb_sc, kb_sc, qd_sc, kd_sc, u_sc, w_sc, be_sc, gc_sc, egl_sc,
                qkd_sc, *, chunk, valid):
    t = pl.program_id(1)
    tl = x_ref.shape[1]
    c = chunk
    nck = tl // c
    nh = GDN_HEADS
    dk = GDN_DK

    @pl.when(t == 0)
    def _():
        xbuf[0:SUBLANES, :] = cprev_ref[0]
        s_sc[...] = s0_ref[0]

    xbuf[SUBLANES:SUBLANES + tl, :] = x_ref[0]
    cw = cw_ref[...]
    conv = xbuf[5:5 + tl, :] * cw[0:1, :]
    for j in range(1, GDN_CONV):
        conv = conv + xbuf[5 + j:5 + j + tl, :] * cw[j:j + 1, :]
    xbuf[0:SUBLANES, :] = xbuf[tl:tl + SUBLANES, :]
    act = _silu(conv)

    row = lax.broadcasted_iota(I32, (tl, 1), 0)
    rmask = (row % c) < valid
    bg = bg_ref[0]
    beta = jax.nn.sigmoid(bg)
    g = -jnp.exp(alog_ref[...]) * jax.nn.softplus(bg + dtb_ref[...])
    g = jnp.where(rmask, g, 0.0)
    rc = row % c
    s = 1
    while s < c:
        g = g + jnp.where(rc >= s, pltpu.roll(g, s, axis=0), 0.0)
        s *= 2
    glast = jnp.broadcast_to(g.reshape(nck, c, LANES)[:, c - 1:c, :], (nck, c, LANES)).reshape(tl, LANES)
    egc = jnp.exp(g)
    kfac = jnp.exp(glast - g)
    be_sc[...] = beta
    gc_sc[...] = g
    egl_sc[...] = jnp.exp(glast)
    for h in range(nh):
        hs = slice(h * dk, (h + 1) * dk)
        qh = act[:, h * dk:(h + 1) * dk]
        kh = act[:, (nh + h) * dk:(nh + h + 1) * dk]
        vh = act[:, (2 * nh + h) * dk:(2 * nh + h + 1) * dk]
        qh = qh * lax.rsqrt(jnp.sum(qh * qh, axis=-1, keepdims=True) + NORM_EPS) * (dk ** -0.5)
        kh = kh * lax.rsqrt(jnp.sum(kh * kh, axis=-1, keepdims=True) + NORM_EPS)
        qh = jnp.where(rmask, qh, 0.0)
        kh = jnp.where(rmask, kh, 0.0)
        vh = jnp.where(rmask, vh, 0.0)
        b_h = beta[:, h:h + 1]
        e_h = egc[:, nh + h:nh + h + 1]
        q_sc[:, hs] = qh
        k_sc[:, hs] = kh
        vb_sc[:, hs] = vh * b_h
        kb_sc[:, hs] = kh * (b_h * e_h)
        qd_sc[:, hs] = qh * e_h
        kd_sc[:, hs] = kh * kfac[:, nh + h:nh + h + 1]

    ii = lax.broadcasted_iota(I32, (c, nh * c), 0)
    jl = lax.broadcasted_iota(I32, (c, nh * c), 1) % c
    eye_cat = (ii == jl).astype(F32)
    gn = gn_ref[...]
    n_factors = max(1, int(math.ceil(math.log2(valid))))

    ng = math.gcd(nck, GDN_SOLVE_GROUP)

    def solve_body(gi, carry):
        rows_g = [pl.ds(pl.multiple_of((gi * ng + g) * c, c), c) for g in range(ng)]
        a_g, qkd_g = [], []
        for rows in rows_g:
            gcs = gc_sc[rows, :]
            bes = be_sc[rows, :]
            kk, qk, gexp, bexp = [], [], [], []
            for h in range(nh):
                hs = slice(h * dk, (h + 1) * dk)
                kb16 = k_sc[rows, hs].astype(BF16)
                kk.append(_mm(kb16, kb16, NT))
                qk.append(_mm(q_sc[rows, hs].astype(BF16), kb16, NT))
                gexp.append(jnp.broadcast_to(gcs[:, nh + h:nh + h + 1], (c, c)))
                bexp.append(jnp.broadcast_to(bes[:, h:h + 1], (c, c)))
            kk, qk, gexp, bexp = [jnp.concatenate(v, axis=1) for v in (kk, qk, gexp, bexp)]
            grow = jnp.sum(jnp.where(ii == jl, gexp, 0.0), axis=0, keepdims=True)
            decay = jnp.where(ii >= jl, jnp.exp(gexp - grow), 0.0)
            a_g.append(jnp.where(ii > jl, bexp * kk * decay, 0.0))
            qkd_g.append(qk * decay)
        t_g = [eye_cat - a for a in a_g]
        p_g = a_g
        for _ in range(n_factors - 1):
            p_g = [_mm3(p, _block_rows(p, c, nh)) for p in p_g]
            t_g = [_mm3(tv, _block_rows(eye_cat + p, c, nh)) for tv, p in zip(t_g, p_g)]
        for rows, tv, qkd in zip(rows_g, t_g, qkd_g):
            rhs = jnp.concatenate(
                [jnp.concatenate([vb_sc[rows, h * dk:(h + 1) * dk], kb_sc[rows, h * dk:(h + 1) * dk]], axis=1)
                 for h in range(nh)], axis=0)
            sol = _mm3(_block_rows(tv, c, nh), rhs)
            for h in range(nh):
                hs = slice(h * dk, (h + 1) * dk)
                u_sc[rows, hs] = sol[h * c:(h + 1) * c, :GDN_DV]
                w_sc[rows, hs] = sol[h * c:(h + 1) * c, GDN_DV:]
            qkd_sc[rows, :] = qkd
        return carry

    lax.fori_loop(0, nck // ng, solve_body, 0)

    def scan_body(ci, carry):
        rows = pl.ds(pl.multiple_of(ci * c, c), c)
        egl = egl_sc[rows, :]
        sts, v_news, o1s = [], [], []
        for h in range(nh):
            hs = slice(h * dk, (h + 1) * dk)
            st = s_sc[h]
            wq = jnp.concatenate([w_sc[rows, hs], qd_sc[rows, hs]], axis=0)
            r = _mm(wq.astype(BF16), st.astype(BF16))
            sts.append(st)
            v_news.append(u_sc[rows, hs] - r[:c])
            o1s.append(r[c:])
        o2 = _mm(_block_rows(qkd_sc[rows, :], c, nh).astype(BF16), jnp.concatenate(v_news, axis=0).astype(BF16))
        for h in range(nh):
            hs = slice(h * dk, (h + 1) * dk)
            upd = _mm(kd_sc[rows, hs].astype(BF16), v_news[h].astype(BF16), TN)
            s_sc[h] = sts[h] * egl[0:1, nh + h:nh + h + 1] + upd
            o = o1s[h] + o2[h * c:(h + 1) * c]
            o_ref[0, rows, hs] = _rms(o, gn) * _silu(z_ref[0, rows, hs])
        return carry

    lax.fori_loop(0, nck, scan_body, 0)

    @pl.when(t == pl.num_programs(1) - 1)
    def _():
        sfin_ref[0] = s_sc[...]


def gdn(x, z, bg, cprev, s0, conv_w, a_log, dt_bias, gnorm, chunk, valid, tl):
    n, lp, cq = x.shape
    nt = lp // tl
    nv = GDN_HEADS * GDN_DV
    alog = jnp.zeros((1, LANES), F32).at[0, GDN_HEADS:2 * GDN_HEADS].set(a_log)
    dtb = jnp.zeros((1, LANES), F32).at[0, GDN_HEADS:2 * GDN_HEADS].set(dt_bias)
    seq3 = lambda w: pl.BlockSpec((1, tl, w), lambda i, t: (i, t, 0))
    return pl.pallas_call(
        functools.partial(_gdn_kernel, chunk=chunk, valid=valid),
        grid=(n, nt),
        in_specs=[
            seq3(cq), seq3(nv), seq3(LANES),
            pl.BlockSpec((1, SUBLANES, cq), lambda i, t: (i, 0, 0)),
            pl.BlockSpec((1, GDN_HEADS, GDN_DK, GDN_DV), lambda i, t: (i, 0, 0, 0)),
            _full((GDN_CONV, cq)), _full((1, LANES)), _full((1, LANES)), _full((1, GDN_DV)),
        ],
        out_specs=[seq3(nv), pl.BlockSpec((1, GDN_HEADS, GDN_DK, GDN_DV), lambda i, t: (i, 0, 0, 0))],
        out_shape=[jax.ShapeDtypeStruct((n, lp, nv), F32),
                   jax.ShapeDtypeStruct((n, GDN_HEADS, GDN_DK, GDN_DV), F32)],
        scratch_shapes=[
            pltpu.VMEM((tl + SUBLANES, cq), F32),
            pltpu.VMEM((GDN_HEADS, GDN_DK, GDN_DV), F32),
        ] + [pltpu.VMEM((tl, nv), F32)] * 8 + [pltpu.VMEM((tl, LANES), F32)] * 3
        + [pltpu.VMEM((tl, GDN_HEADS * chunk), F32)],
        compiler_params=_cp(("parallel", "arbitrary")),
        name="gdn",
    )(x, z, bg, cprev, s0, conv_w, alog, dtb, gnorm.reshape(1, GDN_DV))


def _even_out_kernel(x_ref, oa_ref, ob_ref, woa_ref, wob_ref, ln_ref, wg_ref, wu_ref, wd_ref,
                     o_ref, x1_sc, h_sc, acc_sc):
    f = pl.program_id(1)

    @pl.when(f == 0)
    def _():
        x1 = x_ref[...] + (_mm(oa_ref[...].astype(BF16), woa_ref[...]) + _mm(ob_ref[...].astype(BF16), wob_ref[...]))
        x1_sc[...] = x1
        h_sc[...] = _rms(x1, ln_ref[...]).astype(BF16)
        acc_sc[...] = jnp.zeros_like(acc_sc)

    h = h_sc[...]
    act = (_silu(_mm(h, wg_ref[...])) * _mm(h, wu_ref[...])).astype(BF16)
    acc_sc[...] += _mm(act, wd_ref[...])

    @pl.when(f == pl.num_programs(1) - 1)
    def _():
        o_ref[...] = x1_sc[...] + acc_sc[...]


def even_out(x, oa, ob, woa, wob, ln, wg, wu, wd):
    t, d = x.shape
    ff = wg.shape[1]
    tm = _row_tile(t, (1024, 512, 256, 128, 64, 32, 16, 8))
    tf = 256 if ff % 256 == 0 else LANES
    row = lambda w: pl.BlockSpec((tm, w), lambda i, f: (i, 0))
    return pl.pallas_call(
        _even_out_kernel,
        grid=(t // tm, ff // tf),
        in_specs=[row(d), row(oa.shape[1]), row(ob.shape[1]), _full(woa.shape), _full(wob.shape), _full((1, d)),
                  pl.BlockSpec((d, tf), lambda i, f: (0, f)), pl.BlockSpec((d, tf), lambda i, f: (0, f)),
                  pl.BlockSpec((tf, d), lambda i, f: (f, 0))],
        out_specs=row(d),
        out_shape=jax.ShapeDtypeStruct((t, d), F32),
        scratch_shapes=[pltpu.VMEM((tm, d), F32), pltpu.VMEM((tm, d), BF16), pltpu.VMEM((tm, d), F32)],
        compiler_params=_cp(("parallel", "arbitrary")),
        name="even_out",
    )(x, oa, ob, woa, wob, ln, wg, wu, wd)


def _rope_lanes(v, c, s):
    n = v.shape[1]
    lane = lax.broadcasted_iota(I32, (1, n), 1) % LANES
    sw = jnp.where(lane < MLA_ROPE // 2, pltpu.roll(v, n - MLA_ROPE // 2, axis=1), pltpu.roll(v, MLA_ROPE // 2, axis=1))
    return v * c + sw * s


def _odd_in_kernel(x_ref, ln_ref, wxb_ref, wgb_ref, wcq_ref, wckv_ref, wkpe_ref, qn_ref, kvn_ref, wuq_ref,
                   cos_ref, sin_ref, *rest, absorbed):
    if absorbed:
        wabs_ref, xb_ref, gb_ref, q_ref, qlat_ref, rows_ref = rest
    else:
        wuk_ref, wuv_ref, xb_ref, gb_ref, q_ref, k_ref, v_ref, rows_ref = rest
    h = _rms(x_ref[...], ln_ref[...]).astype(BF16)
    xb_ref[...] = _mm(h, wxb_ref[...])
    gb_ref[...] = _mm(h, wgb_ref[...])
    cq = _mm(h, wcq_ref[...])
    ckv = _mm(h, wckv_ref[...])
    kpe = _mm(h, wkpe_ref[...])
    cqn = _rms(cq, qn_ref[...]).astype(BF16)
    ckvn = _rms(ckv, kvn_ref[...])
    c128 = cos_ref[...]
    s128 = sin_ref[...]
    nh = MLA_HEADS
    q = _rope_lanes(_mm(cqn, wuq_ref[...]), jnp.concatenate([c128] * nh, axis=1), jnp.concatenate([s128] * nh, axis=1))
    kpe_r = _rope_lanes(kpe, c128, s128)
    qb = q.astype(BF16)
    q_ref[...] = qb
    rows_ref[:, 0:MLA_KV_RANK] = ckvn
    rows_ref[:, MLA_KV_RANK:MLA_ROW] = kpe_r[:, 0:MLA_ROPE]
    if absorbed:
        qlat_ref[...] = _mm(qb, wabs_ref[...])
    else:
        ckvb = ckvn.astype(BF16)
        k_ref[...] = (_mm(ckvb, wuk_ref[...]) + jnp.concatenate([kpe_r] * nh, axis=1)).astype(BF16)
        v_ref[...] = _mm(ckvb, wuv_ref[...]).astype(BF16)


def odd_in(x, ln, ws, qn, kvn, wuq, cos_t, sin_t, extra, absorbed):
    t, d = x.shape
    tm = _row_tile(t, (512, 256, 128, 64, 32, 16, 8))
    nblk = cos_t.shape[0] // tm
    hb = MLA_HEADS * LANES
    row = lambda w: pl.BlockSpec((tm, w), lambda i: (i, 0))
    tbl = pl.BlockSpec((tm, LANES), lambda i: (i % nblk, 0))
    if absorbed:
        outs = [(LRU_WIDTH, F32), (LRU_WIDTH, F32), (hb, BF16), (extra[0].shape[1], F32), (MLA_ROW, F32)]
    else:
        outs = [(LRU_WIDTH, F32), (LRU_WIDTH, F32), (hb, BF16), (hb, BF16), (hb, BF16), (MLA_ROW, F32)]
    return pl.pallas_call(
        functools.partial(_odd_in_kernel, absorbed=absorbed),
        grid=(t // tm,),
        in_specs=[row(d), _full((1, d))] + [_full(w.shape) for w in ws]
        + [_full(qn.shape), _full(kvn.shape), _full(wuq.shape), tbl, tbl] + [_full(w.shape) for w in extra],
        out_specs=[row(w) for w, _ in outs],
        out_shape=[jax.ShapeDtypeStruct((t, w), dt) for w, dt in outs],
        compiler_params=_cp(("parallel",)),
        name="odd_in",
    )(x, ln, *ws, qn, kvn, wuq, cos_t, sin_t, *extra)


def _expm1(x):
    u = jnp.exp(x)
    um1 = u - 1.0
    small = um1 * x / jnp.log(u)
    return jnp.where(um1 == 0.0, x, jnp.where(jnp.abs(x) < 0.5, small, um1))


def _lru_gates(xc, wa_ref, wx_ref, ba_ref, bx_ref, lam_ref):
    xcb = xc.astype(BF16)
    r = jax.nn.sigmoid(_mm(xcb, wa_ref[...]) + ba_ref[...])
    i = jax.nn.sigmoid(_mm(xcb, wx_ref[...]) + bx_ref[...])
    log_a = -LRU_C * r * jax.nn.softplus(-lam_ref[...])
    a = jnp.exp(log_a)
    b = jnp.sqrt(-_expm1(2.0 * log_a)) * (i * xc)
    return a, b


def _lru_prompt_kernel(x_ref, g_ref, cprev_ref, h0_ref, cw_ref, cb_ref, wa_ref, wx_ref, ba_ref, bx_ref, lam_ref,
                       y_ref, hl_ref, xbuf, h_sc):
    t = pl.program_id(1)
    tl = x_ref.shape[1]

    @pl.when(t == 0)
    def _():
        xbuf[0:SUBLANES, :] = cprev_ref[0]
        h_sc[0:1, :] = h0_ref[0]

    xbuf[SUBLANES:SUBLANES + tl, :] = x_ref[0]
    cw = cw_ref[...]
    xc = xbuf[5:5 + tl, :] * cw[0:1, :]
    for j in range(1, cw.shape[0]):
        xc = xc + xbuf[5 + j:5 + j + tl, :] * cw[j:j + 1, :]
    xc = xc + cb_ref[...]
    xbuf[0:SUBLANES, :] = xbuf[tl:tl + SUBLANES, :]
    a, b = _lru_gates(xc, wa_ref, wx_ref, ba_ref, bx_ref, lam_ref)
    row = lax.broadcasted_iota(I32, (tl, 1), 0)
    s = 1
    while s < tl:
        m = row >= s
        b = jnp.where(m, a * pltpu.roll(b, s, axis=0) + b, b)
        a = jnp.where(m, a * pltpu.roll(a, s, axis=0), a)
        s *= 2
    hs = a * h_sc[0:1, :] + b
    h_sc[0:1, :] = hs[tl - 1:tl, :]
    y_ref[0] = hs * jax.nn.gelu(g_ref[0])
    hl_ref[0] = hs[tl - 1:tl, :]


def lru_prompt(xb, gb, cprev, h0, cw, cb, wa, wx, ba, bx, lam, tl):
    n, l, c = xb.shape
    seq = pl.BlockSpec((1, tl, c), lambda i, t: (i, t, 0))
    return pl.pallas_call(
        _lru_prompt_kernel,
        grid=(n, l // tl),
        in_specs=[seq, seq, pl.BlockSpec((1, SUBLANES, c), lambda i, t: (i, 0, 0)),
                  pl.BlockSpec((1, 1, c), lambda i, t: (i, 0, 0)),
                  _full(cw.shape), _full((1, c)), _full(wa.shape), _full(wx.shape), _full((1, c)), _full((1, c)),
                  _full((1, c))],
        out_specs=[seq, pl.BlockSpec((1, 1, c), lambda i, t: (i, 0, 0))],
        out_shape=[jax.ShapeDtypeStruct((n, l, c), F32), jax.ShapeDtypeStruct((n, 1, c), F32)],
        scratch_shapes=[pltpu.VMEM((tl + SUBLANES, c), F32), pltpu.VMEM((SUBLANES, c), F32)],
        compiler_params=_cp(("parallel", "arbitrary")),
        name="lru_prompt",
    )(xb, gb, cprev, h0, cw, cb, wa, wx, ba, bx, lam)


def _lru_sample_kernel(x_ref, g_ref, cprev_ref, h0_ref, cw_ref, cb_ref, wa_ref, wx_ref, ba_ref, bx_ref, lam_ref,
                       y_ref, hl_ref):
    l = x_ref.shape[0]
    cw = cw_ref[...]
    nw = cw.shape[0]
    xx = [cprev_ref[j] for j in range(nw - 1)] + [x_ref[j] for j in range(l)]
    h = h0_ref[...]
    for i in range(l):
        xc = xx[i] * cw[0:1, :]
        for j in range(1, nw):
            xc = xc + xx[i + j] * cw[j:j + 1, :]
        xc = xc + cb_ref[...]
        a, b = _lru_gates(xc, wa_ref, wx_ref, ba_ref, bx_ref, lam_ref)
        h = a * h + b
        y_ref[i] = h * jax.nn.gelu(g_ref[i])
    hl_ref[...] = h


def lru_sample(xb, gb, cprev, h0, cw, cb, wa, wx, ba, bx, lam):
    l, n, c = xb.shape
    return pl.pallas_call(
        _lru_sample_kernel,
        out_shape=[jax.ShapeDtypeStruct((l, n, c), F32), jax.ShapeDtypeStruct((n, c), F32)],
        compiler_params=pltpu.CompilerParams(vmem_limit_bytes=VMEM_LIMIT),
        name="lru_sample",
    )(xb, gb, cprev, h0, cw, cb, wa, wx, ba, bx, lam)


MLA_HEADS_PER_STEP = 2


def _mla_prompt_kernel(q_ref, k_ref, v_ref, o_ref, m_sc, l_sc, acc_sc, *, scale):
    qi = pl.program_id(2)
    tq = q_ref.shape[0]
    tk = tq
    g = q_ref.shape[1] // LANES
    c1 = scale * math.log2(math.e)
    m_sc[...] = jnp.full_like(m_sc, -jnp.inf)
    l_sc[...] = jnp.zeros_like(l_sc)
    acc_sc[...] = jnp.zeros_like(acc_sc)
    on_or_below = lax.broadcasted_iota(I32, (tq, tk), 1) <= lax.broadcasted_iota(I32, (tq, tk), 0)

    def step(ki, masked):
        rows = pl.ds(pl.multiple_of(ki * tk, tk), tk)
        scores = [_mm(q_ref[:, h * LANES:(h + 1) * LANES], k_ref[rows, h * LANES:(h + 1) * LANES], NT) * c1
                  for h in range(g)]
        for h in range(g):
            hs = slice(h * LANES, (h + 1) * LANES)
            s = scores[h]
            if masked:
                s = jnp.where(on_or_below, s, -jnp.inf)
            m_prev = m_sc[h]
            m_new = jnp.maximum(m_prev, jnp.max(s, axis=-1, keepdims=True))
            alpha = jnp.exp2(m_prev - m_new)
            p = jnp.exp2(s - jnp.concatenate([m_new] * (tk // LANES), axis=1))
            l_sc[h] = alpha * l_sc[h] + jnp.sum(p, axis=-1, keepdims=True)
            acc_sc[:, hs] = alpha * acc_sc[:, hs] + _mm(p.astype(BF16), v_ref[rows, hs])
            m_sc[h] = m_new

    def body(ki, carry):
        step(ki, False)
        return carry

    lax.fori_loop(0, qi, body, 0)
    step(qi, True)
    for h in range(g):
        hs = slice(h * LANES, (h + 1) * LANES)
        o_ref[:, hs] = (acc_sc[:, hs] / l_sc[h]).astype(o_ref.dtype)


def mla_prompt(q, k, v, n, l, tq):
    nq = l // tq
    g = MLA_HEADS_PER_STEP
    w = g * LANES
    scale = (MLA_NOPE + MLA_ROPE) ** -0.5
    return pl.pallas_call(
        functools.partial(_mla_prompt_kernel, scale=scale),
        grid=(n, MLA_HEADS // g, nq),
        in_specs=[pl.BlockSpec((tq, w), lambda i, h, j: (i * nq + j, h)),
                  pl.BlockSpec((l, w), lambda i, h, j: (i, h)), pl.BlockSpec((l, w), lambda i, h, j: (i, h))],
        out_specs=pl.BlockSpec((tq, w), lambda i, h, j: (i * nq + j, h)),
        out_shape=jax.ShapeDtypeStruct(q.shape, BF16),
        scratch_shapes=[pltpu.VMEM((g, tq, LANES), F32), pltpu.VMEM((g, tq, LANES), F32), pltpu.VMEM((tq, w), F32)],
        compiler_params=_cp(("parallel", "parallel", "arbitrary")),
        name="mla_prompt",
    )(q, k, v)


MLA_SAMPLE_PAGES = 32
MLA_SAMPLE_GROUPS = 2


def _mla_sample_kernel(pt_ref, q_ref, rows_ref, *rest, n_pages_step, l, scale):
    page_refs = rest[:n_pages_step]
    o_ref, m_sc, l_sc, acc_sc = rest[n_pages_step:]
    j = pl.program_id(1)

    @pl.when(j == 0)
    def _():
        m_sc[...] = jnp.full_like(m_sc, -jnp.inf)
        l_sc[...] = jnp.zeros_like(l_sc)
        acc_sc[...] = jnp.zeros_like(acc_sc)

    q = q_ref[0]
    per = n_pages_step // MLA_SAMPLE_GROUPS
    kts = [jnp.concatenate([r[0].astype(BF16) for r in page_refs[g * per:(g + 1) * per]], axis=1)
           for g in range(MLA_SAMPLE_GROUPS)]
    ss = [_mm(q, kt) * scale for kt in kts]
    m_run = m_sc[...]
    l_run = l_sc[...]
    acc = acc_sc[...]
    for kt, s in zip(kts, ss):
        m_new = jnp.maximum(m_run, jnp.max(s, axis=-1, keepdims=True))
        alpha = jnp.exp(m_run - m_new)
        p32 = jnp.exp(s - m_new)
        l_run = alpha * l_run + jnp.sum(p32, axis=-1, keepdims=True)
        acc = alpha * acc + _mm(p32.astype(BF16), kt, NT)
        m_run = m_new
    m_sc[...] = m_run
    l_sc[...] = l_run
    acc_sc[...] = acc

    @pl.when(j == pl.num_programs(1) - 1)
    def _():
        qf = q.astype(F32)
        rows = rows_ref[0]
        tok = lax.broadcasted_iota(I32, (qf.shape[0], 1), 0) // MLA_HEADS
        sn = []
        for mm in range(l):
            sm = jnp.sum(qf * rows[mm:mm + 1, :], axis=-1, keepdims=True) * scale
            sn.append(jnp.where(tok >= mm, sm, -jnp.inf))
        m_old = m_sc[...]
        m_fin = m_old
        for sm in sn:
            m_fin = jnp.maximum(m_fin, sm)
        al = jnp.exp(m_old - m_fin)
        lsum = al * l_sc[...]
        acc = al * acc_sc[...]
        for mm in range(l):
            pm = jnp.exp(sn[mm] - m_fin)
            lsum = lsum + pm
            acc = acc + pm * rows[mm:mm + 1, :]
        o_ref[0] = acc / lsum


def mla_sample(page_table, q_cat, rows, cache_t, n_pages_step):
    n, r, c = q_cat.shape
    l = rows.shape[1]
    n_pages = page_table.shape[1]
    steps = n_pages // n_pages_step
    scale = (MLA_NOPE + MLA_ROPE) ** -0.5
    pt = page_table.reshape(-1)

    def page_map(k):
        return lambda i, j, pt_ref: (pt_ref[i * n_pages + j * n_pages_step + k], 0, 0)

    return pl.pallas_call(
        functools.partial(_mla_sample_kernel, n_pages_step=n_pages_step, l=l, scale=scale),
        grid_spec=pltpu.PrefetchScalarGridSpec(
            num_scalar_prefetch=1,
            grid=(n, steps),
            in_specs=[pl.BlockSpec((1, r, c), lambda i, j, pt_ref: (i, 0, 0)),
                      pl.BlockSpec((1, l, c), lambda i, j, pt_ref: (i, 0, 0))]
            + [pl.BlockSpec((1, c, PAGE_SIZE), page_map(k)) for k in range(n_pages_step)],
            out_specs=pl.BlockSpec((1, r, c), lambda i, j, pt_ref: (i, 0, 0)),
            scratch_shapes=[pltpu.VMEM((r, 1), F32), pltpu.VMEM((r, 1), F32), pltpu.VMEM((r, c), F32)],
        ),
        out_shape=jax.ShapeDtypeStruct((n, r, c), F32),
        compiler_params=_cp(("parallel", "arbitrary")),
        name="mla_sample",
    )(pt, q_cat, rows, *([cache_t] * n_pages_step))


def _matmul_kernel(x_ref, w_ref, o_ref):
    o_ref[...] = _mm(x_ref[...].astype(BF16), w_ref[...]).astype(o_ref.dtype)


def matmul(x, w, out_dtype):
    t, kd = x.shape
    tm = _row_tile(t, (512, 256, 128, 64, 32, 16, 8))
    return pl.pallas_call(
        _matmul_kernel,
        grid=(t // tm,),
        in_specs=[pl.BlockSpec((tm, kd), lambda i: (i, 0)), _full(w.shape)],
        out_specs=pl.BlockSpec((tm, w.shape[1]), lambda i: (i, 0)),
        out_shape=jax.ShapeDtypeStruct((t, w.shape[1]), out_dtype),
        compiler_params=_cp(("parallel",)),
        name="matmul",
    )(x, w)


def _odd_out_kernel(x_ref, yc_ref, od_ref, woc_ref, wod_ref, ln_ref, rw_ref, rb_ref, *rest):
    x1_ref, hn_ref, idx_ref, gate_ref = rest[-4:]
    x1 = x_ref[...] + (_mm(yc_ref[...].astype(BF16), woc_ref[...]) + _mm(od_ref[...], wod_ref[...]))
    x1_ref[...] = x1
    hn = _rms(x1, ln_ref[...])
    hn_ref[...] = hn.astype(BF16)
    logits = _mm(hn.astype(BF16), rw_ref[...]) + rb_ref[...]
    lane = lax.broadcasted_iota(I32, logits.shape, 1)
    logits = jnp.where(lane < N_EXPERTS, logits, -jnp.inf)
    m1 = jnp.max(logits, axis=-1, keepdims=True)
    i1 = jnp.min(jnp.where(logits == m1, lane, LANES), axis=-1, keepdims=True)
    l2 = jnp.where(lane == i1, -jnp.inf, logits)
    m2 = jnp.max(l2, axis=-1, keepdims=True)
    i2 = jnp.min(jnp.where(l2 == m2, lane, LANES), axis=-1, keepdims=True)
    e2 = jnp.exp(m2 - m1)
    den = 1.0 + e2
    idx_ref[...] = jnp.where(lane == 0, i1, jnp.where(lane == 1, i2, 0))
    gate_ref[...] = jnp.where(lane == 0, 1.0 / den, jnp.where(lane == 1, e2 / den, 0.0))


def odd_out(x, yc, od, woc, wod, ln, rw, rb, hn_rows, hn_row0, hn_buf=None):
    t, d = x.shape
    tm = _row_tile(t, (512, 256, 128, 64, 32, 16, 8))
    assert hn_row0 % tm == 0
    off = hn_row0 // tm
    row = lambda w: pl.BlockSpec((tm, w), lambda i: (i, 0))
    args = (x, yc, od, woc, wod, ln, rw, rb) + (() if hn_buf is None else (hn_buf,))
    return pl.pallas_call(
        _odd_out_kernel,
        grid=(t // tm,),
        in_specs=[row(d), row(yc.shape[1]), row(od.shape[1]), _full(woc.shape), _full(wod.shape), _full((1, d)),
                  _full(rw.shape), _full(rb.shape)] + ([] if hn_buf is None else [pl.BlockSpec(memory_space=pl.ANY)]),
        out_specs=[row(d), pl.BlockSpec((tm, d), lambda i: (i + off, 0)), row(LANES), row(LANES)],
        out_shape=[jax.ShapeDtypeStruct((t, d), F32), jax.ShapeDtypeStruct((hn_rows, d), BF16),
                   jax.ShapeDtypeStruct((t, LANES), I32), jax.ShapeDtypeStruct((t, LANES), F32)],
        input_output_aliases={} if hn_buf is None else {8: 1},
        compiler_params=_cp(("parallel",)),
        name="odd_out",
    )(*args)


EXPERT_ROWS = 512
EXPERT_FF_CHUNK = 512
MOE_SUB = 128
MOE_PAIRS_PER_STEP = 4
F_VALID, F_FIRST, F_LAST = 1, 2, 4


def _moe_expert_kernel(sb_ref, fl_ref, pj_ref, lo_ref, hi_ref, be_ref, *refs):
    del pj_ref, be_ref
    nu = MOE_PAIRS_PER_STEP
    tiles = [refs[3 * u:3 * u + 3] for u in range(nu)]
    wg_ref, wu_ref, wd_ref, o_ref, xacc, gacc = refs[3 * nu:]
    s = pl.program_id(0)
    fl = fl_ref[s]
    er = xacc.shape[0]

    @pl.when((fl & F_VALID) != 0)
    def _():
        @pl.when((fl & F_FIRST) != 0)
        def _():
            xacc[...] = jnp.zeros_like(xacc)
            gacc[...] = jnp.zeros_like(gacc)

        for u, (slot_ref, gate_ref, hn_ref) in enumerate(tiles):
            tile = hn_ref.shape[0]
            for sb in range(er // MOE_SUB):
                @pl.when(jnp.logical_and(lo_ref[s * nu + u] <= sb, sb <= hi_ref[s * nu + u]))
                def _(sb=sb, slot_ref=slot_ref, gate_ref=gate_ref, hn_ref=hn_ref, tile=tile):
                    rs = slice(sb * MOE_SUB, (sb + 1) * MOE_SUB)
                    srow = sb_ref[s] * er + sb * MOE_SUB + lax.broadcasted_iota(I32, (MOE_SUB, tile), 0)
                    m0 = slot_ref[0:1, :] == srow
                    m1 = slot_ref[1:2, :] == srow
                    sel = jnp.where(m0, 1.0, jnp.where(m1, 1.0, 0.0)).astype(BF16)
                    xacc[rs, :] += _mm(sel, hn_ref[...])
                    g = jnp.where(m0, gate_ref[0:1, :], jnp.where(m1, gate_ref[1:2, :], 0.0))
                    gacc[rs, :] += jnp.sum(g, axis=1, keepdims=True)

        @pl.when((fl & F_LAST) != 0)
        def _():
            x = xacc[...].astype(BF16)
            acc = None
            for f in range(0, wg_ref.shape[2], EXPERT_FF_CHUNK):
                gg = _mm(x, wg_ref[0, :, f:f + EXPERT_FF_CHUNK])
                uu = _mm(x, wu_ref[0, :, f:f + EXPERT_FF_CHUNK])
                y = _mm((_silu(gg) * uu).astype(BF16), wd_ref[0, f:f + EXPERT_FF_CHUNK, :])
                acc = y if acc is None else acc + y
            o_ref[...] = (acc * gacc[...]).astype(o_ref.dtype)


def moe_experts(plan, slot_t, gate_t, hn, wg, wu, wd, tile):
    sb, fl, pj, lo, hi, blk_e, n_blocks = plan
    d = hn.shape[1]
    ff = wg.shape[2]
    nu = MOE_PAIRS_PER_STEP
    wmap = lambda s, sb, fl, pj, lo, hi, be: (be[sb[s]], 0, 0)
    tile_specs, tile_args = [], []
    for u in range(nu):
        tmap = lambda s, sb, fl, pj, lo, hi, be, u=u: (0, pj[s * nu + u])
        hmap = lambda s, sb, fl, pj, lo, hi, be, u=u: (pj[s * nu + u], 0)
        tile_specs += [pl.BlockSpec((TOP_K, tile), tmap), pl.BlockSpec((TOP_K, tile), tmap),
                       pl.BlockSpec((tile, d), hmap)]
        tile_args += [slot_t, gate_t, hn]
    return pl.pallas_call(
        _moe_expert_kernel,
        grid_spec=pltpu.PrefetchScalarGridSpec(
            num_scalar_prefetch=6,
            grid=(sb.shape[0],),
            in_specs=tile_specs + [pl.BlockSpec((1, d, ff), wmap), pl.BlockSpec((1, d, ff), wmap),
                                   pl.BlockSpec((1, ff, d), wmap)],
            out_specs=pl.BlockSpec((EXPERT_ROWS, d), lambda s, sb, fl, pj, lo, hi, be: (sb[s], 0)),
            scratch_shapes=[pltpu.VMEM((EXPERT_ROWS, d), F32), pltpu.VMEM((EXPERT_ROWS, 1), F32)],
        ),
        out_shape=jax.ShapeDtypeStruct((n_blocks * EXPERT_ROWS, d), BF16),
        compiler_params=_cp(("arbitrary",)),
        name="moe_experts",
    )(sb, fl, pj, lo, hi, blk_e, *tile_args, wg, wu, wd)


def _moe_combine_kernel(sj_ref, fl_ref, qb_ref, lo_ref, hi_ref, slot_ref, x1_ref, *refs):
    del sj_ref
    nu = MOE_PAIRS_PER_STEP
    yb_refs = refs[:nu]
    fn_ref, o_ref, yacc = refs[nu:]
    s = pl.program_id(0)
    fl = fl_ref[s]
    tile = x1_ref.shape[0]
    er = yb_refs[0].shape[0]

    @pl.when((fl & F_VALID) != 0)
    def _():
        @pl.when((fl & F_FIRST) != 0)
        def _():
            yacc[...] = jnp.zeros_like(yacc)

        for u, yb_ref in enumerate(yb_refs):
            for sb in range(er // MOE_SUB):
                @pl.when(jnp.logical_and(lo_ref[s * nu + u] <= sb, sb <= hi_ref[s * nu + u]))
                def _(sb=sb, u=u, yb_ref=yb_ref):
                    scol = qb_ref[s * nu + u] * er + sb * MOE_SUB + lax.broadcasted_iota(I32, (tile, MOE_SUB), 1)
                    sl = slot_ref[...]
                    sel = jnp.where(sl[:, 0:1] == scol, 1.0, jnp.where(sl[:, 1:2] == scol, 1.0, 0.0)).astype(BF16)
                    yacc[...] += _mm(sel, yb_ref[sb * MOE_SUB:(sb + 1) * MOE_SUB, :])

        @pl.when((fl & F_LAST) != 0)
        def _():
            o_ref[...] = _rms(x1_ref[...] + yacc[...], fn_ref[...])


def moe_combine(sched, slot_cols, x1, yb, fnorm, tile):
    sj, fl, qb, lo, hi = sched
    t, d = x1.shape
    nu = MOE_PAIRS_PER_STEP
    tmap = lambda s, sj, fl, qb, lo, hi: (sj[s], 0)
    yb_specs = [pl.BlockSpec((EXPERT_ROWS, d), lambda s, sj, fl, qb, lo, hi, u=u: (qb[s * nu + u], 0))
                for u in range(nu)]
    return pl.pallas_call(
        _moe_combine_kernel,
        grid_spec=pltpu.PrefetchScalarGridSpec(
            num_scalar_prefetch=5,
            grid=(sj.shape[0],),
            in_specs=[pl.BlockSpec((tile, LANES), tmap), pl.BlockSpec((tile, d), tmap)] + yb_specs
            + [pl.BlockSpec((1, d), lambda s, sj, fl, qb, lo, hi: (0, 0))],
            out_specs=pl.BlockSpec((tile, d), tmap),
            scratch_shapes=[pltpu.VMEM((tile, d), F32)],
        ),
        out_shape=jax.ShapeDtypeStruct((t, d), F32),
        compiler_params=_cp(("arbitrary",)),
        name="moe_combine",
    )(sj, fl, qb, lo, hi, slot_cols, x1, *([yb] * nu), fnorm)


def _ragged_steps(counts, pmax):
    cum = jnp.cumsum(counts)
    total = cum[-1]
    ar = jnp.arange(pmax, dtype=I32)
    p = jnp.minimum(ar, jnp.maximum(total - 1, 0))
    row = jnp.minimum(jnp.sum((cum[None, :] <= p[:, None]).astype(I32), axis=1), counts.shape[0] - 1)
    off = p - (cum[row] - counts[row])
    return row, off, ar < total, total


def _group_flags(gid, valid, total):
    pmax = gid.shape[0]
    ar = jnp.arange(pmax, dtype=I32)
    prev = jnp.concatenate([gid[:1] - 1, gid[:-1]])
    nxt = jnp.concatenate([gid[1:], gid[-1:] + 1])
    first = jnp.logical_or(ar == 0, gid != prev)
    last = jnp.logical_or(ar == total - 1, gid != nxt)
    fl = F_VALID + F_FIRST * first.astype(I32) + F_LAST * last.astype(I32)
    return jnp.where(valid, fl, 0).astype(I32)


def _grouped_steps(cnt, smax):
    nu = MOE_PAIRS_PER_STEP
    row, off, valid, total = _ragged_steps((cnt + nu - 1) // nu, smax)
    k = off[:, None] * nu + jnp.arange(nu, dtype=I32)[None, :]
    c = cnt[row][:, None]
    pvalid = jnp.logical_and(valid[:, None], k < c)
    pidx = (jnp.cumsum(cnt) - cnt)[row][:, None] + jnp.minimum(k, jnp.maximum(c - 1, 0))
    return row, _group_flags(row, valid, total), pidx.reshape(-1), pvalid.reshape(-1)


def _moe_plan(e_all, tile):
    t = e_all.shape[0]
    nt = t // tile
    er = EXPERT_ROWS
    ex = jnp.arange(N_EXPERTS, dtype=I32)[None, :]
    oh = jnp.logical_or(e_all[:, 0:1] == ex, e_all[:, 1:2] == ex).astype(I32)
    cs = jnp.cumsum(oh, axis=0)
    rank = cs - oh
    counts = cs[-1]
    padded = (counts + er - 1) // er * er
    pad_end = jnp.cumsum(padded)
    pad_start = pad_end - padded
    slot_e = pad_start[None, :] + rank
    slot = jnp.stack([jnp.sum(jnp.where(e_all[:, k:k + 1] == ex, slot_e, 0), axis=1) for k in range(TOP_K)],
                     axis=1).astype(I32)
    n_blocks = -(-(t * TOP_K) // er) + N_EXPERTS
    blk = jnp.arange(n_blocks, dtype=I32)
    blk_e = jnp.minimum(jnp.sum((blk[:, None] * er >= pad_end[None, :]).astype(I32), axis=1), N_EXPERTS - 1)
    n_used = pad_end[-1] // er
    tstart = jnp.concatenate([rank[::tile], counts[None, :]], axis=0)
    first_slot = pad_start[None, :] + tstart[:-1]
    last_slot = pad_start[None, :] + tstart[1:] - 1

    def sub_range(b, fs, ls):
        lo = jnp.maximum(fs - b * er, 0) // MOE_SUB
        hi = jnp.where(ls >= fs, jnp.minimum(ls - b * er, er - 1), -1) // MOE_SUB
        return lo.astype(I32), jnp.maximum(hi, -1).astype(I32)

    be_oh = (blk_e[:, None] == ex).astype(I32)
    r0 = blk * er - jnp.sum(be_oh * pad_start[None, :], axis=1)
    r1 = jnp.minimum(r0 + er - 1, jnp.sum(be_oh * counts[None, :], axis=1) - 1)
    ts_b = jnp.sum(be_oh[:, None, :] * tstart[None, 1:, :], axis=2)
    jlo = jnp.minimum(jnp.sum((ts_b <= r0[:, None]).astype(I32), axis=1), nt - 1)
    jhi = jnp.minimum(jnp.sum((ts_b <= r1[:, None]).astype(I32), axis=1), nt - 1)
    cnt_b = jnp.where(blk < n_used, jhi - jlo + 1, 0)
    nu = MOE_PAIRS_PER_STEP
    pmax = n_blocks + nt * N_EXPERTS
    row, off, _, _ = _ragged_steps(cnt_b, pmax)
    pj = (jlo[row] + off).astype(I32)
    je = pj * N_EXPERTS + blk_e[row]
    lo, hi = sub_range(row, first_slot.reshape(-1)[je], last_slot.reshape(-1)[je])
    sb, sfl, pidx, pvalid = _grouped_steps(cnt_b, n_blocks + -(-pmax // nu))
    expert_plan = (sb, sfl, pj[pidx], jnp.where(pvalid, lo[pidx], 1), jnp.where(pvalid, hi[pidx], 0),
                   blk_e.astype(I32), n_blocks)

    def combine_sched(j0, j1):
        ntr = j1 - j0
        fs = first_slot[j0:j1].reshape(-1)
        ls = last_slot[j0:j1].reshape(-1)
        blo = fs // er
        nb = jnp.where(ls >= fs, ls // er - blo + 1, 0)
        pmax_c = min(n_blocks + ntr * N_EXPERTS, 2 * ntr * N_EXPERTS)
        r, o, _, _ = _ragged_steps(nb, pmax_c)
        qb = (blo[r] + o).astype(I32)
        lo_c, hi_c = sub_range(qb, fs[r], ls[r])
        cnt_j = jnp.sum(nb.reshape(ntr, N_EXPERTS), axis=1)
        sj, sfl_c, pidx_c, pvalid_c = _grouped_steps(cnt_j, ntr + -(-pmax_c // nu))
        return (sj, sfl_c, qb[pidx_c], jnp.where(pvalid_c, lo_c[pidx_c], 1), jnp.where(pvalid_c, hi_c[pidx_c], 0))

    return slot, expert_plan, combine_sched


def _rope_tables(pos):
    half = MLA_ROPE // 2
    inv = ROPE_THETA ** (-jnp.arange(half, dtype=F32) * 2.0 / MLA_ROPE)
    ang = pos.astype(F32)[:, None] * inv[None, :]
    cos = jnp.cos(ang)
    sin = jnp.sin(ang)
    n = pos.shape[0]
    cos_t = jnp.concatenate([cos, cos, jnp.ones((n, LANES - MLA_ROPE), F32)], axis=1)
    sin_t = jnp.concatenate([-sin, sin, jnp.zeros((n, LANES - MLA_ROPE), F32)], axis=1)
    return cos_t, sin_t


def _block_diag(w):
    b, i, j = w.shape
    eye = jnp.eye(b, dtype=w.dtype)
    return (eye[:, None, :, None] * w[:, :, None, :]).reshape(b * i, b * j)


def _pad_rows(a, rows):
    return jnp.pad(a, ((0, 0), (0, rows - a.shape[1]), (0, 0)))


def kernel(x_prompt, x_sample, state_swa_kv, state_gdn_conv, state_gdn_s, state_lru_conv, state_lru_h, cache_mla, page_table, e_ln_mix, e_w_in, e_gdn_conv_w, e_gdn_a_log, e_gdn_dt_bias, e_gdn_norm, e_swa_sinks, e_w_out, e_ln_ffn, e_ffn_gate, e_ffn_up, e_ffn_down, o_ln_mix, o_w_in, o_lru_conv_w, o_lru_conv_b, o_lru_w_a, o_lru_b_a, o_lru_w_x, o_lru_b_x, o_lru_lambda, o_mla_q_norm, o_mla_w_uq, o_mla_kv_norm, o_mla_w_uk, o_mla_w_uv, o_w_out, o_ln_ffn, o_router_w, o_router_b, o_exp_gate, o_exp_up, o_exp_down, final_norm):
    nb, lp, d = x_prompt.shape
    ns, ls, _ = x_sample.shape
    past_len = page_table.shape[1] * PAGE_SIZE
    tp, ts = nb * lp, ns * ls
    xp = x_prompt.reshape(tp, d)
    xs = x_sample.reshape(ts, d)
    row1 = lambda v: v.reshape(1, -1)

    na = (SWA_HEADS + 2 * SWA_KV_HEADS) * SWA_HEAD_DIM
    nz = GDN_HEADS * GDN_DV
    w_in = e_w_in[0].astype(BF16)
    o1, o2, o3 = na, na + GDN_QKV, na + GDN_QKV + nz
    w_groups = [w_in[:, :o1], w_in[:, o1:o2], w_in[:, o2:o3],
                jnp.pad(w_in[:, o3:], ((0, 0), (0, LANES - 2 * GDN_HEADS)))]
    ln = row1(e_ln_mix[0])
    qkv_p, gx_p, z_p, bg_p = norm_proj(xp, ln, w_groups)
    qkv_s, gx_s, z_s, bg_s = norm_proj(xs, ln, w_groups)

    sinks = e_swa_sinks[0]
    oa_p = swa_prompt(qkv_p, sinks, nb, lp)
    wbuf = state_swa_kv.shape[2]
    nkv = 2 * SWA_KV_HEADS * SWA_HEAD_DIM
    buf = state_swa_kv[0].reshape(ns, wbuf, nkv)
    qkv_s3 = qkv_s.reshape(ns, ls, na)
    oa_s = swa_sample(_pad_rows(qkv_s3, SUBLANES), buf, sinks, ls)[:, :ls].reshape(ts, -1)
    kv_p = qkv_p.reshape(nb, lp, na)[:, lp - SWA_WINDOW:, na - nkv:]
    swa_kv_p = kv_p.reshape(1, nb, SWA_WINDOW, 2, SWA_KV_HEADS, SWA_HEAD_DIM)
    kv_s = jnp.concatenate([buf, qkv_s3[:, :, na - nkv:]], axis=1)[:, ls:]
    swa_kv_s = kv_s.reshape(1, ns, wbuf, 2, SWA_KV_HEADS, SWA_HEAD_DIM)

    gx_p3 = gx_p.reshape(nb, lp, GDN_QKV)
    gx_s3 = gx_s.reshape(ns, ls, GDN_QKV)
    gargs = (e_gdn_conv_w[0], e_gdn_a_log[0], e_gdn_dt_bias[0], e_gdn_norm[0])
    ob_p, gs_p = gdn(gx_p3, z_p.reshape(nb, lp, nz), bg_p.reshape(nb, lp, LANES),
                     jnp.zeros((nb, SUBLANES, GDN_QKV), F32), jnp.zeros((nb, GDN_HEADS, GDN_DK, GDN_DV), F32),
                     *gargs, chunk=math.gcd(lp, GDN_CHUNK), valid=math.gcd(lp, GDN_CHUNK), tl=min(lp, 512))
    cprev_s = jnp.pad(state_gdn_conv[0], ((0, 0), (SUBLANES - (GDN_CONV - 1), 0), (0, 0)))
    assert ls <= GDN_CHUNK and math.gcd(ls, GDN_CHUNK) == ls
    ob_s, gs_s = gdn(_pad_rows(gx_s3, GDN_CHUNK), _pad_rows(z_s.reshape(ns, ls, nz), GDN_CHUNK),
                     _pad_rows(bg_s.reshape(ns, ls, LANES), GDN_CHUNK), cprev_s, state_gdn_s[0],
                     *gargs, chunk=GDN_CHUNK, valid=ls, tl=GDN_CHUNK)
    ob_s = ob_s[:, :ls].reshape(ts, nz)
    gconv_p = gx_p3[:, lp - (GDN_CONV - 1):][None]
    gconv_s = gx_s3[:, ls - (GDN_CONV - 1):][None]

    w_out = e_w_out[0].astype(BF16)
    nqa = SWA_HEADS * SWA_HEAD_DIM
    ffn = (row1(e_ln_ffn[0]), e_ffn_gate[0].astype(BF16), e_ffn_up[0].astype(BF16), e_ffn_down[0].astype(BF16))
    xp = even_out(xp, oa_p, ob_p.reshape(tp, nz), w_out[:nqa], w_out[nqa:], *ffn)
    xs = even_out(xs, oa_s, ob_s, w_out[:nqa], w_out[nqa:], *ffn)

    w_in = o_w_in[0].astype(BF16)
    c0 = LRU_WIDTH
    c1 = 2 * LRU_WIDTH
    c2 = c1 + MLA_Q_RANK
    c3 = c2 + MLA_KV_RANK
    w_groups = [w_in[:, :c0], w_in[:, c0:c1], w_in[:, c1:c2], w_in[:, c2:c3],
                jnp.pad(w_in[:, c3:], ((0, 0), (0, LANES - MLA_ROPE)))]
    hd_q = MLA_NOPE + MLA_ROPE
    half = MLA_ROPE // 2
    wq = o_mla_w_uq[0].reshape(MLA_Q_RANK, MLA_HEADS, hd_q)
    wuq = jnp.concatenate([wq[:, :, MLA_NOPE:], wq[:, :, :MLA_NOPE],
                           jnp.zeros((MLA_Q_RANK, MLA_HEADS, LANES - hd_q), F32)], axis=2)
    wuq = wuq.reshape(MLA_Q_RANK, MLA_HEADS * LANES).astype(BF16)
    w_uk = o_mla_w_uk[0]
    w_uv = o_mla_w_uv[0]
    wuk = jnp.pad(w_uk, ((0, 0), (0, 0), (MLA_ROPE, LANES - hd_q))).reshape(MLA_KV_RANK, -1).astype(BF16)
    wuv = jnp.pad(w_uv, ((0, 0), (0, 0), (0, LANES - MLA_V))).reshape(MLA_KV_RANK, -1).astype(BF16)
    wabs = _block_diag(jnp.pad(jnp.transpose(w_uk, (1, 2, 0)), ((0, 0), (MLA_ROPE, LANES - hd_q), (0, 0)))).astype(BF16)
    wuv_bd = _block_diag(jnp.pad(jnp.transpose(w_uv, (1, 0, 2)), ((0, 0), (0, 0), (0, LANES - MLA_V)))).astype(BF16)
    ln = row1(o_ln_mix[0])
    qn, kvn = row1(o_mla_q_norm[0]), row1(o_mla_kv_norm[0])
    tm_p = _row_tile(tp, (512, 256, 128, 64, 32, 16, 8))
    assert lp % tm_p == 0 or tm_p % lp == 0
    pos_p = jnp.arange(max(lp, tm_p), dtype=I32) % lp
    pos_s = past_len + (jnp.arange(ts, dtype=I32) % ls)
    cos_p, sin_p = _rope_tables(pos_p)
    cos_s, sin_s = _rope_tables(pos_s)
    xb_p, gb_p, q_p, k_p, v_p, rows_p = odd_in(xp, ln, w_groups, qn, kvn, wuq, cos_p, sin_p, [wuk, wuv], False)
    xb_s, gb_s, q_s, qlat_s, rows_s = odd_in(xs, ln, w_groups, qn, kvn, wuq, cos_s, sin_s, [wabs], True)

    lru_w = (o_lru_conv_w[0], row1(o_lru_conv_b[0]), _block_diag(o_lru_w_a[0]).astype(BF16),
             _block_diag(o_lru_w_x[0]).astype(BF16), row1(o_lru_b_a[0]), row1(o_lru_b_x[0]), row1(o_lru_lambda[0]))
    xb_p3 = xb_p.reshape(nb, lp, LRU_WIDTH)
    yc_p, lh_p = lru_prompt(xb_p3, gb_p.reshape(nb, lp, LRU_WIDTH), jnp.zeros((nb, SUBLANES, LRU_WIDTH), F32),
                            jnp.zeros((nb, 1, LRU_WIDTH), F32), *lru_w, tl=min(lp, 256))
    xb_s3 = xb_s.reshape(ns, ls, LRU_WIDTH)
    tmaj = lambda a: jnp.transpose(a, (1, 0, 2))
    yc_s, lh_s = lru_sample(tmaj(xb_s3), tmaj(gb_s.reshape(ns, ls, LRU_WIDTH)), tmaj(state_lru_conv[0]),
                            state_lru_h[0], *lru_w)
    yc_s = tmaj(yc_s).reshape(ts, LRU_WIDTH)
    lconv_p = xb_p3[:, lp - 3:][None]
    lconv_s = xb_s3[:, ls - 3:][None]

    od_p = mla_prompt(q_p, k_p, v_p, nb, lp, tq=min(lp, 512))
    q_pe = q_s.reshape(ts, MLA_HEADS, LANES)[:, :, :MLA_ROPE]
    q_cat = jnp.concatenate([qlat_s.reshape(ts, MLA_HEADS, MLA_KV_RANK).astype(BF16), q_pe], axis=-1)
    q_cat = q_cat.reshape(ns, ls * MLA_HEADS, MLA_ROW)
    n_pages = page_table.shape[1]
    cache_t = jnp.swapaxes(cache_mla.reshape(cache_mla.shape[1:]), 1, 2)
    o_lat = mla_sample(page_table, q_cat, rows_s.reshape(ns, ls, MLA_ROW), cache_t,
                       n_pages_step=math.gcd(n_pages, MLA_SAMPLE_PAGES))
    o_lat = o_lat.reshape(ts, MLA_HEADS, MLA_ROW)[:, :, :MLA_KV_RANK].reshape(ts, MLA_HEADS * MLA_KV_RANK)
    od_s = matmul(o_lat, wuv_bd, BF16)

    w_out = o_w_out[0].astype(BF16)
    wod = jnp.pad(w_out[LRU_WIDTH:].reshape(MLA_HEADS, MLA_V, d), ((0, 0), (0, LANES - MLA_V), (0, 0)))
    wod = wod.reshape(MLA_HEADS * LANES, d)
    rw = jnp.pad(o_router_w[0], ((0, 0), (0, LANES - N_EXPERTS))).astype(BF16)
    rb = jnp.pad(o_router_b[0], (0, LANES - N_EXPERTS)).reshape(1, LANES)
    lnf = row1(o_ln_ffn[0])
    tall = tp + ts
    x1_p, hn, idx_p, gate_p = odd_out(xp, yc_p.reshape(tp, LRU_WIDTH), od_p, w_out[:LRU_WIDTH], wod, lnf, rw, rb,
                                      tall, 0)
    x1_s, hn, idx_s, gate_s = odd_out(xs, yc_s, od_s, w_out[:LRU_WIDTH], wod, lnf, rw, rb, tall, tp, hn)

    tile = math.gcd(math.gcd(tp, ts), EXPERT_ROWS)
    e_all = jnp.concatenate([idx_p[:, :TOP_K], idx_s[:, :TOP_K]], axis=0)
    g_all = jnp.concatenate([gate_p[:, :TOP_K], gate_s[:, :TOP_K]], axis=0)
    slot, expert_plan, combine_sched = _moe_plan(e_all, tile)
    yb = moe_experts(expert_plan, slot.T, g_all.T, hn, o_exp_gate[0].astype(BF16), o_exp_up[0].astype(BF16),
                     o_exp_down[0].astype(BF16), tile)
    slot_cols = jnp.pad(slot, ((0, 0), (0, LANES - TOP_K)))
    fnw = row1(final_norm)
    y_p = moe_combine(combine_sched(0, tp // tile), slot_cols[:tp], x1_p, yb, fnw, tile).reshape(nb, lp, d)
    y_s = moe_combine(combine_sched(tp // tile, tall // tile), slot_cols[tp:], x1_s, yb, fnw, tile).reshape(ns, ls, d)

    return (y_p, y_s, swa_kv_p, swa_kv_s, gconv_p, gconv_s, gs_p[None], gs_s[None],
            lconv_p, lconv_s, lh_p.reshape(1, nb, LRU_WIDTH), lh_s[None],
            rows_p.reshape(1, nb, lp // PAGE_SIZE, PAGE_SIZE, MLA_ROW), rows_s.reshape(1, ns, ls, MLA_ROW))
```

```python
import functools
import math

import jax
import jax.numpy as jnp
from jax import lax
from jax.experimental import pallas as pl
from jax.experimental.pallas import tpu as pltpu

F32 = jnp.float32
BF16 = jnp.bfloat16
I32 = jnp.int32
HI = lax.Precision.HIGHEST

D_MODEL = 1024
PAGE_SIZE = 128
SWA_WINDOW = 128
SWA_HEADS = 8
SWA_KV_HEADS = 2
SWA_GROUP = SWA_HEADS // SWA_KV_HEADS
SWA_HEAD_DIM = 64
GDN_HEADS = 4
GDN_DK = 128
GDN_DV = 128
GDN_CONV = 4
GDN_CHUNK = 64
GDN_QKV = GDN_HEADS * (2 * GDN_DK + GDN_DV)
LRU_WIDTH = 512
LRU_BLOCKS = 8
LRU_BLOCK_W = LRU_WIDTH // LRU_BLOCKS
LRU_C = 8.0
MLA_HEADS = 8
MLA_Q_RANK = 384
MLA_KV_RANK = 256
MLA_NOPE = 64
MLA_ROPE = 32
MLA_V = 64
MLA_ROW = MLA_KV_RANK + MLA_ROPE
ROPE_THETA = 10000.0
D_FF = 2816
N_EXPERTS = 8
TOP_K = 2
MOE_FF = 2048
NORM_EPS = 1e-6

LANES = 128
SUBLANES = 8
VMEM_LIMIT = 56 << 20

NN = (((1,), (0,)), ((), ()))
NT = (((1,), (1,)), ((), ()))
TN = (((0,), (0,)), ((), ()))


def _mm(a, b, dims=NN, precision=None):
    return lax.dot_general(a, b, dims, precision=precision, preferred_element_type=F32)


def _cp(sem):
    return pltpu.CompilerParams(dimension_semantics=sem, vmem_limit_bytes=VMEM_LIMIT)


def _rms(x, w):
    return x * lax.rsqrt(jnp.mean(x * x, axis=-1, keepdims=True) + NORM_EPS) * w


def _silu(x):
    return x * jax.nn.sigmoid(x)


def _full(shape):
    nd = len(shape)
    return pl.BlockSpec(shape, lambda *a: (0,) * nd)


def _row_tile(t, pref):
    for c in pref:
        if t % c == 0:
            return c
    return t


def _norm_proj_kernel(x_ref, ln_ref, *refs):
    n = len(refs) // 2
    h = _rms(x_ref[...], ln_ref[...]).astype(BF16)
    for w_ref, o_ref in zip(refs[:n], refs[n:]):
        o_ref[...] = _mm(h, w_ref[...])


def norm_proj(x, ln, ws):
    t, d = x.shape
    tm = _row_tile(t, (512, 256, 128, 64, 32, 16, 8))
    return pl.pallas_call(
        _norm_proj_kernel,
        grid=(t // tm,),
        in_specs=[pl.BlockSpec((tm, d), lambda i: (i, 0)), _full((1, d))] + [_full(w.shape) for w in ws],
        out_specs=[pl.BlockSpec((tm, w.shape[1]), lambda i: (i, 0)) for w in ws],
        out_shape=[jax.ShapeDtypeStruct((t, w.shape[1]), F32) for w in ws],
        compiler_params=_cp(("parallel",)),
        name="norm_proj",
    )(x, ln, *ws)


def _swa_softmax_pv(parts, sink):
    m = sink
    for s, _ in parts:
        m = jnp.maximum(m, jnp.max(s, axis=-1, keepdims=True))
    den = jnp.exp(sink - m)
    o = None
    for s, v in parts:
        p = jnp.exp(s - m)
        den = den + jnp.sum(p, axis=-1, keepdims=True)
        pv = _mm(p.astype(BF16), v)
        o = pv if o is None else o + pv
    return o / den


def _swa_prompt_kernel(sink_ref, q_ref, kvc_ref, kvp_ref, o_ref):
    b = pl.program_id(1)
    w = SWA_WINDOW
    hd = SWA_HEAD_DIM
    ng = SWA_GROUP
    q = q_ref[...]
    band = jnp.concatenate([kvp_ref[...], kvc_ref[...]], axis=0).astype(BF16)
    r = lax.broadcasted_iota(I32, (ng * w, 1), 0)
    grp = r // w
    qi = r % w
    t = lax.broadcasted_iota(I32, (ng * w, 2 * w), 1)
    dist = qi + w - t
    valid = jnp.logical_and(jnp.logical_and(dist >= 0, dist < w), jnp.logical_or(t >= w, b > 0))
    distf = dist.astype(F32)
    outs = [None] * SWA_HEADS
    scores = []
    for j in range(SWA_KV_HEADS):
        qs = jnp.concatenate([q[:, (j * ng + g) * hd:(j * ng + g + 1) * hd] for g in range(ng)], axis=0)
        scores.append(_mm(qs.astype(BF16), band[:, j * hd:(j + 1) * hd], NT) * (hd ** -0.5))
    for j in range(SWA_KV_HEADS):
        heads = [j * ng + g for g in range(ng)]
        slope = jnp.zeros((ng * w, 1), F32)
        sink = jnp.zeros((ng * w, 1), F32)
        for g, h in enumerate(heads):
            slope = jnp.where(grp == g, 2.0 ** (-8.0 * (h + 1) / SWA_HEADS), slope)
            sink = jnp.where(grp == g, sink_ref[h], sink)
        vb = band[:, (SWA_KV_HEADS + j) * hd:(SWA_KV_HEADS + j + 1) * hd]
        s = jnp.where(valid, scores[j] - slope * distf, -jnp.inf)
        o = _swa_softmax_pv([(s, vb)], sink)
        for g, h in enumerate(heads):
            outs[h] = o[g * w:(g + 1) * w, :]
    o_ref[...] = jnp.concatenate(outs, axis=1)


def swa_prompt(qkv, sinks, n, l):
    w = SWA_WINDOW
    nb = l // w
    nq = SWA_HEADS * SWA_HEAD_DIM
    nkv = 2 * SWA_KV_HEADS * SWA_HEAD_DIM
    return pl.pallas_call(
        _swa_prompt_kernel,
        grid=(n, nb),
        in_specs=[
            pl.BlockSpec(memory_space=pltpu.SMEM),
            pl.BlockSpec((w, nq), lambda i, b: (i * nb + b, 0)),
            pl.BlockSpec((w, nkv), lambda i, b: (i * nb + b, nq // nkv)),
            pl.BlockSpec((w, nkv), lambda i, b: (i * nb + jnp.maximum(b - 1, 0), nq // nkv)),
        ],
        out_specs=pl.BlockSpec((w, nq), lambda i, b: (i * nb + b, 0)),
        out_shape=jax.ShapeDtypeStruct((n * l, nq), F32),
        compiler_params=_cp(("parallel", "arbitrary")),
        name="swa_prompt",
    )(sinks, qkv, qkv, qkv)


SWA_SAMPLE_REQS = 4


def _swa_sample_kernel(sink_ref, q_ref, buf_ref, o_ref, *, l):
    hd = SWA_HEAD_DIM
    ng = SWA_GROUP
    nr, lp, _ = q_ref.shape
    wb = buf_ref.shape[1]
    r = lax.broadcasted_iota(I32, (ng * lp, 1), 0)
    grp = r // lp
    qi = r % lp
    dist_b = qi + wb - lax.broadcasted_iota(I32, (ng * lp, wb), 1)
    valid_b = dist_b < SWA_WINDOW
    tn = lax.broadcasted_iota(I32, (ng * lp, lp), 1)
    dist_n = qi - tn
    valid_n = jnp.logical_and(tn <= qi, tn < l)
    slopes, sinks = [], []
    for j in range(SWA_KV_HEADS):
        slope = jnp.zeros((ng * lp, 1), F32)
        sink = jnp.zeros((ng * lp, 1), F32)
        for g in range(ng):
            h = j * ng + g
            slope = jnp.where(grp == g, 2.0 ** (-8.0 * (h + 1) / SWA_HEADS), slope)
            sink = jnp.where(grp == g, sink_ref[h], sink)
        slopes.append(slope)
        sinks.append(sink)
    chains = []
    for rq in range(nr):
        x = q_ref[rq]
        buf = buf_ref[rq].astype(BF16)
        kvn = x[:, SWA_HEADS * hd:].astype(BF16)
        for j in range(SWA_KV_HEADS):
            qs = jnp.concatenate([x[:, (j * ng + g) * hd:(j * ng + g + 1) * hd] for g in range(ng)], axis=0)
            qs = qs.astype(BF16)
            ks = slice(j * hd, (j + 1) * hd)
            vs = slice((SWA_KV_HEADS + j) * hd, (SWA_KV_HEADS + j + 1) * hd)
            s_b = _mm(qs, buf[:, ks], NT) * (hd ** -0.5)
            s_n = _mm(qs, kvn[:, ks], NT) * (hd ** -0.5)
            chains.append((rq, j, s_b, s_n, buf[:, vs], kvn[:, vs]))
    outs = [[None] * SWA_HEADS for _ in range(nr)]
    for rq, j, s_b, s_n, vb, vn in chains:
        s_b = jnp.where(valid_b, s_b - slopes[j] * dist_b.astype(F32), -jnp.inf)
        s_n = jnp.where(valid_n, s_n - slopes[j] * dist_n.astype(F32), -jnp.inf)
        o = _swa_softmax_pv([(s_b, vb), (s_n, vn)], sinks[j])
        for g in range(ng):
            outs[rq][j * ng + g] = o[g * lp:(g + 1) * lp, :]
    for rq in range(nr):
        o_ref[rq] = jnp.concatenate(outs[rq], axis=1)


def swa_sample(qkv_pad, buf, sinks, l):
    n, lp, c = qkv_pad.shape
    wb = buf.shape[1]
    nq = SWA_HEADS * SWA_HEAD_DIM
    nr = math.gcd(n, SWA_SAMPLE_REQS)
    return pl.pallas_call(
        functools.partial(_swa_sample_kernel, l=l),
        grid=(n // nr,),
        in_specs=[
            pl.BlockSpec(memory_space=pltpu.SMEM),
            pl.BlockSpec((nr, lp, c), lambda i: (i, 0, 0)),
            pl.BlockSpec((nr, wb, buf.shape[2]), lambda i: (i, 0, 0)),
        ],
        out_specs=pl.BlockSpec((nr, lp, nq), lambda i: (i, 0, 0)),
        out_shape=jax.ShapeDtypeStruct((n, lp, nq), F32),
        compiler_params=_cp(("parallel",)),
        name="swa_sample",
    )(sinks, qkv_pad, buf)


GDN_SOLVE_GROUP = 8


def _mm3(a, b):
    ah = a.astype(BF16)
    al = (a - ah.astype(F32)).astype(BF16)
    bh = b.astype(BF16)
    bl = (b - bh.astype(F32)).astype(BF16)
    return _mm(jnp.concatenate([ah, ah, al], axis=1), jnp.concatenate([bh, bl, bh], axis=0))


def _block_rows(x, c, nh):
    blk = lax.broadcasted_iota(I32, x.shape, 1) // c
    return jnp.concatenate([jnp.where(blk == i, x, 0.0) for i in range(nh)], axis=0)


def _gdn_kernel(x_ref, z_ref, bg_ref, cprev_ref, s0_ref, cw_ref, alog_ref, dtb_ref, gn_ref,
                o_ref, sfin_ref, xbuf, s_sc, q_sc, k_sc, vb_sc, kb_sc, qd_sc, kd_sc, u_sc, w_sc, be_sc, gc_sc, egl_sc,
                qkd_sc, *, chunk, valid):
    t = pl.program_id(1)
    tl = x_ref.shape[1]
    c = chunk
    nck = tl // c
    nh = GDN_HEADS
    dk = GDN_DK

    @pl.when(t == 0)
    def _():
        xbuf[0:SUBLANES, :] = cprev_ref[0]
        s_sc[...] = s0_ref[0]

    xbuf[SUBLANES:SUBLANES + tl, :] = x_ref[0]
    cw = cw_ref[...]
    conv = xbuf[5:5 + tl, :] * cw[0:1, :]
    for j in range(1, GDN_CONV):
        conv = conv + xbuf[5 + j:5 + j + tl, :] * cw[j:j + 1, :]
    xbuf[0:SUBLANES, :] = xbuf[tl:tl + SUBLANES, :]
    act = _silu(conv)

    row = lax.broadcasted_iota(I32, (tl, 1), 0)
    rmask = (row % c) < valid
    bg = bg_ref[0]
    beta = jax.nn.sigmoid(bg)
    g = -jnp.exp(alog_ref[...]) * jax.nn.softplus(bg + dtb_ref[...])
    g = jnp.where(rmask, g, 0.0)
    rc = row % c
    s = 1
    while s < c:
        g = g + jnp.where(rc >= s, pltpu.roll(g, s, axis=0), 0.0)
        s *= 2
    glast = jnp.broadcast_to(g.reshape(nck, c, LANES)[:, c - 1:c, :], (nck, c, LANES)).reshape(tl, LANES)
    egc = jnp.exp(g)
    kfac = jnp.exp(glast - g)
    be_sc[...] = beta
    gc_sc[...] = g
    egl_sc[...] = jnp.exp(glast)
    for h in range(nh):
        hs = slice(h * dk, (h + 1) * dk)
        qh = act[:, h * dk:(h + 1) * dk]
        kh = act[:, (nh + h) * dk:(nh + h + 1) * dk]
        vh = act[:, (2 * nh + h) * dk:(2 * nh + h + 1) * dk]
        qh = qh * lax.rsqrt(jnp.sum(qh * qh, axis=-1, keepdims=True) + NORM_EPS) * (dk ** -0.5)
        kh = kh * lax.rsqrt(jnp.sum(kh * kh, axis=-1, keepdims=True) + NORM_EPS)
        qh = jnp.where(rmask, qh, 0.0)
        kh = jnp.where(rmask, kh, 0.0)
        vh = jnp.where(rmask, vh, 0.0)
        b_h = beta[:, h:h + 1]
        e_h = egc[:, nh + h:nh + h + 1]
        q_sc[:, hs] = qh
        k_sc[:, hs] = kh
        vb_sc[:, hs] = vh * b_h
        kb_sc[:, hs] = kh * (b_h * e_h)
        qd_sc[:, hs] = qh * e_h
        kd_sc[:, hs] = kh * kfac[:, nh + h:nh + h + 1]

    ii = lax.broadcasted_iota(I32, (c, nh * c), 0)
    jl = lax.broadcasted_iota(I32, (c, nh * c), 1) % c
    eye_cat = (ii == jl).astype(F32)
    gn = gn_ref[...]
    n_factors = max(1, int(math.ceil(math.log2(valid))))

    ng = math.gcd(nck, GDN_SOLVE_GROUP)

    def solve_body(gi, carry):
        rows_g = [pl.ds(pl.multiple_of((gi * ng + g) * c, c), c) for g in range(ng)]
        a_g, qkd_g = [], []
        for rows in rows_g:
            gcs = gc_sc[rows, :]
            bes = be_sc[rows, :]
            kk, qk, gexp, bexp = [], [], [], []
            for h in range(nh):
                hs = slice(h * dk, (h + 1) * dk)
                kb16 = k_sc[rows, hs].astype(BF16)
                kk.append(_mm(kb16, kb16, NT))
                qk.append(_mm(q_sc[rows, hs].astype(BF16), kb16, NT))
                gexp.append(jnp.broadcast_to(gcs[:, nh + h:nh + h + 1], (c, c)))
                bexp.append(jnp.broadcast_to(bes[:, h:h + 1], (c, c)))
            kk, qk, gexp, bexp = [jnp.concatenate(v, axis=1) for v in (kk, qk, gexp, bexp)]
            grow = jnp.sum(jnp.where(ii == jl, gexp, 0.0), axis=0, keepdims=True)
            decay = jnp.where(ii >= jl, jnp.exp(gexp - grow), 0.0)
            a_g.append(jnp.where(ii > jl, bexp * kk * decay, 0.0))
            qkd_g.append(qk * decay)
        t_g = [eye_cat - a for a in a_g]
        p_g = a_g
        for _ in range(n_factors - 1):
            p_g = [_mm3(p, _block_rows(p, c, nh)) for p in p_g]
            t_g = [_mm3(tv, _block_rows(eye_cat + p, c, nh)) for tv, p in zip(t_g, p_g)]
        for rows, tv, qkd in zip(rows_g, t_g, qkd_g):
            rhs = jnp.concatenate(
                [jnp.concatenate([vb_sc[rows, h * dk:(h + 1) * dk], kb_sc[rows, h * dk:(h + 1) * dk]], axis=1)
                 for h in range(nh)], axis=0)
            sol = _mm3(_block_rows(tv, c, nh), rhs)
            for h in range(nh):
                hs = slice(h * dk, (h + 1) * dk)
                u_sc[rows, hs] = sol[h * c:(h + 1) * c, :GDN_DV]
                w_sc[rows, hs] = sol[h * c:(h + 1) * c, GDN_DV:]
            qkd_sc[rows, :] = qkd
        return carry

    lax.fori_loop(0, nck // ng, solve_body, 0)

    def scan_body(ci, carry):
        rows = pl.ds(pl.multiple_of(ci * c, c), c)
        egl = egl_sc[rows, :]
        sts, v_news, o1s = [], [], []
        for h in range(nh):
            hs = slice(h * dk, (h + 1) * dk)
            st = s_sc[h]
            wq = jnp.concatenate([w_sc[rows, hs], qd_sc[rows, hs]], axis=0)
            r = _mm(wq.astype(BF16), st.astype(BF16))
            sts.append(st)
            v_news.append(u_sc[rows, hs] - r[:c])
            o1s.append(r[c:])
        o2 = _mm(_block_rows(qkd_sc[rows, :], c, nh).astype(BF16), jnp.concatenate(v_news, axis=0).astype(BF16))
        for h in range(nh):
            hs = slice(h * dk, (h + 1) * dk)
            upd = _mm(kd_sc[rows, hs].astype(BF16), v_news[h].astype(BF16), TN)
            s_sc[h] = sts[h] * egl[0:1, nh + h:nh + h + 1] + upd
            o = o1s[h] + o2[h * c:(h + 1) * c]
            o_ref[0, rows, hs] = _rms(o, gn) * _silu(z_ref[0, rows, hs])
        return carry

    lax.fori_loop(0, nck, scan_body, 0)

    @pl.when(t == pl.num_programs(1) - 1)
    def _():
        sfin_ref[0] = s_sc[...]


def gdn(x, z, bg, cprev, s0, conv_w, a_log, dt_bias, gnorm, chunk, valid, tl):
    n, lp, cq = x.shape
    nt = lp // tl
    nv = GDN_HEADS * GDN_DV
    alog = jnp.zeros((1, LANES), F32).at[0, GDN_HEADS:2 * GDN_HEADS].set(a_log)
    dtb = jnp.zeros((1, LANES), F32).at[0, GDN_HEADS:2 * GDN_HEADS].set(dt_bias)
    seq3 = lambda w: pl.BlockSpec((1, tl, w), lambda i, t: (i, t, 0))
    return pl.pallas_call(
        functools.partial(_gdn_kernel, chunk=chunk, valid=valid),
        grid=(n, nt),
        in_specs=[
            seq3(cq), seq3(nv), seq3(LANES),
            pl.BlockSpec((1, SUBLANES, cq), lambda i, t: (i, 0, 0)),
            pl.BlockSpec((1, GDN_HEADS, GDN_DK, GDN_DV), lambda i, t: (i, 0, 0, 0)),
            _full((GDN_CONV, cq)), _full((1, LANES)), _full((1, LANES)), _full((1, GDN_DV)),
        ],
        out_specs=[seq3(nv), pl.BlockSpec((1, GDN_HEADS, GDN_DK, GDN_DV), lambda i, t: (i, 0, 0, 0))],
        out_shape=[jax.ShapeDtypeStruct((n, lp, nv), F32),
                   jax.ShapeDtypeStruct((n, GDN_HEADS, GDN_DK, GDN_DV), F32)],
        scratch_shapes=[
            pltpu.VMEM((tl + SUBLANES, cq), F32),
            pltpu.VMEM((GDN_HEADS, GDN_DK, GDN_DV), F32),
        ] + [pltpu.VMEM((tl, nv), F32)] * 8 + [pltpu.VMEM((tl, LANES), F32)] * 3
        + [pltpu.VMEM((tl, GDN_HEADS * chunk), F32)],
        compiler_params=_cp(("parallel", "arbitrary")),
        name="gdn",
    )(x, z, bg, cprev, s0, conv_w, alog, dtb, gnorm.reshape(1, GDN_DV))


def _even_out_kernel(x_ref, oa_ref, ob_ref, woa_ref, wob_ref, ln_ref, wg_ref, wu_ref, wd_ref,
                     o_ref, x1_sc, h_sc, acc_sc):
    f = pl.program_id(1)

    @pl.when(f == 0)
    def _():
        x1 = x_ref[...] + (_mm(oa_ref[...].astype(BF16), woa_ref[...]) + _mm(ob_ref[...].astype(BF16), wob_ref[...]))
        x1_sc[...] = x1
        h_sc[...] = _rms(x1, ln_ref[...]).astype(BF16)
        acc_sc[...] = jnp.zeros_like(acc_sc)

    h = h_sc[...]
    act = (_silu(_mm(h, wg_ref[...])) * _mm(h, wu_ref[...])).astype(BF16)
    acc_sc[...] += _mm(act, wd_ref[...])

    @pl.when(f == pl.num_programs(1) - 1)
    def _():
        o_ref[...] = x1_sc[...] + acc_sc[...]


def even_out(x, oa, ob, woa, wob, ln, wg, wu, wd):
    t, d = x.shape
    ff = wg.shape[1]
    tm = _row_tile(t, (1024, 512, 256, 128, 64, 32, 16, 8))
    tf = 256 if ff % 256 == 0 else LANES
    row = lambda w: pl.BlockSpec((tm, w), lambda i, f: (i, 0))
    return pl.pallas_call(
        _even_out_kernel,
        grid=(t // tm, ff // tf),
        in_specs=[row(d), row(oa.shape[1]), row(ob.shape[1]), _full(woa.shape), _full(wob.shape), _full((1, d)),
                  pl.BlockSpec((d, tf), lambda i, f: (0, f)), pl.BlockSpec((d, tf), lambda i, f: (0, f)),
                  pl.BlockSpec((tf, d), lambda i, f: (f, 0))],
        out_specs=row(d),
        out_shape=jax.ShapeDtypeStruct((t, d), F32),
        scratch_shapes=[pltpu.VMEM((tm, d), F32), pltpu.VMEM((tm, d), BF16), pltpu.VMEM((tm, d), F32)],
        compiler_params=_cp(("parallel", "arbitrary")),
        name="even_out",
    )(x, oa, ob, woa, wob, ln, wg, wu, wd)


def _rope_lanes(v, c, s):
    n = v.shape[1]
    lane = lax.broadcasted_iota(I32, (1, n), 1) % LANES
    sw = jnp.where(lane < MLA_ROPE // 2, pltpu.roll(v, n - MLA_ROPE // 2, axis=1), pltpu.roll(v, MLA_ROPE // 2, axis=1))
    return v * c + sw * s


def _odd_in_kernel(x_ref, ln_ref, wxb_ref, wgb_ref, wcq_ref, wckv_ref, wkpe_ref, qn_ref, kvn_ref, wuq_ref,
                   cos_ref, sin_ref, *rest, absorbed):
    if absorbed:
        wabs_ref, xb_ref, gb_ref, q_ref, qlat_ref, rows_ref = rest
    else:
        wuk_ref, wuv_ref, xb_ref, gb_ref, q_ref, k_ref, v_ref, rows_ref = rest
    h = _rms(x_ref[...], ln_ref[...]).astype(BF16)
    xb_ref[...] = _mm(h, wxb_ref[...])
    gb_ref[...] = _mm(h, wgb_ref[...])
    cq = _mm(h, wcq_ref[...])
    ckv = _mm(h, wckv_ref[...])
    kpe = _mm(h, wkpe_ref[...])
    cqn = _rms(cq, qn_ref[...]).astype(BF16)
    ckvn = _rms(ckv, kvn_ref[...])
    c128 = cos_ref[...]
    s128 = sin_ref[...]
    nh = MLA_HEADS
    q = _rope_lanes(_mm(cqn, wuq_ref[...]), jnp.concatenate([c128] * nh, axis=1), jnp.concatenate([s128] * nh, axis=1))
    kpe_r = _rope_lanes(kpe, c128, s128)
    qb = q.astype(BF16)
    q_ref[...] = qb
    rows_ref[:, 0:MLA_KV_RANK] = ckvn
    rows_ref[:, MLA_KV_RANK:MLA_ROW] = kpe_r[:, 0:MLA_ROPE]
    if absorbed:
        qlat_ref[...] = _mm(qb, wabs_ref[...])
    else:
        ckvb = ckvn.astype(BF16)
        k_ref[...] = (_mm(ckvb, wuk_ref[...]) + jnp.concatenate([kpe_r] * nh, axis=1)).astype(BF16)
        v_ref[...] = _mm(ckvb, wuv_ref[...]).astype(BF16)


def odd_in(x, ln, ws, qn, kvn, wuq, cos_t, sin_t, extra, absorbed):
    t, d = x.shape
    tm = _row_tile(t, (512, 256, 128, 64, 32, 16, 8))
    nblk = cos_t.shape[0] // tm
    hb = MLA_HEADS * LANES
    row = lambda w: pl.BlockSpec((tm, w), lambda i: (i, 0))
    tbl = pl.BlockSpec((tm, LANES), lambda i: (i % nblk, 0))
    if absorbed:
        outs = [(LRU_WIDTH, F32), (LRU_WIDTH, F32), (hb, BF16), (extra[0].shape[1], F32), (MLA_ROW, F32)]
    else:
        outs = [(LRU_WIDTH, F32), (LRU_WIDTH, F32), (hb, BF16), (hb, BF16), (hb, BF16), (MLA_ROW, F32)]
    return pl.pallas_call(
        functools.partial(_odd_in_kernel, absorbed=absorbed),
        grid=(t // tm,),
        in_specs=[row(d), _full((1, d))] + [_full(w.shape) for w in ws]
        + [_full(qn.shape), _full(kvn.shape), _full(wuq.shape), tbl, tbl] + [_full(w.shape) for w in extra],
        out_specs=[row(w) for w, _ in outs],
        out_shape=[jax.ShapeDtypeStruct((t, w), dt) for w, dt in outs],
        compiler_params=_cp(("parallel",)),
        name="odd_in",
    )(x, ln, *ws, qn, kvn, wuq, cos_t, sin_t, *extra)


def _expm1(x):
    u = jnp.exp(x)
    um1 = u - 1.0
    small = um1 * x / jnp.log(u)
    return jnp.where(um1 == 0.0, x, jnp.where(jnp.abs(x) < 0.5, small, um1))


def _lru_gates(xc, wa_ref, wx_ref, ba_ref, bx_ref, lam_ref):
    xcb = xc.astype(BF16)
    r = jax.nn.sigmoid(_mm(xcb, wa_ref[...]) + ba_ref[...])
    i = jax.nn.sigmoid(_mm(xcb, wx_ref[...]) + bx_ref[...])
    log_a = -LRU_C * r * jax.nn.softplus(-lam_ref[...])
    a = jnp.exp(log_a)
    b = jnp.sqrt(-_expm1(2.0 * log_a)) * (i * xc)
    return a, b


def _lru_prompt_kernel(x_ref, g_ref, cprev_ref, h0_ref, cw_ref, cb_ref, wa_ref, wx_ref, ba_ref, bx_ref, lam_ref,
                       y_ref, hl_ref, xbuf, h_sc):
    t = pl.program_id(1)
    tl = x_ref.shape[1]

    @pl.when(t == 0)
    def _():
        xbuf[0:SUBLANES, :] = cprev_ref[0]
        h_sc[0:1, :] = h0_ref[0]

    xbuf[SUBLANES:SUBLANES + tl, :] = x_ref[0]
    cw = cw_ref[...]
    xc = xbuf[5:5 + tl, :] * cw[0:1, :]
    for j in range(1, cw.shape[0]):
        xc = xc + xbuf[5 + j:5 + j + tl, :] * cw[j:j + 1, :]
    xc = xc + cb_ref[...]
    xbuf[0:SUBLANES, :] = xbuf[tl:tl + SUBLANES, :]
    a, b = _lru_gates(xc, wa_ref, wx_ref, ba_ref, bx_ref, lam_ref)
    row = lax.broadcasted_iota(I32, (tl, 1), 0)
    s = 1
    while s < tl:
        m = row >= s
        b = jnp.where(m, a * pltpu.roll(b, s, axis=0) + b, b)
        a = jnp.where(m, a * pltpu.roll(a, s, axis=0), a)
        s *= 2
    hs = a * h_sc[0:1, :] + b
    h_sc[0:1, :] = hs[tl - 1:tl, :]
    y_ref[0] = hs * jax.nn.gelu(g_ref[0])
    hl_ref[0] = hs[tl - 1:tl, :]


def lru_prompt(xb, gb, cprev, h0, cw, cb, wa, wx, ba, bx, lam, tl):
    n, l, c = xb.shape
    seq = pl.BlockSpec((1, tl, c), lambda i, t: (i, t, 0))
    return pl.pallas_call(
        _lru_prompt_kernel,
        grid=(n, l // tl),
        in_specs=[seq, seq, pl.BlockSpec((1, SUBLANES, c), lambda i, t: (i, 0, 0)),
                  pl.BlockSpec((1, 1, c), lambda i, t: (i, 0, 0)),
                  _full(cw.shape), _full((1, c)), _full(wa.shape), _full(wx.shape), _full((1, c)), _full((1, c)),
                  _full((1, c))],
        out_specs=[seq, pl.BlockSpec((1, 1, c), lambda i, t: (i, 0, 0))],
        out_shape=[jax.ShapeDtypeStruct((n, l, c), F32), jax.ShapeDtypeStruct((n, 1, c), F32)],
        scratch_shapes=[pltpu.VMEM((tl + SUBLANES, c), F32), pltpu.VMEM((SUBLANES, c), F32)],
        compiler_params=_cp(("parallel", "arbitrary")),
        name="lru_prompt",
    )(xb, gb, cprev, h0, cw, cb, wa, wx, ba, bx, lam)


def _lru_sample_kernel(x_ref, g_ref, cprev_ref, h0_ref, cw_ref, cb_ref, wa_ref, wx_ref, ba_ref, bx_ref, lam_ref,
                       y_ref, hl_ref):
    l = x_ref.shape[0]
    cw = cw_ref[...]
    nw = cw.shape[0]
    xx = [cprev_ref[j] for j in range(nw - 1)] + [x_ref[j] for j in range(l)]
    h = h0_ref[...]
    for i in range(l):
        xc = xx[i] * cw[0:1, :]
        for j in range(1, nw):
            xc = xc + xx[i + j] * cw[j:j + 1, :]
        xc = xc + cb_ref[...]
        a, b = _lru_gates(xc, wa_ref, wx_ref, ba_ref, bx_ref, lam_ref)
        h = a * h + b
        y_ref[i] = h * jax.nn.gelu(g_ref[i])
    hl_ref[...] = h


def lru_sample(xb, gb, cprev, h0, cw, cb, wa, wx, ba, bx, lam):
    l, n, c = xb.shape
    return pl.pallas_call(
        _lru_sample_kernel,
        out_shape=[jax.ShapeDtypeStruct((l, n, c), F32), jax.ShapeDtypeStruct((n, c), F32)],
        compiler_params=pltpu.CompilerParams(vmem_limit_bytes=VMEM_LIMIT),
        name="lru_sample",
    )(xb, gb, cprev, h0, cw, cb, wa, wx, ba, bx, lam)


MLA_HEADS_PER_STEP = 4


def _mla_prompt_kernel(q_ref, k_ref, v_ref, o_ref, m_sc, l_sc, acc_sc, *, scale):
    qi = pl.program_id(2)
    tq = q_ref.shape[0]
    tk = tq
    g = q_ref.shape[1] // LANES
    c1 = scale * math.log2(math.e)
    m_sc[...] = jnp.full_like(m_sc, -jnp.inf)
    l_sc[...] = jnp.zeros_like(l_sc)
    acc_sc[...] = jnp.zeros_like(acc_sc)
    on_or_below = lax.broadcasted_iota(I32, (tq, tk), 1) <= lax.broadcasted_iota(I32, (tq, tk), 0)

    def step(ki, masked):
        rows = pl.ds(pl.multiple_of(ki * tk, tk), tk)
        scores = [_mm(q_ref[:, h * LANES:(h + 1) * LANES], k_ref[rows, h * LANES:(h + 1) * LANES], NT) * c1
                  for h in range(g)]
        for h in range(g):
            hs = slice(h * LANES, (h + 1) * LANES)
            s = scores[h]
            if masked:
                s = jnp.where(on_or_below, s, -jnp.inf)
            m_prev = m_sc[h]
            m_new = jnp.maximum(m_prev, jnp.max(s, axis=-1, keepdims=True))
            alpha = jnp.exp2(m_prev - m_new)
            p = jnp.exp2(s - jnp.concatenate([m_new] * (tk // LANES), axis=1))
            l_sc[h] = alpha * l_sc[h] + jnp.sum(p, axis=-1, keepdims=True)
            acc_sc[:, hs] = alpha * acc_sc[:, hs] + _mm(p.astype(BF16), v_ref[rows, hs])
            m_sc[h] = m_new

    def body(ki, carry):
        step(ki, False)
        return carry

    lax.fori_loop(0, qi, body, 0)
    step(qi, True)
    for h in range(g):
        hs = slice(h * LANES, (h + 1) * LANES)
        o_ref[:, hs] = (acc_sc[:, hs] / l_sc[h]).astype(o_ref.dtype)


def mla_prompt(q, k, v, n, l, tq):
    nq = l // tq
    g = MLA_HEADS_PER_STEP
    w = g * LANES
    scale = (MLA_NOPE + MLA_ROPE) ** -0.5
    return pl.pallas_call(
        functools.partial(_mla_prompt_kernel, scale=scale),
        grid=(n, MLA_HEADS // g, nq),
        in_specs=[pl.BlockSpec((tq, w), lambda i, h, j: (i * nq + j, h)),
                  pl.BlockSpec((l, w), lambda i, h, j: (i, h)), pl.BlockSpec((l, w), lambda i, h, j: (i, h))],
        out_specs=pl.BlockSpec((tq, w), lambda i, h, j: (i * nq + j, h)),
        out_shape=jax.ShapeDtypeStruct(q.shape, BF16),
        scratch_shapes=[pltpu.VMEM((g, tq, LANES), F32), pltpu.VMEM((g, tq, LANES), F32), pltpu.VMEM((tq, w), F32)],
        compiler_params=_cp(("parallel", "parallel", "arbitrary")),
        name="mla_prompt",
    )(q, k, v)


MLA_SAMPLE_PAGES = 32
MLA_SAMPLE_GROUPS = 2


def _page_copy(pt_ref, cache_hbm, buf, sems, step, k, slot, n_pages_step):
    return pltpu.make_async_copy(cache_hbm.at[pt_ref[step * n_pages_step + k]], buf.at[slot, k], sems.at[slot])


def _mla_sample_kernel(pt_ref, q_ref, rows_ref, cache_hbm, o_ref, buf, sems, m_sc, l_sc, acc_sc,
                       *, n_pages_step, l, scale):
    j = pl.program_id(1)
    steps = pl.num_programs(1)
    total = pl.num_programs(0) * steps
    s_glob = pl.program_id(0) * steps + j
    slot = lax.rem(s_glob, 2)
    nxt = jnp.minimum(s_glob + 1, total - 1)
    copy = functools.partial(_page_copy, pt_ref, cache_hbm, buf, sems, n_pages_step=n_pages_step)

    @pl.when(s_glob == 0)
    def _():
        for k in range(n_pages_step):
            copy(0, k, 0).start()

    @pl.when(j == 0)
    def _():
        m_sc[...] = jnp.full_like(m_sc, -jnp.inf)
        l_sc[...] = jnp.zeros_like(l_sc)
        acc_sc[...] = jnp.zeros_like(acc_sc)

    for k in range(n_pages_step):
        copy(s_glob, k, slot).wait()
    for k in range(n_pages_step):
        copy(nxt, k, 1 - slot).start()

    q = q_ref[0]
    per = n_pages_step // MLA_SAMPLE_GROUPS
    kts = [jnp.concatenate([buf[slot, k].astype(BF16) for k in range(g * per, (g + 1) * per)], axis=1)
           for g in range(MLA_SAMPLE_GROUPS)]
    ss = [_mm(q, kt) * scale for kt in kts]
    m_run = m_sc[...]
    l_run = l_sc[...]
    acc = acc_sc[...]
    for kt, s in zip(kts, ss):
        m_new = jnp.maximum(m_run, jnp.max(s, axis=-1, keepdims=True))
        alpha = jnp.exp(m_run - m_new)
        p32 = jnp.exp(s - m_new)
        l_run = alpha * l_run + jnp.sum(p32, axis=-1, keepdims=True)
        acc = alpha * acc + _mm(p32.astype(BF16), kt, NT)
        m_run = m_new
    m_sc[...] = m_run
    l_sc[...] = l_run
    acc_sc[...] = acc

    @pl.when(j == pl.num_programs(1) - 1)
    def _():
        qf = q.astype(F32)
        rows = rows_ref[0]
        tok = lax.broadcasted_iota(I32, (qf.shape[0], 1), 0) // MLA_HEADS
        sn = []
        for mm in range(l):
            sm = jnp.sum(qf * rows[mm:mm + 1, :], axis=-1, keepdims=True) * scale
            sn.append(jnp.where(tok >= mm, sm, -jnp.inf))
        m_old = m_sc[...]
        m_fin = m_old
        for sm in sn:
            m_fin = jnp.maximum(m_fin, sm)
        al = jnp.exp(m_old - m_fin)
        lsum = al * l_sc[...]
        acc = al * acc_sc[...]
        for mm in range(l):
            pm = jnp.exp(sn[mm] - m_fin)
            lsum = lsum + pm
            acc = acc + pm * rows[mm:mm + 1, :]
        o_ref[0] = acc / lsum

    @pl.when(s_glob == total - 1)
    def _():
        for k in range(n_pages_step):
            copy(nxt, k, 1 - slot).wait()


def mla_sample(page_table, q_cat, rows, cache_t, n_pages_step):
    n, r, c = q_cat.shape
    l = rows.shape[1]
    n_pages = page_table.shape[1]
    steps = n_pages // n_pages_step
    scale = (MLA_NOPE + MLA_ROPE) ** -0.5
    assert steps * n_pages_step == n_pages
    pt = page_table.reshape(-1)
    return pl.pallas_call(
        functools.partial(_mla_sample_kernel, n_pages_step=n_pages_step, l=l, scale=scale),
        grid_spec=pltpu.PrefetchScalarGridSpec(
            num_scalar_prefetch=1,
            grid=(n, steps),
            in_specs=[pl.BlockSpec((1, r, c), lambda i, j, pt_ref: (i, 0, 0)),
                      pl.BlockSpec((1, l, c), lambda i, j, pt_ref: (i, 0, 0)),
                      pl.BlockSpec(memory_space=pl.ANY)],
            out_specs=pl.BlockSpec((1, r, c), lambda i, j, pt_ref: (i, 0, 0)),
            scratch_shapes=[pltpu.VMEM((2, n_pages_step, c, PAGE_SIZE), F32), pltpu.SemaphoreType.DMA((2,)),
                            pltpu.VMEM((r, 1), F32), pltpu.VMEM((r, 1), F32), pltpu.VMEM((r, c), F32)],
        ),
        out_shape=jax.ShapeDtypeStruct((n, r, c), F32),
        compiler_params=_cp(("arbitrary", "arbitrary")),
        name="mla_sample",
    )(pt, q_cat, rows, cache_t)


def _matmul_kernel(x_ref, w_ref, o_ref):
    o_ref[...] = _mm(x_ref[...].astype(BF16), w_ref[...]).astype(o_ref.dtype)


def matmul(x, w, out_dtype):
    t, kd = x.shape
    tm = _row_tile(t, (512, 256, 128, 64, 32, 16, 8))
    return pl.pallas_call(
        _matmul_kernel,
        grid=(t // tm,),
        in_specs=[pl.BlockSpec((tm, kd), lambda i: (i, 0)), _full(w.shape)],
        out_specs=pl.BlockSpec((tm, w.shape[1]), lambda i: (i, 0)),
        out_shape=jax.ShapeDtypeStruct((t, w.shape[1]), out_dtype),
        compiler_params=_cp(("parallel",)),
        name="matmul",
    )(x, w)


def _odd_out_kernel(x_ref, yc_ref, od_ref, woc_ref, wod_ref, ln_ref, rw_ref, rb_ref, *rest):
    x1_ref, hn_ref, idx_ref, gate_ref = rest[-4:]
    x1 = x_ref[...] + (_mm(yc_ref[...].astype(BF16), woc_ref[...]) + _mm(od_ref[...], wod_ref[...]))
    x1_ref[...] = x1
    hn = _rms(x1, ln_ref[...])
    hn_ref[...] = hn.astype(BF16)
    logits = _mm(hn.astype(BF16), rw_ref[...]) + rb_ref[...]
    lane = lax.broadcasted_iota(I32, logits.shape, 1)
    logits = jnp.where(lane < N_EXPERTS, logits, -jnp.inf)
    m1 = jnp.max(logits, axis=-1, keepdims=True)
    i1 = jnp.min(jnp.where(logits == m1, lane, LANES), axis=-1, keepdims=True)
    l2 = jnp.where(lane == i1, -jnp.inf, logits)
    m2 = jnp.max(l2, axis=-1, keepdims=True)
    i2 = jnp.min(jnp.where(l2 == m2, lane, LANES), axis=-1, keepdims=True)
    e2 = jnp.exp(m2 - m1)
    den = 1.0 + e2
    idx_ref[...] = jnp.where(lane == 0, i1, jnp.where(lane == 1, i2, 0))
    gate_ref[...] = jnp.where(lane == 0, 1.0 / den, jnp.where(lane == 1, e2 / den, 0.0))


def odd_out(x, yc, od, woc, wod, ln, rw, rb, hn_rows, hn_row0, hn_buf=None):
    t, d = x.shape
    tm = _row_tile(t, (512, 256, 128, 64, 32, 16, 8))
    assert hn_row0 % tm == 0
    off = hn_row0 // tm
    row = lambda w: pl.BlockSpec((tm, w), lambda i: (i, 0))
    args = (x, yc, od, woc, wod, ln, rw, rb) + (() if hn_buf is None else (hn_buf,))
    return pl.pallas_call(
        _odd_out_kernel,
        grid=(t // tm,),
        in_specs=[row(d), row(yc.shape[1]), row(od.shape[1]), _full(woc.shape), _full(wod.shape), _full((1, d)),
                  _full(rw.shape), _full(rb.shape)] + ([] if hn_buf is None else [pl.BlockSpec(memory_space=pl.ANY)]),
        out_specs=[row(d), pl.BlockSpec((tm, d), lambda i: (i + off, 0)), row(LANES), row(LANES)],
        out_shape=[jax.ShapeDtypeStruct((t, d), F32), jax.ShapeDtypeStruct((hn_rows, d), BF16),
                   jax.ShapeDtypeStruct((t, LANES), I32), jax.ShapeDtypeStruct((t, LANES), F32)],
        input_output_aliases={} if hn_buf is None else {8: 1},
        compiler_params=_cp(("parallel",)),
        name="odd_out",
    )(*args)


EXPERT_ROWS = 512
EXPERT_FF_CHUNK = 512
MOE_SUB = 128
MOE_PAIRS_PER_STEP = 4
F_VALID, F_FIRST, F_LAST, F_EMPTY = 1, 2, 4, 8


def _moe_expert_kernel(sb_ref, fl_ref, pj_ref, lo_ref, hi_ref, be_ref, *refs):
    del pj_ref, be_ref
    nu = MOE_PAIRS_PER_STEP
    tiles = [refs[3 * u:3 * u + 3] for u in range(nu)]
    wg_ref, wu_ref, wd_ref, o_ref, xacc, gacc = refs[3 * nu:]
    s = pl.program_id(0)
    fl = fl_ref[s]
    er = xacc.shape[0]

    @pl.when((fl & F_VALID) != 0)
    def _():
        @pl.when((fl & F_FIRST) != 0)
        def _():
            xacc[...] = jnp.zeros_like(xacc)
            gacc[...] = jnp.zeros_like(gacc)

        for u, (slot_ref, gate_ref, hn_ref) in enumerate(tiles):
            tile = hn_ref.shape[0]
            for sb in range(er // MOE_SUB):
                @pl.when(jnp.logical_and(lo_ref[s * nu + u] <= sb, sb <= hi_ref[s * nu + u]))
                def _(sb=sb, slot_ref=slot_ref, gate_ref=gate_ref, hn_ref=hn_ref, tile=tile):
                    rs = slice(sb * MOE_SUB, (sb + 1) * MOE_SUB)
                    srow = sb_ref[s] * er + sb * MOE_SUB + lax.broadcasted_iota(I32, (MOE_SUB, tile), 0)
                    m0 = slot_ref[0:1, :] == srow
                    m1 = slot_ref[1:2, :] == srow
                    sel = jnp.where(m0, 1.0, jnp.where(m1, 1.0, 0.0)).astype(BF16)
                    xacc[rs, :] += _mm(sel, hn_ref[...])
                    g = jnp.where(m0, gate_ref[0:1, :], jnp.where(m1, gate_ref[1:2, :], 0.0))
                    gacc[rs, :] += jnp.sum(g, axis=1, keepdims=True)

        @pl.when(jnp.logical_and((fl & F_LAST) != 0, (fl & F_EMPTY) == 0))
        def _():
            x = xacc[...].astype(BF16)
            acc = None
            for f in range(0, wg_ref.shape[2], EXPERT_FF_CHUNK):
                gg = _mm(x, wg_ref[0, :, f:f + EXPERT_FF_CHUNK])
                uu = _mm(x, wu_ref[0, :, f:f + EXPERT_FF_CHUNK])
                y = _mm((_silu(gg) * uu).astype(BF16), wd_ref[0, f:f + EXPERT_FF_CHUNK, :])
                acc = y if acc is None else acc + y
            o_ref[...] = (acc * gacc[...]).astype(o_ref.dtype)

        @pl.when((fl & F_EMPTY) != 0)
        def _():
            o_ref[...] = jnp.zeros_like(o_ref)


def moe_experts(plan, slot_t, gate_t, hn, wg, wu, wd, tile):
    sb, fl, pj, lo, hi, blk_e, n_blocks = plan
    d = hn.shape[1]
    ff = wg.shape[2]
    nu = MOE_PAIRS_PER_STEP
    wmap = lambda s, sb, fl, pj, lo, hi, be: (be[sb[s]], 0, 0)
    tile_specs, tile_args = [], []
    for u in range(nu):
        tmap = lambda s, sb, fl, pj, lo, hi, be, u=u: (0, pj[s * nu + u])
        hmap = lambda s, sb, fl, pj, lo, hi, be, u=u: (pj[s * nu + u], 0)
        tile_specs += [pl.BlockSpec((TOP_K, tile), tmap), pl.BlockSpec((TOP_K, tile), tmap),
                       pl.BlockSpec((tile, d), hmap)]
        tile_args += [slot_t, gate_t, hn]
    return pl.pallas_call(
        _moe_expert_kernel,
        grid_spec=pltpu.PrefetchScalarGridSpec(
            num_scalar_prefetch=6,
            grid=(sb.shape[0],),
            in_specs=tile_specs + [pl.BlockSpec((1, d, ff), wmap), pl.BlockSpec((1, d, ff), wmap),
                                   pl.BlockSpec((1, ff, d), wmap)],
            out_specs=pl.BlockSpec((EXPERT_ROWS, d), lambda s, sb, fl, pj, lo, hi, be: (sb[s], 0)),
            scratch_shapes=[pltpu.VMEM((EXPERT_ROWS, d), F32), pltpu.VMEM((EXPERT_ROWS, 1), F32)],
        ),
        out_shape=jax.ShapeDtypeStruct((n_blocks * EXPERT_ROWS, d), BF16),
        compiler_params=_cp(("arbitrary",)),
        name="moe_experts",
    )(sb, fl, pj, lo, hi, blk_e, *tile_args, wg, wu, wd)


def _moe_combine_kernel(sj_ref, fl_ref, qb_ref, lo_ref, hi_ref, slot_ref, x1_ref, *refs):
    del sj_ref
    nu = MOE_PAIRS_PER_STEP
    yb_refs = refs[:nu]
    fn_ref, o_ref, yacc = refs[nu:]
    s = pl.program_id(0)
    fl = fl_ref[s]
    tile = x1_ref.shape[0]
    er = yb_refs[0].shape[0]

    @pl.when((fl & F_VALID) != 0)
    def _():
        @pl.when((fl & F_FIRST) != 0)
        def _():
            yacc[...] = jnp.zeros_like(yacc)

        for u, yb_ref in enumerate(yb_refs):
            for sb in range(er // MOE_SUB):
                @pl.when(jnp.logical_and(lo_ref[s * nu + u] <= sb, sb <= hi_ref[s * nu + u]))
                def _(sb=sb, u=u, yb_ref=yb_ref):
                    scol = qb_ref[s * nu + u] * er + sb * MOE_SUB + lax.broadcasted_iota(I32, (tile, MOE_SUB), 1)
                    sl = slot_ref[...]
                    sel = jnp.where(sl[:, 0:1] == scol, 1.0, jnp.where(sl[:, 1:2] == scol, 1.0, 0.0)).astype(BF16)
                    yacc[...] += _mm(sel, yb_ref[sb * MOE_SUB:(sb + 1) * MOE_SUB, :])

        @pl.when((fl & F_LAST) != 0)
        def _():
            o_ref[...] = _rms(x1_ref[...] + yacc[...], fn_ref[...])


def moe_combine(sched, slot_cols, x1, yb, fnorm, tile):
    sj, fl, qb, lo, hi = sched
    t, d = x1.shape
    nu = MOE_PAIRS_PER_STEP
    tmap = lambda s, sj, fl, qb, lo, hi: (sj[s], 0)
    yb_specs = [pl.BlockSpec((EXPERT_ROWS, d), lambda s, sj, fl, qb, lo, hi, u=u: (qb[s * nu + u], 0))
                for u in range(nu)]
    return pl.pallas_call(
        _moe_combine_kernel,
        grid_spec=pltpu.PrefetchScalarGridSpec(
            num_scalar_prefetch=5,
            grid=(sj.shape[0],),
            in_specs=[pl.BlockSpec((tile, LANES), tmap), pl.BlockSpec((tile, d), tmap)] + yb_specs
            + [pl.BlockSpec((1, d), lambda s, sj, fl, qb, lo, hi: (0, 0))],
            out_specs=pl.BlockSpec((tile, d), tmap),
            scratch_shapes=[pltpu.VMEM((tile, d), F32)],
        ),
        out_shape=jax.ShapeDtypeStruct((t, d), F32),
        compiler_params=_cp(("arbitrary",)),
        name="moe_combine",
    )(sj, fl, qb, lo, hi, slot_cols, x1, *([yb] * nu), fnorm)


def _ragged_steps(counts, pmax):
    cum = jnp.cumsum(counts)
    total = cum[-1]
    ar = jnp.arange(pmax, dtype=I32)
    p = jnp.minimum(ar, jnp.maximum(total - 1, 0))
    row = jnp.minimum(jnp.sum((cum[None, :] <= p[:, None]).astype(I32), axis=1), counts.shape[0] - 1)
    off = p - (cum[row] - counts[row])
    return row, off, ar < total, total


def _group_flags(gid, valid, total):
    pmax = gid.shape[0]
    ar = jnp.arange(pmax, dtype=I32)
    prev = jnp.concatenate([gid[:1] - 1, gid[:-1]])
    nxt = jnp.concatenate([gid[1:], gid[-1:] + 1])
    first = jnp.logical_or(ar == 0, gid != prev)
    last = jnp.logical_or(ar == total - 1, gid != nxt)
    fl = F_VALID + F_FIRST * first.astype(I32) + F_LAST * last.astype(I32)
    return jnp.where(valid, fl, 0).astype(I32)


def _grouped_steps(cnt, smax):
    nu = MOE_PAIRS_PER_STEP
    row, off, valid, total = _ragged_steps((cnt + nu - 1) // nu, smax)
    k = off[:, None] * nu + jnp.arange(nu, dtype=I32)[None, :]
    c = cnt[row][:, None]
    pvalid = jnp.logical_and(valid[:, None], k < c)
    pidx = (jnp.cumsum(cnt) - cnt)[row][:, None] + jnp.minimum(k, jnp.maximum(c - 1, 0))
    return row, _group_flags(row, valid, total), pidx.reshape(-1), pvalid.reshape(-1)


def _moe_plan(e_all, tile):
    t = e_all.shape[0]
    nt = t // tile
    er = EXPERT_ROWS
    ex = jnp.arange(N_EXPERTS, dtype=I32)[None, :]
    oh = jnp.logical_or(e_all[:, 0:1] == ex, e_all[:, 1:2] == ex).astype(I32)
    cs = jnp.cumsum(oh, axis=0)
    rank = cs - oh
    counts = cs[-1]
    padded = (counts + er - 1) // er * er
    pad_end = jnp.cumsum(padded)
    pad_start = pad_end - padded
    slot_e = pad_start[None, :] + rank
    slot = jnp.stack([jnp.sum(jnp.where(e_all[:, k:k + 1] == ex, slot_e, 0), axis=1) for k in range(TOP_K)],
                     axis=1).astype(I32)
    n_blocks = -(-(t * TOP_K) // er) + N_EXPERTS
    blk = jnp.arange(n_blocks, dtype=I32)
    blk_e = jnp.minimum(jnp.sum((blk[:, None] * er >= pad_end[None, :]).astype(I32), axis=1), N_EXPERTS - 1)
    n_used = pad_end[-1] // er
    tstart = jnp.concatenate([rank[::tile], counts[None, :]], axis=0)
    first_slot = pad_start[None, :] + tstart[:-1]
    last_slot = pad_start[None, :] + tstart[1:] - 1

    def sub_range(b, fs, ls):
        lo = jnp.maximum(fs - b * er, 0) // MOE_SUB
        hi = jnp.where(ls >= fs, jnp.minimum(ls - b * er, er - 1), -1) // MOE_SUB
        return lo.astype(I32), jnp.maximum(hi, -1).astype(I32)

    be_oh = (blk_e[:, None] == ex).astype(I32)
    r0 = blk * er - jnp.sum(be_oh * pad_start[None, :], axis=1)
    r1 = jnp.minimum(r0 + er - 1, jnp.sum(be_oh * counts[None, :], axis=1) - 1)
    ts_b = jnp.sum(be_oh[:, None, :] * tstart[None, 1:, :], axis=2)
    jlo = jnp.minimum(jnp.sum((ts_b <= r0[:, None]).astype(I32), axis=1), nt - 1)
    jhi = jnp.minimum(jnp.sum((ts_b <= r1[:, None]).astype(I32), axis=1), nt - 1)
    cnt_b = jnp.where(blk < n_used, jhi - jlo + 1, 1)
    nu = MOE_PAIRS_PER_STEP
    pmax = n_blocks + nt * N_EXPERTS
    row, off, _, _ = _ragged_steps(cnt_b, pmax)
    pj = (jlo[row] + off).astype(I32)
    je = pj * N_EXPERTS + blk_e[row]
    lo, hi = sub_range(row, first_slot.reshape(-1)[je], last_slot.reshape(-1)[je])
    sb, sfl, pidx, pvalid = _grouped_steps(cnt_b, n_blocks + -(-pmax // nu))
    sfl = jnp.where(jnp.logical_and(sfl != 0, sb >= n_used), sfl + F_EMPTY, sfl)
    pvalid = jnp.logical_and(pvalid, jnp.repeat(sb < n_used, nu))
    expert_plan = (sb, sfl, pj[pidx], jnp.where(pvalid, lo[pidx], 1), jnp.where(pvalid, hi[pidx], 0),
                   blk_e.astype(I32), n_blocks)

    def combine_sched(j0, j1):
        ntr = j1 - j0
        fs = first_slot[j0:j1].reshape(-1)
        ls = last_slot[j0:j1].reshape(-1)
        blo = fs // er
        nb = jnp.where(ls >= fs, ls // er - blo + 1, 0)
        pmax_c = min(n_blocks + ntr * N_EXPERTS, 2 * ntr * N_EXPERTS)
        r, o, _, _ = _ragged_steps(nb, pmax_c)
        qb = (blo[r] + o).astype(I32)
        lo_c, hi_c = sub_range(qb, fs[r], ls[r])
        cnt_j = jnp.sum(nb.reshape(ntr, N_EXPERTS), axis=1)
        sj, sfl_c, pidx_c, pvalid_c = _grouped_steps(cnt_j, ntr + -(-pmax_c // nu))
        return (sj, sfl_c, qb[pidx_c], jnp.where(pvalid_c, lo_c[pidx_c], 1), jnp.where(pvalid_c, hi_c[pidx_c], 0))

    return slot, expert_plan, combine_sched


def _rope_tables(pos):
    half = MLA_ROPE // 2
    inv = ROPE_THETA ** (-jnp.arange(half, dtype=F32) * 2.0 / MLA_ROPE)
    ang = pos.astype(F32)[:, None] * inv[None, :]
    cos = jnp.cos(ang)
    sin = jnp.sin(ang)
    n = pos.shape[0]
    cos_t = jnp.concatenate([cos, cos, jnp.ones((n, LANES - MLA_ROPE), F32)], axis=1)
    sin_t = jnp.concatenate([-sin, sin, jnp.zeros((n, LANES - MLA_ROPE), F32)], axis=1)
    return cos_t, sin_t


def _block_diag(w):
    b, i, j = w.shape
    eye = jnp.eye(b, dtype=w.dtype)
    return (eye[:, None, :, None] * w[:, :, None, :]).reshape(b * i, b * j)


def _pad_rows(a, rows):
    return jnp.pad(a, ((0, 0), (0, rows - a.shape[1]), (0, 0)))


def kernel(x_prompt, x_sample, state_swa_kv, state_gdn_conv, state_gdn_s, state_lru_conv, state_lru_h, cache_mla, page_table, e_ln_mix, e_w_in, e_gdn_conv_w, e_gdn_a_log, e_gdn_dt_bias, e_gdn_norm, e_swa_sinks, e_w_out, e_ln_ffn, e_ffn_gate, e_ffn_up, e_ffn_down, o_ln_mix, o_w_in, o_lru_conv_w, o_lru_conv_b, o_lru_w_a, o_lru_b_a, o_lru_w_x, o_lru_b_x, o_lru_lambda, o_mla_q_norm, o_mla_w_uq, o_mla_kv_norm, o_mla_w_uk, o_mla_w_uv, o_w_out, o_ln_ffn, o_router_w, o_router_b, o_exp_gate, o_exp_up, o_exp_down, final_norm):
    nb, lp, d = x_prompt.shape
    ns, ls, _ = x_sample.shape
    past_len = page_table.shape[1] * PAGE_SIZE
    tp, ts = nb * lp, ns * ls
    xp = x_prompt.reshape(tp, d)
    xs = x_sample.reshape(ts, d)
    row1 = lambda v: v.reshape(1, -1)

    na = (SWA_HEADS + 2 * SWA_KV_HEADS) * SWA_HEAD_DIM
    nz = GDN_HEADS * GDN_DV
    w_in = e_w_in[0].astype(BF16)
    o1, o2, o3 = na, na + GDN_QKV, na + GDN_QKV + nz
    w_groups = [w_in[:, :o1], w_in[:, o1:o2], w_in[:, o2:o3],
                jnp.pad(w_in[:, o3:], ((0, 0), (0, LANES - 2 * GDN_HEADS)))]
    ln = row1(e_ln_mix[0])
    qkv_p, gx_p, z_p, bg_p = norm_proj(xp, ln, w_groups)
    qkv_s, gx_s, z_s, bg_s = norm_proj(xs, ln, w_groups)

    sinks = e_swa_sinks[0]
    oa_p = swa_prompt(qkv_p, sinks, nb, lp)
    wbuf = state_swa_kv.shape[2]
    nkv = 2 * SWA_KV_HEADS * SWA_HEAD_DIM
    buf = state_swa_kv[0].reshape(ns, wbuf, nkv)
    qkv_s3 = qkv_s.reshape(ns, ls, na)
    oa_s = swa_sample(_pad_rows(qkv_s3, SUBLANES), buf, sinks, ls)[:, :ls].reshape(ts, -1)
    kv_p = qkv_p.reshape(nb, lp, na)[:, lp - SWA_WINDOW:, na - nkv:]
    swa_kv_p = kv_p.reshape(1, nb, SWA_WINDOW, 2, SWA_KV_HEADS, SWA_HEAD_DIM)
    kv_s = jnp.concatenate([buf, qkv_s3[:, :, na - nkv:]], axis=1)[:, ls:]
    swa_kv_s = kv_s.reshape(1, ns, wbuf, 2, SWA_KV_HEADS, SWA_HEAD_DIM)

    gx_p3 = gx_p.reshape(nb, lp, GDN_QKV)
    gx_s3 = gx_s.reshape(ns, ls, GDN_QKV)
    gargs = (e_gdn_conv_w[0], e_gdn_a_log[0], e_gdn_dt_bias[0], e_gdn_norm[0])
    ob_p, gs_p = gdn(gx_p3, z_p.reshape(nb, lp, nz), bg_p.reshape(nb, lp, LANES),
                     jnp.zeros((nb, SUBLANES, GDN_QKV), F32), jnp.zeros((nb, GDN_HEADS, GDN_DK, GDN_DV), F32),
                     *gargs, chunk=math.gcd(lp, GDN_CHUNK), valid=math.gcd(lp, GDN_CHUNK), tl=min(lp, 512))
    cprev_s = jnp.pad(state_gdn_conv[0], ((0, 0), (SUBLANES - (GDN_CONV - 1), 0), (0, 0)))
    assert ls <= GDN_CHUNK and math.gcd(ls, GDN_CHUNK) == ls
    ob_s, gs_s = gdn(_pad_rows(gx_s3, GDN_CHUNK), _pad_rows(z_s.reshape(ns, ls, nz), GDN_CHUNK),
                     _pad_rows(bg_s.reshape(ns, ls, LANES), GDN_CHUNK), cprev_s, state_gdn_s[0],
                     *gargs, chunk=GDN_CHUNK, valid=ls, tl=GDN_CHUNK)
    ob_s = ob_s[:, :ls].reshape(ts, nz)
    gconv_p = gx_p3[:, lp - (GDN_CONV - 1):][None]
    gconv_s = gx_s3[:, ls - (GDN_CONV - 1):][None]

    w_out = e_w_out[0].astype(BF16)
    nqa = SWA_HEADS * SWA_HEAD_DIM
    ffn = (row1(e_ln_ffn[0]), e_ffn_gate[0].astype(BF16), e_ffn_up[0].astype(BF16), e_ffn_down[0].astype(BF16))
    xp = even_out(xp, oa_p, ob_p.reshape(tp, nz), w_out[:nqa], w_out[nqa:], *ffn)
    xs = even_out(xs, oa_s, ob_s, w_out[:nqa], w_out[nqa:], *ffn)

    w_in = o_w_in[0].astype(BF16)
    c0 = LRU_WIDTH
    c1 = 2 * LRU_WIDTH
    c2 = c1 + MLA_Q_RANK
    c3 = c2 + MLA_KV_RANK
    w_groups = [w_in[:, :c0], w_in[:, c0:c1], w_in[:, c1:c2], w_in[:, c2:c3],
                jnp.pad(w_in[:, c3:], ((0, 0), (0, LANES - MLA_ROPE)))]
    hd_q = MLA_NOPE + MLA_ROPE
    half = MLA_ROPE // 2
    wq = o_mla_w_uq[0].reshape(MLA_Q_RANK, MLA_HEADS, hd_q)
    wuq = jnp.concatenate([wq[:, :, MLA_NOPE:], wq[:, :, :MLA_NOPE],
                           jnp.zeros((MLA_Q_RANK, MLA_HEADS, LANES - hd_q), F32)], axis=2)
    wuq = wuq.reshape(MLA_Q_RANK, MLA_HEADS * LANES).astype(BF16)
    w_uk = o_mla_w_uk[0]
    w_uv = o_mla_w_uv[0]
    wuk = jnp.pad(w_uk, ((0, 0), (0, 0), (MLA_ROPE, LANES - hd_q))).reshape(MLA_KV_RANK, -1).astype(BF16)
    wuv = jnp.pad(w_uv, ((0, 0), (0, 0), (0, LANES - MLA_V))).reshape(MLA_KV_RANK, -1).astype(BF16)
    wabs = _block_diag(jnp.pad(jnp.transpose(w_uk, (1, 2, 0)), ((0, 0), (MLA_ROPE, LANES - hd_q), (0, 0)))).astype(BF16)
    wuv_bd = _block_diag(jnp.pad(jnp.transpose(w_uv, (1, 0, 2)), ((0, 0), (0, 0), (0, LANES - MLA_V)))).astype(BF16)
    ln = row1(o_ln_mix[0])
    qn, kvn = row1(o_mla_q_norm[0]), row1(o_mla_kv_norm[0])
    tm_p = _row_tile(tp, (512, 256, 128, 64, 32, 16, 8))
    assert lp % tm_p == 0 or tm_p % lp == 0
    pos_p = jnp.arange(max(lp, tm_p), dtype=I32) % lp
    pos_s = past_len + (jnp.arange(ts, dtype=I32) % ls)
    cos_p, sin_p = _rope_tables(pos_p)
    cos_s, sin_s = _rope_tables(pos_s)
    xb_p, gb_p, q_p, k_p, v_p, rows_p = odd_in(xp, ln, w_groups, qn, kvn, wuq, cos_p, sin_p, [wuk, wuv], False)
    xb_s, gb_s, q_s, qlat_s, rows_s = odd_in(xs, ln, w_groups, qn, kvn, wuq, cos_s, sin_s, [wabs], True)

    lru_w = (o_lru_conv_w[0], row1(o_lru_conv_b[0]), _block_diag(o_lru_w_a[0]).astype(BF16),
             _block_diag(o_lru_w_x[0]).astype(BF16), row1(o_lru_b_a[0]), row1(o_lru_b_x[0]), row1(o_lru_lambda[0]))
    xb_p3 = xb_p.reshape(nb, lp, LRU_WIDTH)
    yc_p, lh_p = lru_prompt(xb_p3, gb_p.reshape(nb, lp, LRU_WIDTH), jnp.zeros((nb, SUBLANES, LRU_WIDTH), F32),
                            jnp.zeros((nb, 1, LRU_WIDTH), F32), *lru_w, tl=min(lp, 256))
    xb_s3 = xb_s.reshape(ns, ls, LRU_WIDTH)
    tmaj = lambda a: jnp.transpose(a, (1, 0, 2))
    yc_s, lh_s = lru_sample(tmaj(xb_s3), tmaj(gb_s.reshape(ns, ls, LRU_WIDTH)), tmaj(state_lru_conv[0]),
                            state_lru_h[0], *lru_w)
    yc_s = tmaj(yc_s).reshape(ts, LRU_WIDTH)
    lconv_p = xb_p3[:, lp - 3:][None]
    lconv_s = xb_s3[:, ls - 3:][None]

    od_p = mla_prompt(q_p, k_p, v_p, nb, lp, tq=min(lp, 512))
    q_pe = q_s.reshape(ts, MLA_HEADS, LANES)[:, :, :MLA_ROPE]
    q_cat = jnp.concatenate([qlat_s.reshape(ts, MLA_HEADS, MLA_KV_RANK).astype(BF16), q_pe], axis=-1)
    q_cat = q_cat.reshape(ns, ls * MLA_HEADS, MLA_ROW)
    n_pages = page_table.shape[1]
    cache_t = jnp.swapaxes(cache_mla.reshape(cache_mla.shape[1:]), 1, 2)
    o_lat = mla_sample(page_table, q_cat, rows_s.reshape(ns, ls, MLA_ROW), cache_t,
                       n_pages_step=math.gcd(n_pages, MLA_SAMPLE_PAGES))
    o_lat = o_lat.reshape(ts, MLA_HEADS, MLA_ROW)[:, :, :MLA_KV_RANK].reshape(ts, MLA_HEADS * MLA_KV_RANK)
    od_s = matmul(o_lat, wuv_bd, BF16)

    w_out = o_w_out[0].astype(BF16)
    wod = jnp.pad(w_out[LRU_WIDTH:].reshape(MLA_HEADS, MLA_V, d), ((0, 0), (0, LANES - MLA_V), (0, 0)))
    wod = wod.reshape(MLA_HEADS * LANES, d)
    rw = jnp.pad(o_router_w[0], ((0, 0), (0, LANES - N_EXPERTS))).astype(BF16)
    rb = jnp.pad(o_router_b[0], (0, LANES - N_EXPERTS)).reshape(1, LANES)
    lnf = row1(o_ln_ffn[0])
    tall = tp + ts
    x1_p, hn, idx_p, gate_p = odd_out(xp, yc_p.reshape(tp, LRU_WIDTH), od_p, w_out[:LRU_WIDTH], wod, lnf, rw, rb,
                                      tall, 0, jnp.zeros((tall, d), BF16))
    x1_s, hn, idx_s, gate_s = odd_out(xs, yc_s, od_s, w_out[:LRU_WIDTH], wod, lnf, rw, rb, tall, tp, hn)

    tile = math.gcd(math.gcd(tp, ts), EXPERT_ROWS)
    e_all = jnp.concatenate([idx_p[:, :TOP_K], idx_s[:, :TOP_K]], axis=0)
    g_all = jnp.concatenate([gate_p[:, :TOP_K], gate_s[:, :TOP_K]], axis=0)
    slot, expert_plan, combine_sched = _moe_plan(e_all, tile)
    yb = moe_experts(expert_plan, slot.T, g_all.T, hn, o_exp_gate[0].astype(BF16), o_exp_up[0].astype(BF16),
                     o_exp_down[0].astype(BF16), tile)
    slot_cols = jnp.pad(slot, ((0, 0), (0, LANES - TOP_K)))
    fnw = row1(final_norm)
    y_p = moe_combine(combine_sched(0, tp // tile), slot_cols[:tp], x1_p, yb, fnw, tile).reshape(nb, lp, d)
    y_s = moe_combine(combine_sched(tp // tile, tall // tile), slot_cols[tp:], x1_s, yb, fnw, tile).reshape(ns, ls, d)

    return (y_p, y_s, swa_kv_p, swa_kv_s, gconv_p, gconv_s, gs_p[None], gs_s[None],
            lconv_p, lconv_s, lh_p.reshape(1, nb, LRU_WIDTH), lh_s[None],
            rows_p.reshape(1, nb, lp // PAGE_SIZE, PAGE_SIZE, MLA_ROW), rows_s.reshape(1, ns, ls, MLA_ROW))
```

```python
import functools
import math

import jax
import jax.numpy as jnp
from jax import lax
from jax.experimental import pallas as pl
from jax.experimental.pallas import tpu as pltpu

F32 = jnp.float32
BF16 = jnp.bfloat16
I32 = jnp.int32
HI = lax.Precision.HIGHEST

D_MODEL = 1024
PAGE_SIZE = 128
SWA_WINDOW = 128
SWA_HEADS = 8
SWA_KV_HEADS = 2
SWA_GROUP = SWA_HEADS // SWA_KV_HEADS
SWA_HEAD_DIM = 64
GDN_HEADS = 4
GDN_DK = 128
GDN_DV = 128
GDN_CONV = 4
GDN_CHUNK = 64
GDN_QKV = GDN_HEADS * (2 * GDN_DK + GDN_DV)
LRU_WIDTH = 512
LRU_BLOCKS = 8
LRU_BLOCK_W = LRU_WIDTH // LRU_BLOCKS
LRU_C = 8.0
MLA_HEADS = 8
MLA_Q_RANK = 384
MLA_KV_RANK = 256
MLA_NOPE = 64
MLA_ROPE = 32
MLA_V = 64
MLA_ROW = MLA_KV_RANK + MLA_ROPE
ROPE_THETA = 10000.0
D_FF = 2816
N_EXPERTS = 8
TOP_K = 2
MOE_FF = 2048
NORM_EPS = 1e-6

LANES = 128
SUBLANES = 8
VMEM_LIMIT = 56 << 20

NN = (((1,), (0,)), ((), ()))
NT = (((1,), (1,)), ((), ()))
TN = (((0,), (0,)), ((), ()))


def _mm(a, b, dims=NN, precision=None):
    return lax.dot_general(a, b, dims, precision=precision, preferred_element_type=F32)


def _cp(sem):
    return pltpu.CompilerParams(dimension_semantics=sem, vmem_limit_bytes=VMEM_LIMIT)


def _rms(x, w):
    return x * lax.rsqrt(jnp.mean(x * x, axis=-1, keepdims=True) + NORM_EPS) * w


def _silu(x):
    return x * jax.nn.sigmoid(x)


def _full(shape):
    nd = len(shape)
    return pl.BlockSpec(shape, lambda *a: (0,) * nd)


def _row_tile(t, pref):
    for c in pref:
        if t % c == 0:
            return c
    return t


def _norm_proj_kernel(x_ref, ln_ref, *refs):
    n = len(refs) // 2
    h = _rms(x_ref[...], ln_ref[...]).astype(BF16)
    for w_ref, o_ref in zip(refs[:n], refs[n:]):
        o_ref[...] = _mm(h, w_ref[...])


def norm_proj(x, ln, ws):
    t, d = x.shape
    tm = _row_tile(t, (512, 256, 128, 64, 32, 16, 8))
    return pl.pallas_call(
        _norm_proj_kernel,
        grid=(t // tm,),
        in_specs=[pl.BlockSpec((tm, d), lambda i: (i, 0)), _full((1, d))] + [_full(w.shape) for w in ws],
        out_specs=[pl.BlockSpec((tm, w.shape[1]), lambda i: (i, 0)) for w in ws],
        out_shape=[jax.ShapeDtypeStruct((t, w.shape[1]), F32) for w in ws],
        compiler_params=_cp(("parallel",)),
        name="norm_proj",
    )(x, ln, *ws)


def _swa_softmax_pv(parts, sink):
    m = sink
    for s, _ in parts:
        m = jnp.maximum(m, jnp.max(s, axis=-1, keepdims=True))
    den = jnp.exp(sink - m)
    o = None
    for s, v in parts:
        p = jnp.exp(s - m)
        den = den + jnp.sum(p, axis=-1, keepdims=True)
        pv = _mm(p.astype(BF16), v)
        o = pv if o is None else o + pv
    return o / den


def _swa_prompt_kernel(sink_ref, q_ref, kvc_ref, kvp_ref, o_ref):
    b = pl.program_id(1)
    w = SWA_WINDOW
    hd = SWA_HEAD_DIM
    ng = SWA_GROUP
    q = q_ref[...]
    band = jnp.concatenate([kvp_ref[...], kvc_ref[...]], axis=0).astype(BF16)
    r = lax.broadcasted_iota(I32, (ng * w, 1), 0)
    grp = r // w
    qi = r % w
    t = lax.broadcasted_iota(I32, (ng * w, 2 * w), 1)
    dist = qi + w - t
    valid = jnp.logical_and(jnp.logical_and(dist >= 0, dist < w), jnp.logical_or(t >= w, b > 0))
    distf = dist.astype(F32)
    outs = [None] * SWA_HEADS
    scores = []
    for j in range(SWA_KV_HEADS):
        qs = jnp.concatenate([q[:, (j * ng + g) * hd:(j * ng + g + 1) * hd] for g in range(ng)], axis=0)
        scores.append(_mm(qs.astype(BF16), band[:, j * hd:(j + 1) * hd], NT) * (hd ** -0.5))
    for j in range(SWA_KV_HEADS):
        heads = [j * ng + g for g in range(ng)]
        slope = jnp.zeros((ng * w, 1), F32)
        sink = jnp.zeros((ng * w, 1), F32)
        for g, h in enumerate(heads):
            slope = jnp.where(grp == g, 2.0 ** (-8.0 * (h + 1) / SWA_HEADS), slope)
            sink = jnp.where(grp == g, sink_ref[h], sink)
        vb = band[:, (SWA_KV_HEADS + j) * hd:(SWA_KV_HEADS + j + 1) * hd]
        s = jnp.where(valid, scores[j] - slope * distf, -jnp.inf)
        o = _swa_softmax_pv([(s, vb)], sink)
        for g, h in enumerate(heads):
            outs[h] = o[g * w:(g + 1) * w, :]
    o_ref[...] = jnp.concatenate(outs, axis=1)


def swa_prompt(qkv, sinks, n, l):
    w = SWA_WINDOW
    nb = l // w
    nq = SWA_HEADS * SWA_HEAD_DIM
    nkv = 2 * SWA_KV_HEADS * SWA_HEAD_DIM
    return pl.pallas_call(
        _swa_prompt_kernel,
        grid=(n, nb),
        in_specs=[
            pl.BlockSpec(memory_space=pltpu.SMEM),
            pl.BlockSpec((w, nq), lambda i, b: (i * nb + b, 0)),
            pl.BlockSpec((w, nkv), lambda i, b: (i * nb + b, nq // nkv)),
            pl.BlockSpec((w, nkv), lambda i, b: (i * nb + jnp.maximum(b - 1, 0), nq // nkv)),
        ],
        out_specs=pl.BlockSpec((w, nq), lambda i, b: (i * nb + b, 0)),
        out_shape=jax.ShapeDtypeStruct((n * l, nq), F32),
        compiler_params=_cp(("parallel", "arbitrary")),
        name="swa_prompt",
    )(sinks, qkv, qkv, qkv)


SWA_SAMPLE_REQS = 4


def _swa_sample_kernel(sink_ref, q_ref, buf_ref, o_ref, *, l):
    hd = SWA_HEAD_DIM
    ng = SWA_GROUP
    nr, lp, _ = q_ref.shape
    wb = buf_ref.shape[1]
    r = lax.broadcasted_iota(I32, (ng * lp, 1), 0)
    grp = r // lp
    qi = r % lp
    dist_b = qi + wb - lax.broadcasted_iota(I32, (ng * lp, wb), 1)
    valid_b = dist_b < SWA_WINDOW
    tn = lax.broadcasted_iota(I32, (ng * lp, lp), 1)
    dist_n = qi - tn
    valid_n = jnp.logical_and(tn <= qi, tn < l)
    slopes, sinks = [], []
    for j in range(SWA_KV_HEADS):
        slope = jnp.zeros((ng * lp, 1), F32)
        sink = jnp.zeros((ng * lp, 1), F32)
        for g in range(ng):
            h = j * ng + g
            slope = jnp.where(grp == g, 2.0 ** (-8.0 * (h + 1) / SWA_HEADS), slope)
            sink = jnp.where(grp == g, sink_ref[h], sink)
        slopes.append(slope)
        sinks.append(sink)
    chains = []
    for rq in range(nr):
        x = q_ref[rq]
        buf = buf_ref[rq].astype(BF16)
        kvn = x[:, SWA_HEADS * hd:].astype(BF16)
        for j in range(SWA_KV_HEADS):
            qs = jnp.concatenate([x[:, (j * ng + g) * hd:(j * ng + g + 1) * hd] for g in range(ng)], axis=0)
            qs = qs.astype(BF16)
            ks = slice(j * hd, (j + 1) * hd)
            vs = slice((SWA_KV_HEADS + j) * hd, (SWA_KV_HEADS + j + 1) * hd)
            s_b = _mm(qs, buf[:, ks], NT) * (hd ** -0.5)
            s_n = _mm(qs, kvn[:, ks], NT) * (hd ** -0.5)
            chains.append((rq, j, s_b, s_n, buf[:, vs], kvn[:, vs]))
    outs = [[None] * SWA_HEADS for _ in range(nr)]
    for rq, j, s_b, s_n, vb, vn in chains:
        s_b = jnp.where(valid_b, s_b - slopes[j] * dist_b.astype(F32), -jnp.inf)
        s_n = jnp.where(valid_n, s_n - slopes[j] * dist_n.astype(F32), -jnp.inf)
        o = _swa_softmax_pv([(s_b, vb), (s_n, vn)], sinks[j])
        for g in range(ng):
            outs[rq][j * ng + g] = o[g * lp:(g + 1) * lp, :]
    for rq in range(nr):
        o_ref[rq] = jnp.concatenate(outs[rq], axis=1)


def swa_sample(qkv_pad, buf, sinks, l):
    n, lp, c = qkv_pad.shape
    wb = buf.shape[1]
    nq = SWA_HEADS * SWA_HEAD_DIM
    nr = math.gcd(n, SWA_SAMPLE_REQS)
    return pl.pallas_call(
        functools.partial(_swa_sample_kernel, l=l),
        grid=(n // nr,),
        in_specs=[
            pl.BlockSpec(memory_space=pltpu.SMEM),
            pl.BlockSpec((nr, lp, c), lambda i: (i, 0, 0)),
            pl.BlockSpec((nr, wb, buf.shape[2]), lambda i: (i, 0, 0)),
        ],
        out_specs=pl.BlockSpec((nr, lp, nq), lambda i: (i, 0, 0)),
        out_shape=jax.ShapeDtypeStruct((n, lp, nq), F32),
        compiler_params=_cp(("parallel",)),
        name="swa_sample",
    )(sinks, qkv_pad, buf)


GDN_SOLVE_GROUP = 8
GDN_SAMPLE_PAD = 16


def _mm3(a, b):
    ah = a.astype(BF16)
    al = (a - ah.astype(F32)).astype(BF16)
    bh = b.astype(BF16)
    bl = (b - bh.astype(F32)).astype(BF16)
    return _mm(jnp.concatenate([ah, ah, al], axis=1), jnp.concatenate([bh, bl, bh], axis=0))


def _block_rows(x, c, nh):
    blk = lax.broadcasted_iota(I32, x.shape, 1) // c
    return jnp.concatenate([jnp.where(blk == i, x, 0.0) for i in range(nh)], axis=0)


def _gdn_kernel(x_ref, z_ref, bg_ref, cprev_ref, s0_ref, cw_ref, alog_ref, dtb_ref, gn_ref,
                o_ref, sfin_ref, xbuf, s_sc, q_sc, k_sc, vb_sc, kb_sc, qd_sc, kd_sc, u_sc, w_sc, be_sc, gc_sc, egl_sc,
                qkd_sc, *, chunk, valid):
    t = pl.program_id(1)
    tl = x_ref.shape[1]
    c = chunk
    nck = tl // c
    nh = GDN_HEADS
    dk = GDN_DK

    @pl.when(t == 0)
    def _():
        xbuf[0:SUBLANES, :] = cprev_ref[0]
        s_sc[...] = s0_ref[0]

    xbuf[SUBLANES:SUBLANES + tl, :] = x_ref[0]
    cw = cw_ref[...]
    conv = xbuf[5:5 + tl, :] * cw[0:1, :]
    for j in range(1, GDN_CONV):
        conv = conv + xbuf[5 + j:5 + j + tl, :] * cw[j:j + 1, :]
    xbuf[0:SUBLANES, :] = xbuf[tl:tl + SUBLANES, :]
    act = _silu(conv)

    row = lax.broadcasted_iota(I32, (tl, 1), 0)
    rmask = (row % c) < valid
    bg = bg_ref[0]
    beta = jax.nn.sigmoid(bg)
    g = -jnp.exp(alog_ref[...]) * jax.nn.softplus(bg + dtb_ref[...])
    g = jnp.where(rmask, g, 0.0)
    rc = row % c
    s = 1
    while s < c:
        g = g + jnp.where(rc >= s, pltpu.roll(g, s, axis=0), 0.0)
        s *= 2
    glast = jnp.broadcast_to(g.reshape(nck, c, LANES)[:, c - 1:c, :], (nck, c, LANES)).reshape(tl, LANES)
    egc = jnp.exp(g)
    kfac = jnp.exp(glast - g)
    be_sc[...] = beta
    gc_sc[...] = g
    egl_sc[...] = jnp.exp(glast)
    for h in range(nh):
        hs = slice(h * dk, (h + 1) * dk)
        qh = act[:, h * dk:(h + 1) * dk]
        kh = act[:, (nh + h) * dk:(nh + h + 1) * dk]
        vh = act[:, (2 * nh + h) * dk:(2 * nh + h + 1) * dk]
        qh = qh * lax.rsqrt(jnp.sum(qh * qh, axis=-1, keepdims=True) + NORM_EPS) * (dk ** -0.5)
        kh = kh * lax.rsqrt(jnp.sum(kh * kh, axis=-1, keepdims=True) + NORM_EPS)
        qh = jnp.where(rmask, qh, 0.0)
        kh = jnp.where(rmask, kh, 0.0)
        vh = jnp.where(rmask, vh, 0.0)
        b_h = beta[:, h:h + 1]
        e_h = egc[:, nh + h:nh + h + 1]
        q_sc[:, hs] = qh
        k_sc[:, hs] = kh
        vb_sc[:, hs] = vh * b_h
        kb_sc[:, hs] = kh * (b_h * e_h)
        qd_sc[:, hs] = qh * e_h
        kd_sc[:, hs] = kh * kfac[:, nh + h:nh + h + 1]

    ii = lax.broadcasted_iota(I32, (c, nh * c), 0)
    jl = lax.broadcasted_iota(I32, (c, nh * c), 1) % c
    eye_cat = (ii == jl).astype(F32)
    gn = gn_ref[...]
    n_factors = max(1, int(math.ceil(math.log2(valid))))

    ng = math.gcd(nck, GDN_SOLVE_GROUP)

    def solve_body(gi, carry):
        rows_g = [pl.ds(pl.multiple_of((gi * ng + g) * c, c), c) for g in range(ng)]
        a_g, qkd_g = [], []
        for rows in rows_g:
            gcs = gc_sc[rows, :]
            bes = be_sc[rows, :]
            kk, qk, gexp, bexp = [], [], [], []
            for h in range(nh):
                hs = slice(h * dk, (h + 1) * dk)
                kb16 = k_sc[rows, hs].astype(BF16)
                kk.append(_mm(kb16, kb16, NT))
                qk.append(_mm(q_sc[rows, hs].astype(BF16), kb16, NT))
                gexp.append(jnp.broadcast_to(gcs[:, nh + h:nh + h + 1], (c, c)))
                bexp.append(jnp.broadcast_to(bes[:, h:h + 1], (c, c)))
            kk, qk, gexp, bexp = [jnp.concatenate(v, axis=1) for v in (kk, qk, gexp, bexp)]
            grow = jnp.sum(jnp.where(ii == jl, gexp, 0.0), axis=0, keepdims=True)
            decay = jnp.where(ii >= jl, jnp.exp(gexp - grow), 0.0)
            a_g.append(jnp.where(ii > jl, bexp * kk * decay, 0.0))
            qkd_g.append(qk * decay)
        t_g = [eye_cat - a for a in a_g]
        p_g = a_g
        for _ in range(n_factors - 1):
            p_g = [_mm3(p, _block_rows(p, c, nh)) for p in p_g]
            t_g = [_mm3(tv, _block_rows(eye_cat + p, c, nh)) for tv, p in zip(t_g, p_g)]
        for rows, tv, qkd in zip(rows_g, t_g, qkd_g):
            rhs = jnp.concatenate(
                [jnp.concatenate([vb_sc[rows, h * dk:(h + 1) * dk], kb_sc[rows, h * dk:(h + 1) * dk]], axis=1)
                 for h in range(nh)], axis=0)
            sol = _mm3(_block_rows(tv, c, nh), rhs)
            for h in range(nh):
                hs = slice(h * dk, (h + 1) * dk)
                u_sc[rows, hs] = sol[h * c:(h + 1) * c, :GDN_DV]
                w_sc[rows, hs] = sol[h * c:(h + 1) * c, GDN_DV:]
            qkd_sc[rows, :] = qkd
        return carry

    lax.fori_loop(0, nck // ng, solve_body, 0)

    def scan_body(ci, carry):
        rows = pl.ds(pl.multiple_of(ci * c, c), c)
        egl = egl_sc[rows, :]
        sts, v_news, o1s = [], [], []
        for h in range(nh):
            hs = slice(h * dk, (h + 1) * dk)
            st = s_sc[h]
            wq = jnp.concatenate([w_sc[rows, hs], qd_sc[rows, hs]], axis=0)
            r = _mm(wq.astype(BF16), st.astype(BF16))
            sts.append(st)
            v_news.append(u_sc[rows, hs] - r[:c])
            o1s.append(r[c:])
        o2 = _mm(_block_rows(qkd_sc[rows, :], c, nh).astype(BF16), jnp.concatenate(v_news, axis=0).astype(BF16))
        for h in range(nh):
            hs = slice(h * dk, (h + 1) * dk)
            upd = _mm(kd_sc[rows, hs].astype(BF16), v_news[h].astype(BF16), TN)
            s_sc[h] = sts[h] * egl[0:1, nh + h:nh + h + 1] + upd
            o = o1s[h] + o2[h * c:(h + 1) * c]
            o_ref[0, rows, hs] = _rms(o, gn) * _silu(z_ref[0, rows, hs])
        return carry

    lax.fori_loop(0, nck, scan_body, 0)

    @pl.when(t == pl.num_programs(1) - 1)
    def _():
        sfin_ref[0] = s_sc[...]


def gdn(x, z, bg, cprev, s0, conv_w, a_log, dt_bias, gnorm, chunk, valid, tl):
    n, lp, cq = x.shape
    nt = lp // tl
    nv = GDN_HEADS * GDN_DV
    alog = jnp.zeros((1, LANES), F32).at[0, GDN_HEADS:2 * GDN_HEADS].set(a_log)
    dtb = jnp.zeros((1, LANES), F32).at[0, GDN_HEADS:2 * GDN_HEADS].set(dt_bias)
    seq3 = lambda w: pl.BlockSpec((1, tl, w), lambda i, t: (i, t, 0))
    return pl.pallas_call(
        functools.partial(_gdn_kernel, chunk=chunk, valid=valid),
        grid=(n, nt),
        in_specs=[
            seq3(cq), seq3(nv), seq3(LANES),
            pl.BlockSpec((1, SUBLANES, cq), lambda i, t: (i, 0, 0)),
            pl.BlockSpec((1, GDN_HEADS, GDN_DK, GDN_DV), lambda i, t: (i, 0, 0, 0)),
            _full((GDN_CONV, cq)), _full((1, LANES)), _full((1, LANES)), _full((1, GDN_DV)),
        ],
        out_specs=[seq3(nv), pl.BlockSpec((1, GDN_HEADS, GDN_DK, GDN_DV), lambda i, t: (i, 0, 0, 0))],
        out_shape=[jax.ShapeDtypeStruct((n, lp, nv), F32),
                   jax.ShapeDtypeStruct((n, GDN_HEADS, GDN_DK, GDN_DV), F32)],
        scratch_shapes=[
            pltpu.VMEM((tl + SUBLANES, cq), F32),
            pltpu.VMEM((GDN_HEADS, GDN_DK, GDN_DV), F32),
        ] + [pltpu.VMEM((tl, nv), F32)] * 8 + [pltpu.VMEM((tl, LANES), F32)] * 3
        + [pltpu.VMEM((tl, GDN_HEADS * chunk), F32)],
        compiler_params=_cp(("parallel", "arbitrary")),
        name="gdn",
    )(x, z, bg, cprev, s0, conv_w, alog, dtb, gnorm.reshape(1, GDN_DV))


def _even_out_kernel(x_ref, oa_ref, ob_ref, woa_ref, wob_ref, ln_ref, wg_ref, wu_ref, wd_ref,
                     o_ref, x1_sc, h_sc, acc_sc):
    f = pl.program_id(1)

    @pl.when(f == 0)
    def _():
        x1 = x_ref[...] + (_mm(oa_ref[...].astype(BF16), woa_ref[...]) + _mm(ob_ref[...].astype(BF16), wob_ref[...]))
        x1_sc[...] = x1
        h_sc[...] = _rms(x1, ln_ref[...]).astype(BF16)
        acc_sc[...] = jnp.zeros_like(acc_sc)

    h = h_sc[...]
    act = (_silu(_mm(h, wg_ref[...])) * _mm(h, wu_ref[...])).astype(BF16)
    acc_sc[...] += _mm(act, wd_ref[...])

    @pl.when(f == pl.num_programs(1) - 1)
    def _():
        o_ref[...] = x1_sc[...] + acc_sc[...]


def even_out(x, oa, ob, woa, wob, ln, wg, wu, wd):
    t, d = x.shape
    ff = wg.shape[1]
    tm = _row_tile(t, (512, 256, 128, 64, 32, 16, 8))
    tf = next(c for c in (1408, 256, LANES) if ff % c == 0)
    row = lambda w: pl.BlockSpec((tm, w), lambda i, f: (i, 0))
    return pl.pallas_call(
        _even_out_kernel,
        grid=(t // tm, ff // tf),
        in_specs=[row(d), row(oa.shape[1]), row(ob.shape[1]), _full(woa.shape), _full(wob.shape), _full((1, d)),
                  pl.BlockSpec((d, tf), lambda i, f: (0, f)), pl.BlockSpec((d, tf), lambda i, f: (0, f)),
                  pl.BlockSpec((tf, d), lambda i, f: (f, 0))],
        out_specs=row(d),
        out_shape=jax.ShapeDtypeStruct((t, d), F32),
        scratch_shapes=[pltpu.VMEM((tm, d), F32), pltpu.VMEM((tm, d), BF16), pltpu.VMEM((tm, d), F32)],
        compiler_params=_cp(("parallel", "arbitrary")),
        name="even_out",
    )(x, oa, ob, woa, wob, ln, wg, wu, wd)


def _rope_lanes(v, c, s):
    n = v.shape[1]
    lane = lax.broadcasted_iota(I32, (1, n), 1) % LANES
    sw = jnp.where(lane < MLA_ROPE // 2, pltpu.roll(v, n - MLA_ROPE // 2, axis=1), pltpu.roll(v, MLA_ROPE // 2, axis=1))
    return v * c + sw * s


def _odd_in_kernel(x_ref, ln_ref, wxb_ref, wgb_ref, wcq_ref, wckv_ref, wkpe_ref, qn_ref, kvn_ref, wuq_ref,
                   cos_ref, sin_ref, *rest, absorbed):
    if absorbed:
        wabs_ref, xb_ref, gb_ref, q_ref, qlat_ref, rows_ref = rest
    else:
        wuk_ref, wuv_ref, xb_ref, gb_ref, q_ref, k_ref, v_ref, rows_ref = rest
    h = _rms(x_ref[...], ln_ref[...]).astype(BF16)
    xb_ref[...] = _mm(h, wxb_ref[...])
    gb_ref[...] = _mm(h, wgb_ref[...])
    cq = _mm(h, wcq_ref[...])
    ckv = _mm(h, wckv_ref[...])
    kpe = _mm(h, wkpe_ref[...])
    cqn = _rms(cq, qn_ref[...]).astype(BF16)
    ckvn = _rms(ckv, kvn_ref[...])
    c128 = cos_ref[...]
    s128 = sin_ref[...]
    nh = MLA_HEADS
    q = _rope_lanes(_mm(cqn, wuq_ref[...]), jnp.concatenate([c128] * nh, axis=1), jnp.concatenate([s128] * nh, axis=1))
    kpe_r = _rope_lanes(kpe, c128, s128)
    qb = q.astype(BF16)
    q_ref[...] = qb
    rows_ref[:, 0:MLA_KV_RANK] = ckvn
    rows_ref[:, MLA_KV_RANK:MLA_ROW] = kpe_r[:, 0:MLA_ROPE]
    if absorbed:
        qlat_ref[...] = _mm(qb, wabs_ref[...])
    else:
        ckvb = ckvn.astype(BF16)
        k_ref[...] = (_mm(ckvb, wuk_ref[...]) + jnp.concatenate([kpe_r] * nh, axis=1)).astype(BF16)
        v_ref[...] = _mm(ckvb, wuv_ref[...]).astype(BF16)


def odd_in(x, ln, ws, qn, kvn, wuq, cos_t, sin_t, extra, absorbed):
    t, d = x.shape
    tm = _row_tile(t, (512, 256, 128, 64, 32, 16, 8))
    nblk = cos_t.shape[0] // tm
    hb = MLA_HEADS * LANES
    row = lambda w: pl.BlockSpec((tm, w), lambda i: (i, 0))
    tbl = pl.BlockSpec((tm, LANES), lambda i: (i % nblk, 0))
    if absorbed:
        outs = [(LRU_WIDTH, F32), (LRU_WIDTH, F32), (hb, BF16), (extra[0].shape[1], F32), (MLA_ROW, F32)]
    else:
        outs = [(LRU_WIDTH, F32), (LRU_WIDTH, F32), (hb, BF16), (hb, BF16), (hb, BF16), (MLA_ROW, F32)]
    return pl.pallas_call(
        functools.partial(_odd_in_kernel, absorbed=absorbed),
        grid=(t // tm,),
        in_specs=[row(d), _full((1, d))] + [_full(w.shape) for w in ws]
        + [_full(qn.shape), _full(kvn.shape), _full(wuq.shape), tbl, tbl] + [_full(w.shape) for w in extra],
        out_specs=[row(w) for w, _ in outs],
        out_shape=[jax.ShapeDtypeStruct((t, w), dt) for w, dt in outs],
        compiler_params=_cp(("parallel",)),
        name="odd_in",
    )(x, ln, *ws, qn, kvn, wuq, cos_t, sin_t, *extra)


def _expm1(x):
    u = jnp.exp(x)
    um1 = u - 1.0
    small = um1 * x / jnp.log(u)
    return jnp.where(um1 == 0.0, x, jnp.where(jnp.abs(x) < 0.5, small, um1))


def _lru_gates(xc, wa_ref, wx_ref, ba_ref, bx_ref, lam_ref):
    xcb = xc.astype(BF16)
    r = jax.nn.sigmoid(_mm(xcb, wa_ref[...]) + ba_ref[...])
    i = jax.nn.sigmoid(_mm(xcb, wx_ref[...]) + bx_ref[...])
    log_a = -LRU_C * r * jax.nn.softplus(-lam_ref[...])
    a = jnp.exp(log_a)
    b = jnp.sqrt(-_expm1(2.0 * log_a)) * (i * xc)
    return a, b


def _lru_prompt_kernel(x_ref, g_ref, cprev_ref, h0_ref, cw_ref, cb_ref, wa_ref, wx_ref, ba_ref, bx_ref, lam_ref,
                       y_ref, hl_ref, xbuf, h_sc):
    t = pl.program_id(1)
    tl = x_ref.shape[1]

    @pl.when(t == 0)
    def _():
        xbuf[0:SUBLANES, :] = cprev_ref[0]
        h_sc[0:1, :] = h0_ref[0]

    xbuf[SUBLANES:SUBLANES + tl, :] = x_ref[0]
    cw = cw_ref[...]
    xc = xbuf[5:5 + tl, :] * cw[0:1, :]
    for j in range(1, cw.shape[0]):
        xc = xc + xbuf[5 + j:5 + j + tl, :] * cw[j:j + 1, :]
    xc = xc + cb_ref[...]
    xbuf[0:SUBLANES, :] = xbuf[tl:tl + SUBLANES, :]
    a, b = _lru_gates(xc, wa_ref, wx_ref, ba_ref, bx_ref, lam_ref)
    row = lax.broadcasted_iota(I32, (tl, 1), 0)
    s = 1
    while s < tl:
        m = row >= s
        b = jnp.where(m, a * pltpu.roll(b, s, axis=0) + b, b)
        a = jnp.where(m, a * pltpu.roll(a, s, axis=0), a)
        s *= 2
    hs = a * h_sc[0:1, :] + b
    h_sc[0:1, :] = hs[tl - 1:tl, :]
    y_ref[0] = hs * jax.nn.gelu(g_ref[0])
    hl_ref[0] = hs[tl - 1:tl, :]


def lru_prompt(xb, gb, cprev, h0, cw, cb, wa, wx, ba, bx, lam, tl):
    n, l, c = xb.shape
    seq = pl.BlockSpec((1, tl, c), lambda i, t: (i, t, 0))
    return pl.pallas_call(
        _lru_prompt_kernel,
        grid=(n, l // tl),
        in_specs=[seq, seq, pl.BlockSpec((1, SUBLANES, c), lambda i, t: (i, 0, 0)),
                  pl.BlockSpec((1, 1, c), lambda i, t: (i, 0, 0)),
                  _full(cw.shape), _full((1, c)), _full(wa.shape), _full(wx.shape), _full((1, c)), _full((1, c)),
                  _full((1, c))],
        out_specs=[seq, pl.BlockSpec((1, 1, c), lambda i, t: (i, 0, 0))],
        out_shape=[jax.ShapeDtypeStruct((n, l, c), F32), jax.ShapeDtypeStruct((n, 1, c), F32)],
        scratch_shapes=[pltpu.VMEM((tl + SUBLANES, c), F32), pltpu.VMEM((SUBLANES, c), F32)],
        compiler_params=_cp(("parallel", "arbitrary")),
        name="lru_prompt",
    )(xb, gb, cprev, h0, cw, cb, wa, wx, ba, bx, lam)


def _lru_sample_kernel(x_ref, g_ref, cprev_ref, h0_ref, cw_ref, cb_ref, wa_ref, wx_ref, ba_ref, bx_ref, lam_ref,
                       y_ref, hl_ref):
    l = x_ref.shape[0]
    cw = cw_ref[...]
    nw = cw.shape[0]
    xx = [cprev_ref[j] for j in range(nw - 1)] + [x_ref[j] for j in range(l)]
    h = h0_ref[...]
    for i in range(l):
        xc = xx[i] * cw[0:1, :]
        for j in range(1, nw):
            xc = xc + xx[i + j] * cw[j:j + 1, :]
        xc = xc + cb_ref[...]
        a, b = _lru_gates(xc, wa_ref, wx_ref, ba_ref, bx_ref, lam_ref)
        h = a * h + b
        y_ref[i] = h * jax.nn.gelu(g_ref[i])
    hl_ref[...] = h


def lru_sample(xb, gb, cprev, h0, cw, cb, wa, wx, ba, bx, lam):
    l, n, c = xb.shape
    return pl.pallas_call(
        _lru_sample_kernel,
        out_shape=[jax.ShapeDtypeStruct((l, n, c), F32), jax.ShapeDtypeStruct((n, c), F32)],
        compiler_params=pltpu.CompilerParams(vmem_limit_bytes=VMEM_LIMIT),
        name="lru_sample",
    )(xb, gb, cprev, h0, cw, cb, wa, wx, ba, bx, lam)


MLA_HEADS_PER_STEP = 4


def _mla_prompt_kernel(q_ref, k_ref, v_ref, o_ref, m_sc, l_sc, acc_sc, *, scale):
    qi = pl.program_id(2)
    tq = q_ref.shape[0]
    tk = tq
    g = q_ref.shape[1] // LANES
    c1 = scale * math.log2(math.e)
    m_sc[...] = jnp.full_like(m_sc, -jnp.inf)
    l_sc[...] = jnp.zeros_like(l_sc)
    acc_sc[...] = jnp.zeros_like(acc_sc)
    on_or_below = lax.broadcasted_iota(I32, (tq, tk), 1) <= lax.broadcasted_iota(I32, (tq, tk), 0)

    def step(ki, masked):
        rows = pl.ds(pl.multiple_of(ki * tk, tk), tk)
        scores = [_mm(q_ref[:, h * LANES:(h + 1) * LANES], k_ref[rows, h * LANES:(h + 1) * LANES], NT) * c1
                  for h in range(g)]
        for h in range(g):
            hs = slice(h * LANES, (h + 1) * LANES)
            s = scores[h]
            if masked:
                s = jnp.where(on_or_below, s, -jnp.inf)
            m_prev = m_sc[h]
            m_new = jnp.maximum(m_prev, jnp.max(s, axis=-1, keepdims=True))
            alpha = jnp.exp2(m_prev - m_new)
            p = jnp.exp2(s - jnp.concatenate([m_new] * (tk // LANES), axis=1))
            l_sc[h] = alpha * l_sc[h] + jnp.sum(p, axis=-1, keepdims=True)
            acc_sc[:, hs] = alpha * acc_sc[:, hs] + _mm(p.astype(BF16), v_ref[rows, hs])
            m_sc[h] = m_new

    def body(ki, carry):
        step(ki, False)
        return carry

    lax.fori_loop(0, qi, body, 0)
    step(qi, True)
    for h in range(g):
        hs = slice(h * LANES, (h + 1) * LANES)
        o_ref[:, hs] = (acc_sc[:, hs] / l_sc[h]).astype(o_ref.dtype)


def mla_prompt(q, k, v, n, l, tq):
    nq = l // tq
    g = MLA_HEADS_PER_STEP
    w = g * LANES
    scale = (MLA_NOPE + MLA_ROPE) ** -0.5
    return pl.pallas_call(
        functools.partial(_mla_prompt_kernel, scale=scale),
        grid=(n, MLA_HEADS // g, nq),
        in_specs=[pl.BlockSpec((tq, w), lambda i, h, j: (i * nq + j, h)),
                  pl.BlockSpec((l, w), lambda i, h, j: (i, h)), pl.BlockSpec((l, w), lambda i, h, j: (i, h))],
        out_specs=pl.BlockSpec((tq, w), lambda i, h, j: (i * nq + j, h)),
        out_shape=jax.ShapeDtypeStruct(q.shape, BF16),
        scratch_shapes=[pltpu.VMEM((g, tq, LANES), F32), pltpu.VMEM((g, tq, LANES), F32), pltpu.VMEM((tq, w), F32)],
        compiler_params=_cp(("parallel", "parallel", "arbitrary")),
        name="mla_prompt",
    )(q, k, v)


MLA_SAMPLE_PAGES = 32
MLA_SAMPLE_GROUPS = 2


def _mla_sample_kernel(pt_ref, q_ref, rows_ref, *rest, n_pages_step, l, scale):
    page_refs = rest[:n_pages_step]
    o_ref, m_sc, l_sc, acc_sc = rest[n_pages_step:]
    j = pl.program_id(1)

    @pl.when(j == 0)
    def _():
        m_sc[...] = jnp.full_like(m_sc, -jnp.inf)
        l_sc[...] = jnp.zeros_like(l_sc)
        acc_sc[...] = jnp.zeros_like(acc_sc)

    q = q_ref[0]
    per = n_pages_step // MLA_SAMPLE_GROUPS
    kts = [jnp.concatenate([r[0].astype(BF16) for r in page_refs[g * per:(g + 1) * per]], axis=1)
           for g in range(MLA_SAMPLE_GROUPS)]
    ss = [_mm(q, kt) * scale for kt in kts]
    m_run = m_sc[...]
    l_run = l_sc[...]
    acc = acc_sc[...]
    for kt, s in zip(kts, ss):
        m_new = jnp.maximum(m_run, jnp.max(s, axis=-1, keepdims=True))
        alpha = jnp.exp(m_run - m_new)
        p32 = jnp.exp(s - m_new)
        l_run = alpha * l_run + jnp.sum(p32, axis=-1, keepdims=True)
        acc = alpha * acc + _mm(p32.astype(BF16), kt, NT)
        m_run = m_new
    m_sc[...] = m_run
    l_sc[...] = l_run
    acc_sc[...] = acc

    @pl.when(j == pl.num_programs(1) - 1)
    def _():
        qf = q.astype(F32)
        rows = rows_ref[0]
        tok = lax.broadcasted_iota(I32, (qf.shape[0], 1), 0) // MLA_HEADS
        sn = []
        for mm in range(l):
            sm = jnp.sum(qf * rows[mm:mm + 1, :], axis=-1, keepdims=True) * scale
            sn.append(jnp.where(tok >= mm, sm, -jnp.inf))
        m_old = m_sc[...]
        m_fin = m_old
        for sm in sn:
            m_fin = jnp.maximum(m_fin, sm)
        al = jnp.exp(m_old - m_fin)
        lsum = al * l_sc[...]
        acc = al * acc_sc[...]
        for mm in range(l):
            pm = jnp.exp(sn[mm] - m_fin)
            lsum = lsum + pm
            acc = acc + pm * rows[mm:mm + 1, :]
        o_ref[0] = acc / lsum


def mla_sample(page_table, q_cat, rows, cache_t, n_pages_step):
    n, r, c = q_cat.shape
    l = rows.shape[1]
    n_pages = page_table.shape[1]
    steps = n_pages // n_pages_step
    scale = (MLA_NOPE + MLA_ROPE) ** -0.5
    pt = page_table.reshape(-1)

    def page_map(k):
        return lambda i, j, pt_ref: (pt_ref[i * n_pages + j * n_pages_step + k], 0, 0)

    return pl.pallas_call(
        functools.partial(_mla_sample_kernel, n_pages_step=n_pages_step, l=l, scale=scale),
        grid_spec=pltpu.PrefetchScalarGridSpec(
            num_scalar_prefetch=1,
            grid=(n, steps),
            in_specs=[pl.BlockSpec((1, r, c), lambda i, j, pt_ref: (i, 0, 0)),
                      pl.BlockSpec((1, l, c), lambda i, j, pt_ref: (i, 0, 0))]
            + [pl.BlockSpec((1, c, PAGE_SIZE), page_map(k)) for k in range(n_pages_step)],
            out_specs=pl.BlockSpec((1, r, c), lambda i, j, pt_ref: (i, 0, 0)),
            scratch_shapes=[pltpu.VMEM((r, 1), F32), pltpu.VMEM((r, 1), F32), pltpu.VMEM((r, c), F32)],
        ),
        out_shape=jax.ShapeDtypeStruct((n, r, c), F32),
        compiler_params=_cp(("parallel", "arbitrary")),
        name="mla_sample",
    )(pt, q_cat, rows, *([cache_t] * n_pages_step))


def _matmul_kernel(x_ref, w_ref, o_ref):
    o_ref[...] = _mm(x_ref[...].astype(BF16), w_ref[...]).astype(o_ref.dtype)


def matmul(x, w, out_dtype):
    t, kd = x.shape
    tm = _row_tile(t, (512, 256, 128, 64, 32, 16, 8))
    return pl.pallas_call(
        _matmul_kernel,
        grid=(t // tm,),
        in_specs=[pl.BlockSpec((tm, kd), lambda i: (i, 0)), _full(w.shape)],
        out_specs=pl.BlockSpec((tm, w.shape[1]), lambda i: (i, 0)),
        out_shape=jax.ShapeDtypeStruct((t, w.shape[1]), out_dtype),
        compiler_params=_cp(("parallel",)),
        name="matmul",
    )(x, w)


def _odd_out_kernel(x_ref, yc_ref, od_ref, woc_ref, wod_ref, ln_ref, rw_ref, rb_ref, *rest):
    x1_ref, hn_ref, idx_ref, gate_ref = rest[-4:]
    x1 = x_ref[...] + (_mm(yc_ref[...].astype(BF16), woc_ref[...]) + _mm(od_ref[...], wod_ref[...]))
    x1_ref[...] = x1
    hn = _rms(x1, ln_ref[...])
    hn_ref[...] = hn.astype(BF16)
    logits = _mm(hn.astype(BF16), rw_ref[...]) + rb_ref[...]
    lane = lax.broadcasted_iota(I32, logits.shape, 1)
    logits = jnp.where(lane < N_EXPERTS, logits, -jnp.inf)
    m1 = jnp.max(logits, axis=-1, keepdims=True)
    i1 = jnp.min(jnp.where(logits == m1, lane, LANES), axis=-1, keepdims=True)
    l2 = jnp.where(lane == i1, -jnp.inf, logits)
    m2 = jnp.max(l2, axis=-1, keepdims=True)
    i2 = jnp.min(jnp.where(l2 == m2, lane, LANES), axis=-1, keepdims=True)
    e2 = jnp.exp(m2 - m1)
    den = 1.0 + e2
    idx_ref[...] = jnp.where(lane == 0, i1, jnp.where(lane == 1, i2, 0))
    gate_ref[...] = jnp.where(lane == 0, 1.0 / den, jnp.where(lane == 1, e2 / den, 0.0))


def odd_out(x, yc, od, woc, wod, ln, rw, rb, hn_rows, hn_row0, hn_buf=None):
    t, d = x.shape
    tm = _row_tile(t, (512, 256, 128, 64, 32, 16, 8))
    assert hn_row0 % tm == 0
    off = hn_row0 // tm
    row = lambda w: pl.BlockSpec((tm, w), lambda i: (i, 0))
    args = (x, yc, od, woc, wod, ln, rw, rb) + (() if hn_buf is None else (hn_buf,))
    return pl.pallas_call(
        _odd_out_kernel,
        grid=(t // tm,),
        in_specs=[row(d), row(yc.shape[1]), row(od.shape[1]), _full(woc.shape), _full(wod.shape), _full((1, d)),
                  _full(rw.shape), _full(rb.shape)] + ([] if hn_buf is None else [pl.BlockSpec(memory_space=pl.ANY)]),
        out_specs=[row(d), pl.BlockSpec((tm, d), lambda i: (i + off, 0)), row(LANES), row(LANES)],
        out_shape=[jax.ShapeDtypeStruct((t, d), F32), jax.ShapeDtypeStruct((hn_rows, d), BF16),
                   jax.ShapeDtypeStruct((t, LANES), I32), jax.ShapeDtypeStruct((t, LANES), F32)],
        input_output_aliases={} if hn_buf is None else {8: 1},
        compiler_params=_cp(("parallel",)),
        name="odd_out",
    )(*args)


EXPERT_ROWS = 512
EXPERT_FF_CHUNK = 512
MOE_SUB = 128
MOE_PAIRS_PER_STEP = 4
F_VALID, F_FIRST, F_LAST, F_EMPTY = 1, 2, 4, 8


def _moe_expert_kernel(sb_ref, fl_ref, pj_ref, lo_ref, hi_ref, be_ref, *refs):
    del pj_ref, be_ref
    nu = MOE_PAIRS_PER_STEP
    tiles = [refs[3 * u:3 * u + 3] for u in range(nu)]
    wg_ref, wu_ref, wd_ref, o_ref, xacc, gacc = refs[3 * nu:]
    s = pl.program_id(0)
    fl = fl_ref[s]
    er = xacc.shape[0]

    @pl.when((fl & F_VALID) != 0)
    def _():
        @pl.when((fl & F_FIRST) != 0)
        def _():
            xacc[...] = jnp.zeros_like(xacc)
            gacc[...] = jnp.zeros_like(gacc)

        for u, (slot_ref, gate_ref, hn_ref) in enumerate(tiles):
            tile = hn_ref.shape[0]
            for sb in range(er // MOE_SUB):
                @pl.when(jnp.logical_and(lo_ref[s * nu + u] <= sb, sb <= hi_ref[s * nu + u]))
                def _(sb=sb, slot_ref=slot_ref, gate_ref=gate_ref, hn_ref=hn_ref, tile=tile):
                    rs = slice(sb * MOE_SUB, (sb + 1) * MOE_SUB)
                    srow = sb_ref[s] * er + sb * MOE_SUB + lax.broadcasted_iota(I32, (MOE_SUB, tile), 0)
                    m0 = slot_ref[0:1, :] == srow
                    m1 = slot_ref[1:2, :] == srow
                    sel = jnp.where(m0, 1.0, jnp.where(m1, 1.0, 0.0)).astype(BF16)
                    xacc[rs, :] += _mm(sel, hn_ref[...])
                    g = jnp.where(m0, gate_ref[0:1, :], jnp.where(m1, gate_ref[1:2, :], 0.0))
                    gacc[rs, :] += jnp.sum(g, axis=1, keepdims=True)

        @pl.when(jnp.logical_and((fl & F_LAST) != 0, (fl & F_EMPTY) == 0))
        def _():
            x = xacc[...].astype(BF16)
            acc = None
            for f in range(0, wg_ref.shape[2], EXPERT_FF_CHUNK):
                gg = _mm(x, wg_ref[0, :, f:f + EXPERT_FF_CHUNK])
                uu = _mm(x, wu_ref[0, :, f:f + EXPERT_FF_CHUNK])
                y = _mm((_silu(gg) * uu).astype(BF16), wd_ref[0, f:f + EXPERT_FF_CHUNK, :])
                acc = y if acc is None else acc + y
            o_ref[...] = (acc * gacc[...]).astype(o_ref.dtype)

        @pl.when((fl & F_EMPTY) != 0)
        def _():
            o_ref[...] = jnp.zeros_like(o_ref)


def moe_experts(plan, slot_t, gate_t, hn, wg, wu, wd, tile):
    sb, fl, pj, lo, hi, blk_e, n_blocks = plan
    d = hn.shape[1]
    ff = wg.shape[2]
    nu = MOE_PAIRS_PER_STEP
    wmap = lambda s, sb, fl, pj, lo, hi, be: (be[sb[s]], 0, 0)
    tile_specs, tile_args = [], []
    for u in range(nu):
        tmap = lambda s, sb, fl, pj, lo, hi, be, u=u: (0, pj[s * nu + u])
        hmap = lambda s, sb, fl, pj, lo, hi, be, u=u: (pj[s * nu + u], 0)
        tile_specs += [pl.BlockSpec((TOP_K, tile), tmap), pl.BlockSpec((TOP_K, tile), tmap),
                       pl.BlockSpec((tile, d), hmap)]
        tile_args += [slot_t, gate_t, hn]
    return pl.pallas_call(
        _moe_expert_kernel,
        grid_spec=pltpu.PrefetchScalarGridSpec(
            num_scalar_prefetch=6,
            grid=(sb.shape[0],),
            in_specs=tile_specs + [pl.BlockSpec((1, d, ff), wmap), pl.BlockSpec((1, d, ff), wmap),
                                   pl.BlockSpec((1, ff, d), wmap)],
            out_specs=pl.BlockSpec((EXPERT_ROWS, d), lambda s, sb, fl, pj, lo, hi, be: (sb[s], 0)),
            scratch_shapes=[pltpu.VMEM((EXPERT_ROWS, d), F32), pltpu.VMEM((EXPERT_ROWS, 1), F32)],
        ),
        out_shape=jax.ShapeDtypeStruct((n_blocks * EXPERT_ROWS, d), BF16),
        compiler_params=_cp(("arbitrary",)),
        name="moe_experts",
    )(sb, fl, pj, lo, hi, blk_e, *tile_args, wg, wu, wd)


def _moe_combine_kernel(sj_ref, fl_ref, qb_ref, lo_ref, hi_ref, slot_ref, x1_ref, *refs):
    del sj_ref
    nu = MOE_PAIRS_PER_STEP
    yb_refs = refs[:nu]
    fn_ref, o_ref, yacc = refs[nu:]
    s = pl.program_id(0)
    fl = fl_ref[s]
    tile = x1_ref.shape[0]
    er = yb_refs[0].shape[0]

    @pl.when((fl & F_VALID) != 0)
    def _():
        @pl.when((fl & F_FIRST) != 0)
        def _():
            yacc[...] = jnp.zeros_like(yacc)

        for u, yb_ref in enumerate(yb_refs):
            for sb in range(er // MOE_SUB):
                @pl.when(jnp.logical_and(lo_ref[s * nu + u] <= sb, sb <= hi_ref[s * nu + u]))
                def _(sb=sb, u=u, yb_ref=yb_ref):
                    scol = qb_ref[s * nu + u] * er + sb * MOE_SUB + lax.broadcasted_iota(I32, (tile, MOE_SUB), 1)
                    sl = slot_ref[...]
                    sel = jnp.where(sl[:, 0:1] == scol, 1.0, jnp.where(sl[:, 1:2] == scol, 1.0, 0.0)).astype(BF16)
                    yacc[...] += _mm(sel, yb_ref[sb * MOE_SUB:(sb + 1) * MOE_SUB, :])

        @pl.when((fl & F_LAST) != 0)
        def _():
            o_ref[...] = _rms(x1_ref[...] + yacc[...], fn_ref[...])


def moe_combine(sched, slot_cols, x1, yb, fnorm, tile):
    sj, fl, qb, lo, hi = sched
    t, d = x1.shape
    nu = MOE_PAIRS_PER_STEP
    tmap = lambda s, sj, fl, qb, lo, hi: (sj[s], 0)
    yb_specs = [pl.BlockSpec((EXPERT_ROWS, d), lambda s, sj, fl, qb, lo, hi, u=u: (qb[s * nu + u], 0))
                for u in range(nu)]
    return pl.pallas_call(
        _moe_combine_kernel,
        grid_spec=pltpu.PrefetchScalarGridSpec(
            num_scalar_prefetch=5,
            grid=(sj.shape[0],),
            in_specs=[pl.BlockSpec((tile, LANES), tmap), pl.BlockSpec((tile, d), tmap)] + yb_specs
            + [pl.BlockSpec((1, d), lambda s, sj, fl, qb, lo, hi: (0, 0))],
            out_specs=pl.BlockSpec((tile, d), tmap),
            scratch_shapes=[pltpu.VMEM((tile, d), F32)],
        ),
        out_shape=jax.ShapeDtypeStruct((t, d), F32),
        compiler_params=_cp(("arbitrary",)),
        name="moe_combine",
    )(sj, fl, qb, lo, hi, slot_cols, x1, *([yb] * nu), fnorm)


def _ragged_steps(counts, pmax):
    cum = jnp.cumsum(counts)
    total = cum[-1]
    ar = jnp.arange(pmax, dtype=I32)
    p = jnp.minimum(ar, jnp.maximum(total - 1, 0))
    row = jnp.minimum(jnp.sum((cum[None, :] <= p[:, None]).astype(I32), axis=1), counts.shape[0] - 1)
    off = p - (cum[row] - counts[row])
    return row, off, ar < total, total


def _group_flags(gid, valid, total):
    pmax = gid.shape[0]
    ar = jnp.arange(pmax, dtype=I32)
    prev = jnp.concatenate([gid[:1] - 1, gid[:-1]])
    nxt = jnp.concatenate([gid[1:], gid[-1:] + 1])
    first = jnp.logical_or(ar == 0, gid != prev)
    last = jnp.logical_or(ar == total - 1, gid != nxt)
    fl = F_VALID + F_FIRST * first.astype(I32) + F_LAST * last.astype(I32)
    return jnp.where(valid, fl, 0).astype(I32)


def _grouped_steps(cnt, smax):
    nu = MOE_PAIRS_PER_STEP
    row, off, valid, total = _ragged_steps((cnt + nu - 1) // nu, smax)
    k = off[:, None] * nu + jnp.arange(nu, dtype=I32)[None, :]
    c = cnt[row][:, None]
    pvalid = jnp.logical_and(valid[:, None], k < c)
    pidx = (jnp.cumsum(cnt) - cnt)[row][:, None] + jnp.minimum(k, jnp.maximum(c - 1, 0))
    return row, _group_flags(row, valid, total), pidx.reshape(-1), pvalid.reshape(-1)


def _moe_plan(e_all, tile):
    t = e_all.shape[0]
    nt = t // tile
    er = EXPERT_ROWS
    ex = jnp.arange(N_EXPERTS, dtype=I32)[None, :]
    oh = jnp.logical_or(e_all[:, 0:1] == ex, e_all[:, 1:2] == ex).astype(I32)
    cs = jnp.cumsum(oh, axis=0)
    rank = cs - oh
    counts = cs[-1]
    padded = (counts + er - 1) // er * er
    pad_end = jnp.cumsum(padded)
    pad_start = pad_end - padded
    slot_e = pad_start[None, :] + rank
    slot = jnp.stack([jnp.sum(jnp.where(e_all[:, k:k + 1] == ex, slot_e, 0), axis=1) for k in range(TOP_K)],
                     axis=1).astype(I32)
    n_blocks = -(-(t * TOP_K) // er) + N_EXPERTS
    blk = jnp.arange(n_blocks, dtype=I32)
    blk_e = jnp.minimum(jnp.sum((blk[:, None] * er >= pad_end[None, :]).astype(I32), axis=1), N_EXPERTS - 1)
    n_used = pad_end[-1] // er
    tstart = jnp.concatenate([rank[::tile], counts[None, :]], axis=0)
    first_slot = pad_start[None, :] + tstart[:-1]
    last_slot = pad_start[None, :] + tstart[1:] - 1

    def sub_range(b, fs, ls):
        lo = jnp.maximum(fs - b * er, 0) // MOE_SUB
        hi = jnp.where(ls >= fs, jnp.minimum(ls - b * er, er - 1), -1) // MOE_SUB
        return lo.astype(I32), jnp.maximum(hi, -1).astype(I32)

    be_oh = (blk_e[:, None] == ex).astype(I32)
    r0 = blk * er - jnp.sum(be_oh * pad_start[None, :], axis=1)
    r1 = jnp.minimum(r0 + er - 1, jnp.sum(be_oh * counts[None, :], axis=1) - 1)
    ts_b = jnp.sum(be_oh[:, None, :] * tstart[None, 1:, :], axis=2)
    jlo = jnp.minimum(jnp.sum((ts_b <= r0[:, None]).astype(I32), axis=1), nt - 1)
    jhi = jnp.minimum(jnp.sum((ts_b <= r1[:, None]).astype(I32), axis=1), nt - 1)
    cnt_b = jnp.where(blk < n_used, jhi - jlo + 1, 1)
    nu = MOE_PAIRS_PER_STEP
    pmax = n_blocks + nt * N_EXPERTS
    row, off, _, _ = _ragged_steps(cnt_b, pmax)
    pj = (jlo[row] + off).astype(I32)
    je = pj * N_EXPERTS + blk_e[row]
    lo, hi = sub_range(row, first_slot.reshape(-1)[je], last_slot.reshape(-1)[je])
    sb, sfl, pidx, pvalid = _grouped_steps(cnt_b, n_blocks + -(-pmax // nu))
    sfl = jnp.where(jnp.logical_and(sfl != 0, sb >= n_used), sfl + F_EMPTY, sfl)
    pvalid = jnp.logical_and(pvalid, jnp.repeat(sb < n_used, nu))
    expert_plan = (sb, sfl, pj[pidx], jnp.where(pvalid, lo[pidx], 1), jnp.where(pvalid, hi[pidx], 0),
                   blk_e.astype(I32), n_blocks)

    def combine_sched(j0, j1):
        ntr = j1 - j0
        fs = first_slot[j0:j1].reshape(-1)
        ls = last_slot[j0:j1].reshape(-1)
        blo = fs // er
        nb = jnp.where(ls >= fs, ls // er - blo + 1, 0)
        pmax_c = min(n_blocks + ntr * N_EXPERTS, 2 * ntr * N_EXPERTS)
        r, o, _, _ = _ragged_steps(nb, pmax_c)
        qb = (blo[r] + o).astype(I32)
        lo_c, hi_c = sub_range(qb, fs[r], ls[r])
        cnt_j = jnp.sum(nb.reshape(ntr, N_EXPERTS), axis=1)
        sj, sfl_c, pidx_c, pvalid_c = _grouped_steps(cnt_j, ntr + -(-pmax_c // nu))
        return (sj, sfl_c, qb[pidx_c], jnp.where(pvalid_c, lo_c[pidx_c], 1), jnp.where(pvalid_c, hi_c[pidx_c], 0))

    return slot, expert_plan, combine_sched


def _rope_tables(pos):
    half = MLA_ROPE // 2
    inv = ROPE_THETA ** (-jnp.arange(half, dtype=F32) * 2.0 / MLA_ROPE)
    ang = pos.astype(F32)[:, None] * inv[None, :]
    cos = jnp.cos(ang)
    sin = jnp.sin(ang)
    n = pos.shape[0]
    cos_t = jnp.concatenate([cos, cos, jnp.ones((n, LANES - MLA_ROPE), F32)], axis=1)
    sin_t = jnp.concatenate([-sin, sin, jnp.zeros((n, LANES - MLA_ROPE), F32)], axis=1)
    return cos_t, sin_t


def _block_diag(w):
    b, i, j = w.shape
    eye = jnp.eye(b, dtype=w.dtype)
    return (eye[:, None, :, None] * w[:, :, None, :]).reshape(b * i, b * j)


def _pad_rows(a, rows):
    return jnp.pad(a, ((0, 0), (0, rows - a.shape[1]), (0, 0)))


def kernel(x_prompt, x_sample, state_swa_kv, state_gdn_conv, state_gdn_s, state_lru_conv, state_lru_h, cache_mla, page_table, e_ln_mix, e_w_in, e_gdn_conv_w, e_gdn_a_log, e_gdn_dt_bias, e_gdn_norm, e_swa_sinks, e_w_out, e_ln_ffn, e_ffn_gate, e_ffn_up, e_ffn_down, o_ln_mix, o_w_in, o_lru_conv_w, o_lru_conv_b, o_lru_w_a, o_lru_b_a, o_lru_w_x, o_lru_b_x, o_lru_lambda, o_mla_q_norm, o_mla_w_uq, o_mla_kv_norm, o_mla_w_uk, o_mla_w_uv, o_w_out, o_ln_ffn, o_router_w, o_router_b, o_exp_gate, o_exp_up, o_exp_down, final_norm):
    nb, lp, d = x_prompt.shape
    ns, ls, _ = x_sample.shape
    past_len = page_table.shape[1] * PAGE_SIZE
    tp, ts = nb * lp, ns * ls
    xp = x_prompt.reshape(tp, d)
    xs = x_sample.reshape(ts, d)
    row1 = lambda v: v.reshape(1, -1)

    na = (SWA_HEADS + 2 * SWA_KV_HEADS) * SWA_HEAD_DIM
    nz = GDN_HEADS * GDN_DV
    w_in = e_w_in[0].astype(BF16)
    o1, o2, o3 = na, na + GDN_QKV, na + GDN_QKV + nz
    w_groups = [w_in[:, :o1], w_in[:, o1:o2], w_in[:, o2:o3],
                jnp.pad(w_in[:, o3:], ((0, 0), (0, LANES - 2 * GDN_HEADS)))]
    ln = row1(e_ln_mix[0])
    qkv_p, gx_p, z_p, bg_p = norm_proj(xp, ln, w_groups)
    qkv_s, gx_s, z_s, bg_s = norm_proj(xs, ln, w_groups)

    sinks = e_swa_sinks[0]
    oa_p = swa_prompt(qkv_p, sinks, nb, lp)
    wbuf = state_swa_kv.shape[2]
    nkv = 2 * SWA_KV_HEADS * SWA_HEAD_DIM
    buf = state_swa_kv[0].reshape(ns, wbuf, nkv)
    qkv_s3 = qkv_s.reshape(ns, ls, na)
    oa_s = swa_sample(_pad_rows(qkv_s3, SUBLANES), buf, sinks, ls)[:, :ls].reshape(ts, -1)
    kv_p = qkv_p.reshape(nb, lp, na)[:, lp - SWA_WINDOW:, na - nkv:]
    swa_kv_p = kv_p.reshape(1, nb, SWA_WINDOW, 2, SWA_KV_HEADS, SWA_HEAD_DIM)
    kv_s = jnp.concatenate([buf, qkv_s3[:, :, na - nkv:]], axis=1)[:, ls:]
    swa_kv_s = kv_s.reshape(1, ns, wbuf, 2, SWA_KV_HEADS, SWA_HEAD_DIM)

    gx_p3 = gx_p.reshape(nb, lp, GDN_QKV)
    gx_s3 = gx_s.reshape(ns, ls, GDN_QKV)
    gargs = (e_gdn_conv_w[0], e_gdn_a_log[0], e_gdn_dt_bias[0], e_gdn_norm[0])
    ob_p, gs_p = gdn(gx_p3, z_p.reshape(nb, lp, nz), bg_p.reshape(nb, lp, LANES),
                     jnp.zeros((nb, SUBLANES, GDN_QKV), F32), jnp.zeros((nb, GDN_HEADS, GDN_DK, GDN_DV), F32),
                     *gargs, chunk=math.gcd(lp, GDN_CHUNK), valid=math.gcd(lp, GDN_CHUNK), tl=min(lp, 512))
    cprev_s = jnp.pad(state_gdn_conv[0], ((0, 0), (SUBLANES - (GDN_CONV - 1), 0), (0, 0)))
    assert ls <= GDN_SAMPLE_PAD and math.gcd(ls, GDN_CHUNK) == ls
    ob_s, gs_s = gdn(_pad_rows(gx_s3, GDN_SAMPLE_PAD), _pad_rows(z_s.reshape(ns, ls, nz), GDN_SAMPLE_PAD),
                     _pad_rows(bg_s.reshape(ns, ls, LANES), GDN_SAMPLE_PAD), cprev_s, state_gdn_s[0],
                     *gargs, chunk=GDN_SAMPLE_PAD, valid=ls, tl=GDN_SAMPLE_PAD)
    ob_s = ob_s[:, :ls].reshape(ts, nz)
    gconv_p = gx_p3[:, lp - (GDN_CONV - 1):][None]
    gconv_s = gx_s3[:, ls - (GDN_CONV - 1):][None]

    w_out = e_w_out[0].astype(BF16)
    nqa = SWA_HEADS * SWA_HEAD_DIM
    ffn = (row1(e_ln_ffn[0]), e_ffn_gate[0].astype(BF16), e_ffn_up[0].astype(BF16), e_ffn_down[0].astype(BF16))
    xp = even_out(xp, oa_p, ob_p.reshape(tp, nz), w_out[:nqa], w_out[nqa:], *ffn)
    xs = even_out(xs, oa_s, ob_s, w_out[:nqa], w_out[nqa:], *ffn)

    w_in = o_w_in[0].astype(BF16)
    c0 = LRU_WIDTH
    c1 = 2 * LRU_WIDTH
    c2 = c1 + MLA_Q_RANK
    c3 = c2 + MLA_KV_RANK
    w_groups = [w_in[:, :c0], w_in[:, c0:c1], w_in[:, c1:c2], w_in[:, c2:c3],
                jnp.pad(w_in[:, c3:], ((0, 0), (0, LANES - MLA_ROPE)))]
    hd_q = MLA_NOPE + MLA_ROPE
    half = MLA_ROPE // 2
    wq = o_mla_w_uq[0].reshape(MLA_Q_RANK, MLA_HEADS, hd_q)
    wuq = jnp.concatenate([wq[:, :, MLA_NOPE:], wq[:, :, :MLA_NOPE],
                           jnp.zeros((MLA_Q_RANK, MLA_HEADS, LANES - hd_q), F32)], axis=2)
    wuq = wuq.reshape(MLA_Q_RANK, MLA_HEADS * LANES).astype(BF16)
    w_uk = o_mla_w_uk[0]
    w_uv = o_mla_w_uv[0]
    wuk = jnp.pad(w_uk, ((0, 0), (0, 0), (MLA_ROPE, LANES - hd_q))).reshape(MLA_KV_RANK, -1).astype(BF16)
    wuv = jnp.pad(w_uv, ((0, 0), (0, 0), (0, LANES - MLA_V))).reshape(MLA_KV_RANK, -1).astype(BF16)
    wabs = _block_diag(jnp.pad(jnp.transpose(w_uk, (1, 2, 0)), ((0, 0), (MLA_ROPE, LANES - hd_q), (0, 0)))).astype(BF16)
    wuv_bd = _block_diag(jnp.pad(jnp.transpose(w_uv, (1, 0, 2)), ((0, 0), (0, 0), (0, LANES - MLA_V)))).astype(BF16)
    ln = row1(o_ln_mix[0])
    qn, kvn = row1(o_mla_q_norm[0]), row1(o_mla_kv_norm[0])
    tm_p = _row_tile(tp, (512, 256, 128, 64, 32, 16, 8))
    assert lp % tm_p == 0 or tm_p % lp == 0
    pos_p = jnp.arange(max(lp, tm_p), dtype=I32) % lp
    pos_s = past_len + (jnp.arange(ts, dtype=I32) % ls)
    cos_p, sin_p = _rope_tables(pos_p)
    cos_s, sin_s = _rope_tables(pos_s)
    xb_p, gb_p, q_p, k_p, v_p, rows_p = odd_in(xp, ln, w_groups, qn, kvn, wuq, cos_p, sin_p, [wuk, wuv], False)
    xb_s, gb_s, q_s, qlat_s, rows_s = odd_in(xs, ln, w_groups, qn, kvn, wuq, cos_s, sin_s, [wabs], True)

    lru_w = (o_lru_conv_w[0], row1(o_lru_conv_b[0]), _block_diag(o_lru_w_a[0]).astype(BF16),
             _block_diag(o_lru_w_x[0]).astype(BF16), row1(o_lru_b_a[0]), row1(o_lru_b_x[0]), row1(o_lru_lambda[0]))
    xb_p3 = xb_p.reshape(nb, lp, LRU_WIDTH)
    yc_p, lh_p = lru_prompt(xb_p3, gb_p.reshape(nb, lp, LRU_WIDTH), jnp.zeros((nb, SUBLANES, LRU_WIDTH), F32),
                            jnp.zeros((nb, 1, LRU_WIDTH), F32), *lru_w, tl=min(lp, 256))
    xb_s3 = xb_s.reshape(ns, ls, LRU_WIDTH)
    tmaj = lambda a: jnp.transpose(a, (1, 0, 2))
    yc_s, lh_s = lru_sample(tmaj(xb_s3), tmaj(gb_s.reshape(ns, ls, LRU_WIDTH)), tmaj(state_lru_conv[0]),
                            state_lru_h[0], *lru_w)
    yc_s = tmaj(yc_s).reshape(ts, LRU_WIDTH)
    lconv_p = xb_p3[:, lp - 3:][None]
    lconv_s = xb_s3[:, ls - 3:][None]

    od_p = mla_prompt(q_p, k_p, v_p, nb, lp, tq=min(lp, 512))
    q_pe = q_s.reshape(ts, MLA_HEADS, LANES)[:, :, :MLA_ROPE]
    q_cat = jnp.concatenate([qlat_s.reshape(ts, MLA_HEADS, MLA_KV_RANK).astype(BF16), q_pe], axis=-1)
    q_cat = q_cat.reshape(ns, ls * MLA_HEADS, MLA_ROW)
    n_pages = page_table.shape[1]
    cache_t = jnp.swapaxes(cache_mla.reshape(cache_mla.shape[1:]), 1, 2)
    o_lat = mla_sample(page_table, q_cat, rows_s.reshape(ns, ls, MLA_ROW), cache_t,
                       n_pages_step=math.gcd(n_pages, MLA_SAMPLE_PAGES))
    o_lat = o_lat.reshape(ts, MLA_HEADS, MLA_ROW)[:, :, :MLA_KV_RANK].reshape(ts, MLA_HEADS * MLA_KV_RANK)
    od_s = matmul(o_lat, wuv_bd, BF16)

    w_out = o_w_out[0].astype(BF16)
    wod = jnp.pad(w_out[LRU_WIDTH:].reshape(MLA_HEADS, MLA_V, d), ((0, 0), (0, LANES - MLA_V), (0, 0)))
    wod = wod.reshape(MLA_HEADS * LANES, d)
    rw = jnp.pad(o_router_w[0], ((0, 0), (0, LANES - N_EXPERTS))).astype(BF16)
    rb = jnp.pad(o_router_b[0], (0, LANES - N_EXPERTS)).reshape(1, LANES)
    lnf = row1(o_ln_ffn[0])
    tall = tp + ts
    x1_p, hn, idx_p, gate_p = odd_out(xp, yc_p.reshape(tp, LRU_WIDTH), od_p, w_out[:LRU_WIDTH], wod, lnf, rw, rb,
                                      tall, 0, jnp.zeros((tall, d), BF16))
    x1_s, hn, idx_s, gate_s = odd_out(xs, yc_s, od_s, w_out[:LRU_WIDTH], wod, lnf, rw, rb, tall, tp, hn)

    tile = math.gcd(math.gcd(tp, ts), EXPERT_ROWS)
    e_all = jnp.concatenate([idx_p[:, :TOP_K], idx_s[:, :TOP_K]], axis=0)
    g_all = jnp.concatenate([gate_p[:, :TOP_K], gate_s[:, :TOP_K]], axis=0)
    slot, expert_plan, combine_sched = _moe_plan(e_all, tile)
    yb = moe_experts(expert_plan, slot.T, g_all.T, hn, o_exp_gate[0].astype(BF16), o_exp_up[0].astype(BF16),
                     o_exp_down[0].astype(BF16), tile)
    slot_cols = jnp.pad(slot, ((0, 0), (0, LANES - TOP_K)))
    fnw = row1(final_norm)
    y_p = moe_combine(combine_sched(0, tp // tile), slot_cols[:tp], x1_p, yb, fnw, tile).reshape(nb, lp, d)
    y_s = moe_combine(combine_sched(tp // tile, tall // tile), slot_cols[tp:], x1_s, yb, fnw, tile).reshape(ns, ls, d)

    return (y_p, y_s, swa_kv_p, swa_kv_s, gconv_p, gconv_s, gs_p[None], gs_s[None],
            lconv_p, lconv_s, lh_p.reshape(1, nb, LRU_WIDTH), lh_s[None],
            rows_p.reshape(1, nb, lp // PAGE_SIZE, PAGE_SIZE, MLA_ROW), rows_s.reshape(1, ns, ls, MLA_ROW))
```

```python
import functools
import math

import jax
import jax.numpy as jnp
from jax import lax
from jax.experimental import pallas as pl
from jax.experimental.pallas import tpu as pltpu

F32 = jnp.float32
BF16 = jnp.bfloat16
I32 = jnp.int32
HI = lax.Precision.HIGHEST

D_MODEL = 1024
PAGE_SIZE = 128
SWA_WINDOW = 128
SWA_HEADS = 8
SWA_KV_HEADS = 2
SWA_GROUP = SWA_HEADS // SWA_KV_HEADS
SWA_HEAD_DIM = 64
GDN_HEADS = 4
GDN_DK = 128
GDN_DV = 128
GDN_CONV = 4
GDN_CHUNK = 64
GDN_QKV = GDN_HEADS * (2 * GDN_DK + GDN_DV)
LRU_WIDTH = 512
LRU_BLOCKS = 8
LRU_BLOCK_W = LRU_WIDTH // LRU_BLOCKS
LRU_C = 8.0
MLA_HEADS = 8
MLA_Q_RANK = 384
MLA_KV_RANK = 256
MLA_NOPE = 64
MLA_ROPE = 32
MLA_V = 64
MLA_ROW = MLA_KV_RANK + MLA_ROPE
ROPE_THETA = 10000.0
D_FF = 2816
N_EXPERTS = 8
TOP_K = 2
MOE_FF = 2048
NORM_EPS = 1e-6

LANES = 128
SUBLANES = 8
VMEM_LIMIT = 56 << 20

NN = (((1,), (0,)), ((), ()))
NT = (((1,), (1,)), ((), ()))
TN = (((0,), (0,)), ((), ()))


def _mm(a, b, dims=NN, precision=None):
    return lax.dot_general(a, b, dims, precision=precision, preferred_element_type=F32)


def _cp(sem):
    return pltpu.CompilerParams(dimension_semantics=sem, vmem_limit_bytes=VMEM_LIMIT)


def _rms(x, w):
    return x * lax.rsqrt(jnp.mean(x * x, axis=-1, keepdims=True) + NORM_EPS) * w


def _silu(x):
    return x * jax.nn.sigmoid(x)


def _full(shape):
    nd = len(shape)
    return pl.BlockSpec(shape, lambda *a: (0,) * nd)


def _row_tile(t, pref):
    for c in pref:
        if t % c == 0:
            return c
    return t


def _norm_proj_kernel(x_ref, ln_ref, *refs):
    n = len(refs) // 2
    h = _rms(x_ref[...], ln_ref[...]).astype(BF16)
    for w_ref, o_ref in zip(refs[:n], refs[n:]):
        o_ref[...] = _mm(h, w_ref[...])


def norm_proj(x, ln, ws):
    t, d = x.shape
    tm = _row_tile(t, (512, 256, 128, 64, 32, 16, 8))
    return pl.pallas_call(
        _norm_proj_kernel,
        grid=(t // tm,),
        in_specs=[pl.BlockSpec((tm, d), lambda i: (i, 0)), _full((1, d))] + [_full(w.shape) for w in ws],
        out_specs=[pl.BlockSpec((tm, w.shape[1]), lambda i: (i, 0)) for w in ws],
        out_shape=[jax.ShapeDtypeStruct((t, w.shape[1]), F32) for w in ws],
        compiler_params=_cp(("parallel",)),
        name="norm_proj",
    )(x, ln, *ws)


def _swa_softmax_pv(parts, sink):
    m = sink
    for s, _ in parts:
        m = jnp.maximum(m, jnp.max(s, axis=-1, keepdims=True))
    den = jnp.exp(sink - m)
    o = None
    for s, v in parts:
        p = jnp.exp(s - m)
        den = den + jnp.sum(p, axis=-1, keepdims=True)
        pv = _mm(p.astype(BF16), v)
        o = pv if o is None else o + pv
    return o / den


def _swa_prompt_kernel(sink_ref, q_ref, kvc_ref, kvp_ref, o_ref):
    b = pl.program_id(1)
    w = SWA_WINDOW
    hd = SWA_HEAD_DIM
    ng = SWA_GROUP
    q = q_ref[...]
    band = jnp.concatenate([kvp_ref[...], kvc_ref[...]], axis=0).astype(BF16)
    r = lax.broadcasted_iota(I32, (ng * w, 1), 0)
    grp = r // w
    qi = r % w
    t = lax.broadcasted_iota(I32, (ng * w, 2 * w), 1)
    dist = qi + w - t
    valid = jnp.logical_and(jnp.logical_and(dist >= 0, dist < w), jnp.logical_or(t >= w, b > 0))
    distf = dist.astype(F32)
    outs = [None] * SWA_HEADS
    scores = []
    for j in range(SWA_KV_HEADS):
        qs = jnp.concatenate([q[:, (j * ng + g) * hd:(j * ng + g + 1) * hd] for g in range(ng)], axis=0)
        scores.append(_mm(qs.astype(BF16), band[:, j * hd:(j + 1) * hd], NT) * (hd ** -0.5))
    for j in range(SWA_KV_HEADS):
        heads = [j * ng + g for g in range(ng)]
        slope = jnp.zeros((ng * w, 1), F32)
        sink = jnp.zeros((ng * w, 1), F32)
        for g, h in enumerate(heads):
            slope = jnp.where(grp == g, 2.0 ** (-8.0 * (h + 1) / SWA_HEADS), slope)
            sink = jnp.where(grp == g, sink_ref[h], sink)
        vb = band[:, (SWA_KV_HEADS + j) * hd:(SWA_KV_HEADS + j + 1) * hd]
        s = jnp.where(valid, scores[j] - slope * distf, -jnp.inf)
        o = _swa_softmax_pv([(s, vb)], sink)
        for g, h in enumerate(heads):
            outs[h] = o[g * w:(g + 1) * w, :]
    o_ref[...] = jnp.concatenate(outs, axis=1)


def swa_prompt(qkv, sinks, n, l):
    w = SWA_WINDOW
    nb = l // w
    nq = SWA_HEADS * SWA_HEAD_DIM
    nkv = 2 * SWA_KV_HEADS * SWA_HEAD_DIM
    return pl.pallas_call(
        _swa_prompt_kernel,
        grid=(n, nb),
        in_specs=[
            pl.BlockSpec(memory_space=pltpu.SMEM),
            pl.BlockSpec((w, nq), lambda i, b: (i * nb + b, 0)),
            pl.BlockSpec((w, nkv), lambda i, b: (i * nb + b, nq // nkv)),
            pl.BlockSpec((w, nkv), lambda i, b: (i * nb + jnp.maximum(b - 1, 0), nq // nkv)),
        ],
        out_specs=pl.BlockSpec((w, nq), lambda i, b: (i * nb + b, 0)),
        out_shape=jax.ShapeDtypeStruct((n * l, nq), F32),
        compiler_params=_cp(("parallel", "arbitrary")),
        name="swa_prompt",
    )(sinks, qkv, qkv, qkv)


SWA_SAMPLE_REQS = 4


def _swa_sample_kernel(sink_ref, q_ref, buf_ref, o_ref, *, l):
    hd = SWA_HEAD_DIM
    ng = SWA_GROUP
    nr, lp, _ = q_ref.shape
    wb = buf_ref.shape[1]
    r = lax.broadcasted_iota(I32, (ng * lp, 1), 0)
    grp = r // lp
    qi = r % lp
    dist_b = qi + wb - lax.broadcasted_iota(I32, (ng * lp, wb), 1)
    valid_b = dist_b < SWA_WINDOW
    tn = lax.broadcasted_iota(I32, (ng * lp, lp), 1)
    dist_n = qi - tn
    valid_n = jnp.logical_and(tn <= qi, tn < l)
    slopes, sinks = [], []
    for j in range(SWA_KV_HEADS):
        slope = jnp.zeros((ng * lp, 1), F32)
        sink = jnp.zeros((ng * lp, 1), F32)
        for g in range(ng):
            h = j * ng + g
            slope = jnp.where(grp == g, 2.0 ** (-8.0 * (h + 1) / SWA_HEADS), slope)
            sink = jnp.where(grp == g, sink_ref[h], sink)
        slopes.append(slope)
        sinks.append(sink)
    chains = []
    for rq in range(nr):
        x = q_ref[rq]
        buf = buf_ref[rq].astype(BF16)
        kvn = x[:, SWA_HEADS * hd:].astype(BF16)
        for j in range(SWA_KV_HEADS):
            qs = jnp.concatenate([x[:, (j * ng + g) * hd:(j * ng + g + 1) * hd] for g in range(ng)], axis=0)
            qs = qs.astype(BF16)
            ks = slice(j * hd, (j + 1) * hd)
            vs = slice((SWA_KV_HEADS + j) * hd, (SWA_KV_HEADS + j + 1) * hd)
            s_b = _mm(qs, buf[:, ks], NT) * (hd ** -0.5)
            s_n = _mm(qs, kvn[:, ks], NT) * (hd ** -0.5)
            chains.append((rq, j, s_b, s_n, buf[:, vs], kvn[:, vs]))
    outs = [[None] * SWA_HEADS for _ in range(nr)]
    for rq, j, s_b, s_n, vb, vn in chains:
        s_b = jnp.where(valid_b, s_b - slopes[j] * dist_b.astype(F32), -jnp.inf)
        s_n = jnp.where(valid_n, s_n - slopes[j] * dist_n.astype(F32), -jnp.inf)
        o = _swa_softmax_pv([(s_b, vb), (s_n, vn)], sinks[j])
        for g in range(ng):
            outs[rq][j * ng + g] = o[g * lp:(g + 1) * lp, :]
    for rq in range(nr):
        o_ref[rq] = jnp.concatenate(outs[rq], axis=1)


def swa_sample(qkv_pad, buf, sinks, l):
    n, lp, c = qkv_pad.shape
    wb = buf.shape[1]
    nq = SWA_HEADS * SWA_HEAD_DIM
    nr = math.gcd(n, SWA_SAMPLE_REQS)
    return pl.pallas_call(
        functools.partial(_swa_sample_kernel, l=l),
        grid=(n // nr,),
        in_specs=[
            pl.BlockSpec(memory_space=pltpu.SMEM),
            pl.BlockSpec((nr, lp, c), lambda i: (i, 0, 0)),
            pl.BlockSpec((nr, wb, buf.shape[2]), lambda i: (i, 0, 0)),
        ],
        out_specs=pl.BlockSpec((nr, lp, nq), lambda i: (i, 0, 0)),
        out_shape=jax.ShapeDtypeStruct((n, lp, nq), F32),
        compiler_params=_cp(("parallel",)),
        name="swa_sample",
    )(sinks, qkv_pad, buf)


GDN_SOLVE_GROUP = 8
GDN_SAMPLE_PAD = 16


def _mm3(a, b):
    ah = a.astype(BF16)
    al = (a - ah.astype(F32)).astype(BF16)
    bh = b.astype(BF16)
    bl = (b - bh.astype(F32)).astype(BF16)
    return _mm(jnp.concatenate([ah, ah, al], axis=1), jnp.concatenate([bh, bl, bh], axis=0))


def _block_rows(x, c, nh):
    blk = lax.broadcasted_iota(I32, x.shape, 1) // c
    return jnp.concatenate([jnp.where(blk == i, x, 0.0) for i in range(nh)], axis=0)


def _gdn_kernel(x_ref, z_ref, bg_ref, cprev_ref, s0_ref, cw_ref, alog_ref, dtb_ref, gn_ref,
                o_ref, sfin_ref, xbuf, s_sc, q_sc, k_sc, vb_sc, kb_sc, qd_sc, kd_sc, u_sc, w_sc, be_sc, gc_sc, egl_sc,
                qkd_sc, *, chunk, valid):
    t = pl.program_id(1)
    tl = x_ref.shape[1]
    c = chunk
    nck = tl // c
    nh = GDN_HEADS
    dk = GDN_DK

    @pl.when(t == 0)
    def _():
        xbuf[0:SUBLANES, :] = cprev_ref[0]
        s_sc[...] = s0_ref[0]

    xbuf[SUBLANES:SUBLANES + tl, :] = x_ref[0]
    cw = cw_ref[...]
    conv = xbuf[5:5 + tl, :] * cw[0:1, :]
    for j in range(1, GDN_CONV):
        conv = conv + xbuf[5 + j:5 + j + tl, :] * cw[j:j + 1, :]
    xbuf[0:SUBLANES, :] = xbuf[tl:tl + SUBLANES, :]
    act = _silu(conv)

    row = lax.broadcasted_iota(I32, (tl, 1), 0)
    rmask = (row % c) < valid
    bg = bg_ref[0]
    beta = jax.nn.sigmoid(bg)
    g = -jnp.exp(alog_ref[...]) * jax.nn.softplus(bg + dtb_ref[...])
    g = jnp.where(rmask, g, 0.0)
    rc = row % c
    s = 1
    while s < c:
        g = g + jnp.where(rc >= s, pltpu.roll(g, s, axis=0), 0.0)
        s *= 2
    glast = jnp.broadcast_to(g.reshape(nck, c, LANES)[:, c - 1:c, :], (nck, c, LANES)).reshape(tl, LANES)
    egc = jnp.exp(g)
    kfac = jnp.exp(glast - g)
    be_sc[...] = beta
    gc_sc[...] = g
    egl_sc[...] = jnp.exp(glast)
    for h in range(nh):
        hs = slice(h * dk, (h + 1) * dk)
        qh = act[:, h * dk:(h + 1) * dk]
        kh = act[:, (nh + h) * dk:(nh + h + 1) * dk]
        vh = act[:, (2 * nh + h) * dk:(2 * nh + h + 1) * dk]
        qh = qh * lax.rsqrt(jnp.sum(qh * qh, axis=-1, keepdims=True) + NORM_EPS) * (dk ** -0.5)
        kh = kh * lax.rsqrt(jnp.sum(kh * kh, axis=-1, keepdims=True) + NORM_EPS)
        qh = jnp.where(rmask, qh, 0.0)
        kh = jnp.where(rmask, kh, 0.0)
        vh = jnp.where(rmask, vh, 0.0)
        b_h = beta[:, h:h + 1]
        e_h = egc[:, nh + h:nh + h + 1]
        q_sc[:, hs] = qh
        k_sc[:, hs] = kh
        vb_sc[:, hs] = vh * b_h
        kb_sc[:, hs] = kh * (b_h * e_h)
        qd_sc[:, hs] = qh * e_h
        kd_sc[:, hs] = kh * kfac[:, nh + h:nh + h + 1]

    ii = lax.broadcasted_iota(I32, (c, nh * c), 0)
    jl = lax.broadcasted_iota(I32, (c, nh * c), 1) % c
    eye_cat = (ii == jl).astype(F32)
    gn = gn_ref[...]
    n_factors = max(1, int(math.ceil(math.log2(valid))))

    ng = math.gcd(nck, GDN_SOLVE_GROUP)

    def solve_body(gi, carry):
        rows_g = [pl.ds(pl.multiple_of((gi * ng + g) * c, c), c) for g in range(ng)]
        a_g, qkd_g = [], []
        for rows in rows_g:
            gcs = gc_sc[rows, :]
            bes = be_sc[rows, :]
            kk, qk, gexp, bexp = [], [], [], []
            for h in range(nh):
                hs = slice(h * dk, (h + 1) * dk)
                kb16 = k_sc[rows, hs].astype(BF16)
                kk.append(_mm(kb16, kb16, NT))
                qk.append(_mm(q_sc[rows, hs].astype(BF16), kb16, NT))
                gexp.append(jnp.broadcast_to(gcs[:, nh + h:nh + h + 1], (c, c)))
                bexp.append(jnp.broadcast_to(bes[:, h:h + 1], (c, c)))
            kk, qk, gexp, bexp = [jnp.concatenate(v, axis=1) for v in (kk, qk, gexp, bexp)]
            grow = jnp.sum(jnp.where(ii == jl, gexp, 0.0), axis=0, keepdims=True)
            decay = jnp.where(ii >= jl, jnp.exp(gexp - grow), 0.0)
            a_g.append(jnp.where(ii > jl, bexp * kk * decay, 0.0))
            qkd_g.append(qk * decay)
        t_g = [eye_cat - a for a in a_g]
        p_g = a_g
        for _ in range(n_factors - 1):
            p_g = [_mm3(p, _block_rows(p, c, nh)) for p in p_g]
            t_g = [_mm3(tv, _block_rows(eye_cat + p, c, nh)) for tv, p in zip(t_g, p_g)]
        for rows, tv, qkd in zip(rows_g, t_g, qkd_g):
            rhs = jnp.concatenate(
                [jnp.concatenate([vb_sc[rows, h * dk:(h + 1) * dk], kb_sc[rows, h * dk:(h + 1) * dk]], axis=1)
                 for h in range(nh)], axis=0)
            sol = _mm3(_block_rows(tv, c, nh), rhs)
            for h in range(nh):
                hs = slice(h * dk, (h + 1) * dk)
                u_sc[rows, hs] = sol[h * c:(h + 1) * c, :GDN_DV]
                w_sc[rows, hs] = sol[h * c:(h + 1) * c, GDN_DV:]
            qkd_sc[rows, :] = qkd
        return carry

    lax.fori_loop(0, nck // ng, solve_body, 0)

    def scan_body(ci, carry):
        rows = pl.ds(pl.multiple_of(ci * c, c), c)
        egl = egl_sc[rows, :]
        sts, v_news, o1s = [], [], []
        for h in range(nh):
            hs = slice(h * dk, (h + 1) * dk)
            st = s_sc[h]
            wq = jnp.concatenate([w_sc[rows, hs], qd_sc[rows, hs]], axis=0)
            r = _mm(wq.astype(BF16), st.astype(BF16))
            sts.append(st)
            v_news.append(u_sc[rows, hs] - r[:c])
            o1s.append(r[c:])
        o2 = _mm(_block_rows(qkd_sc[rows, :], c, nh).astype(BF16), jnp.concatenate(v_news, axis=0).astype(BF16))
        for h in range(nh):
            hs = slice(h * dk, (h + 1) * dk)
            upd = _mm(kd_sc[rows, hs].astype(BF16), v_news[h].astype(BF16), TN)
            s_sc[h] = sts[h] * egl[0:1, nh + h:nh + h + 1] + upd
            o = o1s[h] + o2[h * c:(h + 1) * c]
            o_ref[0, rows, hs] = _rms(o, gn) * _silu(z_ref[0, rows, hs])
        return carry

    lax.fori_loop(0, nck, scan_body, 0)

    @pl.when(t == pl.num_programs(1) - 1)
    def _():
        sfin_ref[0] = s_sc[...]


def gdn(x, z, bg, cprev, s0, conv_w, a_log, dt_bias, gnorm, chunk, valid, tl):
    n, lp, cq = x.shape
    nt = lp // tl
    nv = GDN_HEADS * GDN_DV
    alog = jnp.zeros((1, LANES), F32).at[0, GDN_HEADS:2 * GDN_HEADS].set(a_log)
    dtb = jnp.zeros((1, LANES), F32).at[0, GDN_HEADS:2 * GDN_HEADS].set(dt_bias)
    seq3 = lambda w: pl.BlockSpec((1, tl, w), lambda i, t: (i, t, 0))
    return pl.pallas_call(
        functools.partial(_gdn_kernel, chunk=chunk, valid=valid),
        grid=(n, nt),
        in_specs=[
            seq3(cq), seq3(nv), seq3(LANES),
            pl.BlockSpec((1, SUBLANES, cq), lambda i, t: (i, 0, 0)),
            pl.BlockSpec((1, GDN_HEADS, GDN_DK, GDN_DV), lambda i, t: (i, 0, 0, 0)),
            _full((GDN_CONV, cq)), _full((1, LANES)), _full((1, LANES)), _full((1, GDN_DV)),
        ],
        out_specs=[seq3(nv), pl.BlockSpec((1, GDN_HEADS, GDN_DK, GDN_DV), lambda i, t: (i, 0, 0, 0))],
        out_shape=[jax.ShapeDtypeStruct((n, lp, nv), F32),
                   jax.ShapeDtypeStruct((n, GDN_HEADS, GDN_DK, GDN_DV), F32)],
        scratch_shapes=[
            pltpu.VMEM((tl + SUBLANES, cq), F32),
            pltpu.VMEM((GDN_HEADS, GDN_DK, GDN_DV), F32),
        ] + [pltpu.VMEM((tl, nv), F32)] * 8 + [pltpu.VMEM((tl, LANES), F32)] * 3
        + [pltpu.VMEM((tl, GDN_HEADS * chunk), F32)],
        compiler_params=_cp(("parallel", "arbitrary")),
        name="gdn",
    )(x, z, bg, cprev, s0, conv_w, alog, dtb, gnorm.reshape(1, GDN_DV))


def _even_out_kernel(x_ref, oa_ref, ob_ref, woa_ref, wob_ref, ln_ref, wg_ref, wu_ref, wd_ref,
                     o_ref, x1_sc, h_sc, acc_sc):
    f = pl.program_id(1)

    @pl.when(f == 0)
    def _():
        x1 = x_ref[...] + (_mm(oa_ref[...].astype(BF16), woa_ref[...]) + _mm(ob_ref[...].astype(BF16), wob_ref[...]))
        x1_sc[...] = x1
        h_sc[...] = _rms(x1, ln_ref[...]).astype(BF16)
        acc_sc[...] = jnp.zeros_like(acc_sc)

    h = h_sc[...]
    act = (_silu(_mm(h, wg_ref[...])) * _mm(h, wu_ref[...])).astype(BF16)
    acc_sc[...] += _mm(act, wd_ref[...])

    @pl.when(f == pl.num_programs(1) - 1)
    def _():
        o_ref[...] = x1_sc[...] + acc_sc[...]


def even_out(x, oa, ob, woa, wob, ln, wg, wu, wd):
    t, d = x.shape
    ff = wg.shape[1]
    tm = _row_tile(t, (512, 256, 128, 64, 32, 16, 8))
    tf = next(c for c in (1408, 256, LANES) if ff % c == 0)
    row = lambda w: pl.BlockSpec((tm, w), lambda i, f: (i, 0))
    return pl.pallas_call(
        _even_out_kernel,
        grid=(t // tm, ff // tf),
        in_specs=[row(d), row(oa.shape[1]), row(ob.shape[1]), _full(woa.shape), _full(wob.shape), _full((1, d)),
                  pl.BlockSpec((d, tf), lambda i, f: (0, f)), pl.BlockSpec((d, tf), lambda i, f: (0, f)),
                  pl.BlockSpec((tf, d), lambda i, f: (f, 0))],
        out_specs=row(d),
        out_shape=jax.ShapeDtypeStruct((t, d), F32),
        scratch_shapes=[pltpu.VMEM((tm, d), F32), pltpu.VMEM((tm, d), BF16), pltpu.VMEM((tm, d), F32)],
        compiler_params=_cp(("parallel", "arbitrary")),
        name="even_out",
    )(x, oa, ob, woa, wob, ln, wg, wu, wd)


def _rope_lanes(v, c, s):
    n = v.shape[1]
    lane = lax.broadcasted_iota(I32, (1, n), 1) % LANES
    sw = jnp.where(lane < MLA_ROPE // 2, pltpu.roll(v, n - MLA_ROPE // 2, axis=1), pltpu.roll(v, MLA_ROPE // 2, axis=1))
    return v * c + sw * s


def _odd_in_kernel(x_ref, ln_ref, wxb_ref, wgb_ref, wcq_ref, wckv_ref, wkpe_ref, qn_ref, kvn_ref, wuq_ref,
                   cos_ref, sin_ref, *rest, absorbed):
    if absorbed:
        wabs_ref, xb_ref, gb_ref, q_ref, qlat_ref, rows_ref = rest
    else:
        wuk_ref, wuv_ref, xb_ref, gb_ref, q_ref, k_ref, v_ref, rows_ref = rest
    h = _rms(x_ref[...], ln_ref[...]).astype(BF16)
    xb_ref[...] = _mm(h, wxb_ref[...])
    gb_ref[...] = _mm(h, wgb_ref[...])
    cq = _mm(h, wcq_ref[...])
    ckv = _mm(h, wckv_ref[...])
    kpe = _mm(h, wkpe_ref[...])
    cqn = _rms(cq, qn_ref[...]).astype(BF16)
    ckvn = _rms(ckv, kvn_ref[...])
    c128 = cos_ref[...]
    s128 = sin_ref[...]
    nh = MLA_HEADS
    q = _rope_lanes(_mm(cqn, wuq_ref[...]), jnp.concatenate([c128] * nh, axis=1), jnp.concatenate([s128] * nh, axis=1))
    kpe_r = _rope_lanes(kpe, c128, s128)
    qb = q.astype(BF16)
    q_ref[...] = qb
    rows_ref[:, 0:MLA_KV_RANK] = ckvn
    rows_ref[:, MLA_KV_RANK:MLA_ROW] = kpe_r[:, 0:MLA_ROPE]
    if absorbed:
        qlat_ref[...] = _mm(qb, wabs_ref[...])
    else:
        ckvb = ckvn.astype(BF16)
        k_ref[...] = (_mm(ckvb, wuk_ref[...]) + jnp.concatenate([kpe_r] * nh, axis=1)).astype(BF16)
        v_ref[...] = _mm(ckvb, wuv_ref[...]).astype(BF16)


def odd_in(x, ln, ws, qn, kvn, wuq, cos_t, sin_t, extra, absorbed):
    t, d = x.shape
    tm = _row_tile(t, (512, 256, 128, 64, 32, 16, 8))
    nblk = cos_t.shape[0] // tm
    hb = MLA_HEADS * LANES
    row = lambda w: pl.BlockSpec((tm, w), lambda i: (i, 0))
    tbl = pl.BlockSpec((tm, LANES), lambda i: (i % nblk, 0))
    if absorbed:
        outs = [(LRU_WIDTH, F32), (LRU_WIDTH, F32), (hb, BF16), (extra[0].shape[1], F32), (MLA_ROW, F32)]
    else:
        outs = [(LRU_WIDTH, F32), (LRU_WIDTH, F32), (hb, BF16), (hb, BF16), (hb, BF16), (MLA_ROW, F32)]
    return pl.pallas_call(
        functools.partial(_odd_in_kernel, absorbed=absorbed),
        grid=(t // tm,),
        in_specs=[row(d), _full((1, d))] + [_full(w.shape) for w in ws]
        + [_full(qn.shape), _full(kvn.shape), _full(wuq.shape), tbl, tbl] + [_full(w.shape) for w in extra],
        out_specs=[row(w) for w, _ in outs],
        out_shape=[jax.ShapeDtypeStruct((t, w), dt) for w, dt in outs],
        compiler_params=_cp(("parallel",)),
        name="odd_in",
    )(x, ln, *ws, qn, kvn, wuq, cos_t, sin_t, *extra)


def _expm1(x):
    u = jnp.exp(x)
    um1 = u - 1.0
    small = um1 * x / jnp.log(u)
    return jnp.where(um1 == 0.0, x, jnp.where(jnp.abs(x) < 0.5, small, um1))


def _lru_gates(xc, wa_ref, wx_ref, ba_ref, bx_ref, lam_ref):
    xcb = xc.astype(BF16)
    r = jax.nn.sigmoid(_mm(xcb, wa_ref[...]) + ba_ref[...])
    i = jax.nn.sigmoid(_mm(xcb, wx_ref[...]) + bx_ref[...])
    log_a = -LRU_C * r * jax.nn.softplus(-lam_ref[...])
    a = jnp.exp(log_a)
    b = jnp.sqrt(-_expm1(2.0 * log_a)) * (i * xc)
    return a, b


def _lru_prompt_kernel(x_ref, g_ref, cprev_ref, h0_ref, cw_ref, cb_ref, wa_ref, wx_ref, ba_ref, bx_ref, lam_ref,
                       y_ref, hl_ref, xbuf, h_sc):
    t = pl.program_id(1)
    tl = x_ref.shape[1]

    @pl.when(t == 0)
    def _():
        xbuf[0:SUBLANES, :] = cprev_ref[0]
        h_sc[0:1, :] = h0_ref[0]

    xbuf[SUBLANES:SUBLANES + tl, :] = x_ref[0]
    cw = cw_ref[...]
    xc = xbuf[5:5 + tl, :] * cw[0:1, :]
    for j in range(1, cw.shape[0]):
        xc = xc + xbuf[5 + j:5 + j + tl, :] * cw[j:j + 1, :]
    xc = xc + cb_ref[...]
    xbuf[0:SUBLANES, :] = xbuf[tl:tl + SUBLANES, :]
    a, b = _lru_gates(xc, wa_ref, wx_ref, ba_ref, bx_ref, lam_ref)
    row = lax.broadcasted_iota(I32, (tl, 1), 0)
    s = 1
    while s < tl:
        m = row >= s
        b = jnp.where(m, a * pltpu.roll(b, s, axis=0) + b, b)
        a = jnp.where(m, a * pltpu.roll(a, s, axis=0), a)
        s *= 2
    hs = a * h_sc[0:1, :] + b
    h_sc[0:1, :] = hs[tl - 1:tl, :]
    y_ref[0] = hs * jax.nn.gelu(g_ref[0])
    hl_ref[0] = hs[tl - 1:tl, :]


def lru_prompt(xb, gb, cprev, h0, cw, cb, wa, wx, ba, bx, lam, tl):
    n, l, c = xb.shape
    seq = pl.BlockSpec((1, tl, c), lambda i, t: (i, t, 0))
    return pl.pallas_call(
        _lru_prompt_kernel,
        grid=(n, l // tl),
        in_specs=[seq, seq, pl.BlockSpec((1, SUBLANES, c), lambda i, t: (i, 0, 0)),
                  pl.BlockSpec((1, 1, c), lambda i, t: (i, 0, 0)),
                  _full(cw.shape), _full((1, c)), _full(wa.shape), _full(wx.shape), _full((1, c)), _full((1, c)),
                  _full((1, c))],
        out_specs=[seq, pl.BlockSpec((1, 1, c), lambda i, t: (i, 0, 0))],
        out_shape=[jax.ShapeDtypeStruct((n, l, c), F32), jax.ShapeDtypeStruct((n, 1, c), F32)],
        scratch_shapes=[pltpu.VMEM((tl + SUBLANES, c), F32), pltpu.VMEM((SUBLANES, c), F32)],
        compiler_params=_cp(("parallel", "arbitrary")),
        name="lru_prompt",
    )(xb, gb, cprev, h0, cw, cb, wa, wx, ba, bx, lam)


def _lru_sample_kernel(x_ref, g_ref, cprev_ref, h0_ref, cw_ref, cb_ref, wa_ref, wx_ref, ba_ref, bx_ref, lam_ref,
                       y_ref, hl_ref):
    l = x_ref.shape[0]
    cw = cw_ref[...]
    nw = cw.shape[0]
    xx = [cprev_ref[j] for j in range(nw - 1)] + [x_ref[j] for j in range(l)]
    h = h0_ref[...]
    for i in range(l):
        xc = xx[i] * cw[0:1, :]
        for j in range(1, nw):
            xc = xc + xx[i + j] * cw[j:j + 1, :]
        xc = xc + cb_ref[...]
        a, b = _lru_gates(xc, wa_ref, wx_ref, ba_ref, bx_ref, lam_ref)
        h = a * h + b
        y_ref[i] = h * jax.nn.gelu(g_ref[i])
    hl_ref[...] = h


def lru_sample(xb, gb, cprev, h0, cw, cb, wa, wx, ba, bx, lam):
    l, n, c = xb.shape
    return pl.pallas_call(
        _lru_sample_kernel,
        out_shape=[jax.ShapeDtypeStruct((l, n, c), F32), jax.ShapeDtypeStruct((n, c), F32)],
        compiler_params=pltpu.CompilerParams(vmem_limit_bytes=VMEM_LIMIT),
        name="lru_sample",
    )(xb, gb, cprev, h0, cw, cb, wa, wx, ba, bx, lam)


MLA_HEADS_PER_STEP = 4


def _mla_prompt_kernel(q_ref, k_ref, v_ref, o_ref, m_sc, l_sc, acc_sc, *, scale):
    qi = pl.program_id(2)
    tq = q_ref.shape[0]
    tk = tq
    g = q_ref.shape[1] // LANES
    c1 = scale * math.log2(math.e)
    m_sc[...] = jnp.full_like(m_sc, -jnp.inf)
    l_sc[...] = jnp.zeros_like(l_sc)
    acc_sc[...] = jnp.zeros_like(acc_sc)
    on_or_below = lax.broadcasted_iota(I32, (tq, tk), 1) <= lax.broadcasted_iota(I32, (tq, tk), 0)

    def step(ki, masked):
        rows = pl.ds(pl.multiple_of(ki * tk, tk), tk)
        scores = [_mm(q_ref[:, h * LANES:(h + 1) * LANES], k_ref[rows, h * LANES:(h + 1) * LANES], NT) * c1
                  for h in range(g)]
        for h in range(g):
            hs = slice(h * LANES, (h + 1) * LANES)
            s = scores[h]
            if masked:
                s = jnp.where(on_or_below, s, -jnp.inf)
            m_prev = m_sc[h]
            m_new = jnp.maximum(m_prev, jnp.max(s, axis=-1, keepdims=True))
            alpha = jnp.exp2(m_prev - m_new)
            p = jnp.exp2(s - jnp.concatenate([m_new] * (tk // LANES), axis=1))
            l_sc[h] = alpha * l_sc[h] + jnp.sum(p, axis=-1, keepdims=True)
            acc_sc[:, hs] = alpha * acc_sc[:, hs] + _mm(p.astype(BF16), v_ref[rows, hs])
            m_sc[h] = m_new

    def body(ki, carry):
        step(ki, False)
        return carry

    lax.fori_loop(0, qi, body, 0)
    step(qi, True)
    for h in range(g):
        hs = slice(h * LANES, (h + 1) * LANES)
        o_ref[:, hs] = (acc_sc[:, hs] / l_sc[h]).astype(o_ref.dtype)


def mla_prompt(q, k, v, n, l, tq):
    nq = l // tq
    g = MLA_HEADS_PER_STEP
    w = g * LANES
    scale = (MLA_NOPE + MLA_ROPE) ** -0.5
    return pl.pallas_call(
        functools.partial(_mla_prompt_kernel, scale=scale),
        grid=(n, MLA_HEADS // g, nq),
        in_specs=[pl.BlockSpec((tq, w), lambda i, h, j: (i * nq + j, h)),
                  pl.BlockSpec((l, w), lambda i, h, j: (i, h)), pl.BlockSpec((l, w), lambda i, h, j: (i, h))],
        out_specs=pl.BlockSpec((tq, w), lambda i, h, j: (i * nq + j, h)),
        out_shape=jax.ShapeDtypeStruct(q.shape, BF16),
        scratch_shapes=[pltpu.VMEM((g, tq, LANES), F32), pltpu.VMEM((g, tq, LANES), F32), pltpu.VMEM((tq, w), F32)],
        compiler_params=_cp(("parallel", "parallel", "arbitrary")),
        name="mla_prompt",
    )(q, k, v)


MLA_SAMPLE_PAGES = 32
MLA_SAMPLE_GROUPS = 2


def _mla_sample_kernel(pt_ref, q_ref, rows_ref, *rest, n_pages_step, l, scale):
    page_refs = rest[:n_pages_step]
    o_ref, m_sc, l_sc, acc_sc = rest[n_pages_step:]
    j = pl.program_id(1)

    @pl.when(j == 0)
    def _():
        m_sc[...] = jnp.full_like(m_sc, -jnp.inf)
        l_sc[...] = jnp.zeros_like(l_sc)
        acc_sc[...] = jnp.zeros_like(acc_sc)

    q = q_ref[0]
    per = n_pages_step // MLA_SAMPLE_GROUPS
    kts = [jnp.concatenate([r[0].astype(BF16) for r in page_refs[g * per:(g + 1) * per]], axis=1)
           for g in range(MLA_SAMPLE_GROUPS)]
    ss = [_mm(q, kt) * scale for kt in kts]
    m_run = m_sc[...]
    l_run = l_sc[...]
    acc = acc_sc[...]
    for kt, s in zip(kts, ss):
        m_new = jnp.maximum(m_run, jnp.max(s, axis=-1, keepdims=True))
        alpha = jnp.exp(m_run - m_new)
        p32 = jnp.exp(s - m_new)
        l_run = alpha * l_run + jnp.sum(p32, axis=-1, keepdims=True)
        acc = alpha * acc + _mm(p32.astype(BF16), kt, NT)
        m_run = m_new
    m_sc[...] = m_run
    l_sc[...] = l_run
    acc_sc[...] = acc

    @pl.when(j == pl.num_programs(1) - 1)
    def _():
        qf = q.astype(F32)
        rows = rows_ref[0]
        tok = lax.broadcasted_iota(I32, (qf.shape[0], 1), 0) // MLA_HEADS
        sn = []
        for mm in range(l):
            sm = jnp.sum(qf * rows[mm:mm + 1, :], axis=-1, keepdims=True) * scale
            sn.append(jnp.where(tok >= mm, sm, -jnp.inf))
        m_old = m_sc[...]
        m_fin = m_old
        for sm in sn:
            m_fin = jnp.maximum(m_fin, sm)
        al = jnp.exp(m_old - m_fin)
        lsum = al * l_sc[...]
        acc = al * acc_sc[...]
        for mm in range(l):
            pm = jnp.exp(sn[mm] - m_fin)
            lsum = lsum + pm
            acc = acc + pm * rows[mm:mm + 1, :]
        o_ref[0] = acc / lsum


def mla_sample(page_table, q_cat, rows, cache_t, n_pages_step):
    n, r, c = q_cat.shape
    l = rows.shape[1]
    n_pages = page_table.shape[1]
    steps = n_pages // n_pages_step
    scale = (MLA_NOPE + MLA_ROPE) ** -0.5
    pt = page_table.reshape(-1)

    def page_map(k):
        return lambda i, j, pt_ref: (pt_ref[i * n_pages + j * n_pages_step + k], 0, 0)

    return pl.pallas_call(
        functools.partial(_mla_sample_kernel, n_pages_step=n_pages_step, l=l, scale=scale),
        grid_spec=pltpu.PrefetchScalarGridSpec(
            num_scalar_prefetch=1,
            grid=(n, steps),
            in_specs=[pl.BlockSpec((1, r, c), lambda i, j, pt_ref: (i, 0, 0)),
                      pl.BlockSpec((1, l, c), lambda i, j, pt_ref: (i, 0, 0))]
            + [pl.BlockSpec((1, c, PAGE_SIZE), page_map(k)) for k in range(n_pages_step)],
            out_specs=pl.BlockSpec((1, r, c), lambda i, j, pt_ref: (i, 0, 0)),
            scratch_shapes=[pltpu.VMEM((r, 1), F32), pltpu.VMEM((r, 1), F32), pltpu.VMEM((r, c), F32)],
        ),
        out_shape=jax.ShapeDtypeStruct((n, r, c), F32),
        compiler_params=_cp(("parallel", "arbitrary")),
        name="mla_sample",
    )(pt, q_cat, rows, *([cache_t] * n_pages_step))


def _matmul_kernel(x_ref, w_ref, o_ref):
    o_ref[...] = _mm(x_ref[...].astype(BF16), w_ref[...]).astype(o_ref.dtype)


def matmul(x, w, out_dtype):
    t, kd = x.shape
    tm = _row_tile(t, (512, 256, 128, 64, 32, 16, 8))
    return pl.pallas_call(
        _matmul_kernel,
        grid=(t // tm,),
        in_specs=[pl.BlockSpec((tm, kd), lambda i: (i, 0)), _full(w.shape)],
        out_specs=pl.BlockSpec((tm, w.shape[1]), lambda i: (i, 0)),
        out_shape=jax.ShapeDtypeStruct((t, w.shape[1]), out_dtype),
        compiler_params=_cp(("parallel",)),
        name="matmul",
    )(x, w)


def _odd_out_kernel(x_ref, yc_ref, od_ref, woc_ref, wod_ref, ln_ref, rw_ref, rb_ref, *rest):
    x1_ref, hn_ref, idx_ref, gate_ref = rest[-4:]
    x1 = x_ref[...] + (_mm(yc_ref[...].astype(BF16), woc_ref[...]) + _mm(od_ref[...], wod_ref[...]))
    x1_ref[...] = x1
    hn = _rms(x1, ln_ref[...])
    hn_ref[...] = hn.astype(BF16)
    logits = _mm(hn.astype(BF16), rw_ref[...]) + rb_ref[...]
    lane = lax.broadcasted_iota(I32, logits.shape, 1)
    logits = jnp.where(lane < N_EXPERTS, logits, -jnp.inf)
    m1 = jnp.max(logits, axis=-1, keepdims=True)
    i1 = jnp.min(jnp.where(logits == m1, lane, LANES), axis=-1, keepdims=True)
    l2 = jnp.where(lane == i1, -jnp.inf, logits)
    m2 = jnp.max(l2, axis=-1, keepdims=True)
    i2 = jnp.min(jnp.where(l2 == m2, lane, LANES), axis=-1, keepdims=True)
    e2 = jnp.exp(m2 - m1)
    den = 1.0 + e2
    idx_ref[...] = jnp.where(lane == 0, i1, jnp.where(lane == 1, i2, 0))
    gate_ref[...] = jnp.where(lane == 0, 1.0 / den, jnp.where(lane == 1, e2 / den, 0.0))


def odd_out(x, yc, od, woc, wod, ln, rw, rb, hn_rows, hn_row0, hn_buf=None):
    t, d = x.shape
    tm = _row_tile(t, (512, 256, 128, 64, 32, 16, 8))
    assert hn_row0 % tm == 0
    off = hn_row0 // tm
    row = lambda w: pl.BlockSpec((tm, w), lambda i: (i, 0))
    args = (x, yc, od, woc, wod, ln, rw, rb) + (() if hn_buf is None else (hn_buf,))
    return pl.pallas_call(
        _odd_out_kernel,
        grid=(t // tm,),
        in_specs=[row(d), row(yc.shape[1]), row(od.shape[1]), _full(woc.shape), _full(wod.shape), _full((1, d)),
                  _full(rw.shape), _full(rb.shape)] + ([] if hn_buf is None else [pl.BlockSpec(memory_space=pl.ANY)]),
        out_specs=[row(d), pl.BlockSpec((tm, d), lambda i: (i + off, 0)), row(LANES), row(LANES)],
        out_shape=[jax.ShapeDtypeStruct((t, d), F32), jax.ShapeDtypeStruct((hn_rows, d), BF16),
                   jax.ShapeDtypeStruct((t, LANES), I32), jax.ShapeDtypeStruct((t, LANES), F32)],
        input_output_aliases={} if hn_buf is None else {8: 1},
        compiler_params=_cp(("parallel",)),
        name="odd_out",
    )(*args)


EXPERT_ROWS = 512
EXPERT_FF_CHUNK = 512
MOE_SUB = 128
MOE_PAIRS_PER_STEP = 4
MOE_COMBINE_UNITS = 8
F_VALID, F_FIRST, F_LAST, F_EMPTY = 1, 2, 4, 8


def _moe_expert_kernel(sb_ref, fl_ref, pj_ref, lo_ref, hi_ref, be_ref, *refs):
    del pj_ref, be_ref
    nu = MOE_PAIRS_PER_STEP
    tiles = [refs[3 * u:3 * u + 3] for u in range(nu)]
    wg_ref, wu_ref, wd_ref, o_ref, xacc, gacc = refs[3 * nu:]
    s = pl.program_id(0)
    fl = fl_ref[s]
    er = xacc.shape[0]

    @pl.when((fl & F_VALID) != 0)
    def _():
        @pl.when((fl & F_FIRST) != 0)
        def _():
            xacc[...] = jnp.zeros_like(xacc)
            gacc[...] = jnp.zeros_like(gacc)

        for u, (slot_ref, gate_ref, hn_ref) in enumerate(tiles):
            tile = hn_ref.shape[0]
            for sb in range(er // MOE_SUB):
                @pl.when(jnp.logical_and(lo_ref[s * nu + u] <= sb, sb <= hi_ref[s * nu + u]))
                def _(sb=sb, slot_ref=slot_ref, gate_ref=gate_ref, hn_ref=hn_ref, tile=tile):
                    rs = slice(sb * MOE_SUB, (sb + 1) * MOE_SUB)
                    srow = sb_ref[s] * er + sb * MOE_SUB + lax.broadcasted_iota(I32, (MOE_SUB, tile), 0)
                    m0 = slot_ref[0:1, :] == srow
                    m1 = slot_ref[1:2, :] == srow
                    sel = jnp.where(m0, 1.0, jnp.where(m1, 1.0, 0.0)).astype(BF16)
                    xacc[rs, :] += _mm(sel, hn_ref[...])
                    g = jnp.where(m0, gate_ref[0:1, :], jnp.where(m1, gate_ref[1:2, :], 0.0))
                    gacc[rs, :] += jnp.sum(g, axis=1, keepdims=True)

        @pl.when(jnp.logical_and((fl & F_LAST) != 0, (fl & F_EMPTY) == 0))
        def _():
            x = xacc[...].astype(BF16)
            acc = None
            for f in range(0, wg_ref.shape[2], EXPERT_FF_CHUNK):
                gg = _mm(x, wg_ref[0, :, f:f + EXPERT_FF_CHUNK])
                uu = _mm(x, wu_ref[0, :, f:f + EXPERT_FF_CHUNK])
                y = _mm((_silu(gg) * uu).astype(BF16), wd_ref[0, f:f + EXPERT_FF_CHUNK, :])
                acc = y if acc is None else acc + y
            o_ref[...] = (acc * gacc[...]).astype(o_ref.dtype)

        @pl.when((fl & F_EMPTY) != 0)
        def _():
            o_ref[...] = jnp.zeros_like(o_ref)


def moe_experts(plan, slot_t, gate_t, hn, wg, wu, wd, tile):
    sb, fl, pj, lo, hi, blk_e, n_blocks = plan
    d = hn.shape[1]
    ff = wg.shape[2]
    nu = MOE_PAIRS_PER_STEP
    wmap = lambda s, sb, fl, pj, lo, hi, be: (be[sb[s]], 0, 0)
    tile_specs, tile_args = [], []
    for u in range(nu):
        tmap = lambda s, sb, fl, pj, lo, hi, be, u=u: (0, pj[s * nu + u])
        hmap = lambda s, sb, fl, pj, lo, hi, be, u=u: (pj[s * nu + u], 0)
        tile_specs += [pl.BlockSpec((TOP_K, tile), tmap), pl.BlockSpec((TOP_K, tile), tmap),
                       pl.BlockSpec((tile, d), hmap)]
        tile_args += [slot_t, gate_t, hn]
    return pl.pallas_call(
        _moe_expert_kernel,
        grid_spec=pltpu.PrefetchScalarGridSpec(
            num_scalar_prefetch=6,
            grid=(sb.shape[0],),
            in_specs=tile_specs + [pl.BlockSpec((1, d, ff), wmap), pl.BlockSpec((1, d, ff), wmap),
                                   pl.BlockSpec((1, ff, d), wmap)],
            out_specs=pl.BlockSpec((EXPERT_ROWS, d), lambda s, sb, fl, pj, lo, hi, be: (sb[s], 0)),
            scratch_shapes=[pltpu.VMEM((EXPERT_ROWS, d), F32), pltpu.VMEM((EXPERT_ROWS, 1), F32)],
        ),
        out_shape=jax.ShapeDtypeStruct((n_blocks * EXPERT_ROWS, d), BF16),
        compiler_params=_cp(("arbitrary",)),
        name="moe_experts",
    )(sb, fl, pj, lo, hi, blk_e, *tile_args, wg, wu, wd)


def _moe_combine_kernel(sj_ref, fl_ref, qb_ref, ok_ref, slot_ref, x1_ref, *refs):
    del sj_ref
    nu = MOE_COMBINE_UNITS
    yb_refs = refs[:nu]
    fn_ref, o_ref, yacc = refs[nu:]
    s = pl.program_id(0)
    fl = fl_ref[s]
    tile = x1_ref.shape[0]

    @pl.when((fl & F_VALID) != 0)
    def _():
        @pl.when((fl & F_FIRST) != 0)
        def _():
            yacc[...] = jnp.zeros_like(yacc)

        for u, yb_ref in enumerate(yb_refs):
            @pl.when(ok_ref[s * nu + u] != 0)
            def _(u=u, yb_ref=yb_ref):
                scol = qb_ref[s * nu + u] * MOE_SUB + lax.broadcasted_iota(I32, (tile, MOE_SUB), 1)
                sl = slot_ref[...]
                sel = jnp.where(sl[:, 0:1] == scol, 1.0, jnp.where(sl[:, 1:2] == scol, 1.0, 0.0)).astype(BF16)
                yacc[...] += _mm(sel, yb_ref[...])

        @pl.when((fl & F_LAST) != 0)
        def _():
            o_ref[...] = _rms(x1_ref[...] + yacc[...], fn_ref[...])


def moe_combine(sched, slot_cols, x1, yb, fnorm, tile):
    sj, fl, qb, ok = sched
    t, d = x1.shape
    nu = MOE_COMBINE_UNITS
    tmap = lambda s, sj, fl, qb, ok: (sj[s], 0)
    yb_specs = [pl.BlockSpec((MOE_SUB, d), lambda s, sj, fl, qb, ok, u=u: (qb[s * nu + u], 0)) for u in range(nu)]
    return pl.pallas_call(
        _moe_combine_kernel,
        grid_spec=pltpu.PrefetchScalarGridSpec(
            num_scalar_prefetch=4,
            grid=(sj.shape[0],),
            in_specs=[pl.BlockSpec((tile, LANES), tmap), pl.BlockSpec((tile, d), tmap)] + yb_specs
            + [pl.BlockSpec((1, d), lambda s, sj, fl, qb, ok: (0, 0))],
            out_specs=pl.BlockSpec((tile, d), tmap),
            scratch_shapes=[pltpu.VMEM((tile, d), F32)],
        ),
        out_shape=jax.ShapeDtypeStruct((t, d), F32),
        compiler_params=_cp(("arbitrary",)),
        name="moe_combine",
    )(sj, fl, qb, ok, slot_cols, x1, *([yb] * nu), fnorm)


def _ragged_steps(counts, pmax):
    cum = jnp.cumsum(counts)
    total = cum[-1]
    ar = jnp.arange(pmax, dtype=I32)
    p = jnp.minimum(ar, jnp.maximum(total - 1, 0))
    row = jnp.minimum(jnp.sum((cum[None, :] <= p[:, None]).astype(I32), axis=1), counts.shape[0] - 1)
    off = p - (cum[row] - counts[row])
    return row, off, ar < total, total


def _group_flags(gid, valid, total):
    pmax = gid.shape[0]
    ar = jnp.arange(pmax, dtype=I32)
    prev = jnp.concatenate([gid[:1] - 1, gid[:-1]])
    nxt = jnp.concatenate([gid[1:], gid[-1:] + 1])
    first = jnp.logical_or(ar == 0, gid != prev)
    last = jnp.logical_or(ar == total - 1, gid != nxt)
    fl = F_VALID + F_FIRST * first.astype(I32) + F_LAST * last.astype(I32)
    return jnp.where(valid, fl, 0).astype(I32)


def _grouped_steps(cnt, nu, smax):
    row, off, valid, total = _ragged_steps((cnt + nu - 1) // nu, smax)
    k = off[:, None] * nu + jnp.arange(nu, dtype=I32)[None, :]
    c = cnt[row][:, None]
    pvalid = jnp.logical_and(valid[:, None], k < c)
    pidx = (jnp.cumsum(cnt) - cnt)[row][:, None] + jnp.minimum(k, jnp.maximum(c - 1, 0))
    return row, _group_flags(row, valid, total), pidx.reshape(-1), pvalid.reshape(-1)


def _moe_plan(e_all, tile):
    t = e_all.shape[0]
    nt = t // tile
    er = EXPERT_ROWS
    ex = jnp.arange(N_EXPERTS, dtype=I32)[None, :]
    oh = jnp.logical_or(e_all[:, 0:1] == ex, e_all[:, 1:2] == ex).astype(I32)
    cs = jnp.cumsum(oh, axis=0)
    rank = cs - oh
    counts = cs[-1]
    padded = (counts + er - 1) // er * er
    pad_end = jnp.cumsum(padded)
    pad_start = pad_end - padded
    slot_e = pad_start[None, :] + rank
    slot = jnp.stack([jnp.sum(jnp.where(e_all[:, k:k + 1] == ex, slot_e, 0), axis=1) for k in range(TOP_K)],
                     axis=1).astype(I32)
    n_blocks = -(-(t * TOP_K) // er) + N_EXPERTS
    blk = jnp.arange(n_blocks, dtype=I32)
    blk_e = jnp.minimum(jnp.sum((blk[:, None] * er >= pad_end[None, :]).astype(I32), axis=1), N_EXPERTS - 1)
    n_used = pad_end[-1] // er
    tstart = jnp.concatenate([rank[::tile], counts[None, :]], axis=0)
    first_slot = pad_start[None, :] + tstart[:-1]
    last_slot = pad_start[None, :] + tstart[1:] - 1

    def sub_range(b, fs, ls):
        lo = jnp.maximum(fs - b * er, 0) // MOE_SUB
        hi = jnp.where(ls >= fs, jnp.minimum(ls - b * er, er - 1), -1) // MOE_SUB
        return lo.astype(I32), jnp.maximum(hi, -1).astype(I32)

    be_oh = (blk_e[:, None] == ex).astype(I32)
    r0 = blk * er - jnp.sum(be_oh * pad_start[None, :], axis=1)
    r1 = jnp.minimum(r0 + er - 1, jnp.sum(be_oh * counts[None, :], axis=1) - 1)
    ts_b = jnp.sum(be_oh[:, None, :] * tstart[None, 1:, :], axis=2)
    jlo = jnp.minimum(jnp.sum((ts_b <= r0[:, None]).astype(I32), axis=1), nt - 1)
    jhi = jnp.minimum(jnp.sum((ts_b <= r1[:, None]).astype(I32), axis=1), nt - 1)
    cnt_b = jnp.where(blk < n_used, jhi - jlo + 1, 1)
    nu = MOE_PAIRS_PER_STEP
    pmax = n_blocks + nt * N_EXPERTS
    row, off, _, _ = _ragged_steps(cnt_b, pmax)
    pj = (jlo[row] + off).astype(I32)
    je = pj * N_EXPERTS + blk_e[row]
    lo, hi = sub_range(row, first_slot.reshape(-1)[je], last_slot.reshape(-1)[je])
    sb, sfl, pidx, pvalid = _grouped_steps(cnt_b, nu, n_blocks + -(-pmax // nu))
    sfl = jnp.where(jnp.logical_and(sfl != 0, sb >= n_used), sfl + F_EMPTY, sfl)
    pvalid = jnp.logical_and(pvalid, jnp.repeat(sb < n_used, nu))
    expert_plan = (sb, sfl, pj[pidx], jnp.where(pvalid, lo[pidx], 1), jnp.where(pvalid, hi[pidx], 0),
                   blk_e.astype(I32), n_blocks)

    def combine_sched(j0, j1):
        ntr = j1 - j0
        nuc = MOE_COMBINE_UNITS
        fs = first_slot[j0:j1].reshape(-1)
        ls = last_slot[j0:j1].reshape(-1)
        blo = fs // MOE_SUB
        nb = jnp.where(ls >= fs, ls // MOE_SUB - blo + 1, 0)
        per_pair = -(-tile // MOE_SUB) + 1
        pmax_c = min(n_blocks * (er // MOE_SUB) + ntr * N_EXPERTS, per_pair * ntr * N_EXPERTS)
        r, o, _, _ = _ragged_steps(nb, pmax_c)
        qb = (blo[r] + o).astype(I32)
        cnt_j = jnp.sum(nb.reshape(ntr, N_EXPERTS), axis=1)
        sj, sfl_c, pidx_c, pvalid_c = _grouped_steps(cnt_j, nuc, ntr + -(-pmax_c // nuc))
        return sj, sfl_c, qb[pidx_c], pvalid_c.astype(I32)

    return slot, expert_plan, combine_sched


def _rope_tables(pos):
    half = MLA_ROPE // 2
    inv = ROPE_THETA ** (-jnp.arange(half, dtype=F32) * 2.0 / MLA_ROPE)
    ang = pos.astype(F32)[:, None] * inv[None, :]
    cos = jnp.cos(ang)
    sin = jnp.sin(ang)
    n = pos.shape[0]
    cos_t = jnp.concatenate([cos, cos, jnp.ones((n, LANES - MLA_ROPE), F32)], axis=1)
    sin_t = jnp.concatenate([-sin, sin, jnp.zeros((n, LANES - MLA_ROPE), F32)], axis=1)
    return cos_t, sin_t


def _block_diag(w):
    b, i, j = w.shape
    eye = jnp.eye(b, dtype=w.dtype)
    return (eye[:, None, :, None] * w[:, :, None, :]).reshape(b * i, b * j)


def _pad_rows(a, rows):
    return jnp.pad(a, ((0, 0), (0, rows - a.shape[1]), (0, 0)))


def kernel(x_prompt, x_sample, state_swa_kv, state_gdn_conv, state_gdn_s, state_lru_conv, state_lru_h, cache_mla, page_table, e_ln_mix, e_w_in, e_gdn_conv_w, e_gdn_a_log, e_gdn_dt_bias, e_gdn_norm, e_swa_sinks, e_w_out, e_ln_ffn, e_ffn_gate, e_ffn_up, e_ffn_down, o_ln_mix, o_w_in, o_lru_conv_w, o_lru_conv_b, o_lru_w_a, o_lru_b_a, o_lru_w_x, o_lru_b_x, o_lru_lambda, o_mla_q_norm, o_mla_w_uq, o_mla_kv_norm, o_mla_w_uk, o_mla_w_uv, o_w_out, o_ln_ffn, o_router_w, o_router_b, o_exp_gate, o_exp_up, o_exp_down, final_norm):
    nb, lp, d = x_prompt.shape
    ns, ls, _ = x_sample.shape
    past_len = page_table.shape[1] * PAGE_SIZE
    tp, ts = nb * lp, ns * ls
    xp = x_prompt.reshape(tp, d)
    xs = x_sample.reshape(ts, d)
    row1 = lambda v: v.reshape(1, -1)

    na = (SWA_HEADS + 2 * SWA_KV_HEADS) * SWA_HEAD_DIM
    nz = GDN_HEADS * GDN_DV
    w_in = e_w_in[0].astype(BF16)
    o1, o2, o3 = na, na + GDN_QKV, na + GDN_QKV + nz
    w_groups = [w_in[:, :o1], w_in[:, o1:o2], w_in[:, o2:o3],
                jnp.pad(w_in[:, o3:], ((0, 0), (0, LANES - 2 * GDN_HEADS)))]
    ln = row1(e_ln_mix[0])
    qkv_p, gx_p, z_p, bg_p = norm_proj(xp, ln, w_groups)
    qkv_s, gx_s, z_s, bg_s = norm_proj(xs, ln, w_groups)

    sinks = e_swa_sinks[0]
    oa_p = swa_prompt(qkv_p, sinks, nb, lp)
    wbuf = state_swa_kv.shape[2]
    nkv = 2 * SWA_KV_HEADS * SWA_HEAD_DIM
    buf = state_swa_kv[0].reshape(ns, wbuf, nkv)
    qkv_s3 = qkv_s.reshape(ns, ls, na)
    oa_s = swa_sample(_pad_rows(qkv_s3, SUBLANES), buf, sinks, ls)[:, :ls].reshape(ts, -1)
    kv_p = qkv_p.reshape(nb, lp, na)[:, lp - SWA_WINDOW:, na - nkv:]
    swa_kv_p = kv_p.reshape(1, nb, SWA_WINDOW, 2, SWA_KV_HEADS, SWA_HEAD_DIM)
    kv_s = jnp.concatenate([buf, qkv_s3[:, :, na - nkv:]], axis=1)[:, ls:]
    swa_kv_s = kv_s.reshape(1, ns, wbuf, 2, SWA_KV_HEADS, SWA_HEAD_DIM)

    gx_p3 = gx_p.reshape(nb, lp, GDN_QKV)
    gx_s3 = gx_s.reshape(ns, ls, GDN_QKV)
    gargs = (e_gdn_conv_w[0], e_gdn_a_log[0], e_gdn_dt_bias[0], e_gdn_norm[0])
    ob_p, gs_p = gdn(gx_p3, z_p.reshape(nb, lp, nz), bg_p.reshape(nb, lp, LANES),
                     jnp.zeros((nb, SUBLANES, GDN_QKV), F32), jnp.zeros((nb, GDN_HEADS, GDN_DK, GDN_DV), F32),
                     *gargs, chunk=math.gcd(lp, GDN_CHUNK), valid=math.gcd(lp, GDN_CHUNK), tl=min(lp, 512))
    cprev_s = jnp.pad(state_gdn_conv[0], ((0, 0), (SUBLANES - (GDN_CONV - 1), 0), (0, 0)))
    assert ls <= GDN_SAMPLE_PAD and math.gcd(ls, GDN_CHUNK) == ls
    ob_s, gs_s = gdn(_pad_rows(gx_s3, GDN_SAMPLE_PAD), _pad_rows(z_s.reshape(ns, ls, nz), GDN_SAMPLE_PAD),
                     _pad_rows(bg_s.reshape(ns, ls, LANES), GDN_SAMPLE_PAD), cprev_s, state_gdn_s[0],
                     *gargs, chunk=GDN_SAMPLE_PAD, valid=ls, tl=GDN_SAMPLE_PAD)
    ob_s = ob_s[:, :ls].reshape(ts, nz)
    gconv_p = gx_p3[:, lp - (GDN_CONV - 1):][None]
    gconv_s = gx_s3[:, ls - (GDN_CONV - 1):][None]

    w_out = e_w_out[0].astype(BF16)
    nqa = SWA_HEADS * SWA_HEAD_DIM
    ffn = (row1(e_ln_ffn[0]), e_ffn_gate[0].astype(BF16), e_ffn_up[0].astype(BF16), e_ffn_down[0].astype(BF16))
    xp = even_out(xp, oa_p, ob_p.reshape(tp, nz), w_out[:nqa], w_out[nqa:], *ffn)
    xs = even_out(xs, oa_s, ob_s, w_out[:nqa], w_out[nqa:], *ffn)

    w_in = o_w_in[0].astype(BF16)
    c0 = LRU_WIDTH
    c1 = 2 * LRU_WIDTH
    c2 = c1 + MLA_Q_RANK
    c3 = c2 + MLA_KV_RANK
    w_groups = [w_in[:, :c0], w_in[:, c0:c1], w_in[:, c1:c2], w_in[:, c2:c3],
                jnp.pad(w_in[:, c3:], ((0, 0), (0, LANES - MLA_ROPE)))]
    hd_q = MLA_NOPE + MLA_ROPE
    half = MLA_ROPE // 2
    wq = o_mla_w_uq[0].reshape(MLA_Q_RANK, MLA_HEADS, hd_q)
    wuq = jnp.concatenate([wq[:, :, MLA_NOPE:], wq[:, :, :MLA_NOPE],
                           jnp.zeros((MLA_Q_RANK, MLA_HEADS, LANES - hd_q), F32)], axis=2)
    wuq = wuq.reshape(MLA_Q_RANK, MLA_HEADS * LANES).astype(BF16)
    w_uk = o_mla_w_uk[0]
    w_uv = o_mla_w_uv[0]
    wuk = jnp.pad(w_uk, ((0, 0), (0, 0), (MLA_ROPE, LANES - hd_q))).reshape(MLA_KV_RANK, -1).astype(BF16)
    wuv = jnp.pad(w_uv, ((0, 0), (0, 0), (0, LANES - MLA_V))).reshape(MLA_KV_RANK, -1).astype(BF16)
    wabs = _block_diag(jnp.pad(jnp.transpose(w_uk, (1, 2, 0)), ((0, 0), (MLA_ROPE, LANES - hd_q), (0, 0)))).astype(BF16)
    wuv_bd = _block_diag(jnp.pad(jnp.transpose(w_uv, (1, 0, 2)), ((0, 0), (0, 0), (0, LANES - MLA_V)))).astype(BF16)
    ln = row1(o_ln_mix[0])
    qn, kvn = row1(o_mla_q_norm[0]), row1(o_mla_kv_norm[0])
    tm_p = _row_tile(tp, (512, 256, 128, 64, 32, 16, 8))
    assert lp % tm_p == 0 or tm_p % lp == 0
    pos_p = jnp.arange(max(lp, tm_p), dtype=I32) % lp
    pos_s = past_len + (jnp.arange(ts, dtype=I32) % ls)
    cos_p, sin_p = _rope_tables(pos_p)
    cos_s, sin_s = _rope_tables(pos_s)
    xb_p, gb_p, q_p, k_p, v_p, rows_p = odd_in(xp, ln, w_groups, qn, kvn, wuq, cos_p, sin_p, [wuk, wuv], False)
    xb_s, gb_s, q_s, qlat_s, rows_s = odd_in(xs, ln, w_groups, qn, kvn, wuq, cos_s, sin_s, [wabs], True)

    lru_w = (o_lru_conv_w[0], row1(o_lru_conv_b[0]), _block_diag(o_lru_w_a[0]).astype(BF16),
             _block_diag(o_lru_w_x[0]).astype(BF16), row1(o_lru_b_a[0]), row1(o_lru_b_x[0]), row1(o_lru_lambda[0]))
    xb_p3 = xb_p.reshape(nb, lp, LRU_WIDTH)
    yc_p, lh_p = lru_prompt(xb_p3, gb_p.reshape(nb, lp, LRU_WIDTH), jnp.zeros((nb, SUBLANES, LRU_WIDTH), F32),
                            jnp.zeros((nb, 1, LRU_WIDTH), F32), *lru_w, tl=min(lp, 256))
    xb_s3 = xb_s.reshape(ns, ls, LRU_WIDTH)
    tmaj = lambda a: jnp.transpose(a, (1, 0, 2))
    yc_s, lh_s = lru_sample(tmaj(xb_s3), tmaj(gb_s.reshape(ns, ls, LRU_WIDTH)), tmaj(state_lru_conv[0]),
                            state_lru_h[0], *lru_w)
    yc_s = tmaj(yc_s).reshape(ts, LRU_WIDTH)
    lconv_p = xb_p3[:, lp - 3:][None]
    lconv_s = xb_s3[:, ls - 3:][None]

    od_p = mla_prompt(q_p, k_p, v_p, nb, lp, tq=min(lp, 512))
    q_pe = q_s.reshape(ts, MLA_HEADS, LANES)[:, :, :MLA_ROPE]
    q_cat = jnp.concatenate([qlat_s.reshape(ts, MLA_HEADS, MLA_KV_RANK).astype(BF16), q_pe], axis=-1)
    q_cat = q_cat.reshape(ns, ls * MLA_HEADS, MLA_ROW)
    n_pages = page_table.shape[1]
    cache_t = jnp.swapaxes(cache_mla.reshape(cache_mla.shape[1:]), 1, 2)
    o_lat = mla_sample(page_table, q_cat, rows_s.reshape(ns, ls, MLA_ROW), cache_t,
                       n_pages_step=math.gcd(n_pages, MLA_SAMPLE_PAGES))
    o_lat = o_lat.reshape(ts, MLA_HEADS, MLA_ROW)[:, :, :MLA_KV_RANK].reshape(ts, MLA_HEADS * MLA_KV_RANK)
    od_s = matmul(o_lat, wuv_bd, BF16)

    w_out = o_w_out[0].astype(BF16)
    wod = jnp.pad(w_out[LRU_WIDTH:].reshape(MLA_HEADS, MLA_V, d), ((0, 0), (0, LANES - MLA_V), (0, 0)))
    wod = wod.reshape(MLA_HEADS * LANES, d)
    rw = jnp.pad(o_router_w[0], ((0, 0), (0, LANES - N_EXPERTS))).astype(BF16)
    rb = jnp.pad(o_router_b[0], (0, LANES - N_EXPERTS)).reshape(1, LANES)
    lnf = row1(o_ln_ffn[0])
    tall = tp + ts
    x1_p, hn, idx_p, gate_p = odd_out(xp, yc_p.reshape(tp, LRU_WIDTH), od_p, w_out[:LRU_WIDTH], wod, lnf, rw, rb,
                                      tall, 0, jnp.zeros((tall, d), BF16))
    x1_s, hn, idx_s, gate_s = odd_out(xs, yc_s, od_s, w_out[:LRU_WIDTH], wod, lnf, rw, rb, tall, tp, hn)

    tile = math.gcd(math.gcd(tp, ts), EXPERT_ROWS)
    e_all = jnp.concatenate([idx_p[:, :TOP_K], idx_s[:, :TOP_K]], axis=0)
    g_all = jnp.concatenate([gate_p[:, :TOP_K], gate_s[:, :TOP_K]], axis=0)
    slot, expert_plan, combine_sched = _moe_plan(e_all, tile)
    yb = moe_experts(expert_plan, slot.T, g_all.T, hn, o_exp_gate[0].astype(BF16), o_exp_up[0].astype(BF16),
                     o_exp_down[0].astype(BF16), tile)
    slot_cols = jnp.pad(slot, ((0, 0), (0, LANES - TOP_K)))
    fnw = row1(final_norm)
    y_p = moe_combine(combine_sched(0, tp // tile), slot_cols[:tp], x1_p, yb, fnw, tile).reshape(nb, lp, d)
    y_s = moe_combine(combine_sched(tp // tile, tall // tile), slot_cols[tp:], x1_s, yb, fnw, tile).reshape(ns, ls, d)

    return (y_p, y_s, swa_kv_p, swa_kv_s, gconv_p, gconv_s, gs_p[None], gs_s[None],
            lconv_p, lconv_s, lh_p.reshape(1, nb, LRU_WIDTH), lh_s[None],
            rows_p.reshape(1, nb, lp // PAGE_SIZE, PAGE_SIZE, MLA_ROW), rows_s.reshape(1, ns, ls, MLA_ROW))
```

```python
import functools
import math

import jax
import jax.numpy as jnp
from jax import lax
from jax.experimental import pallas as pl
from jax.experimental.pallas import tpu as pltpu

F32 = jnp.float32
BF16 = jnp.bfloat16
I32 = jnp.int32
HI = lax.Precision.HIGHEST

D_MODEL = 1024
PAGE_SIZE = 128
SWA_WINDOW = 128
SWA_HEADS = 8
SWA_KV_HEADS = 2
SWA_GROUP = SWA_HEADS // SWA_KV_HEADS
SWA_HEAD_DIM = 64
GDN_HEADS = 4
GDN_DK = 128
GDN_DV = 128
GDN_CONV = 4
GDN_CHUNK = 64
GDN_QKV = GDN_HEADS * (2 * GDN_DK + GDN_DV)
LRU_WIDTH = 512
LRU_BLOCKS = 8
LRU_BLOCK_W = LRU_WIDTH // LRU_BLOCKS
LRU_C = 8.0
MLA_HEADS = 8
MLA_Q_RANK = 384
MLA_KV_RANK = 256
MLA_NOPE = 64
MLA_ROPE = 32
MLA_V = 64
MLA_ROW = MLA_KV_RANK + MLA_ROPE
ROPE_THETA = 10000.0
D_FF = 2816
N_EXPERTS = 8
TOP_K = 2
MOE_FF = 2048
NORM_EPS = 1e-6

LANES = 128
SUBLANES = 8
VMEM_LIMIT = 56 << 20

NN = (((1,), (0,)), ((), ()))
NT = (((1,), (1,)), ((), ()))
TN = (((0,), (0,)), ((), ()))


def _mm(a, b, dims=NN, precision=None):
    return lax.dot_general(a, b, dims, precision=precision, preferred_element_type=F32)


def _cp(sem):
    return pltpu.CompilerParams(dimension_semantics=sem, vmem_limit_bytes=VMEM_LIMIT)


def _rms(x, w):
    return x * lax.rsqrt(jnp.mean(x * x, axis=-1, keepdims=True) + NORM_EPS) * w


def _silu(x):
    return x * jax.nn.sigmoid(x)


def _full(shape):
    nd = len(shape)
    return pl.BlockSpec(shape, lambda *a: (0,) * nd)


def _row_tile(t, pref):
    for c in pref:
        if t % c == 0:
            return c
    return t


def _norm_proj_kernel(x_ref, ln_ref, *refs):
    n = len(refs) // 2
    h = _rms(x_ref[...], ln_ref[...]).astype(BF16)
    for w_ref, o_ref in zip(refs[:n], refs[n:]):
        o_ref[...] = _mm(h, w_ref[...])


def norm_proj(x, ln, ws):
    t, d = x.shape
    tm = _row_tile(t, (512, 256, 128, 64, 32, 16, 8))
    return pl.pallas_call(
        _norm_proj_kernel,
        grid=(t // tm,),
        in_specs=[pl.BlockSpec((tm, d), lambda i: (i, 0)), _full((1, d))] + [_full(w.shape) for w in ws],
        out_specs=[pl.BlockSpec((tm, w.shape[1]), lambda i: (i, 0)) for w in ws],
        out_shape=[jax.ShapeDtypeStruct((t, w.shape[1]), F32) for w in ws],
        compiler_params=_cp(("parallel",)),
        name="norm_proj",
    )(x, ln, *ws)


def _swa_softmax_pv(parts, sink):
    m = sink
    for s, _ in parts:
        m = jnp.maximum(m, jnp.max(s, axis=-1, keepdims=True))
    den = jnp.exp(sink - m)
    o = None
    for s, v in parts:
        p = jnp.exp(s - m)
        den = den + jnp.sum(p, axis=-1, keepdims=True)
        pv = _mm(p.astype(BF16), v)
        o = pv if o is None else o + pv
    return o / den


def _swa_prompt_kernel(sink_ref, q_ref, kvc_ref, kvp_ref, o_ref):
    b = pl.program_id(1)
    w = SWA_WINDOW
    hd = SWA_HEAD_DIM
    ng = SWA_GROUP
    q = q_ref[...]
    band = jnp.concatenate([kvp_ref[...], kvc_ref[...]], axis=0).astype(BF16)
    r = lax.broadcasted_iota(I32, (ng * w, 1), 0)
    grp = r // w
    qi = r % w
    t = lax.broadcasted_iota(I32, (ng * w, 2 * w), 1)
    dist = qi + w - t
    valid = jnp.logical_and(jnp.logical_and(dist >= 0, dist < w), jnp.logical_or(t >= w, b > 0))
    distf = dist.astype(F32)
    outs = [None] * SWA_HEADS
    scores = []
    for j in range(SWA_KV_HEADS):
        qs = jnp.concatenate([q[:, (j * ng + g) * hd:(j * ng + g + 1) * hd] for g in range(ng)], axis=0)
        scores.append(_mm(qs.astype(BF16), band[:, j * hd:(j + 1) * hd], NT) * (hd ** -0.5))
    for j in range(SWA_KV_HEADS):
        heads = [j * ng + g for g in range(ng)]
        slope = jnp.zeros((ng * w, 1), F32)
        sink = jnp.zeros((ng * w, 1), F32)
        for g, h in enumerate(heads):
            slope = jnp.where(grp == g, 2.0 ** (-8.0 * (h + 1) / SWA_HEADS), slope)
            sink = jnp.where(grp == g, sink_ref[h], sink)
        vb = band[:, (SWA_KV_HEADS + j) * hd:(SWA_KV_HEADS + j + 1) * hd]
        s = jnp.where(valid, scores[j] - slope * distf, -jnp.inf)
        o = _swa_softmax_pv([(s, vb)], sink)
        for g, h in enumerate(heads):
            outs[h] = o[g * w:(g + 1) * w, :]
    o_ref[...] = jnp.concatenate(outs, axis=1)


def swa_prompt(qkv, sinks, n, l):
    w = SWA_WINDOW
    nb = l // w
    nq = SWA_HEADS * SWA_HEAD_DIM
    nkv = 2 * SWA_KV_HEADS * SWA_HEAD_DIM
    return pl.pallas_call(
        _swa_prompt_kernel,
        grid=(n, nb),
        in_specs=[
            pl.BlockSpec(memory_space=pltpu.SMEM),
            pl.BlockSpec((w, nq), lambda i, b: (i * nb + b, 0)),
            pl.BlockSpec((w, nkv), lambda i, b: (i * nb + b, nq // nkv)),
            pl.BlockSpec((w, nkv), lambda i, b: (i * nb + jnp.maximum(b - 1, 0), nq // nkv)),
        ],
        out_specs=pl.BlockSpec((w, nq), lambda i, b: (i * nb + b, 0)),
        out_shape=jax.ShapeDtypeStruct((n * l, nq), F32),
        compiler_params=_cp(("parallel", "arbitrary")),
        name="swa_prompt",
    )(sinks, qkv, qkv, qkv)


SWA_SAMPLE_REQS = 4


def _swa_sample_kernel(sink_ref, q_ref, buf_ref, o_ref, *, l):
    hd = SWA_HEAD_DIM
    ng = SWA_GROUP
    nr, lp, _ = q_ref.shape
    wb = buf_ref.shape[1]
    r = lax.broadcasted_iota(I32, (ng * lp, 1), 0)
    grp = r // lp
    qi = r % lp
    dist_b = qi + wb - lax.broadcasted_iota(I32, (ng * lp, wb), 1)
    valid_b = dist_b < SWA_WINDOW
    tn = lax.broadcasted_iota(I32, (ng * lp, lp), 1)
    dist_n = qi - tn
    valid_n = jnp.logical_and(tn <= qi, tn < l)
    slopes, sinks = [], []
    for j in range(SWA_KV_HEADS):
        slope = jnp.zeros((ng * lp, 1), F32)
        sink = jnp.zeros((ng * lp, 1), F32)
        for g in range(ng):
            h = j * ng + g
            slope = jnp.where(grp == g, 2.0 ** (-8.0 * (h + 1) / SWA_HEADS), slope)
            sink = jnp.where(grp == g, sink_ref[h], sink)
        slopes.append(slope)
        sinks.append(sink)
    chains = []
    for rq in range(nr):
        x = q_ref[rq]
        buf = buf_ref[rq].astype(BF16)
        kvn = x[:, SWA_HEADS * hd:].astype(BF16)
        for j in range(SWA_KV_HEADS):
            qs = jnp.concatenate([x[:, (j * ng + g) * hd:(j * ng + g + 1) * hd] for g in range(ng)], axis=0)
            qs = qs.astype(BF16)
            ks = slice(j * hd, (j + 1) * hd)
            vs = slice((SWA_KV_HEADS + j) * hd, (SWA_KV_HEADS + j + 1) * hd)
            s_b = _mm(qs, buf[:, ks], NT) * (hd ** -0.5)
            s_n = _mm(qs, kvn[:, ks], NT) * (hd ** -0.5)
            chains.append((rq, j, s_b, s_n, buf[:, vs], kvn[:, vs]))
    outs = [[None] * SWA_HEADS for _ in range(nr)]
    for rq, j, s_b, s_n, vb, vn in chains:
        s_b = jnp.where(valid_b, s_b - slopes[j] * dist_b.astype(F32), -jnp.inf)
        s_n = jnp.where(valid_n, s_n - slopes[j] * dist_n.astype(F32), -jnp.inf)
        o = _swa_softmax_pv([(s_b, vb), (s_n, vn)], sinks[j])
        for g in range(ng):
            outs[rq][j * ng + g] = o[g * lp:(g + 1) * lp, :]
    for rq in range(nr):
        o_ref[rq] = jnp.concatenate(outs[rq], axis=1)


def swa_sample(qkv_pad, buf, sinks, l):
    n, lp, c = qkv_pad.shape
    wb = buf.shape[1]
    nq = SWA_HEADS * SWA_HEAD_DIM
    nr = math.gcd(n, SWA_SAMPLE_REQS)
    return pl.pallas_call(
        functools.partial(_swa_sample_kernel, l=l),
        grid=(n // nr,),
        in_specs=[
            pl.BlockSpec(memory_space=pltpu.SMEM),
            pl.BlockSpec((nr, lp, c), lambda i: (i, 0, 0)),
            pl.BlockSpec((nr, wb, buf.shape[2]), lambda i: (i, 0, 0)),
        ],
        out_specs=pl.BlockSpec((nr, lp, nq), lambda i: (i, 0, 0)),
        out_shape=jax.ShapeDtypeStruct((n, lp, nq), F32),
        compiler_params=_cp(("parallel",)),
        name="swa_sample",
    )(sinks, qkv_pad, buf)


GDN_SOLVE_GROUP = 8
GDN_SAMPLE_PAD = 16


def _mm3(a, b):
    ah = a.astype(BF16)
    al = (a - ah.astype(F32)).astype(BF16)
    bh = b.astype(BF16)
    bl = (b - bh.astype(F32)).astype(BF16)
    return _mm(jnp.concatenate([ah, ah, al], axis=1), jnp.concatenate([bh, bl, bh], axis=0))


def _block_rows(x, c, nh):
    blk = lax.broadcasted_iota(I32, x.shape, 1) // c
    return jnp.concatenate([jnp.where(blk == i, x, 0.0) for i in range(nh)], axis=0)


def _gdn_kernel(x_ref, z_ref, bg_ref, cprev_ref, s0_ref, cw_ref, alog_ref, dtb_ref, gn_ref,
                o_ref, sfin_ref, xbuf, s_sc, q_sc, k_sc, vb_sc, kb_sc, qd_sc, kd_sc, u_sc, w_sc, be_sc, gc_sc, egl_sc,
                qkd_sc, *, chunk, valid):
    t = pl.program_id(1)
    tl = x_ref.shape[1]
    c = chunk
    nck = tl // c
    nh = GDN_HEADS
    dk = GDN_DK

    @pl.when(t == 0)
    def _():
        xbuf[0:SUBLANES, :] = cprev_ref[0]
        s_sc[...] = s0_ref[0]

    xbuf[SUBLANES:SUBLANES + tl, :] = x_ref[0]
    cw = cw_ref[...]
    conv = xbuf[5:5 + tl, :] * cw[0:1, :]
    for j in range(1, GDN_CONV):
        conv = conv + xbuf[5 + j:5 + j + tl, :] * cw[j:j + 1, :]
    xbuf[0:SUBLANES, :] = xbuf[tl:tl + SUBLANES, :]
    act = _silu(conv)

    row = lax.broadcasted_iota(I32, (tl, 1), 0)
    rmask = (row % c) < valid
    bg = bg_ref[0]
    beta = jax.nn.sigmoid(bg)
    g = -jnp.exp(alog_ref[...]) * jax.nn.softplus(bg + dtb_ref[...])
    g = jnp.where(rmask, g, 0.0)
    rc = row % c
    s = 1
    while s < c:
        g = g + jnp.where(rc >= s, pltpu.roll(g, s, axis=0), 0.0)
        s *= 2
    glast = jnp.broadcast_to(g.reshape(nck, c, LANES)[:, c - 1:c, :], (nck, c, LANES)).reshape(tl, LANES)
    egc = jnp.exp(g)
    kfac = jnp.exp(glast - g)
    be_sc[...] = beta
    gc_sc[...] = g
    egl_sc[...] = jnp.exp(glast)
    for h in range(nh):
        hs = slice(h * dk, (h + 1) * dk)
        qh = act[:, h * dk:(h + 1) * dk]
        kh = act[:, (nh + h) * dk:(nh + h + 1) * dk]
        vh = act[:, (2 * nh + h) * dk:(2 * nh + h + 1) * dk]
        qh = qh * lax.rsqrt(jnp.sum(qh * qh, axis=-1, keepdims=True) + NORM_EPS) * (dk ** -0.5)
        kh = kh * lax.rsqrt(jnp.sum(kh * kh, axis=-1, keepdims=True) + NORM_EPS)
        qh = jnp.where(rmask, qh, 0.0)
        kh = jnp.where(rmask, kh, 0.0)
        vh = jnp.where(rmask, vh, 0.0)
        b_h = beta[:, h:h + 1]
        e_h = egc[:, nh + h:nh + h + 1]
        q_sc[:, hs] = qh
        k_sc[:, hs] = kh
        vb_sc[:, hs] = vh * b_h
        kb_sc[:, hs] = kh * (b_h * e_h)
        qd_sc[:, hs] = qh * e_h
        kd_sc[:, hs] = kh * kfac[:, nh + h:nh + h + 1]

    ii = lax.broadcasted_iota(I32, (c, nh * c), 0)
    jl = lax.broadcasted_iota(I32, (c, nh * c), 1) % c
    eye_cat = (ii == jl).astype(F32)
    gn = gn_ref[...]
    n_factors = max(1, int(math.ceil(math.log2(valid))))

    ng = math.gcd(nck, GDN_SOLVE_GROUP)

    def solve_body(gi, carry):
        rows_g = [pl.ds(pl.multiple_of((gi * ng + g) * c, c), c) for g in range(ng)]
        a_g, qkd_g = [], []
        for rows in rows_g:
            gcs = gc_sc[rows, :]
            bes = be_sc[rows, :]
            kk, qk, gexp, bexp = [], [], [], []
            for h in range(nh):
                hs = slice(h * dk, (h + 1) * dk)
                kb16 = k_sc[rows, hs].astype(BF16)
                kk.append(_mm(kb16, kb16, NT))
                qk.append(_mm(q_sc[rows, hs].astype(BF16), kb16, NT))
                gexp.append(jnp.broadcast_to(gcs[:, nh + h:nh + h + 1], (c, c)))
                bexp.append(jnp.broadcast_to(bes[:, h:h + 1], (c, c)))
            kk, qk, gexp, bexp = [jnp.concatenate(v, axis=1) for v in (kk, qk, gexp, bexp)]
            grow = jnp.sum(jnp.where(ii == jl, gexp, 0.0), axis=0, keepdims=True)
            decay = jnp.where(ii >= jl, jnp.exp(gexp - grow), 0.0)
            a_g.append(jnp.where(ii > jl, bexp * kk * decay, 0.0))
            qkd_g.append(qk * decay)
        t_g = [eye_cat - a for a in a_g]
        p_g = a_g
        for _ in range(n_factors - 1):
            p_g = [_mm3(p, _block_rows(p, c, nh)) for p in p_g]
            t_g = [_mm3(tv, _block_rows(eye_cat + p, c, nh)) for tv, p in zip(t_g, p_g)]
        for rows, tv, qkd in zip(rows_g, t_g, qkd_g):
            rhs = jnp.concatenate(
                [jnp.concatenate([vb_sc[rows, h * dk:(h + 1) * dk], kb_sc[rows, h * dk:(h + 1) * dk]], axis=1)
                 for h in range(nh)], axis=0)
            sol = _mm3(_block_rows(tv, c, nh), rhs)
            for h in range(nh):
                hs = slice(h * dk, (h + 1) * dk)
                u_sc[rows, hs] = sol[h * c:(h + 1) * c, :GDN_DV]
                w_sc[rows, hs] = sol[h * c:(h + 1) * c, GDN_DV:]
            qkd_sc[rows, :] = qkd
        return carry

    lax.fori_loop(0, nck // ng, solve_body, 0)

    def scan_body(ci, carry):
        rows = pl.ds(pl.multiple_of(ci * c, c), c)
        egl = egl_sc[rows, :]
        sts, v_news, o1s = [], [], []
        for h in range(nh):
            hs = slice(h * dk, (h + 1) * dk)
            st = s_sc[h]
            wq = jnp.concatenate([w_sc[rows, hs], qd_sc[rows, hs]], axis=0)
            r = _mm(wq.astype(BF16), st.astype(BF16))
            sts.append(st)
            v_news.append(u_sc[rows, hs] - r[:c])
            o1s.append(r[c:])
        o2 = _mm(_block_rows(qkd_sc[rows, :], c, nh).astype(BF16), jnp.concatenate(v_news, axis=0).astype(BF16))
        for h in range(nh):
            hs = slice(h * dk, (h + 1) * dk)
            upd = _mm(kd_sc[rows, hs].astype(BF16), v_news[h].astype(BF16), TN)
            s_sc[h] = sts[h] * egl[0:1, nh + h:nh + h + 1] + upd
            o = o1s[h] + o2[h * c:(h + 1) * c]
            o_ref[0, rows, hs] = _rms(o, gn) * _silu(z_ref[0, rows, hs])
        return carry

    lax.fori_loop(0, nck, scan_body, 0)

    @pl.when(t == pl.num_programs(1) - 1)
    def _():
        sfin_ref[0] = s_sc[...]


def gdn(x, z, bg, cprev, s0, conv_w, a_log, dt_bias, gnorm, chunk, valid, tl):
    n, lp, cq = x.shape
    nt = lp // tl
    nv = GDN_HEADS * GDN_DV
    alog = jnp.zeros((1, LANES), F32).at[0, GDN_HEADS:2 * GDN_HEADS].set(a_log)
    dtb = jnp.zeros((1, LANES), F32).at[0, GDN_HEADS:2 * GDN_HEADS].set(dt_bias)
    seq3 = lambda w: pl.BlockSpec((1, tl, w), lambda i, t: (i, t, 0))
    return pl.pallas_call(
        functools.partial(_gdn_kernel, chunk=chunk, valid=valid),
        grid=(n, nt),
        in_specs=[
            seq3(cq), seq3(nv), seq3(LANES),
            pl.BlockSpec((1, SUBLANES, cq), lambda i, t: (i, 0, 0)),
            pl.BlockSpec((1, GDN_HEADS, GDN_DK, GDN_DV), lambda i, t: (i, 0, 0, 0)),
            _full((GDN_CONV, cq)), _full((1, LANES)), _full((1, LANES)), _full((1, GDN_DV)),
        ],
        out_specs=[seq3(nv), pl.BlockSpec((1, GDN_HEADS, GDN_DK, GDN_DV), lambda i, t: (i, 0, 0, 0))],
        out_shape=[jax.ShapeDtypeStruct((n, lp, nv), F32),
                   jax.ShapeDtypeStruct((n, GDN_HEADS, GDN_DK, GDN_DV), F32)],
        scratch_shapes=[
            pltpu.VMEM((tl + SUBLANES, cq), F32),
            pltpu.VMEM((GDN_HEADS, GDN_DK, GDN_DV), F32),
        ] + [pltpu.VMEM((tl, nv), F32)] * 8 + [pltpu.VMEM((tl, LANES), F32)] * 3
        + [pltpu.VMEM((tl, GDN_HEADS * chunk), F32)],
        compiler_params=_cp(("parallel", "arbitrary")),
        name="gdn",
    )(x, z, bg, cprev, s0, conv_w, alog, dtb, gnorm.reshape(1, GDN_DV))


def _even_out_kernel(x_ref, oa_ref, ob_ref, woa_ref, wob_ref, ln_ref, wg_ref, wu_ref, wd_ref,
                     o_ref, x1_sc, h_sc, acc_sc):
    f = pl.program_id(1)

    @pl.when(f == 0)
    def _():
        x1 = x_ref[...] + (_mm(oa_ref[...].astype(BF16), woa_ref[...]) + _mm(ob_ref[...].astype(BF16), wob_ref[...]))
        x1_sc[...] = x1
        h_sc[...] = _rms(x1, ln_ref[...]).astype(BF16)
        acc_sc[...] = jnp.zeros_like(acc_sc)

    h = h_sc[...]
    act = (_silu(_mm(h, wg_ref[...])) * _mm(h, wu_ref[...])).astype(BF16)
    acc_sc[...] += _mm(act, wd_ref[...])

    @pl.when(f == pl.num_programs(1) - 1)
    def _():
        o_ref[...] = x1_sc[...] + acc_sc[...]


def even_out(x, oa, ob, woa, wob, ln, wg, wu, wd):
    t, d = x.shape
    ff = wg.shape[1]
    tm = _row_tile(t, (512, 256, 128, 64, 32, 16, 8))
    tf = next(c for c in (1408, 256, LANES) if ff % c == 0)
    row = lambda w: pl.BlockSpec((tm, w), lambda i, f: (i, 0))
    return pl.pallas_call(
        _even_out_kernel,
        grid=(t // tm, ff // tf),
        in_specs=[row(d), row(oa.shape[1]), row(ob.shape[1]), _full(woa.shape), _full(wob.shape), _full((1, d)),
                  pl.BlockSpec((d, tf), lambda i, f: (0, f)), pl.BlockSpec((d, tf), lambda i, f: (0, f)),
                  pl.BlockSpec((tf, d), lambda i, f: (f, 0))],
        out_specs=row(d),
        out_shape=jax.ShapeDtypeStruct((t, d), F32),
        scratch_shapes=[pltpu.VMEM((tm, d), F32), pltpu.VMEM((tm, d), BF16), pltpu.VMEM((tm, d), F32)],
        compiler_params=_cp(("parallel", "arbitrary")),
        name="even_out",
    )(x, oa, ob, woa, wob, ln, wg, wu, wd)


def _rope_lanes(v, c, s):
    n = v.shape[1]
    lane = lax.broadcasted_iota(I32, (1, n), 1) % LANES
    sw = jnp.where(lane < MLA_ROPE // 2, pltpu.roll(v, n - MLA_ROPE // 2, axis=1), pltpu.roll(v, MLA_ROPE // 2, axis=1))
    return v * c + sw * s


def _odd_in_kernel(x_ref, ln_ref, wxb_ref, wgb_ref, wcq_ref, wckv_ref, wkpe_ref, qn_ref, kvn_ref, wuq_ref,
                   cos_ref, sin_ref, *rest, absorbed):
    if absorbed:
        wabs_ref, xb_ref, gb_ref, q_ref, qlat_ref, rows_ref = rest
    else:
        wuk_ref, wuv_ref, xb_ref, gb_ref, q_ref, k_ref, v_ref, rows_ref = rest
    h = _rms(x_ref[...], ln_ref[...]).astype(BF16)
    xb_ref[...] = _mm(h, wxb_ref[...])
    gb_ref[...] = _mm(h, wgb_ref[...])
    cq = _mm(h, wcq_ref[...])
    ckv = _mm(h, wckv_ref[...])
    kpe = _mm(h, wkpe_ref[...])
    cqn = _rms(cq, qn_ref[...]).astype(BF16)
    ckvn = _rms(ckv, kvn_ref[...])
    c128 = cos_ref[...]
    s128 = sin_ref[...]
    nh = MLA_HEADS
    q = _rope_lanes(_mm(cqn, wuq_ref[...]), jnp.concatenate([c128] * nh, axis=1), jnp.concatenate([s128] * nh, axis=1))
    kpe_r = _rope_lanes(kpe, c128, s128)
    qb = q.astype(BF16)
    q_ref[...] = qb
    rows_ref[:, 0:MLA_KV_RANK] = ckvn
    rows_ref[:, MLA_KV_RANK:MLA_ROW] = kpe_r[:, 0:MLA_ROPE]
    if absorbed:
        qlat_ref[...] = _mm(qb, wabs_ref[...])
    else:
        ckvb = ckvn.astype(BF16)
        k_ref[...] = (_mm(ckvb, wuk_ref[...]) + jnp.concatenate([kpe_r] * nh, axis=1)).astype(BF16)
        v_ref[...] = _mm(ckvb, wuv_ref[...]).astype(BF16)


def odd_in(x, ln, ws, qn, kvn, wuq, cos_t, sin_t, extra, absorbed):
    t, d = x.shape
    tm = _row_tile(t, (512, 256, 128, 64, 32, 16, 8))
    nblk = cos_t.shape[0] // tm
    hb = MLA_HEADS * LANES
    row = lambda w: pl.BlockSpec((tm, w), lambda i: (i, 0))
    tbl = pl.BlockSpec((tm, LANES), lambda i: (i % nblk, 0))
    if absorbed:
        outs = [(LRU_WIDTH, F32), (LRU_WIDTH, F32), (hb, BF16), (extra[0].shape[1], F32), (MLA_ROW, F32)]
    else:
        outs = [(LRU_WIDTH, F32), (LRU_WIDTH, F32), (hb, BF16), (hb, BF16), (hb, BF16), (MLA_ROW, F32)]
    return pl.pallas_call(
        functools.partial(_odd_in_kernel, absorbed=absorbed),
        grid=(t // tm,),
        in_specs=[row(d), _full((1, d))] + [_full(w.shape) for w in ws]
        + [_full(qn.shape), _full(kvn.shape), _full(wuq.shape), tbl, tbl] + [_full(w.shape) for w in extra],
        out_specs=[row(w) for w, _ in outs],
        out_shape=[jax.ShapeDtypeStruct((t, w), dt) for w, dt in outs],
        compiler_params=_cp(("parallel",)),
        name="odd_in",
    )(x, ln, *ws, qn, kvn, wuq, cos_t, sin_t, *extra)


def _expm1(x):
    u = jnp.exp(x)
    um1 = u - 1.0
    small = um1 * x / jnp.log(u)
    return jnp.where(um1 == 0.0, x, jnp.where(jnp.abs(x) < 0.5, small, um1))


def _lru_gates(xc, wa_ref, wx_ref, ba_ref, bx_ref, lam_ref):
    xcb = xc.astype(BF16)
    r = jax.nn.sigmoid(_mm(xcb, wa_ref[...]) + ba_ref[...])
    i = jax.nn.sigmoid(_mm(xcb, wx_ref[...]) + bx_ref[...])
    log_a = -LRU_C * r * jax.nn.softplus(-lam_ref[...])
    a = jnp.exp(log_a)
    b = jnp.sqrt(-_expm1(2.0 * log_a)) * (i * xc)
    return a, b


def _lru_prompt_kernel(x_ref, g_ref, cprev_ref, h0_ref, cw_ref, cb_ref, wa_ref, wx_ref, ba_ref, bx_ref, lam_ref,
                       y_ref, hl_ref, xbuf, h_sc):
    t = pl.program_id(1)
    tl = x_ref.shape[1]

    @pl.when(t == 0)
    def _():
        xbuf[0:SUBLANES, :] = cprev_ref[0]
        h_sc[0:1, :] = h0_ref[0]

    xbuf[SUBLANES:SUBLANES + tl, :] = x_ref[0]
    cw = cw_ref[...]
    xc = xbuf[5:5 + tl, :] * cw[0:1, :]
    for j in range(1, cw.shape[0]):
        xc = xc + xbuf[5 + j:5 + j + tl, :] * cw[j:j + 1, :]
    xc = xc + cb_ref[...]
    xbuf[0:SUBLANES, :] = xbuf[tl:tl + SUBLANES, :]
    a, b = _lru_gates(xc, wa_ref, wx_ref, ba_ref, bx_ref, lam_ref)
    row = lax.broadcasted_iota(I32, (tl, 1), 0)
    s = 1
    while s < tl:
        m = row >= s
        b = jnp.where(m, a * pltpu.roll(b, s, axis=0) + b, b)
        a = jnp.where(m, a * pltpu.roll(a, s, axis=0), a)
        s *= 2
    hs = a * h_sc[0:1, :] + b
    h_sc[0:1, :] = hs[tl - 1:tl, :]
    y_ref[0] = hs * jax.nn.gelu(g_ref[0])
    hl_ref[0] = hs[tl - 1:tl, :]


def lru_prompt(xb, gb, cprev, h0, cw, cb, wa, wx, ba, bx, lam, tl):
    n, l, c = xb.shape
    seq = pl.BlockSpec((1, tl, c), lambda i, t: (i, t, 0))
    return pl.pallas_call(
        _lru_prompt_kernel,
        grid=(n, l // tl),
        in_specs=[seq, seq, pl.BlockSpec((1, SUBLANES, c), lambda i, t: (i, 0, 0)),
                  pl.BlockSpec((1, 1, c), lambda i, t: (i, 0, 0)),
                  _full(cw.shape), _full((1, c)), _full(wa.shape), _full(wx.shape), _full((1, c)), _full((1, c)),
                  _full((1, c))],
        out_specs=[seq, pl.BlockSpec((1, 1, c), lambda i, t: (i, 0, 0))],
        out_shape=[jax.ShapeDtypeStruct((n, l, c), F32), jax.ShapeDtypeStruct((n, 1, c), F32)],
        scratch_shapes=[pltpu.VMEM((tl + SUBLANES, c), F32), pltpu.VMEM((SUBLANES, c), F32)],
        compiler_params=_cp(("parallel", "arbitrary")),
        name="lru_prompt",
    )(xb, gb, cprev, h0, cw, cb, wa, wx, ba, bx, lam)


def _lru_sample_kernel(x_ref, g_ref, cprev_ref, h0_ref, cw_ref, cb_ref, wa_ref, wx_ref, ba_ref, bx_ref, lam_ref,
                       y_ref, hl_ref):
    l = x_ref.shape[0]
    cw = cw_ref[...]
    nw = cw.shape[0]
    xx = [cprev_ref[j] for j in range(nw - 1)] + [x_ref[j] for j in range(l)]
    h = h0_ref[...]
    for i in range(l):
        xc = xx[i] * cw[0:1, :]
        for j in range(1, nw):
            xc = xc + xx[i + j] * cw[j:j + 1, :]
        xc = xc + cb_ref[...]
        a, b = _lru_gates(xc, wa_ref, wx_ref, ba_ref, bx_ref, lam_ref)
        h = a * h + b
        y_ref[i] = h * jax.nn.gelu(g_ref[i])
    hl_ref[...] = h


def lru_sample(xb, gb, cprev, h0, cw, cb, wa, wx, ba, bx, lam):
    l, n, c = xb.shape
    return pl.pallas_call(
        _lru_sample_kernel,
        out_shape=[jax.ShapeDtypeStruct((l, n, c), F32), jax.ShapeDtypeStruct((n, c), F32)],
        compiler_params=pltpu.CompilerParams(vmem_limit_bytes=VMEM_LIMIT),
        name="lru_sample",
    )(xb, gb, cprev, h0, cw, cb, wa, wx, ba, bx, lam)


MLA_HEADS_PER_STEP = 4


def _mla_prompt_kernel(q_ref, k_ref, v_ref, o_ref, m_sc, l_sc, acc_sc, *, scale):
    qi = pl.program_id(2)
    tq = q_ref.shape[0]
    tk = tq
    g = q_ref.shape[1] // LANES
    c1 = scale * math.log2(math.e)
    m_sc[...] = jnp.full_like(m_sc, -jnp.inf)
    l_sc[...] = jnp.zeros_like(l_sc)
    acc_sc[...] = jnp.zeros_like(acc_sc)
    on_or_below = lax.broadcasted_iota(I32, (tq, tk), 1) <= lax.broadcasted_iota(I32, (tq, tk), 0)

    def step(ki, masked):
        rows = pl.ds(pl.multiple_of(ki * tk, tk), tk)
        scores = [_mm(q_ref[:, h * LANES:(h + 1) * LANES], k_ref[rows, h * LANES:(h + 1) * LANES], NT) * c1
                  for h in range(g)]
        for h in range(g):
            hs = slice(h * LANES, (h + 1) * LANES)
            s = scores[h]
            if masked:
                s = jnp.where(on_or_below, s, -jnp.inf)
            m_prev = m_sc[h]
            m_new = jnp.maximum(m_prev, jnp.max(s, axis=-1, keepdims=True))
            alpha = jnp.exp2(m_prev - m_new)
            p = jnp.exp2(s - jnp.concatenate([m_new] * (tk // LANES), axis=1))
            l_sc[h] = alpha * l_sc[h] + jnp.sum(p, axis=-1, keepdims=True)
            acc_sc[:, hs] = alpha * acc_sc[:, hs] + _mm(p.astype(BF16), v_ref[rows, hs])
            m_sc[h] = m_new

    def body(ki, carry):
        step(ki, False)
        return carry

    lax.fori_loop(0, qi, body, 0)
    step(qi, True)
    for h in range(g):
        hs = slice(h * LANES, (h + 1) * LANES)
        o_ref[:, hs] = (acc_sc[:, hs] / l_sc[h]).astype(o_ref.dtype)


def mla_prompt(q, k, v, n, l, tq):
    nq = l // tq
    g = MLA_HEADS_PER_STEP
    w = g * LANES
    scale = (MLA_NOPE + MLA_ROPE) ** -0.5
    return pl.pallas_call(
        functools.partial(_mla_prompt_kernel, scale=scale),
        grid=(n, MLA_HEADS // g, nq),
        in_specs=[pl.BlockSpec((tq, w), lambda i, h, j: (i * nq + j, h)),
                  pl.BlockSpec((l, w), lambda i, h, j: (i, h)), pl.BlockSpec((l, w), lambda i, h, j: (i, h))],
        out_specs=pl.BlockSpec((tq, w), lambda i, h, j: (i * nq + j, h)),
        out_shape=jax.ShapeDtypeStruct(q.shape, BF16),
        scratch_shapes=[pltpu.VMEM((g, tq, LANES), F32), pltpu.VMEM((g, tq, LANES), F32), pltpu.VMEM((tq, w), F32)],
        compiler_params=_cp(("parallel", "parallel", "arbitrary")),
        name="mla_prompt",
    )(q, k, v)


MLA_SAMPLE_PAGES = 64
MLA_SAMPLE_GROUPS = 4


def _mla_sample_kernel(pt_ref, q_ref, rows_ref, *rest, n_pages_step, l, scale):
    page_refs = rest[:n_pages_step]
    o_ref, m_sc, l_sc, acc_sc = rest[n_pages_step:]
    j = pl.program_id(1)

    @pl.when(j == 0)
    def _():
        m_sc[...] = jnp.full_like(m_sc, -jnp.inf)
        l_sc[...] = jnp.zeros_like(l_sc)
        acc_sc[...] = jnp.zeros_like(acc_sc)

    q = q_ref[0]
    n_groups = math.gcd(n_pages_step, MLA_SAMPLE_GROUPS)
    per = n_pages_step // n_groups
    kts = [jnp.concatenate([r[0].astype(BF16) for r in page_refs[g * per:(g + 1) * per]], axis=1)
           for g in range(n_groups)]
    ss = [_mm(q, kt) * scale for kt in kts]
    m_run = m_sc[...]
    l_run = l_sc[...]
    acc = acc_sc[...]
    for kt, s in zip(kts, ss):
        m_new = jnp.maximum(m_run, jnp.max(s, axis=-1, keepdims=True))
        alpha = jnp.exp(m_run - m_new)
        p32 = jnp.exp(s - m_new)
        l_run = alpha * l_run + jnp.sum(p32, axis=-1, keepdims=True)
        acc = alpha * acc + _mm(p32.astype(BF16), kt, NT)
        m_run = m_new
    m_sc[...] = m_run
    l_sc[...] = l_run
    acc_sc[...] = acc

    @pl.when(j == pl.num_programs(1) - 1)
    def _():
        qf = q.astype(F32)
        rows = rows_ref[0]
        tok = lax.broadcasted_iota(I32, (qf.shape[0], 1), 0) // MLA_HEADS
        sn = []
        for mm in range(l):
            sm = jnp.sum(qf * rows[mm:mm + 1, :], axis=-1, keepdims=True) * scale
            sn.append(jnp.where(tok >= mm, sm, -jnp.inf))
        m_old = m_sc[...]
        m_fin = m_old
        for sm in sn:
            m_fin = jnp.maximum(m_fin, sm)
        al = jnp.exp(m_old - m_fin)
        lsum = al * l_sc[...]
        acc = al * acc_sc[...]
        for mm in range(l):
            pm = jnp.exp(sn[mm] - m_fin)
            lsum = lsum + pm
            acc = acc + pm * rows[mm:mm + 1, :]
        o_ref[0] = acc / lsum


def mla_sample(page_table, q_cat, rows, cache_t, n_pages_step):
    n, r, c = q_cat.shape
    l = rows.shape[1]
    n_pages = page_table.shape[1]
    steps = n_pages // n_pages_step
    scale = (MLA_NOPE + MLA_ROPE) ** -0.5
    pt = page_table.reshape(-1)

    def page_map(k):
        return lambda i, j, pt_ref: (pt_ref[i * n_pages + j * n_pages_step + k], 0, 0)

    return pl.pallas_call(
        functools.partial(_mla_sample_kernel, n_pages_step=n_pages_step, l=l, scale=scale),
        grid_spec=pltpu.PrefetchScalarGridSpec(
            num_scalar_prefetch=1,
            grid=(n, steps),
            in_specs=[pl.BlockSpec((1, r, c), lambda i, j, pt_ref: (i, 0, 0)),
                      pl.BlockSpec((1, l, c), lambda i, j, pt_ref: (i, 0, 0))]
            + [pl.BlockSpec((1, c, PAGE_SIZE), page_map(k)) for k in range(n_pages_step)],
            out_specs=pl.BlockSpec((1, r, c), lambda i, j, pt_ref: (i, 0, 0)),
            scratch_shapes=[pltpu.VMEM((r, 1), F32), pltpu.VMEM((r, 1), F32), pltpu.VMEM((r, c), F32)],
        ),
        out_shape=jax.ShapeDtypeStruct((n, r, c), F32),
        compiler_params=_cp(("parallel", "arbitrary")),
        name="mla_sample",
    )(pt, q_cat, rows, *([cache_t] * n_pages_step))


def _matmul_kernel(x_ref, w_ref, o_ref):
    o_ref[...] = _mm(x_ref[...].astype(BF16), w_ref[...]).astype(o_ref.dtype)


def matmul(x, w, out_dtype):
    t, kd = x.shape
    tm = _row_tile(t, (512, 256, 128, 64, 32, 16, 8))
    return pl.pallas_call(
        _matmul_kernel,
        grid=(t // tm,),
        in_specs=[pl.BlockSpec((tm, kd), lambda i: (i, 0)), _full(w.shape)],
        out_specs=pl.BlockSpec((tm, w.shape[1]), lambda i: (i, 0)),
        out_shape=jax.ShapeDtypeStruct((t, w.shape[1]), out_dtype),
        compiler_params=_cp(("parallel",)),
        name="matmul",
    )(x, w)


def _odd_out_kernel(x_ref, yc_ref, od_ref, woc_ref, wod_ref, ln_ref, rw_ref, rb_ref, *rest):
    x1_ref, hn_ref, idx_ref, gate_ref = rest[-4:]
    x1 = x_ref[...] + (_mm(yc_ref[...].astype(BF16), woc_ref[...]) + _mm(od_ref[...], wod_ref[...]))
    x1_ref[...] = x1
    hn = _rms(x1, ln_ref[...])
    hn_ref[...] = hn.astype(BF16)
    logits = _mm(hn.astype(BF16), rw_ref[...]) + rb_ref[...]
    lane = lax.broadcasted_iota(I32, logits.shape, 1)
    logits = jnp.where(lane < N_EXPERTS, logits, -jnp.inf)
    m1 = jnp.max(logits, axis=-1, keepdims=True)
    i1 = jnp.min(jnp.where(logits == m1, lane, LANES), axis=-1, keepdims=True)
    l2 = jnp.where(lane == i1, -jnp.inf, logits)
    m2 = jnp.max(l2, axis=-1, keepdims=True)
    i2 = jnp.min(jnp.where(l2 == m2, lane, LANES), axis=-1, keepdims=True)
    e2 = jnp.exp(m2 - m1)
    den = 1.0 + e2
    idx_ref[...] = jnp.where(lane == 0, i1, jnp.where(lane == 1, i2, 0))
    gate_ref[...] = jnp.where(lane == 0, 1.0 / den, jnp.where(lane == 1, e2 / den, 0.0))


def odd_out(x, yc, od, woc, wod, ln, rw, rb, hn_rows, hn_row0, hn_buf=None):
    t, d = x.shape
    tm = _row_tile(t, (512, 256, 128, 64, 32, 16, 8))
    assert hn_row0 % tm == 0
    off = hn_row0 // tm
    row = lambda w: pl.BlockSpec((tm, w), lambda i: (i, 0))
    args = (x, yc, od, woc, wod, ln, rw, rb) + (() if hn_buf is None else (hn_buf,))
    return pl.pallas_call(
        _odd_out_kernel,
        grid=(t // tm,),
        in_specs=[row(d), row(yc.shape[1]), row(od.shape[1]), _full(woc.shape), _full(wod.shape), _full((1, d)),
                  _full(rw.shape), _full(rb.shape)] + ([] if hn_buf is None else [pl.BlockSpec(memory_space=pl.ANY)]),
        out_specs=[row(d), pl.BlockSpec((tm, d), lambda i: (i + off, 0)), row(LANES), row(LANES)],
        out_shape=[jax.ShapeDtypeStruct((t, d), F32), jax.ShapeDtypeStruct((hn_rows, d), BF16),
                   jax.ShapeDtypeStruct((t, LANES), I32), jax.ShapeDtypeStruct((t, LANES), F32)],
        input_output_aliases={} if hn_buf is None else {8: 1},
        compiler_params=_cp(("parallel",)),
        name="odd_out",
    )(*args)


EXPERT_ROWS = 512
EXPERT_FF_CHUNK = 512
MOE_SUB = 128
MOE_PAIRS_PER_STEP = 4
MOE_COMBINE_UNITS = 8
F_VALID, F_FIRST, F_LAST, F_EMPTY = 1, 2, 4, 8


def _moe_expert_kernel(sb_ref, fl_ref, pj_ref, lo_ref, hi_ref, be_ref, *refs):
    del pj_ref, be_ref
    nu = MOE_PAIRS_PER_STEP
    tiles = [refs[3 * u:3 * u + 3] for u in range(nu)]
    wg_ref, wu_ref, wd_ref, o_ref, xacc, gacc = refs[3 * nu:]
    s = pl.program_id(0)
    fl = fl_ref[s]
    er = xacc.shape[0]

    @pl.when((fl & F_VALID) != 0)
    def _():
        @pl.when((fl & F_FIRST) != 0)
        def _():
            xacc[...] = jnp.zeros_like(xacc)
            gacc[...] = jnp.zeros_like(gacc)

        for u, (slot_ref, gate_ref, hn_ref) in enumerate(tiles):
            tile = hn_ref.shape[0]
            for sb in range(er // MOE_SUB):
                @pl.when(jnp.logical_and(lo_ref[s * nu + u] <= sb, sb <= hi_ref[s * nu + u]))
                def _(sb=sb, slot_ref=slot_ref, gate_ref=gate_ref, hn_ref=hn_ref, tile=tile):
                    rs = slice(sb * MOE_SUB, (sb + 1) * MOE_SUB)
                    srow = sb_ref[s] * er + sb * MOE_SUB + lax.broadcasted_iota(I32, (MOE_SUB, tile), 0)
                    m0 = slot_ref[0:1, :] == srow
                    m1 = slot_ref[1:2, :] == srow
                    sel = jnp.where(m0, 1.0, jnp.where(m1, 1.0, 0.0)).astype(BF16)
                    xacc[rs, :] += _mm(sel, hn_ref[...])
                    g = jnp.where(m0, gate_ref[0:1, :], jnp.where(m1, gate_ref[1:2, :], 0.0))
                    gacc[rs, :] += jnp.sum(g, axis=1, keepdims=True)

        @pl.when(jnp.logical_and((fl & F_LAST) != 0, (fl & F_EMPTY) == 0))
        def _():
            x = xacc[...].astype(BF16)
            acc = None
            for f in range(0, wg_ref.shape[2], EXPERT_FF_CHUNK):
                gg = _mm(x, wg_ref[0, :, f:f + EXPERT_FF_CHUNK])
                uu = _mm(x, wu_ref[0, :, f:f + EXPERT_FF_CHUNK])
                y = _mm((_silu(gg) * uu).astype(BF16), wd_ref[0, f:f + EXPERT_FF_CHUNK, :])
                acc = y if acc is None else acc + y
            o_ref[...] = (acc * gacc[...]).astype(o_ref.dtype)

        @pl.when((fl & F_EMPTY) != 0)
        def _():
            o_ref[...] = jnp.zeros_like(o_ref)


def moe_experts(plan, slot_t, gate_t, hn, wg, wu, wd, tile):
    sb, fl, pj, lo, hi, blk_e, n_blocks = plan
    d = hn.shape[1]
    ff = wg.shape[2]
    nu = MOE_PAIRS_PER_STEP
    wmap = lambda s, sb, fl, pj, lo, hi, be: (be[sb[s]], 0, 0)
    tile_specs, tile_args = [], []
    for u in range(nu):
        tmap = lambda s, sb, fl, pj, lo, hi, be, u=u: (0, pj[s * nu + u])
        hmap = lambda s, sb, fl, pj, lo, hi, be, u=u: (pj[s * nu + u], 0)
        tile_specs += [pl.BlockSpec((TOP_K, tile), tmap), pl.BlockSpec((TOP_K, tile), tmap),
                       pl.BlockSpec((tile, d), hmap)]
        tile_args += [slot_t, gate_t, hn]
    return pl.pallas_call(
        _moe_expert_kernel,
        grid_spec=pltpu.PrefetchScalarGridSpec(
            num_scalar_prefetch=6,
            grid=(sb.shape[0],),
            in_specs=tile_specs + [pl.BlockSpec((1, d, ff), wmap), pl.BlockSpec((1, d, ff), wmap),
                                   pl.BlockSpec((1, ff, d), wmap)],
            out_specs=pl.BlockSpec((EXPERT_ROWS, d), lambda s, sb, fl, pj, lo, hi, be: (sb[s], 0)),
            scratch_shapes=[pltpu.VMEM((EXPERT_ROWS, d), F32), pltpu.VMEM((EXPERT_ROWS, 1), F32)],
        ),
        out_shape=jax.ShapeDtypeStruct((n_blocks * EXPERT_ROWS, d), BF16),
        compiler_params=_cp(("arbitrary",)),
        name="moe_experts",
    )(sb, fl, pj, lo, hi, blk_e, *tile_args, wg, wu, wd)


def _moe_combine_kernel(sj_ref, fl_ref, qb_ref, ok_ref, slot_ref, x1_ref, *refs):
    del sj_ref
    nu = MOE_COMBINE_UNITS
    yb_refs = refs[:nu]
    fn_ref, o_ref, yacc = refs[nu:]
    s = pl.program_id(0)
    fl = fl_ref[s]
    tile = x1_ref.shape[0]

    @pl.when((fl & F_VALID) != 0)
    def _():
        @pl.when((fl & F_FIRST) != 0)
        def _():
            yacc[...] = jnp.zeros_like(yacc)

        for u, yb_ref in enumerate(yb_refs):
            @pl.when(ok_ref[s * nu + u] != 0)
            def _(u=u, yb_ref=yb_ref):
                scol = qb_ref[s * nu + u] * MOE_SUB + lax.broadcasted_iota(I32, (tile, MOE_SUB), 1)
                sl = slot_ref[...]
                sel = jnp.where(sl[:, 0:1] == scol, 1.0, jnp.where(sl[:, 1:2] == scol, 1.0, 0.0)).astype(BF16)
                yacc[...] += _mm(sel, yb_ref[...])

        @pl.when((fl & F_LAST) != 0)
        def _():
            o_ref[...] = _rms(x1_ref[...] + yacc[...], fn_ref[...])


def moe_combine(sched, slot_cols, x1, yb, fnorm, tile):
    sj, fl, qb, ok = sched
    t, d = x1.shape
    nu = MOE_COMBINE_UNITS
    tmap = lambda s, sj, fl, qb, ok: (sj[s], 0)
    yb_specs = [pl.BlockSpec((MOE_SUB, d), lambda s, sj, fl, qb, ok, u=u: (qb[s * nu + u], 0)) for u in range(nu)]
    return pl.pallas_call(
        _moe_combine_kernel,
        grid_spec=pltpu.PrefetchScalarGridSpec(
            num_scalar_prefetch=4,
            grid=(sj.shape[0],),
            in_specs=[pl.BlockSpec((tile, LANES), tmap), pl.BlockSpec((tile, d), tmap)] + yb_specs
            + [pl.BlockSpec((1, d), lambda s, sj, fl, qb, ok: (0, 0))],
            out_specs=pl.BlockSpec((tile, d), tmap),
            scratch_shapes=[pltpu.VMEM((tile, d), F32)],
        ),
        out_shape=jax.ShapeDtypeStruct((t, d), F32),
        compiler_params=_cp(("arbitrary",)),
        name="moe_combine",
    )(sj, fl, qb, ok, slot_cols, x1, *([yb] * nu), fnorm)


def _ragged_steps(counts, pmax):
    cum = jnp.cumsum(counts)
    total = cum[-1]
    ar = jnp.arange(pmax, dtype=I32)
    p = jnp.minimum(ar, jnp.maximum(total - 1, 0))
    row = jnp.minimum(jnp.sum((cum[None, :] <= p[:, None]).astype(I32), axis=1), counts.shape[0] - 1)
    off = p - (cum[row] - counts[row])
    return row, off, ar < total, total


def _group_flags(gid, valid, total):
    pmax = gid.shape[0]
    ar = jnp.arange(pmax, dtype=I32)
    prev = jnp.concatenate([gid[:1] - 1, gid[:-1]])
    nxt = jnp.concatenate([gid[1:], gid[-1:] + 1])
    first = jnp.logical_or(ar == 0, gid != prev)
    last = jnp.logical_or(ar == total - 1, gid != nxt)
    fl = F_VALID + F_FIRST * first.astype(I32) + F_LAST * last.astype(I32)
    return jnp.where(valid, fl, 0).astype(I32)


def _grouped_steps(cnt, nu, smax):
    row, off, valid, total = _ragged_steps((cnt + nu - 1) // nu, smax)
    k = off[:, None] * nu + jnp.arange(nu, dtype=I32)[None, :]
    c = cnt[row][:, None]
    pvalid = jnp.logical_and(valid[:, None], k < c)
    pidx = (jnp.cumsum(cnt) - cnt)[row][:, None] + jnp.minimum(k, jnp.maximum(c - 1, 0))
    return row, _group_flags(row, valid, total), pidx.reshape(-1), pvalid.reshape(-1)


def _moe_plan(e_all, tile):
    t = e_all.shape[0]
    nt = t // tile
    er = EXPERT_ROWS
    ex = jnp.arange(N_EXPERTS, dtype=I32)[None, :]
    oh = jnp.logical_or(e_all[:, 0:1] == ex, e_all[:, 1:2] == ex).astype(I32)
    cs = jnp.cumsum(oh, axis=0)
    rank = cs - oh
    counts = cs[-1]
    padded = (counts + er - 1) // er * er
    pad_end = jnp.cumsum(padded)
    pad_start = pad_end - padded
    slot_e = pad_start[None, :] + rank
    slot = jnp.stack([jnp.sum(jnp.where(e_all[:, k:k + 1] == ex, slot_e, 0), axis=1) for k in range(TOP_K)],
                     axis=1).astype(I32)
    n_blocks = -(-(t * TOP_K) // er) + N_EXPERTS
    blk = jnp.arange(n_blocks, dtype=I32)
    blk_e = jnp.minimum(jnp.sum((blk[:, None] * er >= pad_end[None, :]).astype(I32), axis=1), N_EXPERTS - 1)
    n_used = pad_end[-1] // er
    tstart = jnp.concatenate([rank[::tile], counts[None, :]], axis=0)
    first_slot = pad_start[None, :] + tstart[:-1]
    last_slot = pad_start[None, :] + tstart[1:] - 1

    def sub_range(b, fs, ls):
        lo = jnp.maximum(fs - b * er, 0) // MOE_SUB
        hi = jnp.where(ls >= fs, jnp.minimum(ls - b * er, er - 1), -1) // MOE_SUB
        return lo.astype(I32), jnp.maximum(hi, -1).astype(I32)

    be_oh = (blk_e[:, None] == ex).astype(I32)
    r0 = blk * er - jnp.sum(be_oh * pad_start[None, :], axis=1)
    r1 = jnp.minimum(r0 + er - 1, jnp.sum(be_oh * counts[None, :], axis=1) - 1)
    ts_b = jnp.sum(be_oh[:, None, :] * tstart[None, 1:, :], axis=2)
    jlo = jnp.minimum(jnp.sum((ts_b <= r0[:, None]).astype(I32), axis=1), nt - 1)
    jhi = jnp.minimum(jnp.sum((ts_b <= r1[:, None]).astype(I32), axis=1), nt - 1)
    cnt_b = jnp.where(blk < n_used, jhi - jlo + 1, 1)
    nu = MOE_PAIRS_PER_STEP
    pmax = n_blocks + nt * N_EXPERTS
    row, off, _, _ = _ragged_steps(cnt_b, pmax)
    pj = (jlo[row] + off).astype(I32)
    je = pj * N_EXPERTS + blk_e[row]
    lo, hi = sub_range(row, first_slot.reshape(-1)[je], last_slot.reshape(-1)[je])
    sb, sfl, pidx, pvalid = _grouped_steps(cnt_b, nu, n_blocks + -(-pmax // nu))
    sfl = jnp.where(jnp.logical_and(sfl != 0, sb >= n_used), sfl + F_EMPTY, sfl)
    pvalid = jnp.logical_and(pvalid, jnp.repeat(sb < n_used, nu))
    expert_plan = (sb, sfl, pj[pidx], jnp.where(pvalid, lo[pidx], 1), jnp.where(pvalid, hi[pidx], 0),
                   blk_e.astype(I32), n_blocks)

    def combine_sched(j0, j1):
        ntr = j1 - j0
        nuc = MOE_COMBINE_UNITS
        fs = first_slot[j0:j1].reshape(-1)
        ls = last_slot[j0:j1].reshape(-1)
        blo = fs // MOE_SUB
        nb = jnp.where(ls >= fs, ls // MOE_SUB - blo + 1, 0)
        per_pair = -(-tile // MOE_SUB) + 1
        pmax_c = min(n_blocks * (er // MOE_SUB) + ntr * N_EXPERTS, per_pair * ntr * N_EXPERTS)
        r, o, _, _ = _ragged_steps(nb, pmax_c)
        qb = (blo[r] + o).astype(I32)
        cnt_j = jnp.sum(nb.reshape(ntr, N_EXPERTS), axis=1)
        sj, sfl_c, pidx_c, pvalid_c = _grouped_steps(cnt_j, nuc, ntr + -(-pmax_c // nuc))
        return sj, sfl_c, qb[pidx_c], pvalid_c.astype(I32)

    return slot, expert_plan, combine_sched


def _rope_tables(pos):
    half = MLA_ROPE // 2
    inv = ROPE_THETA ** (-jnp.arange(half, dtype=F32) * 2.0 / MLA_ROPE)
    ang = pos.astype(F32)[:, None] * inv[None, :]
    cos = jnp.cos(ang)
    sin = jnp.sin(ang)
    n = pos.shape[0]
    cos_t = jnp.concatenate([cos, cos, jnp.ones((n, LANES - MLA_ROPE), F32)], axis=1)
    sin_t = jnp.concatenate([-sin, sin, jnp.zeros((n, LANES - MLA_ROPE), F32)], axis=1)
    return cos_t, sin_t


def _block_diag(w):
    b, i, j = w.shape
    eye = jnp.eye(b, dtype=w.dtype)
    return (eye[:, None, :, None] * w[:, :, None, :]).reshape(b * i, b * j)


def _pad_rows(a, rows):
    return jnp.pad(a, ((0, 0), (0, rows - a.shape[1]), (0, 0)))


def kernel(x_prompt, x_sample, state_swa_kv, state_gdn_conv, state_gdn_s, state_lru_conv, state_lru_h, cache_mla, page_table, e_ln_mix, e_w_in, e_gdn_conv_w, e_gdn_a_log, e_gdn_dt_bias, e_gdn_norm, e_swa_sinks, e_w_out, e_ln_ffn, e_ffn_gate, e_ffn_up, e_ffn_down, o_ln_mix, o_w_in, o_lru_conv_w, o_lru_conv_b, o_lru_w_a, o_lru_b_a, o_lru_w_x, o_lru_b_x, o_lru_lambda, o_mla_q_norm, o_mla_w_uq, o_mla_kv_norm, o_mla_w_uk, o_mla_w_uv, o_w_out, o_ln_ffn, o_router_w, o_router_b, o_exp_gate, o_exp_up, o_exp_down, final_norm):
    nb, lp, d = x_prompt.shape
    ns, ls, _ = x_sample.shape
    past_len = page_table.shape[1] * PAGE_SIZE
    tp, ts = nb * lp, ns * ls
    xp = x_prompt.reshape(tp, d)
    xs = x_sample.reshape(ts, d)
    row1 = lambda v: v.reshape(1, -1)

    na = (SWA_HEADS + 2 * SWA_KV_HEADS) * SWA_HEAD_DIM
    nz = GDN_HEADS * GDN_DV
    w_in = e_w_in[0].astype(BF16)
    o1, o2, o3 = na, na + GDN_QKV, na + GDN_QKV + nz
    w_groups = [w_in[:, :o1], w_in[:, o1:o2], w_in[:, o2:o3],
                jnp.pad(w_in[:, o3:], ((0, 0), (0, LANES - 2 * GDN_HEADS)))]
    ln = row1(e_ln_mix[0])
    qkv_p, gx_p, z_p, bg_p = norm_proj(xp, ln, w_groups)
    qkv_s, gx_s, z_s, bg_s = norm_proj(xs, ln, w_groups)

    sinks = e_swa_sinks[0]
    oa_p = swa_prompt(qkv_p, sinks, nb, lp)
    wbuf = state_swa_kv.shape[2]
    nkv = 2 * SWA_KV_HEADS * SWA_HEAD_DIM
    buf = state_swa_kv[0].reshape(ns, wbuf, nkv)
    qkv_s3 = qkv_s.reshape(ns, ls, na)
    oa_s = swa_sample(_pad_rows(qkv_s3, SUBLANES), buf, sinks, ls)[:, :ls].reshape(ts, -1)
    kv_p = qkv_p.reshape(nb, lp, na)[:, lp - SWA_WINDOW:, na - nkv:]
    swa_kv_p = kv_p.reshape(1, nb, SWA_WINDOW, 2, SWA_KV_HEADS, SWA_HEAD_DIM)
    kv_s = jnp.concatenate([buf, qkv_s3[:, :, na - nkv:]], axis=1)[:, ls:]
    swa_kv_s = kv_s.reshape(1, ns, wbuf, 2, SWA_KV_HEADS, SWA_HEAD_DIM)

    gx_p3 = gx_p.reshape(nb, lp, GDN_QKV)
    gx_s3 = gx_s.reshape(ns, ls, GDN_QKV)
    gargs = (e_gdn_conv_w[0], e_gdn_a_log[0], e_gdn_dt_bias[0], e_gdn_norm[0])
    ob_p, gs_p = gdn(gx_p3, z_p.reshape(nb, lp, nz), bg_p.reshape(nb, lp, LANES),
                     jnp.zeros((nb, SUBLANES, GDN_QKV), F32), jnp.zeros((nb, GDN_HEADS, GDN_DK, GDN_DV), F32),
                     *gargs, chunk=math.gcd(lp, GDN_CHUNK), valid=math.gcd(lp, GDN_CHUNK), tl=min(lp, 512))
    cprev_s = jnp.pad(state_gdn_conv[0], ((0, 0), (SUBLANES - (GDN_CONV - 1), 0), (0, 0)))
    assert ls <= GDN_SAMPLE_PAD and math.gcd(ls, GDN_CHUNK) == ls
    ob_s, gs_s = gdn(_pad_rows(gx_s3, GDN_SAMPLE_PAD), _pad_rows(z_s.reshape(ns, ls, nz), GDN_SAMPLE_PAD),
                     _pad_rows(bg_s.reshape(ns, ls, LANES), GDN_SAMPLE_PAD), cprev_s, state_gdn_s[0],
                     *gargs, chunk=GDN_SAMPLE_PAD, valid=ls, tl=GDN_SAMPLE_PAD)
    ob_s = ob_s[:, :ls].reshape(ts, nz)
    gconv_p = gx_p3[:, lp - (GDN_CONV - 1):][None]
    gconv_s = gx_s3[:, ls - (GDN_CONV - 1):][None]

    w_out = e_w_out[0].astype(BF16)
    nqa = SWA_HEADS * SWA_HEAD_DIM
    ffn = (row1(e_ln_ffn[0]), e_ffn_gate[0].astype(BF16), e_ffn_up[0].astype(BF16), e_ffn_down[0].astype(BF16))
    xp = even_out(xp, oa_p, ob_p.reshape(tp, nz), w_out[:nqa], w_out[nqa:], *ffn)
    xs = even_out(xs, oa_s, ob_s, w_out[:nqa], w_out[nqa:], *ffn)

    w_in = o_w_in[0].astype(BF16)
    c0 = LRU_WIDTH
    c1 = 2 * LRU_WIDTH
    c2 = c1 + MLA_Q_RANK
    c3 = c2 + MLA_KV_RANK
    w_groups = [w_in[:, :c0], w_in[:, c0:c1], w_in[:, c1:c2], w_in[:, c2:c3],
                jnp.pad(w_in[:, c3:], ((0, 0), (0, LANES - MLA_ROPE)))]
    hd_q = MLA_NOPE + MLA_ROPE
    half = MLA_ROPE // 2
    wq = o_mla_w_uq[0].reshape(MLA_Q_RANK, MLA_HEADS, hd_q)
    wuq = jnp.concatenate([wq[:, :, MLA_NOPE:], wq[:, :, :MLA_NOPE],
                           jnp.zeros((MLA_Q_RANK, MLA_HEADS, LANES - hd_q), F32)], axis=2)
    wuq = wuq.reshape(MLA_Q_RANK, MLA_HEADS * LANES).astype(BF16)
    w_uk = o_mla_w_uk[0]
    w_uv = o_mla_w_uv[0]
    wuk = jnp.pad(w_uk, ((0, 0), (0, 0), (MLA_ROPE, LANES - hd_q))).reshape(MLA_KV_RANK, -1).astype(BF16)
    wuv = jnp.pad(w_uv, ((0, 0), (0, 0), (0, LANES - MLA_V))).reshape(MLA_KV_RANK, -1).astype(BF16)
    wabs = _block_diag(jnp.pad(jnp.transpose(w_uk, (1, 2, 0)), ((0, 0), (MLA_ROPE, LANES - hd_q), (0, 0)))).astype(BF16)
    wuv_bd = _block_diag(jnp.pad(jnp.transpose(w_uv, (1, 0, 2)), ((0, 0), (0, 0), (0, LANES - MLA_V)))).astype(BF16)
    ln = row1(o_ln_mix[0])
    qn, kvn = row1(o_mla_q_norm[0]), row1(o_mla_kv_norm[0])
    tm_p = _row_tile(tp, (512, 256, 128, 64, 32, 16, 8))
    assert lp % tm_p == 0 or tm_p % lp == 0
    pos_p = jnp.arange(max(lp, tm_p), dtype=I32) % lp
    pos_s = past_len + (jnp.arange(ts, dtype=I32) % ls)
    cos_p, sin_p = _rope_tables(pos_p)
    cos_s, sin_s = _rope_tables(pos_s)
    xb_p, gb_p, q_p, k_p, v_p, rows_p = odd_in(xp, ln, w_groups, qn, kvn, wuq, cos_p, sin_p, [wuk, wuv], False)
    xb_s, gb_s, q_s, qlat_s, rows_s = odd_in(xs, ln, w_groups, qn, kvn, wuq, cos_s, sin_s, [wabs], True)

    lru_w = (o_lru_conv_w[0], row1(o_lru_conv_b[0]), _block_diag(o_lru_w_a[0]).astype(BF16),
             _block_diag(o_lru_w_x[0]).astype(BF16), row1(o_lru_b_a[0]), row1(o_lru_b_x[0]), row1(o_lru_lambda[0]))
    xb_p3 = xb_p.reshape(nb, lp, LRU_WIDTH)
    yc_p, lh_p = lru_prompt(xb_p3, gb_p.reshape(nb, lp, LRU_WIDTH), jnp.zeros((nb, SUBLANES, LRU_WIDTH), F32),
                            jnp.zeros((nb, 1, LRU_WIDTH), F32), *lru_w, tl=min(lp, 256))
    xb_s3 = xb_s.reshape(ns, ls, LRU_WIDTH)
    tmaj = lambda a: jnp.transpose(a, (1, 0, 2))
    yc_s, lh_s = lru_sample(tmaj(xb_s3), tmaj(gb_s.reshape(ns, ls, LRU_WIDTH)), tmaj(state_lru_conv[0]),
                            state_lru_h[0], *lru_w)
    yc_s = tmaj(yc_s).reshape(ts, LRU_WIDTH)
    lconv_p = xb_p3[:, lp - 3:][None]
    lconv_s = xb_s3[:, ls - 3:][None]

    od_p = mla_prompt(q_p, k_p, v_p, nb, lp, tq=min(lp, 512))
    q_pe = q_s.reshape(ts, MLA_HEADS, LANES)[:, :, :MLA_ROPE]
    q_cat = jnp.concatenate([qlat_s.reshape(ts, MLA_HEADS, MLA_KV_RANK).astype(BF16), q_pe], axis=-1)
    q_cat = q_cat.reshape(ns, ls * MLA_HEADS, MLA_ROW)
    n_pages = page_table.shape[1]
    cache_t = jnp.swapaxes(cache_mla.reshape(cache_mla.shape[1:]), 1, 2)
    o_lat = mla_sample(page_table, q_cat, rows_s.reshape(ns, ls, MLA_ROW), cache_t,
                       n_pages_step=math.gcd(n_pages, MLA_SAMPLE_PAGES))
    o_lat = o_lat.reshape(ts, MLA_HEADS, MLA_ROW)[:, :, :MLA_KV_RANK].reshape(ts, MLA_HEADS * MLA_KV_RANK)
    od_s = matmul(o_lat, wuv_bd, BF16)

    w_out = o_w_out[0].astype(BF16)
    wod = jnp.pad(w_out[LRU_WIDTH:].reshape(MLA_HEADS, MLA_V, d), ((0, 0), (0, LANES - MLA_V), (0, 0)))
    wod = wod.reshape(MLA_HEADS * LANES, d)
    rw = jnp.pad(o_router_w[0], ((0, 0), (0, LANES - N_EXPERTS))).astype(BF16)
    rb = jnp.pad(o_router_b[0], (0, LANES - N_EXPERTS)).reshape(1, LANES)
    lnf = row1(o_ln_ffn[0])
    tall = tp + ts
    x1_p, hn, idx_p, gate_p = odd_out(xp, yc_p.reshape(tp, LRU_WIDTH), od_p, w_out[:LRU_WIDTH], wod, lnf, rw, rb,
                                      tall, 0, jnp.zeros((tall, d), BF16))
    x1_s, hn, idx_s, gate_s = odd_out(xs, yc_s, od_s, w_out[:LRU_WIDTH], wod, lnf, rw, rb, tall, tp, hn)

    tile = math.gcd(math.gcd(tp, ts), EXPERT_ROWS)
    e_all = jnp.concatenate([idx_p[:, :TOP_K], idx_s[:, :TOP_K]], axis=0)
    g_all = jnp.concatenate([gate_p[:, :TOP_K], gate_s[:, :TOP_K]], axis=0)
    slot, expert_plan, combine_sched = _moe_plan(e_all, tile)
    yb = moe_experts(expert_plan, slot.T, g_all.T, hn, o_exp_gate[0].astype(BF16), o_exp_up[0].astype(BF16),
                     o_exp_down[0].astype(BF16), tile)
    slot_cols = jnp.pad(slot, ((0, 0), (0, LANES - TOP_K)))
    fnw = row1(final_norm)
    y_p = moe_combine(combine_sched(0, tp // tile), slot_cols[:tp], x1_p, yb, fnw, tile).reshape(nb, lp, d)
    y_s = moe_combine(combine_sched(tp // tile, tall // tile), slot_cols[tp:], x1_s, yb, fnw, tile).reshape(ns, ls, d)

    return (y_p, y_s, swa_kv_p, swa_kv_s, gconv_p, gconv_s, gs_p[None], gs_s[None],
            lconv_p, lconv_s, lh_p.reshape(1, nb, LRU_WIDTH), lh_s[None],
            rows_p.reshape(1, nb, lp // PAGE_SIZE, PAGE_SIZE, MLA_ROW), rows_s.reshape(1, ns, ls, MLA_ROW))
```
